```python
import math
import jax
import jax.numpy as jnp
from jax import lax
import numpy as np

D_MODEL = 1024
BATCH = 32
SEQ = 2048
DEPTH = 2
DEC_BATCH = 32
DEC_SEQ = 16
PAST_LEN = 2048

CHUNK = 64
Q_BLOCK = 128
N_BRANCH = 4
BRANCH_W = D_MODEL // N_BRANCH
HEAD_DIM = 64
RET_HEADS = BRANCH_W // HEAD_DIM
RET_THETA = 10000.0
FOX_HEADS = BRANCH_W // HEAD_DIM
DIFF_HEADS = BRANCH_W // HEAD_DIM
DIFF_SUB = HEAD_DIM // 2
ROPE_THETA = 500000.0
ROT_DIM = DIFF_SUB // 4
CONV_W = 31
D_FF = 2816
N_EXPERTS = 8
TOP_K = 2
D_EXPERT = 3584
ALPHA = (2.0 * DEPTH) ** 0.25
BETA = (8.0 * DEPTH) ** -0.25
N_DENSE = (DEPTH + 1) // 2
N_MOE = DEPTH // 2
EPS = 1e-5
NEG = -1e30
FORGET_BIAS_MEAN = 2.0

OFF_RET_Q = 0
OFF_RET_K = OFF_RET_Q + BRANCH_W
OFF_RET_V = OFF_RET_K + BRANCH_W
OFF_RET_G = OFF_RET_V + BRANCH_W
OFF_FOX_Q = OFF_RET_G + BRANCH_W
OFF_FOX_K = OFF_FOX_Q + BRANCH_W
OFF_FOX_V = OFF_FOX_K + BRANCH_W
OFF_FOX_F = OFF_FOX_V + BRANCH_W
OFF_CONV = OFF_FOX_F + FOX_HEADS
OFF_DIFF_Q = OFF_CONV + 2 * BRANCH_W
OFF_DIFF_K = OFF_DIFF_Q + BRANCH_W
OFF_DIFF_V = OFF_DIFF_K + BRANCH_W
OFF_GATE = OFF_DIFF_V + BRANCH_W
N_IN = OFF_GATE + N_BRANCH * D_MODEL

kernel_name = 'hybrid_chunk_stream_encoder_step'


def layer_norm(x, g, b):
    xf = x.astype(jnp.float32)
    mu = jnp.mean(xf, -1, keepdims=True)
    var = jnp.mean(jnp.square(xf - mu), -1, keepdims=True)
    return ((xf - mu) * lax.rsqrt(var + EPS) * g.astype(jnp.float32) + b.astype(jnp.float32)).astype(x.dtype)


def head_norm(x):
    xf = x.astype(jnp.float32)
    mu = jnp.mean(xf, -1, keepdims=True)
    var = jnp.mean(jnp.square(xf - mu), -1, keepdims=True)
    return ((xf - mu) * lax.rsqrt(var + EPS)).astype(x.dtype)


def rms_norm(x, g):
    xf = x.astype(jnp.float32)
    ms = jnp.mean(jnp.square(xf), -1, keepdims=True)
    return (xf * lax.rsqrt(ms + EPS) * g.astype(jnp.float32)).astype(x.dtype)


def rope(x, pos, rot_dim, theta):
    half = rot_dim // 2
    inv_freq = jnp.exp(-math.log(theta) * jnp.arange(half, dtype=jnp.float32) / half)
    ang = pos.astype(jnp.float32)[:, None] * inv_freq[None, :]
    cos = jnp.cos(ang)[None, :, None, :].astype(x.dtype)
    sin = jnp.sin(ang)[None, :, None, :].astype(x.dtype)
    x1, x2, rest = x[..., :half], x[..., half:rot_dim], x[..., rot_dim:]
    return jnp.concatenate([x1 * cos - x2 * sin, x2 * cos + x1 * sin, rest], axis=-1)


def over_query_blocks(fn, *qs):
    t = qs[0].shape[1]
    if t <= Q_BLOCK:
        return fn(*qs)
    nb = t // Q_BLOCK
    blocks = tuple(jnp.moveaxis(a.reshape(a.shape[0], nb, Q_BLOCK, *a.shape[2:]), 1, 0) for a in qs)
    out = lax.map(lambda args: fn(*args), blocks)
    out = jnp.moveaxis(out, 0, 1)
    return out.reshape(out.shape[0], t, *out.shape[3:])


def retention(q, k, v, s0):
    b, t, h, dk = q.shape
    dv = v.shape[-1]
    L = min(t, CHUNK)
    nc = t // L
    dt = q.dtype
    log_g = jnp.log1p(-jnp.exp2(-5.0 - jnp.arange(h, dtype=jnp.float32)))
    idx = jnp.arange(L, dtype=jnp.float32)
    inner = jnp.exp(log_g[:, None, None] * jnp.abs(idx[:, None] - idx[None, :])).astype(dt)
    q_dec = jnp.exp(log_g[:, None] * (idx[None, :] + 1.0)).astype(dt)
    k_dec = jnp.exp(log_g[:, None] * (L - 1.0 - idx[None, :])).astype(dt)
    chunk_dec = jnp.exp(log_g * L)[:, None, None]
    qc = q.reshape(b, nc, L, h, dk)
    kc = k.reshape(b, nc, L, h, dk)
    vc = v.reshape(b, nc, L, h, dv)
    scores = jnp.einsum('bclhd,bcmhd->bchlm', qc, kc) * inner
    y_in = jnp.einsum('bchlm,bcmhe->bclhe', scores, vc)
    kv = jnp.einsum('bcmhd,hm,bcmhe->cbhde', kc, k_dec, vc).astype(jnp.float32)

    def step(s, kv_c):
        return chunk_dec * s + kv_c, s

    s_last, s_prev = lax.scan(step, s0.astype(jnp.float32), kv)
    y_x = jnp.einsum('bclhd,hl,cbhde->bclhe', qc, q_dec, s_prev.astype(dt))
    return (y_in + y_x).reshape(b, t, h, dv), s_last.astype(s0.dtype)


def fox_attend(q, fq, qpos, k, v, fk, kpos):
    s = jnp.einsum('bqhd,bkhd->bhqk', q, k).astype(jnp.float32) * (q.shape[-1] ** -0.5)
    s = s + jnp.moveaxis(fq, -1, 1)[..., :, None] - jnp.moveaxis(fk, -1, 1)[..., None, :]
    vis = kpos[None, :] <= qpos[:, None]
    p = jax.nn.softmax(jnp.where(vis, s, NEG), axis=-1)
    return jnp.einsum('bhqk,bkhd->bqhd', p.astype(v.dtype), v)


def diff_attend(q, qpos, k, v, lam, kpos):
    b, tq, n, ds = q.shape
    h = v.shape[2]
    s = jnp.einsum('bqnd,bknd->bnqk', q, k).astype(jnp.float32) * (ds ** -0.5)
    vis = (kpos[None, :] // CHUNK) <= (qpos[:, None] // CHUNK)
    p = jax.nn.softmax(jnp.where(vis, s, NEG), axis=-1).reshape(b, h, 2, tq, -1)
    a = p[:, :, 0] - lam * p[:, :, 1]
    return jnp.einsum('bhqk,bkhe->bqhe', a.astype(v.dtype), v)


def token_mixers(u, pos, past, lam_init, w_in, b_fox_f, w_conv, b_conv, conv_ln_g, conv_ln_b,
                 diff_lambda, diff_subln_g, w_branch, w_out):
    b, t, _ = u.shape
    dt = u.dtype
    proj = u @ w_in

    def cols(off, n):
        return proj[..., off:off + n]

    def heads(a, h):
        return a.reshape(b, t, h, -1)

    rq = rope(heads(cols(OFF_RET_Q, BRANCH_W), RET_HEADS), pos, HEAD_DIM, RET_THETA)
    rk = rope(heads(cols(OFF_RET_K, BRANCH_W), RET_HEADS), pos, HEAD_DIM, RET_THETA) * (HEAD_DIM ** -0.5)
    rv = heads(cols(OFF_RET_V, BRANCH_W), RET_HEADS)
    s0 = jnp.zeros((b, RET_HEADS, HEAD_DIM, HEAD_DIM), dt) if past is None else past[5]
    ry, ret_state = retention(rq, rk, rv, s0)
    h_ret = (head_norm(ry) * jax.nn.silu(heads(cols(OFF_RET_G, BRANCH_W), RET_HEADS))).reshape(b, t, BRANCH_W)

    fq = heads(cols(OFF_FOX_Q, BRANCH_W), FOX_HEADS)
    fk = heads(cols(OFF_FOX_K, BRANCH_W), FOX_HEADS)
    fv = heads(cols(OFF_FOX_V, BRANCH_W), FOX_HEADS)
    lf = jax.nn.log_sigmoid(cols(OFF_FOX_F, FOX_HEADS).astype(jnp.float32) + b_fox_f.astype(jnp.float32))
    if past is None:
        fk_all, fv_all, lf_all = fk, fv, lf
    else:
        fk_all = jnp.concatenate([past[0], fk], axis=1)
        fv_all = jnp.concatenate([past[1], fv], axis=1)
        lf_all = jnp.concatenate([past[2].astype(jnp.float32), lf], axis=1)
    kpos = jnp.arange(fk_all.shape[1], dtype=jnp.int32)
    f_cum = jnp.cumsum(lf_all, axis=1)
    h_fox = over_query_blocks(
        lambda qb, fb, pb: fox_attend(qb, fb, pb[0], fk_all, fv_all, f_cum, kpos),
        fq, f_cum[:, -t:], pos[None]).reshape(b, t, BRANCH_W)

    cu = cols(OFF_CONV, 2 * BRANCH_W)
    glu = cu[..., :BRANCH_W] * jax.nn.sigmoid(cu[..., BRANCH_W:])
    buf = jnp.zeros((b, CONV_W - 1, BRANCH_W), dt) if past is None else past[6]
    xpad = jnp.concatenate([buf, glu], axis=1)
    conv = lax.conv_general_dilated(xpad, w_conv[:, None, :], window_strides=(1,), padding='VALID',
                                    dimension_numbers=('NWC', 'WIO', 'NWC'),
                                    feature_group_count=BRANCH_W) + b_conv
    h_conv = jax.nn.silu(layer_norm(conv, conv_ln_g, conv_ln_b))
    conv_buf = xpad[:, -(CONV_W - 1):]

    dq = rope(cols(OFF_DIFF_Q, BRANCH_W).reshape(b, t, 2 * DIFF_HEADS, DIFF_SUB), pos, ROT_DIM, ROPE_THETA)
    dk = rope(cols(OFF_DIFF_K, BRANCH_W).reshape(b, t, 2 * DIFF_HEADS, DIFF_SUB), pos, ROT_DIM, ROPE_THETA)
    dv = heads(cols(OFF_DIFF_V, BRANCH_W), DIFF_HEADS)
    lamf = diff_lambda.astype(jnp.float32)
    lam = jnp.exp(jnp.sum(lamf[0] * lamf[1])) - jnp.exp(jnp.sum(lamf[2] * lamf[3])) + lam_init
    if past is None:
        dk_all, dv_all = dk, dv
    else:
        dk_all = jnp.concatenate([past[3], dk], axis=1)
        dv_all = jnp.concatenate([past[4], dv], axis=1)
    dkpos = jnp.arange(dk_all.shape[1], dtype=jnp.int32)
    dy = over_query_blocks(lambda qb, pb: diff_attend(qb, pb[0], dk_all, dv_all, lam, dkpos), dq, pos[None])
    h_diff = (rms_norm(dy, diff_subln_g) * (1.0 - lam_init)).reshape(b, t, BRANCH_W)

    branches = (h_ret, h_fox, h_conv, h_diff)
    merged = jax.nn.sigmoid(cols(OFF_GATE, D_MODEL)) * (branches[0] @ w_branch[0])
    for n in range(1, N_BRANCH):
        merged = merged + jax.nn.sigmoid(cols(OFF_GATE + n * D_MODEL, D_MODEL)) * (branches[n] @ w_branch[n])
    out = merged @ w_out
    new_state = (fk, fv, lf.astype(dt), dk, dv, ret_state, conv_buf)
    return out, new_state


def swiglu(x, w_up, w_down):
    a, g = jnp.split(x @ w_up, 2, axis=-1)
    return (jax.nn.silu(a) * g) @ w_down


def moe_swiglu(x, w_router, b_router, w_exp_in, w_exp_out):
    logits = (x @ w_router).astype(jnp.float32) + b_router.astype(jnp.float32)
    top_v, top_i = lax.top_k(logits, TOP_K)
    wts = jax.nn.softmax(top_v, axis=-1)
    combine = jnp.sum(jax.nn.one_hot(top_i, N_EXPERTS, dtype=jnp.float32) * wts[..., None], axis=-2)
    y = combine[..., 0:1].astype(x.dtype) * swiglu(x, w_exp_in[0], w_exp_out[0])
    for e in range(1, N_EXPERTS):
        y = y + combine[..., e:e + 1].astype(x.dtype) * swiglu(x, w_exp_in[e], w_exp_out[e])
    return y


def trunk_layer(x, c, pos, past, l, w_in, b_fox_f, w_conv, b_conv, conv_ln_g, conv_ln_b, diff_lambda,
                diff_subln_g, w_branch, w_out, w_ada, b_ada, ln_g, ln_b, w_ffn_in, w_ffn_out,
                w_router, b_router, w_exp_in, w_exp_out):
    mod = jnp.einsum('bd,sde->bse', jax.nn.silu(c), w_ada[l]) + b_ada[l]
    shift, scale, gate = jnp.split(mod[:, :, None, :], 3, axis=-1)
    u = x * (1.0 + scale[:, 0]) + shift[:, 0]
    lam_init = 0.8 - 0.6 * math.exp(-0.3 * l)
    mix, st = token_mixers(u, pos, past, lam_init, w_in[l], b_fox_f[l], w_conv[l], b_conv[l],
                           conv_ln_g[l], conv_ln_b[l], diff_lambda[l], diff_subln_g[l], w_branch[l], w_out[l])
    x = layer_norm(ALPHA * x + gate[:, 0] * mix, ln_g[l, 0], ln_b[l, 0])
    u = x * (1.0 + scale[:, 1]) + shift[:, 1]
    if l % 2 == 0:
        f = swiglu(u, w_ffn_in[l // 2], w_ffn_out[l // 2])
    else:
        f = moe_swiglu(u, w_router[l // 2], b_router[l // 2], w_exp_in[l // 2], w_exp_out[l // 2])
    x = layer_norm(ALPHA * x + gate[:, 1] * f, ln_g[l, 1], ln_b[l, 1])
    return x, st


def setup_inputs(seed: int = 0) -> dict:
    key = jax.random.key(seed)
    ks = iter(jax.random.split(key, 40))

    def nrm(shape, s=1.0):
        return s * jax.random.normal(next(ks), shape, jnp.float32)

    D = D_MODEL
    return {
        'x_prompt': nrm((BATCH, SEQ, D)),
        'x_sample': nrm((DEC_BATCH, DEC_SEQ, D)),
        'c_prompt': nrm((BATCH, D)),
        'c_sample': nrm((DEC_BATCH, D)),
        'cache_fox_k': nrm((DEPTH, DEC_BATCH, PAST_LEN, FOX_HEADS, HEAD_DIM)),
        'cache_fox_v': nrm((DEPTH, DEC_BATCH, PAST_LEN, FOX_HEADS, HEAD_DIM)),
        'cache_fox_logf': jax.nn.log_sigmoid(FORGET_BIAS_MEAN + nrm((DEPTH, DEC_BATCH, PAST_LEN, FOX_HEADS))),
        'cache_diff_k': nrm((DEPTH, DEC_BATCH, PAST_LEN, 2 * DIFF_HEADS, DIFF_SUB)),
        'cache_diff_v': nrm((DEPTH, DEC_BATCH, PAST_LEN, DIFF_HEADS, HEAD_DIM)),
        'state_ret': nrm((DEPTH, DEC_BATCH, RET_HEADS, HEAD_DIM, HEAD_DIM)),
        'state_conv': nrm((DEPTH, DEC_BATCH, CONV_W - 1, BRANCH_W), 0.5),
        'w_in': nrm((DEPTH, D, N_IN), D ** -0.5),
        'b_fox_f': FORGET_BIAS_MEAN + nrm((DEPTH, FOX_HEADS), 0.1),
        'w_conv': nrm((DEPTH, CONV_W, BRANCH_W), CONV_W ** -0.5),
        'b_conv': nrm((DEPTH, BRANCH_W), 0.02),
        'conv_ln_g': 1.0 + nrm((DEPTH, BRANCH_W), 0.02),
        'conv_ln_b': nrm((DEPTH, BRANCH_W), 0.02),
        'diff_lambda': nrm((DEPTH, 4, DIFF_SUB), 0.1),
        'diff_subln_g': 1.0 + nrm((DEPTH, HEAD_DIM), 0.02),
        'w_branch': nrm((DEPTH, N_BRANCH, BRANCH_W, D), BRANCH_W ** -0.5),
        'w_out': nrm((DEPTH, D, D), BETA * D ** -0.5),
        'w_ada': nrm((DEPTH, 2, D, 3 * D), 0.5 * D ** -0.5),
        'b_ada': nrm((DEPTH, 2, 3 * D), 0.02),
        'ln_g': 1.0 + nrm((DEPTH, 2, D), 0.02),
        'ln_b': nrm((DEPTH, 2, D), 0.02),
        'w_ffn_in': nrm((N_DENSE, D, 2 * D_FF), D ** -0.5),
        'w_ffn_out': nrm((N_DENSE, D_FF, D), BETA * D_FF ** -0.5),
        'w_router': nrm((N_MOE, D, N_EXPERTS), D ** -0.5),
        'b_router': nrm((N_MOE, N_EXPERTS), 0.01),
        'w_exp_in': nrm((N_MOE, N_EXPERTS, D, 2 * D_EXPERT), D ** -0.5),
        'w_exp_out': nrm((N_MOE, N_EXPERTS, D_EXPERT, D), BETA * D_EXPERT ** -0.5),
    }


def reference(x_prompt, x_sample, c_prompt, c_sample, cache_fox_k, cache_fox_v, cache_fox_logf,
              cache_diff_k, cache_diff_v, state_ret, state_conv, w_in, b_fox_f, w_conv, b_conv,
              conv_ln_g, conv_ln_b, diff_lambda, diff_subln_g, w_branch, w_out, w_ada, b_ada,
              ln_g, ln_b, w_ffn_in, w_ffn_out, w_router, b_router, w_exp_in, w_exp_out):
    pos_p = jnp.arange(x_prompt.shape[1], dtype=jnp.int32)
    past_len = cache_fox_k.shape[2]
    pos_s = past_len + jnp.arange(x_sample.shape[1], dtype=jnp.int32)
    yp, ys = x_prompt, x_sample
    new_p, new_s = [], []
    for l in range(DEPTH):
        past_l = (cache_fox_k[l], cache_fox_v[l], cache_fox_logf[l], cache_diff_k[l], cache_diff_v[l],
                  state_ret[l], state_conv[l])
        yp, st_p = trunk_layer(yp, c_prompt, pos_p, None, l, w_in, b_fox_f, w_conv, b_conv, conv_ln_g,
                               conv_ln_b, diff_lambda, diff_subln_g, w_branch, w_out, w_ada, b_ada,
                               ln_g, ln_b, w_ffn_in, w_ffn_out, w_router, b_router, w_exp_in, w_exp_out)
        ys, st_s = trunk_layer(ys, c_sample, pos_s, past_l, l, w_in, b_fox_f, w_conv, b_conv, conv_ln_g,
                               conv_ln_b, diff_lambda, diff_subln_g, w_branch, w_out, w_ada, b_ada,
                               ln_g, ln_b, w_ffn_in, w_ffn_out, w_router, b_router, w_exp_in, w_exp_out)
        new_p.append(st_p)
        new_s.append(st_s)
    p_fox_k, p_fox_v, p_fox_logf, p_diff_k, p_diff_v, p_ret, p_conv = (jnp.stack(a) for a in zip(*new_p))
    s_fox_k, s_fox_v, s_fox_logf, s_diff_k, s_diff_v, s_ret, s_conv = (jnp.stack(a) for a in zip(*new_s))
    return (yp, ys, p_fox_k, p_fox_v, p_fox_logf, p_diff_k, p_diff_v, p_ret, p_conv,
            s_fox_k, s_fox_v, s_fox_logf, s_diff_k, s_diff_v, s_ret, s_conv)
```

```python
import functools
import math

import jax
import jax.numpy as jnp
from jax import lax
from jax.experimental import pallas as pl
from jax.experimental.pallas import tpu as pltpu

D_MODEL = 1024
BRANCH_W = 256
HEAD_DIM = 64
N_HEADS = 4
DIFF_SUB = 32
N_SUB = 8
ROT_DIM = DIFF_SUB // 4
RET_THETA = 10000.0
ROPE_THETA = 500000.0
CHUNK = 64
CONV_W = 31
D_FF = 2816
N_EXPERTS = 8
D_EXPERT = 3584
DEPTH = 2
ALPHA = (2.0 * DEPTH) ** 0.25
EPS = 1e-5
NEG = -1e30

LANES = 128
KV_BLOCK = 256
HIST_ROWS = 32
VMEM_LIMIT = 56 * 1024 * 1024

F32 = jnp.float32
BF16 = jnp.bfloat16


def _bf(x):
    return x.astype(BF16)


def _dot(a, b):
    return jnp.dot(a, b, preferred_element_type=F32)


def _dot_nt(a, b):
    return lax.dot_general(a, b, (((1,), (1,)), ((), ())), preferred_element_type=F32)


def _dot_tn(a, b):
    return lax.dot_general(a, b, (((0,), (0,)), ((), ())), preferred_element_type=F32)


def _sigmoid(x):
    return 1.0 / (1.0 + jnp.exp(-x))


def _params(n_axes):
    return pltpu.CompilerParams(dimension_semantics=("arbitrary",) * n_axes,
                                vmem_limit_bytes=VMEM_LIMIT)


def _head_sum(y, width):
    n = y.shape[-1]
    r = lax.broadcasted_iota(jnp.int32, (n, n), 0) // width
    c = lax.broadcasted_iota(jnp.int32, (n, n), 1) // width
    bd = jnp.where(r == c, 1.0, 0.0).astype(BF16)
    hi = _bf(y)
    lo = _bf(y - hi.astype(F32))
    return _dot(hi, bd) + _dot(lo, bd)


def _layer_norm_rows(z, g, b):
    mu = jnp.mean(z, axis=-1, keepdims=True)
    d = z - mu
    var = jnp.mean(d * d, axis=-1, keepdims=True)
    return d * lax.rsqrt(var + EPS) * g + b


def _mod_kernel(c_ref, w_ref, b_ref, o_ref):
    c = c_ref[...]
    sc = _bf(c * _sigmoid(c))
    o_ref[...] = _dot(sc, _bf(w_ref[...])) + b_ref[...]


def _modulation(c_all, w_ada, b_ada):
    rows = c_all.shape[0]
    depth = w_ada.shape[0]
    d3 = w_ada.shape[-1]
    nj = d3 // D_MODEL
    return pl.pallas_call(
        _mod_kernel,
        out_shape=jax.ShapeDtypeStruct((depth, 2, rows, d3), F32),
        grid=(depth * 2, nj),
        in_specs=[
            pl.BlockSpec((rows, D_MODEL), lambda i, j: (0, 0)),
            pl.BlockSpec((None, None, D_MODEL, D_MODEL), lambda i, j: (i // 2, i % 2, 0, j)),
            pl.BlockSpec((None, None, 1, D_MODEL), lambda i, j: (i // 2, i % 2, 0, j)),
        ],
        out_specs=pl.BlockSpec((None, None, rows, D_MODEL), lambda i, j: (i // 2, i % 2, 0, j)),
        compiler_params=_params(2),
        name="adaln_modulation",
    )(c_all, w_ada, b_ada.reshape(depth, 2, 1, d3))


def _ret_kernel(x_ref, sc_ref, sh_ref, w_ref, cos_ref, sin_ref, dmask_ref, qdec_ref, kdec_ref,
                cdec_ref, s0_ref, h_ref, sout_ref, s_scr):
    c = pl.program_id(1)

    @pl.when(c == 0)
    def _():
        s_scr[...] = s0_ref[...]

    u = x_ref[...] * (1.0 + sc_ref[...]) + sh_ref[...]
    p = _dot(_bf(u), w_ref[...])
    w = BRANCH_W
    cos = cos_ref[...]
    sin = sin_ref[...]
    q = p[:, 0:w] * cos + p[:, w:2 * w] * sin
    k = (p[:, 2 * w:3 * w] * cos + p[:, 3 * w:4 * w] * sin) * (HEAD_DIM ** -0.5)
    v = p[:, 4 * w:5 * w]
    g = p[:, 5 * w:6 * w]
    rows = q.shape[0]
    lane_head = lax.broadcasted_iota(jnp.int32, (1, w), 1) // HEAD_DIM
    kb = _bf(k)
    vb = _bf(v)
    y = jnp.zeros((rows, w), F32)
    for h in range(N_HEADS):
        mh = lane_head == h
        qh = _bf(jnp.where(mh, q, 0.0))
        a = _dot_nt(qh, kb) * dmask_ref[h]
        y = y + jnp.where(mh, _dot(_bf(a), vb), 0.0)
    s_prev = s_scr[...]
    y = y + _dot(_bf(q * qdec_ref[...]), _bf(s_prev))
    kv = _dot_tn(_bf(k * kdec_ref[...]), vb)
    r = lax.broadcasted_iota(jnp.int32, (w, w), 0) // HEAD_DIM
    cc = lax.broadcasted_iota(jnp.int32, (w, w), 1) // HEAD_DIM
    s_new = cdec_ref[...] * s_prev + jnp.where(r == cc, kv, 0.0)
    s_scr[...] = s_new
    sout_ref[...] = s_new
    mu = _head_sum(y, HEAD_DIM) * (1.0 / HEAD_DIM)
    d = y - mu
    var = _head_sum(d * d, HEAD_DIM) * (1.0 / HEAD_DIM)
    hn = d * lax.rsqrt(var + EPS)
    h_ref[...] = _bf(hn * (g * _sigmoid(g)))


def _retention_tables(chunk):
    log_g = jnp.log1p(-jnp.exp2(-5.0 - jnp.arange(N_HEADS, dtype=F32)))
    idx = jnp.arange(chunk, dtype=F32)
    dist = jnp.abs(idx[:, None] - idx[None, :])
    sub = jnp.arange(chunk) // CHUNK
    vis = sub[None, :] <= sub[:, None]
    dmask = jnp.where(vis[None], jnp.exp(log_g[:, None, None] * dist[None]), 0.0)
    lg_lane = jnp.repeat(log_g, HEAD_DIM)[None, :]
    qdec = jnp.exp(lg_lane * (idx[:, None] + 1.0))
    kdec = jnp.exp(lg_lane * (chunk - 1.0 - idx[:, None]))
    cdec = jnp.exp(lg_lane * chunk)
    return dmask.astype(F32), qdec, kdec, cdec


def _rope_tables(pos, dim, rot_dim, theta, n_rep):
    half = rot_dim // 2
    inv_freq = jnp.exp(-math.log(theta) * jnp.arange(half, dtype=F32) / half)
    ang = pos.astype(F32)[:, None] * inv_freq[None, :]
    t = pos.shape[0]
    cos = jnp.concatenate([jnp.cos(ang), jnp.cos(ang), jnp.ones((t, dim - rot_dim), F32)], axis=1)
    sin = jnp.concatenate([jnp.sin(ang), jnp.sin(ang), jnp.zeros((t, dim - rot_dim), F32)], axis=1)
    return jnp.tile(cos, (1, n_rep)), jnp.tile(sin, (1, n_rep))


def _rotated_columns(w, dim, rot_dim):
    half = rot_dim // 2
    k, n = w.shape
    wh = w.reshape(k, n // dim, dim)
    rot = jnp.concatenate([-wh[..., half:rot_dim], wh[..., :half], jnp.zeros_like(wh[..., rot_dim:])], axis=-1)
    return rot.reshape(k, n)


def _retention(x, scale, shift, w_ret, pos, s0_bd, chunk):
    b, t, _ = x.shape
    nc = t // chunk
    cos, sin = _rope_tables(pos, HEAD_DIM, HEAD_DIM, RET_THETA, N_HEADS)
    dmask, qdec, kdec, cdec = _retention_tables(chunk)
    w = BRANCH_W
    full = lambda shape: pl.BlockSpec(shape, lambda i, j: (0,) * len(shape))
    return pl.pallas_call(
        _ret_kernel,
        out_shape=(jax.ShapeDtypeStruct((b, t, w), BF16), jax.ShapeDtypeStruct((b, w, w), F32)),
        grid=(b, nc),
        in_specs=[
            pl.BlockSpec((None, chunk, D_MODEL), lambda i, j: (i, j, 0)),
            pl.BlockSpec((None, 1, D_MODEL), lambda i, j: (i, 0, 0)),
            pl.BlockSpec((None, 1, D_MODEL), lambda i, j: (i, 0, 0)),
            full((D_MODEL, 6 * w)),
            pl.BlockSpec((chunk, w), lambda i, j: (j, 0)),
            pl.BlockSpec((chunk, w), lambda i, j: (j, 0)),
            full((N_HEADS, chunk, chunk)),
            full((chunk, w)),
            full((chunk, w)),
            full((1, w)),
            pl.BlockSpec((None, w, w), lambda i, j: (i, 0, 0)),
        ],
        out_specs=(pl.BlockSpec((None, chunk, w), lambda i, j: (i, j, 0)),
                   pl.BlockSpec((None, w, w), lambda i, j: (i, 0, 0))),
        scratch_shapes=[pltpu.VMEM((w, w), F32)],
        compiler_params=_params(2),
        name="retention_mixer",
    )(x, scale, shift, w_ret, cos, sin, dmask, qdec, kdec, cdec, s0_bd)


def _conv_kernel(x_ref, sc_ref, sh_ref, w_ref, hist_ref, wc_ref, bc_ref, lg_ref, lb_ref,
                 h_ref, tail_ref, xp_scr):
    c = pl.program_id(1)
    rows = x_ref.shape[0]
    w = BRANCH_W
    pad = HIST_ROWS - (CONV_W - 1)

    @pl.when(c == 0)
    def _():
        xp_scr[0:HIST_ROWS, :] = hist_ref[...]

    u = x_ref[...] * (1.0 + sc_ref[...]) + sh_ref[...]
    p = _dot(_bf(u), w_ref[...])
    xp_scr[HIST_ROWS:HIST_ROWS + rows, :] = p[:, :w] * _sigmoid(p[:, w:])
    acc = jnp.zeros((rows, w), F32) + bc_ref[...]
    for j in range(CONV_W):
        acc = acc + xp_scr[pl.ds(pad + j, rows), :] * wc_ref[j:j + 1, :]
    y = _layer_norm_rows(acc, lg_ref[...], lb_ref[...])
    h_ref[...] = _bf(y * _sigmoid(y))
    tail = xp_scr[rows:rows + HIST_ROWS, :]
    tail_ref[...] = tail
    xp_scr[0:HIST_ROWS, :] = tail


def _conv_mixer(x, scale, shift, w_cv, hist, w_conv, b_conv, ln_g, ln_b, chunk):
    b, t, _ = x.shape
    nc = t // chunk
    w = BRANCH_W
    full = lambda shape: pl.BlockSpec(shape, lambda i, j: (0,) * len(shape))
    wc = jnp.concatenate([w_conv, jnp.zeros((HIST_ROWS - CONV_W, w), F32)], axis=0)
    return pl.pallas_call(
        _conv_kernel,
        out_shape=(jax.ShapeDtypeStruct((b, t, w), BF16), jax.ShapeDtypeStruct((b, HIST_ROWS, w), F32)),
        grid=(b, nc),
        in_specs=[
            pl.BlockSpec((None, chunk, D_MODEL), lambda i, j: (i, j, 0)),
            pl.BlockSpec((None, 1, D_MODEL), lambda i, j: (i, 0, 0)),
            pl.BlockSpec((None, 1, D_MODEL), lambda i, j: (i, 0, 0)),
            full((D_MODEL, 2 * w)),
            pl.BlockSpec((None, HIST_ROWS, w), lambda i, j: (i, 0, 0)),
            full((HIST_ROWS, w)),
            full((1, w)),
            full((1, w)),
            full((1, w)),
        ],
        out_specs=(pl.BlockSpec((None, chunk, w), lambda i, j: (i, j, 0)),
                   pl.BlockSpec((None, HIST_ROWS, w), lambda i, j: (i, 0, 0))),
        scratch_shapes=[pltpu.VMEM((HIST_ROWS + chunk, w), F32)],
        compiler_params=_params(2),
        name="conv_mixer",
    )(x, scale, shift, w_cv, hist, wc, b_conv.reshape(1, w), ln_g.reshape(1, w), ln_b.reshape(1, w))


def _fox_proj_kernel(x_ref, sc_ref, sh_ref, w_ref, bf_ref, q_ref, k_ref, v_ref, kb_ref, vb_ref, lf_ref):
    u = x_ref[...] * (1.0 + sc_ref[...]) + sh_ref[...]
    p = _dot(_bf(u), w_ref[...])
    w = BRANCH_W
    q_ref[...] = _bf(p[:, 0:w] * (HEAD_DIM ** -0.5))
    k = p[:, w:2 * w]
    v = p[:, 2 * w:3 * w]
    k_ref[...] = k
    v_ref[...] = v
    kb_ref[...] = _bf(k)
    vb_ref[...] = _bf(v)
    z = p[:, 3 * w:] + bf_ref[...]
    lf = jnp.minimum(z, 0.0) - jnp.log(1.0 + jnp.exp(-jnp.abs(z)))
    lane = lax.broadcasted_iota(jnp.int32, lf.shape, 1)
    lf_ref[...] = jnp.where(lane < N_HEADS, lf, 0.0)


def _fox_proj(x, scale, shift, w_fox, b_f, tm):
    b, t, _ = x.shape
    w = BRANCH_W
    nw = 3 * w + LANES
    bfp = jnp.concatenate([b_f.astype(F32), jnp.zeros((LANES - N_HEADS,), F32)]).reshape(1, LANES)
    row = lambda width: pl.BlockSpec((None, tm, width), lambda i, j: (i, j, 0))
    sds = lambda width, dt: jax.ShapeDtypeStruct((b, t, width), dt)
    return pl.pallas_call(
        _fox_proj_kernel,
        out_shape=(sds(w, BF16), sds(w, F32), sds(w, F32), sds(w, BF16), sds(w, BF16), sds(LANES, F32)),
        grid=(b, t // tm),
        in_specs=[
            row(D_MODEL),
            pl.BlockSpec((None, 1, D_MODEL), lambda i, j: (i, 0, 0)),
            pl.BlockSpec((None, 1, D_MODEL), lambda i, j: (i, 0, 0)),
            pl.BlockSpec((D_MODEL, nw), lambda i, j: (0, 0)),
            pl.BlockSpec((1, LANES), lambda i, j: (0, 0)),
        ],
        out_specs=(row(w), row(w), row(w), row(w), row(w), row(LANES)),
        compiler_params=_params(2),
        name="fox_projection",
    )(x, scale, shift, w_fox, bfp)


def _cumsum_kernel(lf_ref, col_ref, row_ref, carry_scr):
    j = pl.program_id(1)

    @pl.when(j == 0)
    def _():
        carry_scr[...] = jnp.zeros_like(carry_scr)

    x = lf_ref[...]
    n = x.shape[0]
    r = lax.broadcasted_iota(jnp.int32, (n, n), 0)
    c = lax.broadcasted_iota(jnp.int32, (n, n), 1)
    tri = jnp.where(c <= r, 1.0, 0.0).astype(BF16)
    hi = _bf(x)
    r1 = x - hi.astype(F32)
    mid = _bf(r1)
    lo = _bf(r1 - mid.astype(F32))
    cs = _dot(tri, hi) + _dot(tri, mid) + _dot(tri, lo) + carry_scr[0:1, :]
    col_ref[...] = cs
    row_ref[...] = cs.T[0:8, :]
    carry_scr[...] = jnp.broadcast_to(cs[n - 1:n, :], carry_scr.shape)


def _logf_cumsum(lf):
    b, tk, _ = lf.shape
    nb = tk // KV_BLOCK
    return pl.pallas_call(
        _cumsum_kernel,
        out_shape=(jax.ShapeDtypeStruct((b, tk, LANES), F32),
                   jax.ShapeDtypeStruct((b, nb, 8, KV_BLOCK), F32)),
        grid=(b, nb),
        in_specs=[pl.BlockSpec((None, KV_BLOCK, LANES), lambda i, j: (i, j, 0))],
        out_specs=(pl.BlockSpec((None, KV_BLOCK, LANES), lambda i, j: (i, j, 0)),
                   pl.BlockSpec((None, None, 8, KV_BLOCK), lambda i, j: (i, j, 0, 0))),
        scratch_shapes=[pltpu.VMEM((8, LANES), F32)],
        compiler_params=_params(2),
        name="logf_cumsum",
    )(lf)


def _online_block(s, vs, m, l, acc):
    m_new = jnp.maximum(m, jnp.max(s, axis=-1, keepdims=True))
    alpha = jnp.exp(m - m_new)
    p = jnp.exp(s - m_new)
    l = alpha * l + jnp.sum(p, axis=-1, keepdims=True)
    acc = alpha * acc + _dot(_bf(p), vs)
    return m_new, l, acc


def _fox_attn_kernel(q_ref, k_ref, v_ref, fc_ref, fr_ref, o_ref, *, off, tk_valid):
    i = pl.program_id(1)
    tq = q_ref.shape[0]
    w = BRANCH_W
    q = q_ref[...]
    fc = fc_ref[...]
    q0 = off + i * tq
    nfull = q0 // KV_BLOCK
    lane_head = lax.broadcasted_iota(jnp.int32, (1, w), 1) // HEAD_DIM
    qpos = q0 + lax.broadcasted_iota(jnp.int32, (tq, KV_BLOCK), 0)
    kpos = nfull * KV_BLOCK + lax.broadcasted_iota(jnp.int32, (tq, KV_BLOCK), 1)
    vis = (kpos <= qpos) & (kpos < tk_valid)
    out = jnp.zeros((tq, w), F32)
    for h in range(N_HEADS):
        mh = lane_head == h
        qh = jnp.where(mh, q, jnp.zeros_like(q))
        fq = fc[:, h:h + 1]

        def scores(j):
            start = pl.multiple_of(j * KV_BLOCK, KV_BLOCK)
            ks = k_ref[pl.ds(start, KV_BLOCK), :]
            vs = v_ref[pl.ds(start, KV_BLOCK), :]
            fk = fr_ref[j, h:h + 1, :]
            return _dot_nt(qh, ks) + (fq - fk), vs

        def body(j, carry):
            s, vs = scores(j)
            return _online_block(s, vs, *carry)

        init = (jnp.full((tq, 1), NEG, F32), jnp.zeros((tq, 1), F32), jnp.zeros((tq, w), F32))
        m, l, acc = lax.fori_loop(0, nfull, body, init)
        s, vs = scores(nfull)
        m, l, acc = _online_block(jnp.where(vis, s, NEG), vs, m, l, acc)
        out = out + jnp.where(mh, acc / l, 0.0)
    o_ref[...] = _bf(out)


def _fox_attention(q, kb, vb, fcol, frow, tq, off, tk_valid):
    b, t, w = q.shape
    tk = kb.shape[1]
    nb = tk // KV_BLOCK
    kernel = functools.partial(_fox_attn_kernel, off=off, tk_valid=tk_valid)
    return pl.pallas_call(
        kernel,
        out_shape=jax.ShapeDtypeStruct((b, t, w), BF16),
        grid=(b, t // tq),
        in_specs=[
            pl.BlockSpec((None, tq, w), lambda i, j: (i, j, 0)),
            pl.BlockSpec((None, tk, w), lambda i, j: (i, 0, 0)),
            pl.BlockSpec((None, tk, w), lambda i, j: (i, 0, 0)),
            pl.BlockSpec((None, tq, LANES), lambda i, j: (i, off // tq + j, 0)),
            pl.BlockSpec((None, nb, 8, KV_BLOCK), lambda i, j: (i, 0, 0, 0)),
        ],
        out_specs=pl.BlockSpec((None, tq, w), lambda i, j: (i, j, 0)),
        compiler_params=_params(2),
        name="fox_attention",
    )(q, kb, vb, fcol, frow)


def _diff_proj_kernel(x_ref, sc_ref, sh_ref, w_ref, cos_ref, sin_ref, q_ref, k_ref, v_ref, kb_ref, vb_ref):
    u = x_ref[...] * (1.0 + sc_ref[...]) + sh_ref[...]
    p = _dot(_bf(u), w_ref[...])
    w = BRANCH_W
    cos = cos_ref[...]
    sin = sin_ref[...]
    q_ref[...] = _bf(p[:, 0:w] * cos + p[:, w:2 * w] * sin)
    k = p[:, 2 * w:3 * w] * cos + p[:, 3 * w:4 * w] * sin
    v = p[:, 4 * w:5 * w]
    k_ref[...] = k
    v_ref[...] = v
    kb_ref[...] = _bf(k)
    vb_ref[...] = _bf(v)


def _diff_proj(x, scale, shift, w_diff, pos, tm):
    b, t, _ = x.shape
    w = BRANCH_W
    cos, sin = _rope_tables(pos, DIFF_SUB, ROT_DIM, ROPE_THETA, N_SUB)
    row = lambda width: pl.BlockSpec((None, tm, width), lambda i, j: (i, j, 0))
    sds = lambda dt: jax.ShapeDtypeStruct((b, t, w), dt)
    return pl.pallas_call(
        _diff_proj_kernel,
        out_shape=(sds(BF16), sds(F32), sds(F32), sds(BF16), sds(BF16)),
        grid=(b, t // tm),
        in_specs=[
            row(D_MODEL),
            pl.BlockSpec((None, 1, D_MODEL), lambda i, j: (i, 0, 0)),
            pl.BlockSpec((None, 1, D_MODEL), lambda i, j: (i, 0, 0)),
            pl.BlockSpec((D_MODEL, 5 * w), lambda i, j: (0, 0)),
            pl.BlockSpec((tm, w), lambda i, j: (j, 0)),
            pl.BlockSpec((tm, w), lambda i, j: (j, 0)),
        ],
        out_specs=(row(w), row(w), row(w), row(w), row(w)),
        compiler_params=_params(2),
        name="diff_projection",
    )(x, scale, shift, w_diff, cos, sin)


def _diff_attn_kernel(q_ref, k_ref, v_ref, lam_ref, g_ref, o_ref, *, off, tk_valid, lam_init):
    i = pl.program_id(1)
    tq = q_ref.shape[0]
    w = BRANCH_W
    q = q_ref[...]
    q0 = off + i * tq
    nfull = q0 // KV_BLOCK
    lane = lax.broadcasted_iota(jnp.int32, (1, w), 1)
    qpos = q0 + lax.broadcasted_iota(jnp.int32, (tq, KV_BLOCK), 0)
    kpos = nfull * KV_BLOCK + lax.broadcasted_iota(jnp.int32, (tq, KV_BLOCK), 1)
    vis = ((kpos // CHUNK) <= (qpos // CHUNK)) & (kpos < tk_valid)
    lamv = lam_ref[...]
    lam = (jnp.exp(jnp.sum(lamv[0:1] * lamv[1:2], axis=-1, keepdims=True))
           - jnp.exp(jnp.sum(lamv[2:3] * lamv[3:4], axis=-1, keepdims=True)) + lam_init)
    scale = DIFF_SUB ** -0.5
    out = jnp.zeros((tq, w), F32)
    for n in range(N_SUB):
        mn = (lane // DIFF_SUB) == n
        qn = jnp.where(mn, q, jnp.zeros_like(q))

        def scores(j):
            start = pl.multiple_of(j * KV_BLOCK, KV_BLOCK)
            ks = k_ref[pl.ds(start, KV_BLOCK), :]
            vs = v_ref[pl.ds(start, KV_BLOCK), :]
            return _dot_nt(qn, ks) * scale, vs

        def body(j, carry):
            s, vs = scores(j)
            return _online_block(s, vs, *carry)

        init = (jnp.full((tq, 1), NEG, F32), jnp.zeros((tq, 1), F32), jnp.zeros((tq, w), F32))
        m, l, acc = lax.fori_loop(0, nfull, body, init)
        s, vs = scores(nfull)
        m, l, acc = _online_block(jnp.where(vis, s, NEG), vs, m, l, acc)
        o = acc / l
        coef = 1.0 if n % 2 == 0 else -lam
        out = out + jnp.where((lane // HEAD_DIM) == (n // 2), coef * o, 0.0)
    ms = _head_sum(out * out, HEAD_DIM) * (1.0 / HEAD_DIM)
    o_ref[...] = _bf(out * lax.rsqrt(ms + EPS) * g_ref[...] * (1.0 - lam_init))


def _diff_attention(q, kb, vb, diff_lambda, subln_g, tq, off, tk_valid, lam_init):
    b, t, w = q.shape
    tk = kb.shape[1]
    kernel = functools.partial(_diff_attn_kernel, off=off, tk_valid=tk_valid, lam_init=lam_init)
    g = jnp.tile(subln_g.astype(F32), N_HEADS).reshape(1, w)
    return pl.pallas_call(
        kernel,
        out_shape=jax.ShapeDtypeStruct((b, t, w), BF16),
        grid=(b, t // tq),
        in_specs=[
            pl.BlockSpec((None, tq, w), lambda i, j: (i, j, 0)),
            pl.BlockSpec((None, tk, w), lambda i, j: (i, 0, 0)),
            pl.BlockSpec((None, tk, w), lambda i, j: (i, 0, 0)),
            pl.BlockSpec((4, DIFF_SUB), lambda i, j: (0, 0)),
            pl.BlockSpec((1, w), lambda i, j: (0, 0)),
        ],
        out_specs=pl.BlockSpec((None, tq, w), lambda i, j: (i, j, 0)),
        compiler_params=_params(2),
        name="diff_attention",
    )(q, kb, vb, diff_lambda.astype(F32), g)


def _top2_combine(logits):
    lane = lax.broadcasted_iota(jnp.int32, logits.shape, 1).astype(F32)
    lg = jnp.where(lane < N_EXPERTS, logits, -jnp.inf)
    v1 = jnp.max(lg, axis=-1, keepdims=True)
    i1 = jnp.min(jnp.where(lg == v1, lane, float(LANES)), axis=-1, keepdims=True)
    lg2 = jnp.where(lane == i1, -jnp.inf, lg)
    v2 = jnp.max(lg2, axis=-1, keepdims=True)
    i2 = jnp.min(jnp.where(lg2 == v2, lane, float(LANES)), axis=-1, keepdims=True)
    e2 = jnp.exp(v2 - v1)
    w1 = 1.0 / (1.0 + e2)
    w2 = e2 / (1.0 + e2)
    return jnp.where(lane == i1, w1, 0.0) + jnp.where(lane == i2, w2, 0.0)


def _merge_kernel(*refs, with_router):
    if with_router:
        (x_ref, sc1_ref, sh1_ref, g1_ref, sc2_ref, sh2_ref, hr_ref, hf_ref, hc_ref, hd_ref,
         wg_ref, wb_ref, wo_ref, lg_ref, lb_ref, wr_ref, br_ref, x1_ref, u2_ref, cmb_ref) = refs
    else:
        (x_ref, sc1_ref, sh1_ref, g1_ref, sc2_ref, sh2_ref, hr_ref, hf_ref, hc_ref, hd_ref,
         wg_ref, wb_ref, wo_ref, lg_ref, lb_ref, x1_ref, u2_ref) = refs
    x = x_ref[...]
    u = _bf(x * (1.0 + sc1_ref[...]) + sh1_ref[...])
    merged = None
    for n, h_ref in enumerate((hr_ref, hf_ref, hc_ref, hd_ref)):
        gate = _dot(u, wg_ref[:, n * D_MODEL:(n + 1) * D_MODEL])
        term = _sigmoid(gate) * _dot(h_ref[...], wb_ref[n])
        merged = term if merged is None else merged + term
    mix = _dot(_bf(merged), wo_ref[...])
    x1 = _layer_norm_rows(ALPHA * x + g1_ref[...] * mix, lg_ref[...], lb_ref[...])
    x1_ref[...] = x1
    u2 = _bf(x1 * (1.0 + sc2_ref[...]) + sh2_ref[...])
    u2_ref[...] = u2
    if with_router:
        cmb_ref[...] = _top2_combine(_dot(u2, wr_ref[...]) + br_ref[...])


def _merge(x2d, mods, branches, w_gate, w_branch, w_out, ln_g, ln_b, router, tm, tiles_per_group):
    n = x2d.shape[0]
    r = mods[0].shape[1]
    w = BRANCH_W
    with_router = router is not None
    row = lambda width: pl.BlockSpec((tm, width), lambda i: (i, 0))
    mod_spec = pl.BlockSpec((None, r, D_MODEL), lambda i: (i // tiles_per_group, 0, 0))
    full = lambda shape: pl.BlockSpec(shape, lambda i: (0,) * len(shape))
    in_specs = ([row(D_MODEL)] + [mod_spec] * 5 + [row(w)] * 4
                + [full((D_MODEL, 4 * D_MODEL)), full((4, w, D_MODEL)), full((D_MODEL, D_MODEL)),
                   full((1, D_MODEL)), full((1, D_MODEL))])
    args = [x2d, *mods, *branches, w_gate, w_branch, w_out, ln_g.reshape(1, D_MODEL), ln_b.reshape(1, D_MODEL)]
    out_shape = [jax.ShapeDtypeStruct((n, D_MODEL), F32), jax.ShapeDtypeStruct((n, D_MODEL), BF16)]
    out_specs = [row(D_MODEL), row(D_MODEL)]
    if with_router:
        in_specs += [full((D_MODEL, LANES)), full((1, LANES))]
        args += list(router)
        out_shape.append(jax.ShapeDtypeStruct((n, LANES), F32))
        out_specs.append(row(LANES))
    return pl.pallas_call(
        functools.partial(_merge_kernel, with_router=with_router),
        out_shape=tuple(out_shape),
        grid=(n // tm,),
        in_specs=in_specs,
        out_specs=tuple(out_specs),
        compiler_params=_params(1),
        name="merge_outproj_ln",
    )(*args)


def _ffn_kernel(u_ref, x1_ref, g2_ref, wa_ref, wg_ref, wd_ref, lg_ref, lb_ref, o_ref, acc_scr):
    j = pl.program_id(1)

    @pl.when(j == 0)
    def _():
        acc_scr[...] = jnp.zeros_like(acc_scr)

    u = u_ref[...]
    a = _dot(u, wa_ref[...])
    g = _dot(u, wg_ref[...])
    acc_scr[...] += _dot(_bf(a * _sigmoid(a) * g), wd_ref[...])

    @pl.when(j == pl.num_programs(1) - 1)
    def _():
        z = ALPHA * x1_ref[...] + g2_ref[...] * acc_scr[...]
        o_ref[...] = _layer_norm_rows(z, lg_ref[...], lb_ref[...])


def _ffn_dense(u2, x1, gate2, w_up, w_down, ln_g, ln_b, tm, tiles_per_group, th):
    n = u2.shape[0]
    r = gate2.shape[1]
    nh = D_FF // th
    return pl.pallas_call(
        _ffn_kernel,
        out_shape=jax.ShapeDtypeStruct((n, D_MODEL), F32),
        grid=(n // tm, nh),
        in_specs=[
            pl.BlockSpec((tm, D_MODEL), lambda i, j: (i, 0)),
            pl.BlockSpec((tm, D_MODEL), lambda i, j: (i, 0)),
            pl.BlockSpec((None, r, D_MODEL), lambda i, j: (i // tiles_per_group, 0, 0)),
            pl.BlockSpec((D_MODEL, th), lambda i, j: (0, j)),
            pl.BlockSpec((D_MODEL, th), lambda i, j: (0, nh + j)),
            pl.BlockSpec((th, D_MODEL), lambda i, j: (j, 0)),
            pl.BlockSpec((1, D_MODEL), lambda i, j: (0, 0)),
            pl.BlockSpec((1, D_MODEL), lambda i, j: (0, 0)),
        ],
        out_specs=pl.BlockSpec((tm, D_MODEL), lambda i, j: (i, 0)),
        scratch_shapes=[pltpu.VMEM((tm, D_MODEL), F32)],
        compiler_params=_params(2),
        name="ffn_dense",
    )(u2, x1, gate2, w_up, w_up, w_down, ln_g.reshape(1, D_MODEL), ln_b.reshape(1, D_MODEL))


def _moe_kernel(u_ref, x1_ref, g2_ref, cmb_ref, wa_ref, wg_ref, wd_ref, lg_ref, lb_ref, o_ref, acc_scr):
    e = pl.program_id(1)
    j = pl.program_id(2)

    @pl.when((e == 0) & (j == 0))
    def _():
        acc_scr[...] = jnp.zeros_like(acc_scr)

    u = u_ref[...]
    a = _dot(u, wa_ref[...])
    g = _dot(u, wg_ref[...])
    y = _dot(_bf(a * _sigmoid(a) * g), wd_ref[...])
    cmb = cmb_ref[...]
    lane = lax.broadcasted_iota(jnp.int32, cmb.shape, 1)
    ce = jnp.sum(jnp.where(lane == e, cmb, 0.0), axis=-1, keepdims=True)
    acc_scr[...] += ce * y

    @pl.when((e == pl.num_programs(1) - 1) & (j == pl.num_programs(2) - 1))
    def _():
        z = ALPHA * x1_ref[...] + g2_ref[...] * acc_scr[...]
        o_ref[...] = _layer_norm_rows(z, lg_ref[...], lb_ref[...])


def _ffn_experts(u2, x1, gate2, combine, w_in, w_out, ln_g, ln_b, tm, tiles_per_group, th):
    n = u2.shape[0]
    r = gate2.shape[1]
    nh = D_EXPERT // th
    return pl.pallas_call(
        _moe_kernel,
        out_shape=jax.ShapeDtypeStruct((n, D_MODEL), F32),
        grid=(n // tm, N_EXPERTS, nh),
        in_specs=[
            pl.BlockSpec((tm, D_MODEL), lambda i, e, j: (i, 0)),
            pl.BlockSpec((tm, D_MODEL), lambda i, e, j: (i, 0)),
            pl.BlockSpec((None, r, D_MODEL), lambda i, e, j: (i // tiles_per_group, 0, 0)),
            pl.BlockSpec((tm, LANES), lambda i, e, j: (i, 0)),
            pl.BlockSpec((None, D_MODEL, th), lambda i, e, j: (e, 0, j)),
            pl.BlockSpec((None, D_MODEL, th), lambda i, e, j: (e, 0, nh + j)),
            pl.BlockSpec((None, th, D_MODEL), lambda i, e, j: (e, j, 0)),
            pl.BlockSpec((1, D_MODEL), lambda i, e, j: (0, 0)),
            pl.BlockSpec((1, D_MODEL), lambda i, e, j: (0, 0)),
        ],
        out_specs=pl.BlockSpec((tm, D_MODEL), lambda i, e, j: (i, 0)),
        scratch_shapes=[pltpu.VMEM((tm, D_MODEL), F32)],
        compiler_params=_params(3),
        name="ffn_experts",
    )(u2, x1, gate2, combine, w_in, w_in, w_out, ln_g.reshape(1, D_MODEL), ln_b.reshape(1, D_MODEL))


def _split_w_in(w_in_l):
    w = BRANCH_W
    col = lambda off, n: w_in_l[:, off:off + n]
    o = 0
    ret_q, ret_k, ret_v, ret_g = (col(o + i * w, w) for i in range(4)); o += 4 * w
    fox_q, fox_k, fox_v = (col(o + i * w, w) for i in range(3)); o += 3 * w
    fox_f = col(o, N_HEADS); o += N_HEADS
    conv = col(o, 2 * w); o += 2 * w
    diff_q, diff_k, diff_v = (col(o + i * w, w) for i in range(3)); o += 3 * w
    gate = col(o, 4 * D_MODEL)
    rot_r = lambda m: _rotated_columns(m, HEAD_DIM, HEAD_DIM)
    rot_d = lambda m: _rotated_columns(m, DIFF_SUB, ROT_DIM)
    w_ret = jnp.concatenate([ret_q, rot_r(ret_q), ret_k, rot_r(ret_k), ret_v, ret_g], axis=1)
    w_fox = jnp.concatenate([fox_q, fox_k, fox_v, fox_f, jnp.zeros((D_MODEL, LANES - N_HEADS), F32)], axis=1)
    w_diff = jnp.concatenate([diff_q, rot_d(diff_q), diff_k, rot_d(diff_k), diff_v], axis=1)
    return _bf(w_ret), _bf(w_fox), _bf(conv), _bf(w_diff), _bf(gate)


def _block_diag_state(s):
    b = s.shape[0]
    eye = jnp.eye(N_HEADS, dtype=s.dtype)
    return jnp.einsum('bhde,hg->bhdge', s, eye).reshape(b, BRANCH_W, BRANCH_W)


def _diag_blocks(s_bd):
    b = s_bd.shape[0]
    s4 = s_bd.reshape(b, N_HEADS, HEAD_DIM, N_HEADS, HEAD_DIM)
    return jnp.stack([s4[:, h, :, h, :] for h in range(N_HEADS)], axis=1)


def _pad_rows(a, rows):
    return jnp.concatenate([a, jnp.zeros((a.shape[0], rows - a.shape[1]) + a.shape[2:], a.dtype)], axis=1)


def _trunk_layer(x, mod, pos, past, l, lw, sample):
    b, t, _ = x.shape
    d = D_MODEL
    w = BRANCH_W
    shift1, scale1, gate1 = (mod[0][:, None, i * d:(i + 1) * d] for i in range(3))
    shift2, scale2, gate2 = (mod[1][:, None, i * d:(i + 1) * d] for i in range(3))
    lam_init = 0.8 - 0.6 * math.exp(-0.3 * l)
    chunk = min(t, KV_BLOCK)
    tm_proj = min(t, 1024)

    s0 = jnp.zeros((b, w, w), F32) if past is None else _block_diag_state(past[5])
    h_ret, s_bd = _retention(x, scale1, shift1, lw['w_ret'], pos, s0, chunk)
    ret_state = _diag_blocks(s_bd)

    if past is None:
        hist = jnp.zeros((b, HIST_ROWS, w), F32)
    else:
        hist = jnp.concatenate([jnp.zeros((b, HIST_ROWS - (CONV_W - 1), w), F32), past[6]], axis=1)
    h_conv, tail = _conv_mixer(x, scale1, shift1, lw['w_conv_in'], hist, lw['w_conv'], lw['b_conv'],
                               lw['conv_ln_g'], lw['conv_ln_b'], chunk)
    conv_buf = tail[:, HIST_ROWS - (CONV_W - 1):, :]

    fq, fk, fv, fkb, fvb, lf = _fox_proj(x, scale1, shift1, lw['w_fox'], lw['b_fox_f'], tm_proj)
    if past is None:
        off, tk_valid = 0, t
        k_all, v_all, lf_all = fkb, fvb, lf
    else:
        p_len = past[0].shape[1]
        off, tk_valid = p_len, p_len + t
        tk_pad = p_len + KV_BLOCK
        k_all = _pad_rows(jnp.concatenate([_bf(past[0].reshape(b, p_len, w)), fkb], axis=1), tk_pad)
        v_all = _pad_rows(jnp.concatenate([_bf(past[1].reshape(b, p_len, w)), fvb], axis=1), tk_pad)
        past_lf = jnp.concatenate([past[2].astype(F32), jnp.zeros((b, p_len, LANES - N_HEADS), F32)], axis=2)
        lf_all = _pad_rows(jnp.concatenate([past_lf, lf], axis=1), tk_pad)
    fcol, frow = _logf_cumsum(lf_all)
    tq = min(t, KV_BLOCK)
    h_fox = _fox_attention(fq, k_all, v_all, fcol, frow, tq, off, tk_valid)

    dq, dk, dv, dkb, dvb = _diff_proj(x, scale1, shift1, lw['w_diff'], pos, tm_proj)
    if past is None:
        dk_all, dv_all = dkb, dvb
    else:
        dk_all = _pad_rows(jnp.concatenate([_bf(past[3].reshape(b, p_len, w)), dkb], axis=1), tk_pad)
        dv_all = _pad_rows(jnp.concatenate([_bf(past[4].reshape(b, p_len, w)), dvb], axis=1), tk_pad)
    h_diff = _diff_attention(dq, dk_all, dv_all, lw['diff_lambda'], lw['diff_subln_g'], tq, off, tk_valid, lam_init)

    n = b * t
    if sample:
        tm_merge = tm_ffn = min(n, 256)
        tpg_merge = tpg_ffn = 1
        rows_mod = lambda m, tm: jnp.repeat(m, t, axis=1).reshape(n // tm, tm, d)
    else:
        tm_merge, tm_ffn = min(t, 512), min(t, 1024)
        tpg_merge, tpg_ffn = t // tm_merge, t // tm_ffn
        rows_mod = lambda m, tm: m
    mods = tuple(rows_mod(m, tm_merge) for m in (scale1, shift1, gate1, scale2, shift2))
    branches = tuple(h.reshape(n, w) for h in (h_ret, h_fox, h_conv, h_diff))
    merged = _merge(x.reshape(n, d), mods, branches, lw['w_gate'], lw['w_branch'], lw['w_out'],
                    lw['ln_g'][0], lw['ln_b'][0], lw.get('router'), tm_merge, tpg_merge)
    g2 = rows_mod(gate2, tm_ffn)
    if 'router' in lw:
        x1, u2, combine = merged
        x2 = _ffn_experts(u2, x1, g2, combine, lw['w_exp_in'], lw['w_exp_out'], lw['ln_g'][1], lw['ln_b'][1],
                          tm_ffn, tpg_ffn, 512)
    else:
        x1, u2 = merged
        x2 = _ffn_dense(u2, x1, g2, lw['w_ffn_in'], lw['w_ffn_out'], lw['ln_g'][1], lw['ln_b'][1],
                        tm_ffn, tpg_ffn, 1408)
    heads = lambda a, nh: a.reshape(b, t, nh, w // nh)
    new_state = (heads(fk, N_HEADS), heads(fv, N_HEADS), lf[:, :, :N_HEADS], heads(dk, N_SUB), heads(dv, N_HEADS),
                 ret_state, conv_buf)
    return x2.reshape(b, t, d), new_state


def kernel(x_prompt, x_sample, c_prompt, c_sample, cache_fox_k, cache_fox_v, cache_fox_logf, cache_diff_k, cache_diff_v, state_ret, state_conv, w_in, b_fox_f, w_conv, b_conv, conv_ln_g, conv_ln_b, diff_lambda, diff_subln_g, w_branch, w_out, w_ada, b_ada, ln_g, ln_b, w_ffn_in, w_ffn_out, w_router, b_router, w_exp_in, w_exp_out):
    depth = w_in.shape[0]
    bp = x_prompt.shape[0]
    past_len = cache_fox_k.shape[2]
    pos_p = jnp.arange(x_prompt.shape[1], dtype=jnp.int32)
    pos_s = past_len + jnp.arange(x_sample.shape[1], dtype=jnp.int32)
    mod = _modulation(jnp.concatenate([c_prompt, c_sample], axis=0), w_ada, b_ada)
    yp, ys = x_prompt, x_sample
    new_p, new_s = [], []
    for l in range(depth):
        w_ret, w_fox, w_conv_in, w_diff, w_gate = _split_w_in(w_in[l])
        lw = dict(w_ret=w_ret, w_fox=w_fox, w_conv_in=w_conv_in, w_diff=w_diff, w_gate=w_gate,
                  b_fox_f=b_fox_f[l], w_conv=w_conv[l], b_conv=b_conv[l], conv_ln_g=conv_ln_g[l],
                  conv_ln_b=conv_ln_b[l], diff_lambda=diff_lambda[l], diff_subln_g=diff_subln_g[l],
                  w_branch=_bf(w_branch[l]), w_out=_bf(w_out[l]), ln_g=ln_g[l], ln_b=ln_b[l])
        if l % 2 == 0:
            lw['w_ffn_in'] = _bf(w_ffn_in[l // 2])
            lw['w_ffn_out'] = _bf(w_ffn_out[l // 2])
        else:
            wr = jnp.concatenate([w_router[l // 2], jnp.zeros((D_MODEL, LANES - N_EXPERTS), F32)], axis=1)
            br = jnp.concatenate([b_router[l // 2].astype(F32), jnp.zeros((LANES - N_EXPERTS,), F32)]).reshape(1, LANES)
            lw['router'] = (_bf(wr), br)
            lw['w_exp_in'] = _bf(w_exp_in[l // 2])
            lw['w_exp_out'] = _bf(w_exp_out[l // 2])
        past_l = (cache_fox_k[l], cache_fox_v[l], cache_fox_logf[l], cache_diff_k[l], cache_diff_v[l],
                  state_ret[l], state_conv[l])
        yp, st_p = _trunk_layer(yp, mod[l][:, :bp], pos_p, None, l, lw, sample=False)
        ys, st_s = _trunk_layer(ys, mod[l][:, bp:], pos_s, past_l, l, lw, sample=True)
        new_p.append(st_p)
        new_s.append(st_s)
    outs_p = tuple(jnp.stack(a) for a in zip(*new_p))
    outs_s = tuple(jnp.stack(a) for a in zip(*new_s))
    return (yp, ys) + outs_p + outs_s
```

```python
import functools
import math

import jax
import jax.numpy as jnp
from jax import lax
from jax.experimental import pallas as pl
from jax.experimental.pallas import tpu as pltpu

D_MODEL = 1024
BRANCH_W = 256
HEAD_DIM = 64
N_HEADS = 4
DIFF_SUB = 32
N_SUB = 8
ROT_DIM = DIFF_SUB // 4
RET_THETA = 10000.0
ROPE_THETA = 500000.0
CHUNK = 64
CONV_W = 31
D_FF = 2816
N_EXPERTS = 8
D_EXPERT = 3584
DEPTH = 2
ALPHA = (2.0 * DEPTH) ** 0.25
EPS = 1e-5
NEG = -1e30
LOG2E = math.log2(math.e)

LANES = 128
BF16_ROWS = 16
KV_BLOCK = 256
HIST_ROWS = 32
VMEM_LIMIT = 56 * 1024 * 1024

F32 = jnp.float32
BF16 = jnp.bfloat16


def _bf(x):
    return x.astype(BF16)


def _dot(a, b):
    return jnp.dot(a, b, preferred_element_type=F32)


def _dot_nt(a, b):
    return lax.dot_general(a, b, (((1,), (1,)), ((), ())), preferred_element_type=F32)


def _dot_tn(a, b):
    return lax.dot_general(a, b, (((0,), (0,)), ((), ())), preferred_element_type=F32)


def _sigmoid(x):
    return 1.0 / (1.0 + jnp.exp(-x))


def _params(n_axes):
    return pltpu.CompilerParams(dimension_semantics=("arbitrary",) * n_axes,
                                vmem_limit_bytes=VMEM_LIMIT)


def _head_sum(y, width):
    n = y.shape[-1]
    r = lax.broadcasted_iota(jnp.int32, (n, n), 0) // width
    c = lax.broadcasted_iota(jnp.int32, (n, n), 1) // width
    bd = jnp.where(r == c, 1.0, 0.0).astype(BF16)
    hi = _bf(y)
    lo = _bf(y - hi.astype(F32))
    return _dot(hi, bd) + _dot(lo, bd)


def _layer_norm_rows(z, g, b):
    mu = jnp.mean(z, axis=-1, keepdims=True)
    d = z - mu
    var = jnp.mean(d * d, axis=-1, keepdims=True)
    return d * lax.rsqrt(var + EPS) * g + b


def _mod_kernel(c_ref, w_ref, b_ref, o_ref):
    c = c_ref[...]
    sc = _bf(c * _sigmoid(c))
    o_ref[...] = _dot(sc, _bf(w_ref[...])) + b_ref[...]


def _modulation(c_all, w_ada, b_ada):
    rows = c_all.shape[0]
    depth = w_ada.shape[0]
    d3 = w_ada.shape[-1]
    nj = d3 // D_MODEL
    return pl.pallas_call(
        _mod_kernel,
        out_shape=jax.ShapeDtypeStruct((depth, 2, rows, d3), F32),
        grid=(depth * 2, nj),
        in_specs=[
            pl.BlockSpec((rows, D_MODEL), lambda i, j: (0, 0)),
            pl.BlockSpec((None, None, D_MODEL, D_MODEL), lambda i, j: (i // 2, i % 2, 0, j)),
            pl.BlockSpec((None, None, 1, D_MODEL), lambda i, j: (i // 2, i % 2, 0, j)),
        ],
        out_specs=pl.BlockSpec((None, None, rows, D_MODEL), lambda i, j: (i // 2, i % 2, 0, j)),
        compiler_params=_params(2),
        name="adaln_modulation",
    )(c_all, w_ada, b_ada.reshape(depth, 2, 1, d3))


def _ret_kernel(x_ref, sc_ref, sh_ref, w_ref, cos_ref, sin_ref, dmask_ref, qdec_ref, kdec_ref,
                cdec_ref, s0_ref, h_ref, sout_ref, s_scr):
    c = pl.program_id(1)

    @pl.when(c == 0)
    def _():
        s_scr[...] = s0_ref[...]

    u = x_ref[...] * (1.0 + sc_ref[...]) + sh_ref[...]
    p = _dot(_bf(u), w_ref[...])
    w = BRANCH_W
    cos = cos_ref[...]
    sin = sin_ref[...]
    q = p[:, 0:w] * cos + p[:, w:2 * w] * sin
    k = (p[:, 2 * w:3 * w] * cos + p[:, 3 * w:4 * w] * sin) * (HEAD_DIM ** -0.5)
    v = p[:, 4 * w:5 * w]
    g = p[:, 5 * w:6 * w]
    rows = q.shape[0]
    lane_head = lax.broadcasted_iota(jnp.int32, (1, w), 1) // HEAD_DIM
    kb = _bf(k)
    vb = _bf(v)
    y = jnp.zeros((rows, w), F32)
    for h in range(N_HEADS):
        mh = lane_head == h
        qh = _bf(jnp.where(mh, q, 0.0))
        a = _dot_nt(qh, kb) * dmask_ref[h]
        y = y + jnp.where(mh, _dot(_bf(a), vb), 0.0)
    s_prev = s_scr[...]
    y = y + _dot(_bf(q * qdec_ref[...]), _bf(s_prev))
    kv = _dot_tn(_bf(k * kdec_ref[...]), vb)
    r = lax.broadcasted_iota(jnp.int32, (w, w), 0) // HEAD_DIM
    cc = lax.broadcasted_iota(jnp.int32, (w, w), 1) // HEAD_DIM
    s_new = cdec_ref[...] * s_prev + jnp.where(r == cc, kv, 0.0)
    s_scr[...] = s_new
    sout_ref[...] = s_new
    mu = _head_sum(y, HEAD_DIM) * (1.0 / HEAD_DIM)
    d = y - mu
    var = _head_sum(d * d, HEAD_DIM) * (1.0 / HEAD_DIM)
    hn = d * lax.rsqrt(var + EPS)
    h_ref[...] = _bf(hn * (g * _sigmoid(g)))


def _retention_tables(chunk):
    log_g = jnp.log1p(-jnp.exp2(-5.0 - jnp.arange(N_HEADS, dtype=F32)))
    idx = jnp.arange(chunk, dtype=F32)
    dist = jnp.abs(idx[:, None] - idx[None, :])
    sub = jnp.arange(chunk) // CHUNK
    vis = sub[None, :] <= sub[:, None]
    dmask = jnp.where(vis[None], jnp.exp(log_g[:, None, None] * dist[None]), 0.0)
    lg_lane = jnp.repeat(log_g, HEAD_DIM)[None, :]
    qdec = jnp.exp(lg_lane * (idx[:, None] + 1.0))
    kdec = jnp.exp(lg_lane * (chunk - 1.0 - idx[:, None]))
    cdec = jnp.exp(lg_lane * chunk)
    return dmask.astype(F32), qdec, kdec, cdec


def _rope_tables(pos, dim, rot_dim, theta, n_rep):
    half = rot_dim // 2
    inv_freq = jnp.exp(-math.log(theta) * jnp.arange(half, dtype=F32) / half)
    ang = pos.astype(F32)[:, None] * inv_freq[None, :]
    t = pos.shape[0]
    cos = jnp.concatenate([jnp.cos(ang), jnp.cos(ang), jnp.ones((t, dim - rot_dim), F32)], axis=1)
    sin = jnp.concatenate([jnp.sin(ang), jnp.sin(ang), jnp.zeros((t, dim - rot_dim), F32)], axis=1)
    return jnp.tile(cos, (1, n_rep)), jnp.tile(sin, (1, n_rep))


def _rotated_columns(w, dim, rot_dim):
    half = rot_dim // 2
    k, n = w.shape
    wh = w.reshape(k, n // dim, dim)
    rot = jnp.concatenate([-wh[..., half:rot_dim], wh[..., :half], jnp.zeros_like(wh[..., rot_dim:])], axis=-1)
    return rot.reshape(k, n)


def _retention(x, scale, shift, w_ret, pos, s0_bd, chunk):
    b, t, _ = x.shape
    nc = t // chunk
    cos, sin = _rope_tables(pos, HEAD_DIM, HEAD_DIM, RET_THETA, N_HEADS)
    dmask, qdec, kdec, cdec = _retention_tables(chunk)
    w = BRANCH_W
    full = lambda shape: pl.BlockSpec(shape, lambda i, j: (0,) * len(shape))
    return pl.pallas_call(
        _ret_kernel,
        out_shape=(jax.ShapeDtypeStruct((b, t, w), BF16), jax.ShapeDtypeStruct((b, w, w), F32)),
        grid=(b, nc),
        in_specs=[
            pl.BlockSpec((None, chunk, D_MODEL), lambda i, j: (i, j, 0)),
            pl.BlockSpec((None, 1, D_MODEL), lambda i, j: (i, 0, 0)),
            pl.BlockSpec((None, 1, D_MODEL), lambda i, j: (i, 0, 0)),
            full((D_MODEL, 6 * w)),
            pl.BlockSpec((chunk, w), lambda i, j: (j, 0)),
            pl.BlockSpec((chunk, w), lambda i, j: (j, 0)),
            full((N_HEADS, chunk, chunk)),
            full((chunk, w)),
            full((chunk, w)),
            full((1, w)),
            pl.BlockSpec((None, w, w), lambda i, j: (i, 0, 0)),
        ],
        out_specs=(pl.BlockSpec((None, chunk, w), lambda i, j: (i, j, 0)),
                   pl.BlockSpec((None, w, w), lambda i, j: (i, 0, 0))),
        scratch_shapes=[pltpu.VMEM((w, w), F32)],
        compiler_params=_params(2),
        name="retention_mixer",
    )(x, scale, shift, w_ret, cos, sin, dmask, qdec, kdec, cdec, s0_bd)


def _conv_kernel(x_ref, sc_ref, sh_ref, w_ref, hist_ref, wc_ref, bc_ref, lg_ref, lb_ref,
                 h_ref, tail_ref, xp_scr):
    c = pl.program_id(1)
    rows = x_ref.shape[0]
    w = BRANCH_W
    pad = HIST_ROWS - (CONV_W - 1)

    @pl.when(c == 0)
    def _():
        xp_scr[0:HIST_ROWS, :] = hist_ref[...]

    u = x_ref[...] * (1.0 + sc_ref[...]) + sh_ref[...]
    p = _dot(_bf(u), w_ref[...])
    xp_scr[HIST_ROWS:HIST_ROWS + rows, :] = p[:, :w] * _sigmoid(p[:, w:])
    acc = jnp.zeros((rows, w), F32) + bc_ref[...]
    for j in range(CONV_W):
        acc = acc + xp_scr[pl.ds(pad + j, rows), :] * wc_ref[j:j + 1, :]
    y = _layer_norm_rows(acc, lg_ref[...], lb_ref[...])
    h_ref[...] = _bf(y * _sigmoid(y))
    tail = xp_scr[rows:rows + HIST_ROWS, :]
    tail_ref[...] = tail
    xp_scr[0:HIST_ROWS, :] = tail


def _conv_mixer(x, scale, shift, w_cv, hist, w_conv, b_conv, ln_g, ln_b, chunk):
    b, t, _ = x.shape
    nc = t // chunk
    w = BRANCH_W
    full = lambda shape: pl.BlockSpec(shape, lambda i, j: (0,) * len(shape))
    wc = jnp.concatenate([w_conv, jnp.zeros((HIST_ROWS - CONV_W, w), F32)], axis=0)
    return pl.pallas_call(
        _conv_kernel,
        out_shape=(jax.ShapeDtypeStruct((b, t, w), BF16), jax.ShapeDtypeStruct((b, HIST_ROWS, w), F32)),
        grid=(b, nc),
        in_specs=[
            pl.BlockSpec((None, chunk, D_MODEL), lambda i, j: (i, j, 0)),
            pl.BlockSpec((None, 1, D_MODEL), lambda i, j: (i, 0, 0)),
            pl.BlockSpec((None, 1, D_MODEL), lambda i, j: (i, 0, 0)),
            full((D_MODEL, 2 * w)),
            pl.BlockSpec((None, HIST_ROWS, w), lambda i, j: (i, 0, 0)),
            full((HIST_ROWS, w)),
            full((1, w)),
            full((1, w)),
            full((1, w)),
        ],
        out_specs=(pl.BlockSpec((None, chunk, w), lambda i, j: (i, j, 0)),
                   pl.BlockSpec((None, HIST_ROWS, w), lambda i, j: (i, 0, 0))),
        scratch_shapes=[pltpu.VMEM((HIST_ROWS + chunk, w), F32)],
        compiler_params=_params(2),
        name="conv_mixer",
    )(x, scale, shift, w_cv, hist, wc, b_conv.reshape(1, w), ln_g.reshape(1, w), ln_b.reshape(1, w))


def _fox_proj_kernel(x_ref, sc_ref, sh_ref, w_ref, bf_ref, q_ref, k_ref, v_ref, kb_ref, vb_ref, lf_ref):
    u = x_ref[...] * (1.0 + sc_ref[...]) + sh_ref[...]
    p = _dot(_bf(u), w_ref[...])
    w = BRANCH_W
    q_ref[...] = _bf(p[:, 0:w] * (HEAD_DIM ** -0.5))
    k = p[:, w:2 * w]
    v = p[:, 2 * w:3 * w]
    k_ref[...] = k
    v_ref[...] = v
    kb_ref[...] = _bf(k)
    vb_ref[...] = _bf(v)
    z = p[:, 3 * w:] + bf_ref[...]
    lf = jnp.minimum(z, 0.0) - jnp.log(1.0 + jnp.exp(-jnp.abs(z)))
    lane = lax.broadcasted_iota(jnp.int32, lf.shape, 1)
    lf_ref[...] = jnp.where(lane < N_HEADS, lf, 0.0)


def _fox_proj(x, scale, shift, w_fox, b_f, tm):
    b, t, _ = x.shape
    w = BRANCH_W
    nw = 3 * w + LANES
    bfp = jnp.concatenate([b_f.astype(F32), jnp.zeros((LANES - N_HEADS,), F32)]).reshape(1, LANES)
    row = lambda width: pl.BlockSpec((None, tm, width), lambda i, j: (i, j, 0))
    sds = lambda width, dt: jax.ShapeDtypeStruct((b, t, width), dt)
    return pl.pallas_call(
        _fox_proj_kernel,
        out_shape=(sds(w, BF16), sds(w, F32), sds(w, F32), sds(w, BF16), sds(w, BF16), sds(LANES, F32)),
        grid=(b, t // tm),
        in_specs=[
            row(D_MODEL),
            pl.BlockSpec((None, 1, D_MODEL), lambda i, j: (i, 0, 0)),
            pl.BlockSpec((None, 1, D_MODEL), lambda i, j: (i, 0, 0)),
            pl.BlockSpec((D_MODEL, nw), lambda i, j: (0, 0)),
            pl.BlockSpec((1, LANES), lambda i, j: (0, 0)),
        ],
        out_specs=(row(w), row(w), row(w), row(w), row(w), row(LANES)),
        compiler_params=_params(2),
        name="fox_projection",
    )(x, scale, shift, w_fox, bfp)


def _cumsum_kernel(lf_ref, col_ref, row_ref, carry_scr):
    j = pl.program_id(1)

    @pl.when(j == 0)
    def _():
        carry_scr[...] = jnp.zeros_like(carry_scr)

    x = lf_ref[...]
    n = x.shape[0]
    r = lax.broadcasted_iota(jnp.int32, (n, n), 0)
    c = lax.broadcasted_iota(jnp.int32, (n, n), 1)
    tri = jnp.where(c <= r, 1.0, 0.0).astype(BF16)
    hi = _bf(x)
    r1 = x - hi.astype(F32)
    mid = _bf(r1)
    lo = _bf(r1 - mid.astype(F32))
    cs = _dot(tri, hi) + _dot(tri, mid) + _dot(tri, lo) + carry_scr[0:1, :]
    col_ref[...] = cs
    row_ref[...] = cs.T[0:8, :]
    carry_scr[...] = jnp.broadcast_to(cs[n - 1:n, :], carry_scr.shape)


def _logf_cumsum(lf):
    b, tk, _ = lf.shape
    nb = tk // KV_BLOCK
    return pl.pallas_call(
        _cumsum_kernel,
        out_shape=(jax.ShapeDtypeStruct((b, tk, LANES), F32),
                   jax.ShapeDtypeStruct((b, nb, 8, KV_BLOCK), F32)),
        grid=(b, nb),
        in_specs=[pl.BlockSpec((None, KV_BLOCK, LANES), lambda i, j: (i, j, 0))],
        out_specs=(pl.BlockSpec((None, KV_BLOCK, LANES), lambda i, j: (i, j, 0)),
                   pl.BlockSpec((None, None, 8, KV_BLOCK), lambda i, j: (i, j, 0, 0))),
        scratch_shapes=[pltpu.VMEM((8, LANES), F32)],
        compiler_params=_params(2),
        name="logf_cumsum",
    )(lf)


def _attn_kernel(*refs, fox, off, tk_valid, tq, lam_init):
    if fox:
        q_ref, k_ref, v_ref, fc_ref, fr_ref, o_ref, vt_scr, qm_scr, mb_scr, m_scr, r_scr, acc_scr, s_scr = refs
    else:
        q_ref, k_ref, v_ref, lam_ref, g_ref, o_ref, vt_scr, qm_scr, mb_scr, m_scr, r_scr, acc_scr, s_scr = refs
    i = pl.program_id(1)
    w = BRANCH_W
    n_sub, tqp, _ = qm_scr.shape
    sub_w = w // n_sub
    subs_per_head = n_sub // N_HEADS
    nb = vt_scr.shape[0]
    hd = HEAD_DIM
    c = LOG2E if fox else (DIFF_SUB ** -0.5) * LOG2E
    q0 = off + i * tq
    nfull = q0 // KV_BLOCK

    @pl.when(i == 0)
    def _():
        for jb in range(nb):
            vt = v_ref[jb * KV_BLOCK:(jb + 1) * KV_BLOCK, :].T
            for h in range(N_HEADS):
                vt_scr[jb, h, 0:hd, :] = vt[h * hd:(h + 1) * hd, :]
                vt_scr[jb, h, hd:, :] = jnp.ones((vt_scr.shape[2] - hd, KV_BLOCK), BF16)
        kpos = lax.broadcasted_iota(jnp.int32, (KV_BLOCK, tqp), 0)
        qpos = lax.broadcasted_iota(jnp.int32, (KV_BLOCK, tqp), 1)
        if fox:
            vis = kpos <= qpos
        else:
            vis = (kpos // CHUNK) <= (qpos // CHUNK)
        vis = vis & (kpos < tk_valid - nfull * KV_BLOCK)
        mb_scr[...] = jnp.where(vis, 0.0, NEG)

    q = q_ref[...]
    lane_sub = lax.broadcasted_iota(jnp.int32, (1, w), 1) // sub_w
    if tqp > tq:
        qm_scr[...] = jnp.zeros_like(qm_scr)
    for n in range(n_sub):
        qm_scr[n, 0:tq, :] = jnp.where(lane_sub == n, q, jnp.zeros_like(q))
    m_scr[...] = jnp.full(m_scr.shape, NEG, F32)
    r_scr[...] = jnp.full(r_scr.shape, NEG, F32)
    acc_scr[...] = jnp.zeros_like(acc_scr)

    def scores(j, n, masked):
        start = pl.multiple_of(j * KV_BLOCK, KV_BLOCK)
        h = n // subs_per_head
        s = _dot_nt(k_ref[pl.ds(start, KV_BLOCK), :], qm_scr[n])
        if fox:
            s = s + (fr_ref[h:h + 1, 0:tqp] - fc_ref[pl.ds(start, KV_BLOCK), h:h + 1])
        if masked:
            s = s + mb_scr[...]
        s_scr[n] = s
        m_scr[n] = jnp.maximum(m_scr[n], jnp.max(s, axis=0, keepdims=True))

    def values(j, n):
        h = n // subs_per_head
        m = m_scr[n]
        alpha = jnp.exp2((r_scr[n] - m) * c)
        r_scr[n] = m
        p = jnp.exp2((s_scr[n] - m) * c)
        acc_scr[n] = alpha * acc_scr[n] + _dot(vt_scr[j, h], _bf(p))

    @pl.when(nfull == 0)
    def _():
        for n in range(n_sub):
            scores(0, n, True)

    @pl.when(nfull > 0)
    def _():
        for n in range(n_sub):
            scores(0, n, False)

    def body(j, carry):
        for n in range(n_sub):
            values(j - 1, n)
            scores(j, n, False)
        return carry

    lax.fori_loop(1, nfull, body, 0)

    @pl.when(nfull > 0)
    def _():
        for n in range(n_sub):
            values(nfull - 1, n)
            scores(nfull, n, True)

    for n in range(n_sub):
        values(nfull, n)

    def normalised(n):
        acc = acc_scr[n]
        return acc[0:hd] / acc[hd:hd + 1]

    if fox:
        out_t = jnp.concatenate([normalised(h) for h in range(N_HEADS)], axis=0)
        o_ref[...] = _bf(out_t.T[0:tq, :])
    else:
        lamv = lam_ref[...]
        lam = (jnp.exp(jnp.sum(lamv[0:1] * lamv[1:2], axis=-1, keepdims=True))
               - jnp.exp(jnp.sum(lamv[2:3] * lamv[3:4], axis=-1, keepdims=True)) + lam_init)
        parts = [normalised(2 * h) - lam * normalised(2 * h + 1) for h in range(N_HEADS)]
        dy = jnp.concatenate(parts, axis=0).T[0:tq, :]
        ms = _head_sum(dy * dy, HEAD_DIM) * (1.0 / HEAD_DIM)
        o_ref[...] = _bf(dy * lax.rsqrt(ms + EPS) * g_ref[...] * (1.0 - lam_init))


def _attention(q, kb, vb, extras, *, fox, tq, off, tk_valid, lam_init=0.0):
    b, t, w = q.shape
    tk = kb.shape[1]
    nb = tk // KV_BLOCK
    tqp = max(tq, LANES)
    n_sub = N_HEADS if fox else N_SUB
    assert off % KV_BLOCK == 0 and (tq == KV_BLOCK or t == tq)
    kernel = functools.partial(_attn_kernel, fox=fox, off=off, tk_valid=tk_valid, tq=tq, lam_init=lam_init)
    in_specs = [
        pl.BlockSpec((None, tq, w), lambda i, j: (i, j, 0)),
        pl.BlockSpec((None, tk, w), lambda i, j: (i, 0, 0)),
        pl.BlockSpec((None, tk, w), lambda i, j: (i, 0, 0)),
    ]
    if fox:
        in_specs += [
            pl.BlockSpec((None, tk, LANES), lambda i, j: (i, 0, 0)),
            pl.BlockSpec((None, None, 8, KV_BLOCK), lambda i, j: (i, (off + j * tq) // KV_BLOCK, 0, 0)),
        ]
    else:
        in_specs += [
            pl.BlockSpec((4, DIFF_SUB), lambda i, j: (0, 0)),
            pl.BlockSpec((1, w), lambda i, j: (0, 0)),
        ]
    return pl.pallas_call(
        kernel,
        out_shape=jax.ShapeDtypeStruct((b, t, w), BF16),
        grid=(b, t // tq),
        in_specs=in_specs,
        out_specs=pl.BlockSpec((None, tq, w), lambda i, j: (i, j, 0)),
        scratch_shapes=[
            pltpu.VMEM((nb, N_HEADS, HEAD_DIM + BF16_ROWS, KV_BLOCK), BF16),
            pltpu.VMEM((n_sub, tqp, w), BF16),
            pltpu.VMEM((KV_BLOCK, tqp), F32),
            pltpu.VMEM((n_sub, 1, tqp), F32),
            pltpu.VMEM((n_sub, 1, tqp), F32),
            pltpu.VMEM((n_sub, HEAD_DIM + BF16_ROWS, tqp), F32),
            pltpu.VMEM((n_sub, KV_BLOCK, tqp), F32),
        ],
        compiler_params=_params(2),
        name="fox_attention" if fox else "diff_attention",
    )(q, kb, vb, *extras)


def _fox_attention(q, kb, vb, fcol, frow, tq, off, tk_valid):
    return _attention(q, kb, vb, (fcol, frow), fox=True, tq=tq, off=off, tk_valid=tk_valid)


def _diff_proj_kernel(x_ref, sc_ref, sh_ref, w_ref, cos_ref, sin_ref, q_ref, k_ref, v_ref, kb_ref, vb_ref):
    u = x_ref[...] * (1.0 + sc_ref[...]) + sh_ref[...]
    p = _dot(_bf(u), w_ref[...])
    w = BRANCH_W
    cos = cos_ref[...]
    sin = sin_ref[...]
    q_ref[...] = _bf(p[:, 0:w] * cos + p[:, w:2 * w] * sin)
    k = p[:, 2 * w:3 * w] * cos + p[:, 3 * w:4 * w] * sin
    v = p[:, 4 * w:5 * w]
    k_ref[...] = k
    v_ref[...] = v
    kb_ref[...] = _bf(k)
    vb_ref[...] = _bf(v)


def _diff_proj(x, scale, shift, w_diff, pos, tm):
    b, t, _ = x.shape
    w = BRANCH_W
    cos, sin = _rope_tables(pos, DIFF_SUB, ROT_DIM, ROPE_THETA, N_SUB)
    row = lambda width: pl.BlockSpec((None, tm, width), lambda i, j: (i, j, 0))
    sds = lambda dt: jax.ShapeDtypeStruct((b, t, w), dt)
    return pl.pallas_call(
        _diff_proj_kernel,
        out_shape=(sds(BF16), sds(F32), sds(F32), sds(BF16), sds(BF16)),
        grid=(b, t // tm),
        in_specs=[
            row(D_MODEL),
            pl.BlockSpec((None, 1, D_MODEL), lambda i, j: (i, 0, 0)),
            pl.BlockSpec((None, 1, D_MODEL), lambda i, j: (i, 0, 0)),
            pl.BlockSpec((D_MODEL, 5 * w), lambda i, j: (0, 0)),
            pl.BlockSpec((tm, w), lambda i, j: (j, 0)),
            pl.BlockSpec((tm, w), lambda i, j: (j, 0)),
        ],
        out_specs=(row(w), row(w), row(w), row(w), row(w)),
        compiler_params=_params(2),
        name="diff_projection",
    )(x, scale, shift, w_diff, cos, sin)


def _diff_attention(q, kb, vb, diff_lambda, subln_g, tq, off, tk_valid, lam_init):
    g = jnp.tile(subln_g.astype(F32), N_HEADS).reshape(1, BRANCH_W)
    return _attention(q, kb, vb, (diff_lambda.astype(F32), g), fox=False, tq=tq, off=off,
                      tk_valid=tk_valid, lam_init=lam_init)


def _top2_combine(logits):
    lane = lax.broadcasted_iota(jnp.int32, logits.shape, 1).astype(F32)
    lg = jnp.where(lane < N_EXPERTS, logits, -jnp.inf)
    v1 = jnp.max(lg, axis=-1, keepdims=True)
    i1 = jnp.min(jnp.where(lg == v1, lane, float(LANES)), axis=-1, keepdims=True)
    lg2 = jnp.where(lane == i1, -jnp.inf, lg)
    v2 = jnp.max(lg2, axis=-1, keepdims=True)
    i2 = jnp.min(jnp.where(lg2 == v2, lane, float(LANES)), axis=-1, keepdims=True)
    e2 = jnp.exp(v2 - v1)
    w1 = 1.0 / (1.0 + e2)
    w2 = e2 / (1.0 + e2)
    return jnp.where(lane == i1, w1, 0.0) + jnp.where(lane == i2, w2, 0.0)


def _merge_kernel(*refs, with_router):
    if with_router:
        (x_ref, sc1_ref, sh1_ref, g1_ref, sc2_ref, sh2_ref, hr_ref, hf_ref, hc_ref, hd_ref,
         wg_ref, wb_ref, wo_ref, lg_ref, lb_ref, wr_ref, br_ref, x1_ref, u2_ref, cmb_ref) = refs
    else:
        (x_ref, sc1_ref, sh1_ref, g1_ref, sc2_ref, sh2_ref, hr_ref, hf_ref, hc_ref, hd_ref,
         wg_ref, wb_ref, wo_ref, lg_ref, lb_ref, x1_ref, u2_ref) = refs
    x = x_ref[...]
    u = _bf(x * (1.0 + sc1_ref[...]) + sh1_ref[...])
    merged = None
    for n, h_ref in enumerate((hr_ref, hf_ref, hc_ref, hd_ref)):
        gate = _dot(u, wg_ref[:, n * D_MODEL:(n + 1) * D_MODEL])
        term = _sigmoid(gate) * _dot(h_ref[...], wb_ref[n])
        merged = term if merged is None else merged + term
    mix = _dot(_bf(merged), wo_ref[...])
    x1 = _layer_norm_rows(ALPHA * x + g1_ref[...] * mix, lg_ref[...], lb_ref[...])
    x1_ref[...] = x1
    u2 = _bf(x1 * (1.0 + sc2_ref[...]) + sh2_ref[...])
    u2_ref[...] = u2
    if with_router:
        cmb_ref[...] = _top2_combine(_dot(u2, wr_ref[...]) + br_ref[...])


def _merge(x2d, mods, branches, w_gate, w_branch, w_out, ln_g, ln_b, router, tm, tiles_per_group):
    n = x2d.shape[0]
    r = mods[0].shape[1]
    w = BRANCH_W
    with_router = router is not None
    row = lambda width: pl.BlockSpec((tm, width), lambda i: (i, 0))
    mod_spec = pl.BlockSpec((None, r, D_MODEL), lambda i: (i // tiles_per_group, 0, 0))
    full = lambda shape: pl.BlockSpec(shape, lambda i: (0,) * len(shape))
    in_specs = ([row(D_MODEL)] + [mod_spec] * 5 + [row(w)] * 4
                + [full((D_MODEL, 4 * D_MODEL)), full((4, w, D_MODEL)), full((D_MODEL, D_MODEL)),
                   full((1, D_MODEL)), full((1, D_MODEL))])
    args = [x2d, *mods, *branches, w_gate, w_branch, w_out, ln_g.reshape(1, D_MODEL), ln_b.reshape(1, D_MODEL)]
    out_shape = [jax.ShapeDtypeStruct((n, D_MODEL), F32), jax.ShapeDtypeStruct((n, D_MODEL), BF16)]
    out_specs = [row(D_MODEL), row(D_MODEL)]
    if with_router:
        in_specs += [full((D_MODEL, LANES)), full((1, LANES))]
        args += list(router)
        out_shape.append(jax.ShapeDtypeStruct((n, LANES), F32))
        out_specs.append(row(LANES))
    return pl.pallas_call(
        functools.partial(_merge_kernel, with_router=with_router),
        out_shape=tuple(out_shape),
        grid=(n // tm,),
        in_specs=in_specs,
        out_specs=tuple(out_specs),
        compiler_params=_params(1),
        name="merge_outproj_ln",
    )(*args)


def _ffn_kernel(u_ref, x1_ref, g2_ref, wa_ref, wg_ref, wd_ref, lg_ref, lb_ref, o_ref, acc_scr):
    j = pl.program_id(1)

    @pl.when(j == 0)
    def _():
        acc_scr[...] = jnp.zeros_like(acc_scr)

    u = u_ref[...]
    a = _dot(u, wa_ref[...])
    g = _dot(u, wg_ref[...])
    acc_scr[...] += _dot(_bf(a * _sigmoid(a) * g), wd_ref[...])

    @pl.when(j == pl.num_programs(1) - 1)
    def _():
        z = ALPHA * x1_ref[...] + g2_ref[...] * acc_scr[...]
        o_ref[...] = _layer_norm_rows(z, lg_ref[...], lb_ref[...])


def _ffn_dense(u2, x1, gate2, w_up, w_down, ln_g, ln_b, tm, tiles_per_group, th):
    n = u2.shape[0]
    r = gate2.shape[1]
    nh = D_FF // th
    return pl.pallas_call(
        _ffn_kernel,
        out_shape=jax.ShapeDtypeStruct((n, D_MODEL), F32),
        grid=(n // tm, nh),
        in_specs=[
            pl.BlockSpec((tm, D_MODEL), lambda i, j: (i, 0)),
            pl.BlockSpec((tm, D_MODEL), lambda i, j: (i, 0)),
            pl.BlockSpec((None, r, D_MODEL), lambda i, j: (i // tiles_per_group, 0, 0)),
            pl.BlockSpec((D_MODEL, th), lambda i, j: (0, j)),
            pl.BlockSpec((D_MODEL, th), lambda i, j: (0, nh + j)),
            pl.BlockSpec((th, D_MODEL), lambda i, j: (j, 0)),
            pl.BlockSpec((1, D_MODEL), lambda i, j: (0, 0)),
            pl.BlockSpec((1, D_MODEL), lambda i, j: (0, 0)),
        ],
        out_specs=pl.BlockSpec((tm, D_MODEL), lambda i, j: (i, 0)),
        scratch_shapes=[pltpu.VMEM((tm, D_MODEL), F32)],
        compiler_params=_params(2),
        name="ffn_dense",
    )(u2, x1, gate2, w_up, w_up, w_down, ln_g.reshape(1, D_MODEL), ln_b.reshape(1, D_MODEL))


def _moe_kernel(u_ref, x1_ref, g2_ref, cmb_ref, wa_ref, wg_ref, wd_ref, lg_ref, lb_ref, o_ref, acc_scr):
    e = pl.program_id(1)
    j = pl.program_id(2)

    @pl.when((e == 0) & (j == 0))
    def _():
        acc_scr[...] = jnp.zeros_like(acc_scr)

    u = u_ref[...]
    a = _dot(u, wa_ref[...])
    g = _dot(u, wg_ref[...])
    y = _dot(_bf(a * _sigmoid(a) * g), wd_ref[...])
    cmb = cmb_ref[...]
    lane = lax.broadcasted_iota(jnp.int32, cmb.shape, 1)
    ce = jnp.sum(jnp.where(lane == e, cmb, 0.0), axis=-1, keepdims=True)
    acc_scr[...] += ce * y

    @pl.when((e == pl.num_programs(1) - 1) & (j == pl.num_programs(2) - 1))
    def _():
        z = ALPHA * x1_ref[...] + g2_ref[...] * acc_scr[...]
        o_ref[...] = _layer_norm_rows(z, lg_ref[...], lb_ref[...])


def _ffn_experts(u2, x1, gate2, combine, w_in, w_out, ln_g, ln_b, tm, tiles_per_group, th):
    n = u2.shape[0]
    r = gate2.shape[1]
    nh = D_EXPERT // th
    return pl.pallas_call(
        _moe_kernel,
        out_shape=jax.ShapeDtypeStruct((n, D_MODEL), F32),
        grid=(n // tm, N_EXPERTS, nh),
        in_specs=[
            pl.BlockSpec((tm, D_MODEL), lambda i, e, j: (i, 0)),
            pl.BlockSpec((tm, D_MODEL), lambda i, e, j: (i, 0)),
            pl.BlockSpec((None, r, D_MODEL), lambda i, e, j: (i // tiles_per_group, 0, 0)),
            pl.BlockSpec((tm, LANES), lambda i, e, j: (i, 0)),
            pl.BlockSpec((None, D_MODEL, th), lambda i, e, j: (e, 0, j)),
            pl.BlockSpec((None, D_MODEL, th), lambda i, e, j: (e, 0, nh + j)),
            pl.BlockSpec((None, th, D_MODEL), lambda i, e, j: (e, j, 0)),
            pl.BlockSpec((1, D_MODEL), lambda i, e, j: (0, 0)),
            pl.BlockSpec((1, D_MODEL), lambda i, e, j: (0, 0)),
        ],
        out_specs=pl.BlockSpec((tm, D_MODEL), lambda i, e, j: (i, 0)),
        scratch_shapes=[pltpu.VMEM((tm, D_MODEL), F32)],
        compiler_params=_params(3),
        name="ffn_experts",
    )(u2, x1, gate2, combine, w_in, w_in, w_out, ln_g.reshape(1, D_MODEL), ln_b.reshape(1, D_MODEL))


def _split_w_in(w_in_l):
    w = BRANCH_W
    col = lambda off, n: w_in_l[:, off:off + n]
    o = 0
    ret_q, ret_k, ret_v, ret_g = (col(o + i * w, w) for i in range(4)); o += 4 * w
    fox_q, fox_k, fox_v = (col(o + i * w, w) for i in range(3)); o += 3 * w
    fox_f = col(o, N_HEADS); o += N_HEADS
    conv = col(o, 2 * w); o += 2 * w
    diff_q, diff_k, diff_v = (col(o + i * w, w) for i in range(3)); o += 3 * w
    gate = col(o, 4 * D_MODEL)
    rot_r = lambda m: _rotated_columns(m, HEAD_DIM, HEAD_DIM)
    rot_d = lambda m: _rotated_columns(m, DIFF_SUB, ROT_DIM)
    w_ret = jnp.concatenate([ret_q, rot_r(ret_q), ret_k, rot_r(ret_k), ret_v, ret_g], axis=1)
    w_fox = jnp.concatenate([fox_q, fox_k, fox_v, fox_f, jnp.zeros((D_MODEL, LANES - N_HEADS), F32)], axis=1)
    w_diff = jnp.concatenate([diff_q, rot_d(diff_q), diff_k, rot_d(diff_k), diff_v], axis=1)
    return _bf(w_ret), _bf(w_fox), _bf(conv), _bf(w_diff), _bf(gate)


def _block_diag_state(s):
    b = s.shape[0]
    eye = jnp.eye(N_HEADS, dtype=s.dtype)
    return jnp.einsum('bhde,hg->bhdge', s, eye).reshape(b, BRANCH_W, BRANCH_W)


def _diag_blocks(s_bd):
    b = s_bd.shape[0]
    s4 = s_bd.reshape(b, N_HEADS, HEAD_DIM, N_HEADS, HEAD_DIM)
    return jnp.stack([s4[:, h, :, h, :] for h in range(N_HEADS)], axis=1)


def _pad_rows(a, rows):
    return jnp.concatenate([a, jnp.zeros((a.shape[0], rows - a.shape[1]) + a.shape[2:], a.dtype)], axis=1)


def _trunk_layer(x, mod, pos, past, l, lw, sample):
    b, t, _ = x.shape
    d = D_MODEL
    w = BRANCH_W
    shift1, scale1, gate1 = (mod[0][:, None, i * d:(i + 1) * d] for i in range(3))
    shift2, scale2, gate2 = (mod[1][:, None, i * d:(i + 1) * d] for i in range(3))
    lam_init = 0.8 - 0.6 * math.exp(-0.3 * l)
    chunk = min(t, KV_BLOCK)
    tm_proj = min(t, 1024)

    s0 = jnp.zeros((b, w, w), F32) if past is None else _block_diag_state(past[5])
    h_ret, s_bd = _retention(x, scale1, shift1, lw['w_ret'], pos, s0, chunk)
    ret_state = _diag_blocks(s_bd)

    if past is None:
        hist = jnp.zeros((b, HIST_ROWS, w), F32)
    else:
        hist = jnp.concatenate([jnp.zeros((b, HIST_ROWS - (CONV_W - 1), w), F32), past[6]], axis=1)
    h_conv, tail = _conv_mixer(x, scale1, shift1, lw['w_conv_in'], hist, lw['w_conv'], lw['b_conv'],
                               lw['conv_ln_g'], lw['conv_ln_b'], chunk)
    conv_buf = tail[:, HIST_ROWS - (CONV_W - 1):, :]

    fq, fk, fv, fkb, fvb, lf = _fox_proj(x, scale1, shift1, lw['w_fox'], lw['b_fox_f'], tm_proj)
    if past is None:
        off, tk_valid = 0, t
        k_all, v_all, lf_all = fkb, fvb, lf
    else:
        p_len = past[0].shape[1]
        off, tk_valid = p_len, p_len + t
        tk_pad = p_len + KV_BLOCK
        k_all = _pad_rows(jnp.concatenate([_bf(past[0].reshape(b, p_len, w)), fkb], axis=1), tk_pad)
        v_all = _pad_rows(jnp.concatenate([_bf(past[1].reshape(b, p_len, w)), fvb], axis=1), tk_pad)
        past_lf = jnp.concatenate([past[2].astype(F32), jnp.zeros((b, p_len, LANES - N_HEADS), F32)], axis=2)
        lf_all = _pad_rows(jnp.concatenate([past_lf, lf], axis=1), tk_pad)
    fcol, frow = _logf_cumsum(lf_all)
    tq = min(t, KV_BLOCK)
    h_fox = _fox_attention(fq, k_all, v_all, fcol, frow, tq, off, tk_valid)

    dq, dk, dv, dkb, dvb = _diff_proj(x, scale1, shift1, lw['w_diff'], pos, tm_proj)
    if past is None:
        dk_all, dv_all = dkb, dvb
    else:
        dk_all = _pad_rows(jnp.concatenate([_bf(past[3].reshape(b, p_len, w)), dkb], axis=1), tk_pad)
        dv_all = _pad_rows(jnp.concatenate([_bf(past[4].reshape(b, p_len, w)), dvb], axis=1), tk_pad)
    h_diff = _diff_attention(dq, dk_all, dv_all, lw['diff_lambda'], lw['diff_subln_g'], tq, off, tk_valid, lam_init)

    n = b * t
    if sample:
        tm_merge = tm_ffn = min(n, 256)
        tpg_merge = tpg_ffn = 1
        rows_mod = lambda m, tm: jnp.repeat(m, t, axis=1).reshape(n // tm, tm, d)
    else:
        tm_merge, tm_ffn = min(t, 512), min(t, 1024)
        tpg_merge, tpg_ffn = t // tm_merge, t // tm_ffn
        rows_mod = lambda m, tm: m
    mods = tuple(rows_mod(m, tm_merge) for m in (scale1, shift1, gate1, scale2, shift2))
    branches = tuple(h.reshape(n, w) for h in (h_ret, h_fox, h_conv, h_diff))
    merged = _merge(x.reshape(n, d), mods, branches, lw['w_gate'], lw['w_branch'], lw['w_out'],
                    lw['ln_g'][0], lw['ln_b'][0], lw.get('router'), tm_merge, tpg_merge)
    g2 = rows_mod(gate2, tm_ffn)
    if 'router' in lw:
        x1, u2, combine = merged
        x2 = _ffn_experts(u2, x1, g2, combine, lw['w_exp_in'], lw['w_exp_out'], lw['ln_g'][1], lw['ln_b'][1],
                          tm_ffn, tpg_ffn, 512)
    else:
        x1, u2 = merged
        x2 = _ffn_dense(u2, x1, g2, lw['w_ffn_in'], lw['w_ffn_out'], lw['ln_g'][1], lw['ln_b'][1],
                        tm_ffn, tpg_ffn, 1408)
    heads = lambda a, nh: a.reshape(b, t, nh, w // nh)
    new_state = (heads(fk, N_HEADS), heads(fv, N_HEADS), lf[:, :, :N_HEADS], heads(dk, N_SUB), heads(dv, N_HEADS),
                 ret_state, conv_buf)
    return x2.reshape(b, t, d), new_state


def kernel(x_prompt, x_sample, c_prompt, c_sample, cache_fox_k, cache_fox_v, cache_fox_logf, cache_diff_k, cache_diff_v, state_ret, state_conv, w_in, b_fox_f, w_conv, b_conv, conv_ln_g, conv_ln_b, diff_lambda, diff_subln_g, w_branch, w_out, w_ada, b_ada, ln_g, ln_b, w_ffn_in, w_ffn_out, w_router, b_router, w_exp_in, w_exp_out):
    depth = w_in.shape[0]
    bp = x_prompt.shape[0]
    past_len = cache_fox_k.shape[2]
    pos_p = jnp.arange(x_prompt.shape[1], dtype=jnp.int32)
    pos_s = past_len + jnp.arange(x_sample.shape[1], dtype=jnp.int32)
    mod = _modulation(jnp.concatenate([c_prompt, c_sample], axis=0), w_ada, b_ada)
    yp, ys = x_prompt, x_sample
    new_p, new_s = [], []
    for l in range(depth):
        w_ret, w_fox, w_conv_in, w_diff, w_gate = _split_w_in(w_in[l])
        lw = dict(w_ret=w_ret, w_fox=w_fox, w_conv_in=w_conv_in, w_diff=w_diff, w_gate=w_gate,
                  b_fox_f=b_fox_f[l], w_conv=w_conv[l], b_conv=b_conv[l], conv_ln_g=conv_ln_g[l],
                  conv_ln_b=conv_ln_b[l], diff_lambda=diff_lambda[l], diff_subln_g=diff_subln_g[l],
                  w_branch=_bf(w_branch[l]), w_out=_bf(w_out[l]), ln_g=ln_g[l], ln_b=ln_b[l])
        if l % 2 == 0:
            lw['w_ffn_in'] = _bf(w_ffn_in[l // 2])
            lw['w_ffn_out'] = _bf(w_ffn_out[l // 2])
        else:
            wr = jnp.concatenate([w_router[l // 2], jnp.zeros((D_MODEL, LANES - N_EXPERTS), F32)], axis=1)
            br = jnp.concatenate([b_router[l // 2].astype(F32), jnp.zeros((LANES - N_EXPERTS,), F32)]).reshape(1, LANES)
            lw['router'] = (_bf(wr), br)
            lw['w_exp_in'] = _bf(w_exp_in[l // 2])
            lw['w_exp_out'] = _bf(w_exp_out[l // 2])
        past_l = (cache_fox_k[l], cache_fox_v[l], cache_fox_logf[l], cache_diff_k[l], cache_diff_v[l],
                  state_ret[l], state_conv[l])
        yp, st_p = _trunk_layer(yp, mod[l][:, :bp], pos_p, None, l, lw, sample=False)
        ys, st_s = _trunk_layer(ys, mod[l][:, bp:], pos_s, past_l, l, lw, sample=True)
        new_p.append(st_p)
        new_s.append(st_s)
    outs_p = tuple(jnp.stack(a) for a in zip(*new_p))
    outs_s = tuple(jnp.stack(a) for a in zip(*new_s))
    return (yp, ys) + outs_p + outs_s
```

```python
import functools
import math

import jax
import jax.numpy as jnp
from jax import lax
from jax.experimental import pallas as pl
from jax.experimental.pallas import tpu as pltpu

D_MODEL = 1024
BRANCH_W = 256
HEAD_DIM = 64
N_HEADS = 4
DIFF_SUB = 32
N_SUB = 8
ROT_DIM = DIFF_SUB // 4
RET_THETA = 10000.0
ROPE_THETA = 500000.0
CHUNK = 64
CONV_W = 31
D_FF = 2816
N_EXPERTS = 8
D_EXPERT = 3584
DEPTH = 2
ALPHA = (2.0 * DEPTH) ** 0.25
EPS = 1e-5
NEG = -1e30
LOG2E = math.log2(math.e)

LANES = 128
BF16_ROWS = 16
KV_BLOCK = 256
HIST_ROWS = 32
VMEM_LIMIT = 56 * 1024 * 1024

F32 = jnp.float32
BF16 = jnp.bfloat16


def _bf(x):
    return x.astype(BF16)


def _dot(a, b):
    return jnp.dot(a, b, preferred_element_type=F32)


def _dot_nt(a, b):
    return lax.dot_general(a, b, (((1,), (1,)), ((), ())), preferred_element_type=F32)


def _dot_tn(a, b):
    return lax.dot_general(a, b, (((0,), (0,)), ((), ())), preferred_element_type=F32)


def _sigmoid(x):
    return 1.0 / (1.0 + jnp.exp(-x))


def _params(n_axes):
    return pltpu.CompilerParams(dimension_semantics=("arbitrary",) * n_axes,
                                vmem_limit_bytes=VMEM_LIMIT)


def _head_sum(y, width):
    n = y.shape[-1]
    r = lax.broadcasted_iota(jnp.int32, (n, n), 0) // width
    c = lax.broadcasted_iota(jnp.int32, (n, n), 1) // width
    bd = jnp.where(r == c, 1.0, 0.0).astype(BF16)
    hi = _bf(y)
    lo = _bf(y - hi.astype(F32))
    return _dot(hi, bd) + _dot(lo, bd)


def _layer_norm_rows(z, g, b):
    mu = jnp.mean(z, axis=-1, keepdims=True)
    d = z - mu
    var = jnp.mean(d * d, axis=-1, keepdims=True)
    return d * lax.rsqrt(var + EPS) * g + b


def _mod_kernel(c_ref, w_ref, b_ref, o_ref):
    c = c_ref[...]
    sc = _bf(c * _sigmoid(c))
    o_ref[...] = _dot(sc, _bf(w_ref[...])) + b_ref[...]


def _modulation(c_all, w_ada, b_ada):
    rows = c_all.shape[0]
    depth = w_ada.shape[0]
    d3 = w_ada.shape[-1]
    nj = d3 // D_MODEL
    return pl.pallas_call(
        _mod_kernel,
        out_shape=jax.ShapeDtypeStruct((depth, 2, rows, d3), F32),
        grid=(depth * 2, nj),
        in_specs=[
            pl.BlockSpec((rows, D_MODEL), lambda i, j: (0, 0)),
            pl.BlockSpec((None, None, D_MODEL, D_MODEL), lambda i, j: (i // 2, i % 2, 0, j)),
            pl.BlockSpec((None, None, 1, D_MODEL), lambda i, j: (i // 2, i % 2, 0, j)),
        ],
        out_specs=pl.BlockSpec((None, None, rows, D_MODEL), lambda i, j: (i // 2, i % 2, 0, j)),
        compiler_params=_params(2),
        name="adaln_modulation",
    )(c_all, w_ada, b_ada.reshape(depth, 2, 1, d3))


def _ret_kernel(x_ref, sc_ref, sh_ref, w_ref, cos_ref, sin_ref, dmask_ref, qdec_ref, kdec_ref,
                cdec_ref, s0_ref, h_ref, sout_ref, s_scr):
    c = pl.program_id(1)

    @pl.when(c == 0)
    def _():
        s_scr[...] = s0_ref[...]

    u = x_ref[...] * (1.0 + sc_ref[...]) + sh_ref[...]
    p = _dot(_bf(u), w_ref[...])
    w = BRANCH_W
    cos = cos_ref[...]
    sin = sin_ref[...]
    q = p[:, 0:w] * cos + p[:, w:2 * w] * sin
    k = (p[:, 2 * w:3 * w] * cos + p[:, 3 * w:4 * w] * sin) * (HEAD_DIM ** -0.5)
    v = p[:, 4 * w:5 * w]
    g = p[:, 5 * w:6 * w]
    rows = q.shape[0]
    lane_head = lax.broadcasted_iota(jnp.int32, (1, w), 1) // HEAD_DIM
    kb = _bf(k)
    vb = _bf(v)
    y = jnp.zeros((rows, w), F32)
    for h in range(N_HEADS):
        mh = lane_head == h
        qh = _bf(jnp.where(mh, q, 0.0))
        a = _dot_nt(qh, kb) * dmask_ref[h]
        y = y + jnp.where(mh, _dot(_bf(a), vb), 0.0)
    s_prev = s_scr[...]
    y = y + _dot(_bf(q * qdec_ref[...]), _bf(s_prev))
    kv = _dot_tn(_bf(k * kdec_ref[...]), vb)
    r = lax.broadcasted_iota(jnp.int32, (w, w), 0) // HEAD_DIM
    cc = lax.broadcasted_iota(jnp.int32, (w, w), 1) // HEAD_DIM
    s_new = cdec_ref[...] * s_prev + jnp.where(r == cc, kv, 0.0)
    s_scr[...] = s_new
    sout_ref[...] = s_new
    mu = _head_sum(y, HEAD_DIM) * (1.0 / HEAD_DIM)
    d = y - mu
    var = _head_sum(d * d, HEAD_DIM) * (1.0 / HEAD_DIM)
    hn = d * lax.rsqrt(var + EPS)
    h_ref[...] = _bf(hn * (g * _sigmoid(g)))


def _retention_tables(chunk):
    log_g = jnp.log1p(-jnp.exp2(-5.0 - jnp.arange(N_HEADS, dtype=F32)))
    idx = jnp.arange(chunk, dtype=F32)
    dist = jnp.abs(idx[:, None] - idx[None, :])
    sub = jnp.arange(chunk) // CHUNK
    vis = sub[None, :] <= sub[:, None]
    dmask = jnp.where(vis[None], jnp.exp(log_g[:, None, None] * dist[None]), 0.0)
    lg_lane = jnp.repeat(log_g, HEAD_DIM)[None, :]
    qdec = jnp.exp(lg_lane * (idx[:, None] + 1.0))
    kdec = jnp.exp(lg_lane * (chunk - 1.0 - idx[:, None]))
    cdec = jnp.exp(lg_lane * chunk)
    return dmask.astype(F32), qdec, kdec, cdec


def _rope_tables(pos, dim, rot_dim, theta, n_rep):
    half = rot_dim // 2
    inv_freq = jnp.exp(-math.log(theta) * jnp.arange(half, dtype=F32) / half)
    ang = pos.astype(F32)[:, None] * inv_freq[None, :]
    t = pos.shape[0]
    cos = jnp.concatenate([jnp.cos(ang), jnp.cos(ang), jnp.ones((t, dim - rot_dim), F32)], axis=1)
    sin = jnp.concatenate([jnp.sin(ang), jnp.sin(ang), jnp.zeros((t, dim - rot_dim), F32)], axis=1)
    return jnp.tile(cos, (1, n_rep)), jnp.tile(sin, (1, n_rep))


def _rotated_columns(w, dim, rot_dim):
    half = rot_dim // 2
    k, n = w.shape
    wh = w.reshape(k, n // dim, dim)
    rot = jnp.concatenate([-wh[..., half:rot_dim], wh[..., :half], jnp.zeros_like(wh[..., rot_dim:])], axis=-1)
    return rot.reshape(k, n)


def _retention(x, scale, shift, w_ret, pos, s0_bd, chunk):
    b, t, _ = x.shape
    nc = t // chunk
    cos, sin = _rope_tables(pos, HEAD_DIM, HEAD_DIM, RET_THETA, N_HEADS)
    dmask, qdec, kdec, cdec = _retention_tables(chunk)
    w = BRANCH_W
    full = lambda shape: pl.BlockSpec(shape, lambda i, j: (0,) * len(shape))
    return pl.pallas_call(
        _ret_kernel,
        out_shape=(jax.ShapeDtypeStruct((b, t, w), BF16), jax.ShapeDtypeStruct((b, w, w), F32)),
        grid=(b, nc),
        in_specs=[
            pl.BlockSpec((None, chunk, D_MODEL), lambda i, j: (i, j, 0)),
            pl.BlockSpec((None, 1, D_MODEL), lambda i, j: (i, 0, 0)),
            pl.BlockSpec((None, 1, D_MODEL), lambda i, j: (i, 0, 0)),
            full((D_MODEL, 6 * w)),
            pl.BlockSpec((chunk, w), lambda i, j: (j, 0)),
            pl.BlockSpec((chunk, w), lambda i, j: (j, 0)),
            full((N_HEADS, chunk, chunk)),
            full((chunk, w)),
            full((chunk, w)),
            full((1, w)),
            pl.BlockSpec((None, w, w), lambda i, j: (i, 0, 0)),
        ],
        out_specs=(pl.BlockSpec((None, chunk, w), lambda i, j: (i, j, 0)),
                   pl.BlockSpec((None, w, w), lambda i, j: (i, 0, 0))),
        scratch_shapes=[pltpu.VMEM((w, w), F32)],
        compiler_params=_params(2),
        name="retention_mixer",
    )(x, scale, shift, w_ret, cos, sin, dmask, qdec, kdec, cdec, s0_bd)


def _conv_kernel(x_ref, sc_ref, sh_ref, w_ref, hist_ref, wc_ref, bc_ref, lg_ref, lb_ref,
                 h_ref, tail_ref, xp_scr):
    c = pl.program_id(1)
    rows = x_ref.shape[0]
    w = BRANCH_W
    pad = HIST_ROWS - (CONV_W - 1)

    @pl.when(c == 0)
    def _():
        xp_scr[0:HIST_ROWS, :] = hist_ref[...]

    u = x_ref[...] * (1.0 + sc_ref[...]) + sh_ref[...]
    p = _dot(_bf(u), w_ref[...])
    xp_scr[HIST_ROWS:HIST_ROWS + rows, :] = p[:, :w] * _sigmoid(p[:, w:])
    acc = jnp.zeros((rows, w), F32) + bc_ref[...]
    for j in range(CONV_W):
        acc = acc + xp_scr[pl.ds(pad + j, rows), :] * wc_ref[j:j + 1, :]
    y = _layer_norm_rows(acc, lg_ref[...], lb_ref[...])
    h_ref[...] = _bf(y * _sigmoid(y))
    tail = xp_scr[rows:rows + HIST_ROWS, :]
    tail_ref[...] = tail
    xp_scr[0:HIST_ROWS, :] = tail


def _conv_mixer(x, scale, shift, w_cv, hist, w_conv, b_conv, ln_g, ln_b, chunk):
    b, t, _ = x.shape
    nc = t // chunk
    w = BRANCH_W
    full = lambda shape: pl.BlockSpec(shape, lambda i, j: (0,) * len(shape))
    wc = jnp.concatenate([w_conv, jnp.zeros((HIST_ROWS - CONV_W, w), F32)], axis=0)
    return pl.pallas_call(
        _conv_kernel,
        out_shape=(jax.ShapeDtypeStruct((b, t, w), BF16), jax.ShapeDtypeStruct((b, HIST_ROWS, w), F32)),
        grid=(b, nc),
        in_specs=[
            pl.BlockSpec((None, chunk, D_MODEL), lambda i, j: (i, j, 0)),
            pl.BlockSpec((None, 1, D_MODEL), lambda i, j: (i, 0, 0)),
            pl.BlockSpec((None, 1, D_MODEL), lambda i, j: (i, 0, 0)),
            full((D_MODEL, 2 * w)),
            pl.BlockSpec((None, HIST_ROWS, w), lambda i, j: (i, 0, 0)),
            full((HIST_ROWS, w)),
            full((1, w)),
            full((1, w)),
            full((1, w)),
        ],
        out_specs=(pl.BlockSpec((None, chunk, w), lambda i, j: (i, j, 0)),
                   pl.BlockSpec((None, HIST_ROWS, w), lambda i, j: (i, 0, 0))),
        scratch_shapes=[pltpu.VMEM((HIST_ROWS + chunk, w), F32)],
        compiler_params=_params(2),
        name="conv_mixer",
    )(x, scale, shift, w_cv, hist, wc, b_conv.reshape(1, w), ln_g.reshape(1, w), ln_b.reshape(1, w))


def _fox_proj_kernel(x_ref, sc_ref, sh_ref, w_ref, bf_ref, q_ref, k_ref, v_ref, kb_ref, vb_ref, lf_ref):
    u = x_ref[...] * (1.0 + sc_ref[...]) + sh_ref[...]
    p = _dot(_bf(u), w_ref[...])
    w = BRANCH_W
    q_ref[...] = _bf(p[:, 0:w] * (HEAD_DIM ** -0.5))
    k = p[:, w:2 * w]
    v = p[:, 2 * w:3 * w]
    k_ref[...] = k
    v_ref[...] = v
    kb_ref[...] = _bf(k)
    vb_ref[...] = _bf(v)
    z = p[:, 3 * w:] + bf_ref[...]
    lf = jnp.minimum(z, 0.0) - jnp.log(1.0 + jnp.exp(-jnp.abs(z)))
    lane = lax.broadcasted_iota(jnp.int32, lf.shape, 1)
    lf_ref[...] = jnp.where(lane < N_HEADS, lf, 0.0)


def _fox_proj(x, scale, shift, w_fox, b_f, tm):
    b, t, _ = x.shape
    w = BRANCH_W
    nw = 3 * w + LANES
    bfp = jnp.concatenate([b_f.astype(F32), jnp.zeros((LANES - N_HEADS,), F32)]).reshape(1, LANES)
    row = lambda width: pl.BlockSpec((None, tm, width), lambda i, j: (i, j, 0))
    sds = lambda width, dt: jax.ShapeDtypeStruct((b, t, width), dt)
    return pl.pallas_call(
        _fox_proj_kernel,
        out_shape=(sds(w, BF16), sds(w, F32), sds(w, F32), sds(w, BF16), sds(w, BF16), sds(LANES, F32)),
        grid=(b, t // tm),
        in_specs=[
            row(D_MODEL),
            pl.BlockSpec((None, 1, D_MODEL), lambda i, j: (i, 0, 0)),
            pl.BlockSpec((None, 1, D_MODEL), lambda i, j: (i, 0, 0)),
            pl.BlockSpec((D_MODEL, nw), lambda i, j: (0, 0)),
            pl.BlockSpec((1, LANES), lambda i, j: (0, 0)),
        ],
        out_specs=(row(w), row(w), row(w), row(w), row(w), row(LANES)),
        compiler_params=_params(2),
        name="fox_projection",
    )(x, scale, shift, w_fox, bfp)


def _cumsum_kernel(lf_ref, col_ref, row_ref, carry_scr):
    j = pl.program_id(1)

    @pl.when(j == 0)
    def _():
        carry_scr[...] = jnp.zeros_like(carry_scr)

    x = lf_ref[...]
    n = x.shape[0]
    r = lax.broadcasted_iota(jnp.int32, (n, n), 0)
    c = lax.broadcasted_iota(jnp.int32, (n, n), 1)
    tri = jnp.where(c <= r, 1.0, 0.0).astype(BF16)
    hi = _bf(x)
    r1 = x - hi.astype(F32)
    mid = _bf(r1)
    lo = _bf(r1 - mid.astype(F32))
    cs = _dot(tri, hi) + _dot(tri, mid) + _dot(tri, lo) + carry_scr[0:1, :]
    col_ref[...] = cs
    row_ref[...] = cs.T[0:8, :]
    carry_scr[...] = jnp.broadcast_to(cs[n - 1:n, :], carry_scr.shape)


def _logf_cumsum(lf):
    b, tk, _ = lf.shape
    nb = tk // KV_BLOCK
    return pl.pallas_call(
        _cumsum_kernel,
        out_shape=(jax.ShapeDtypeStruct((b, tk, LANES), F32),
                   jax.ShapeDtypeStruct((b, nb, 8, KV_BLOCK), F32)),
        grid=(b, nb),
        in_specs=[pl.BlockSpec((None, KV_BLOCK, LANES), lambda i, j: (i, j, 0))],
        out_specs=(pl.BlockSpec((None, KV_BLOCK, LANES), lambda i, j: (i, j, 0)),
                   pl.BlockSpec((None, None, 8, KV_BLOCK), lambda i, j: (i, j, 0, 0))),
        scratch_shapes=[pltpu.VMEM((8, LANES), F32)],
        compiler_params=_params(2),
        name="logf_cumsum",
    )(lf)


def _attn_kernel(*refs, fox, off, tk_valid, tq, lam_init):
    if fox:
        q_ref, k_ref, v_ref, fc_ref, fr_ref, o_ref, vt_scr, qm_scr, mb_scr, m_scr, r_scr, acc_scr, s_scr = refs
    else:
        q_ref, k_ref, v_ref, lam_ref, g_ref, o_ref, vt_scr, qm_scr, mb_scr, m_scr, r_scr, acc_scr, s_scr = refs
    i = pl.program_id(1)
    w = BRANCH_W
    n_sub, tqp, _ = qm_scr.shape
    sub_w = w // n_sub
    subs_per_head = n_sub // N_HEADS
    nb = vt_scr.shape[0]
    hd = HEAD_DIM
    c = LOG2E if fox else (DIFF_SUB ** -0.5) * LOG2E
    q0 = off + i * tq
    nfull = q0 // KV_BLOCK

    @pl.when(i == 0)
    def _():
        for jb in range(nb):
            vt = v_ref[jb * KV_BLOCK:(jb + 1) * KV_BLOCK, :].T
            for h in range(N_HEADS):
                vt_scr[jb, h, 0:hd, :] = vt[h * hd:(h + 1) * hd, :]
                vt_scr[jb, h, hd:, :] = jnp.ones((vt_scr.shape[2] - hd, KV_BLOCK), BF16)
        kpos = lax.broadcasted_iota(jnp.int32, (KV_BLOCK, tqp), 0)
        qpos = lax.broadcasted_iota(jnp.int32, (KV_BLOCK, tqp), 1)
        if fox:
            vis = kpos <= qpos
        else:
            vis = (kpos // CHUNK) <= (qpos // CHUNK)
        vis = vis & (kpos < tk_valid - nfull * KV_BLOCK)
        mb_scr[...] = jnp.where(vis, 0.0, NEG)

    q = q_ref[...]
    lane_sub = lax.broadcasted_iota(jnp.int32, (1, w), 1) // sub_w
    if tqp > tq:
        qm_scr[...] = jnp.zeros_like(qm_scr)
    for n in range(n_sub):
        qm_scr[n, 0:tq, :] = jnp.where(lane_sub == n, q, jnp.zeros_like(q))
    m_scr[...] = jnp.full(m_scr.shape, NEG, F32)
    r_scr[...] = jnp.full(r_scr.shape, NEG, F32)
    acc_scr[...] = jnp.zeros_like(acc_scr)

    def scores(j, n, masked):
        start = pl.multiple_of(j * KV_BLOCK, KV_BLOCK)
        h = n // subs_per_head
        s = _dot_nt(k_ref[pl.ds(start, KV_BLOCK), :], qm_scr[n])
        if fox:
            s = s + (fr_ref[h:h + 1, 0:tqp] - fc_ref[pl.ds(start, KV_BLOCK), h:h + 1])
        if masked:
            s = s + mb_scr[...]
        s_scr[n] = s
        m_scr[n] = jnp.maximum(m_scr[n], jnp.max(s, axis=0, keepdims=True))

    def values(j, n):
        h = n // subs_per_head
        m = m_scr[n]
        alpha = jnp.exp2((r_scr[n] - m) * c)
        r_scr[n] = m
        p = jnp.exp2((s_scr[n] - m) * c)
        acc_scr[n] = alpha * acc_scr[n] + _dot(vt_scr[j, h], _bf(p))

    @pl.when(nfull == 0)
    def _():
        for n in range(n_sub):
            scores(0, n, True)

    @pl.when(nfull > 0)
    def _():
        for n in range(n_sub):
            scores(0, n, False)

    def body(j, carry):
        for n in range(n_sub):
            values(j - 1, n)
            scores(j, n, False)
        return carry

    lax.fori_loop(1, nfull, body, 0)

    @pl.when(nfull > 0)
    def _():
        for n in range(n_sub):
            values(nfull - 1, n)
            scores(nfull, n, True)

    for n in range(n_sub):
        values(nfull, n)

    def normalised(n):
        acc = acc_scr[n]
        return acc[0:hd] / acc[hd:hd + 1]

    if fox:
        out_t = jnp.concatenate([normalised(h) for h in range(N_HEADS)], axis=0)
        o_ref[...] = _bf(out_t.T[0:tq, :])
    else:
        lamv = lam_ref[...]
        lam = (jnp.exp(jnp.sum(lamv[0:1] * lamv[1:2], axis=-1, keepdims=True))
               - jnp.exp(jnp.sum(lamv[2:3] * lamv[3:4], axis=-1, keepdims=True)) + lam_init)
        parts = [normalised(2 * h) - lam * normalised(2 * h + 1) for h in range(N_HEADS)]
        dy = jnp.concatenate(parts, axis=0).T[0:tq, :]
        ms = _head_sum(dy * dy, HEAD_DIM) * (1.0 / HEAD_DIM)
        o_ref[...] = _bf(dy * lax.rsqrt(ms + EPS) * g_ref[...] * (1.0 - lam_init))


def _attention(q, kb, vb, extras, *, fox, tq, off, tk_valid, lam_init=0.0):
    b, t, w = q.shape
    tk = kb.shape[1]
    nb = tk // KV_BLOCK
    tqp = max(tq, LANES)
    n_sub = N_HEADS if fox else N_SUB
    assert off % KV_BLOCK == 0 and (tq == KV_BLOCK or t == tq)
    kernel = functools.partial(_attn_kernel, fox=fox, off=off, tk_valid=tk_valid, tq=tq, lam_init=lam_init)
    in_specs = [
        pl.BlockSpec((None, tq, w), lambda i, j: (i, j, 0)),
        pl.BlockSpec((None, tk, w), lambda i, j: (i, 0, 0)),
        pl.BlockSpec((None, tk, w), lambda i, j: (i, 0, 0)),
    ]
    if fox:
        in_specs += [
            pl.BlockSpec((None, tk, LANES), lambda i, j: (i, 0, 0)),
            pl.BlockSpec((None, None, 8, KV_BLOCK), lambda i, j: (i, (off + j * tq) // KV_BLOCK, 0, 0)),
        ]
    else:
        in_specs += [
            pl.BlockSpec((4, DIFF_SUB), lambda i, j: (0, 0)),
            pl.BlockSpec((1, w), lambda i, j: (0, 0)),
        ]
    return pl.pallas_call(
        kernel,
        out_shape=jax.ShapeDtypeStruct((b, t, w), BF16),
        grid=(b, t // tq),
        in_specs=in_specs,
        out_specs=pl.BlockSpec((None, tq, w), lambda i, j: (i, j, 0)),
        scratch_shapes=[
            pltpu.VMEM((nb, N_HEADS, HEAD_DIM + BF16_ROWS, KV_BLOCK), BF16),
            pltpu.VMEM((n_sub, tqp, w), BF16),
            pltpu.VMEM((KV_BLOCK, tqp), F32),
            pltpu.VMEM((n_sub, 1, tqp), F32),
            pltpu.VMEM((n_sub, 1, tqp), F32),
            pltpu.VMEM((n_sub, HEAD_DIM + BF16_ROWS, tqp), F32),
            pltpu.VMEM((n_sub, KV_BLOCK, tqp), F32),
        ],
        compiler_params=_params(2),
        name="fox_attention" if fox else "diff_attention",
    )(q, kb, vb, *extras)


def _fox_attention(q, kb, vb, fcol, frow, tq, off, tk_valid):
    return _attention(q, kb, vb, (fcol, frow), fox=True, tq=tq, off=off, tk_valid=tk_valid)


def _diff_proj_kernel(x_ref, sc_ref, sh_ref, w_ref, cos_ref, sin_ref, q_ref, k_ref, v_ref, kb_ref, vb_ref):
    u = x_ref[...] * (1.0 + sc_ref[...]) + sh_ref[...]
    p = _dot(_bf(u), w_ref[...])
    w = BRANCH_W
    cos = cos_ref[...]
    sin = sin_ref[...]
    q_ref[...] = _bf(p[:, 0:w] * cos + p[:, w:2 * w] * sin)
    k = p[:, 2 * w:3 * w] * cos + p[:, 3 * w:4 * w] * sin
    v = p[:, 4 * w:5 * w]
    k_ref[...] = k
    v_ref[...] = v
    kb_ref[...] = _bf(k)
    vb_ref[...] = _bf(v)


def _diff_proj(x, scale, shift, w_diff, pos, tm):
    b, t, _ = x.shape
    w = BRANCH_W
    cos, sin = _rope_tables(pos, DIFF_SUB, ROT_DIM, ROPE_THETA, N_SUB)
    row = lambda width: pl.BlockSpec((None, tm, width), lambda i, j: (i, j, 0))
    sds = lambda dt: jax.ShapeDtypeStruct((b, t, w), dt)
    return pl.pallas_call(
        _diff_proj_kernel,
        out_shape=(sds(BF16), sds(F32), sds(F32), sds(BF16), sds(BF16)),
        grid=(b, t // tm),
        in_specs=[
            row(D_MODEL),
            pl.BlockSpec((None, 1, D_MODEL), lambda i, j: (i, 0, 0)),
            pl.BlockSpec((None, 1, D_MODEL), lambda i, j: (i, 0, 0)),
            pl.BlockSpec((D_MODEL, 5 * w), lambda i, j: (0, 0)),
            pl.BlockSpec((tm, w), lambda i, j: (j, 0)),
            pl.BlockSpec((tm, w), lambda i, j: (j, 0)),
        ],
        out_specs=(row(w), row(w), row(w), row(w), row(w)),
        compiler_params=_params(2),
        name="diff_projection",
    )(x, scale, shift, w_diff, cos, sin)


def _diff_attention(q, kb, vb, diff_lambda, subln_g, tq, off, tk_valid, lam_init):
    g = jnp.tile(subln_g.astype(F32), N_HEADS).reshape(1, BRANCH_W)
    return _attention(q, kb, vb, (diff_lambda.astype(F32), g), fox=False, tq=tq, off=off,
                      tk_valid=tk_valid, lam_init=lam_init)


ROUTE_W1, ROUTE_W2, ROUTE_I1, ROUTE_I2 = 8, 9, 10, 11


def _top2_route(logits):
    lane = lax.broadcasted_iota(jnp.int32, logits.shape, 1).astype(F32)
    lg = jnp.where(lane < N_EXPERTS, logits, -jnp.inf)
    v1 = jnp.max(lg, axis=-1, keepdims=True)
    i1 = jnp.min(jnp.where(lg == v1, lane, float(LANES)), axis=-1, keepdims=True)
    lg2 = jnp.where(lane == i1, -jnp.inf, lg)
    v2 = jnp.max(lg2, axis=-1, keepdims=True)
    i2 = jnp.min(jnp.where(lg2 == v2, lane, float(LANES)), axis=-1, keepdims=True)
    e2 = jnp.exp(v2 - v1)
    w1 = 1.0 / (1.0 + e2)
    w2 = e2 / (1.0 + e2)
    rec = jnp.where(lane == i1, w1, 0.0) + jnp.where(lane == i2, w2, 0.0)
    for slot, val in ((ROUTE_W1, w1), (ROUTE_W2, w2), (ROUTE_I1, i1), (ROUTE_I2, i2)):
        rec = rec + jnp.where(lane == float(slot), val, 0.0)
    return rec


def _merge_kernel(*refs, with_router):
    if with_router:
        (x_ref, sc1_ref, sh1_ref, g1_ref, sc2_ref, sh2_ref, hr_ref, hf_ref, hc_ref, hd_ref,
         wg_ref, wb_ref, wo_ref, lg_ref, lb_ref, wr_ref, br_ref, x1_ref, u2_ref, rcol_ref, rrow_ref) = refs
    else:
        (x_ref, sc1_ref, sh1_ref, g1_ref, sc2_ref, sh2_ref, hr_ref, hf_ref, hc_ref, hd_ref,
         wg_ref, wb_ref, wo_ref, lg_ref, lb_ref, x1_ref, u2_ref) = refs
    x = x_ref[...]
    u = _bf(x * (1.0 + sc1_ref[...]) + sh1_ref[...])
    merged = None
    for n, h_ref in enumerate((hr_ref, hf_ref, hc_ref, hd_ref)):
        gate = _dot(u, wg_ref[:, n * D_MODEL:(n + 1) * D_MODEL])
        term = _sigmoid(gate) * _dot(h_ref[...], wb_ref[n])
        merged = term if merged is None else merged + term
    mix = _dot(_bf(merged), wo_ref[...])
    x1 = _layer_norm_rows(ALPHA * x + g1_ref[...] * mix, lg_ref[...], lb_ref[...])
    x1_ref[...] = x1
    u2 = x1 * (1.0 + sc2_ref[...]) + sh2_ref[...]
    u2_ref[...] = u2.astype(u2_ref.dtype)
    if with_router:
        rec = _top2_route(_dot(_bf(u2), wr_ref[...]) + br_ref[...])
        rcol_ref[...] = rec
        rrow_ref[...] = rec.T[8:16, :]


def _merge(x2d, mods, branches, w_gate, w_branch, w_out, ln_g, ln_b, router, tm, tiles_per_group):
    n = x2d.shape[0]
    r = mods[0].shape[1]
    w = BRANCH_W
    with_router = router is not None
    row = lambda width: pl.BlockSpec((tm, width), lambda i: (i, 0))
    mod_spec = pl.BlockSpec((None, r, D_MODEL), lambda i: (i // tiles_per_group, 0, 0))
    full = lambda shape: pl.BlockSpec(shape, lambda i: (0,) * len(shape))
    in_specs = ([row(D_MODEL)] + [mod_spec] * 5 + [row(w)] * 4
                + [full((D_MODEL, 4 * D_MODEL)), full((4, w, D_MODEL)), full((D_MODEL, D_MODEL)),
                   full((1, D_MODEL)), full((1, D_MODEL))])
    args = [x2d, *mods, *branches, w_gate, w_branch, w_out, ln_g.reshape(1, D_MODEL), ln_b.reshape(1, D_MODEL)]
    out_shape = [jax.ShapeDtypeStruct((n, D_MODEL), F32),
                 jax.ShapeDtypeStruct((n, D_MODEL), F32 if with_router else BF16)]
    out_specs = [row(D_MODEL), row(D_MODEL)]
    if with_router:
        in_specs += [full((D_MODEL, LANES)), full((1, LANES))]
        args += list(router)
        out_shape += [jax.ShapeDtypeStruct((n, LANES), F32), jax.ShapeDtypeStruct((n // tm, 8, tm), F32)]
        out_specs += [row(LANES), pl.BlockSpec((None, 8, tm), lambda i: (i, 0, 0))]
    return pl.pallas_call(
        functools.partial(_merge_kernel, with_router=with_router),
        out_shape=tuple(out_shape),
        grid=(n // tm,),
        in_specs=in_specs,
        out_specs=tuple(out_specs),
        compiler_params=_params(1),
        name="merge_outproj_ln",
    )(*args)


def _ffn_kernel(u_ref, x1_ref, g2_ref, wa_ref, wg_ref, wd_ref, lg_ref, lb_ref, o_ref, acc_scr):
    j = pl.program_id(1)

    @pl.when(j == 0)
    def _():
        acc_scr[...] = jnp.zeros_like(acc_scr)

    u = u_ref[...]
    a = _dot(u, wa_ref[...])
    g = _dot(u, wg_ref[...])
    acc_scr[...] += _dot(_bf(a * _sigmoid(a) * g), wd_ref[...])

    @pl.when(j == pl.num_programs(1) - 1)
    def _():
        z = ALPHA * x1_ref[...] + g2_ref[...] * acc_scr[...]
        o_ref[...] = _layer_norm_rows(z, lg_ref[...], lb_ref[...])


def _ffn_dense(u2, x1, gate2, w_up, w_down, ln_g, ln_b, tm, tiles_per_group, th):
    n = u2.shape[0]
    r = gate2.shape[1]
    nh = D_FF // th
    return pl.pallas_call(
        _ffn_kernel,
        out_shape=jax.ShapeDtypeStruct((n, D_MODEL), F32),
        grid=(n // tm, nh),
        in_specs=[
            pl.BlockSpec((tm, D_MODEL), lambda i, j: (i, 0)),
            pl.BlockSpec((tm, D_MODEL), lambda i, j: (i, 0)),
            pl.BlockSpec((None, r, D_MODEL), lambda i, j: (i // tiles_per_group, 0, 0)),
            pl.BlockSpec((D_MODEL, th), lambda i, j: (0, j)),
            pl.BlockSpec((D_MODEL, th), lambda i, j: (0, nh + j)),
            pl.BlockSpec((th, D_MODEL), lambda i, j: (j, 0)),
            pl.BlockSpec((1, D_MODEL), lambda i, j: (0, 0)),
            pl.BlockSpec((1, D_MODEL), lambda i, j: (0, 0)),
        ],
        out_specs=pl.BlockSpec((tm, D_MODEL), lambda i, j: (i, 0)),
        scratch_shapes=[pltpu.VMEM((tm, D_MODEL), F32)],
        compiler_params=_params(2),
        name="ffn_dense",
    )(u2, x1, gate2, w_up, w_up, w_down, ln_g.reshape(1, D_MODEL), ln_b.reshape(1, D_MODEL))


def _moe_kernel(u_ref, x1_ref, g2_ref, cmb_ref, wa_ref, wg_ref, wd_ref, lg_ref, lb_ref, o_ref, acc_scr):
    e = pl.program_id(1)
    j = pl.program_id(2)

    @pl.when((e == 0) & (j == 0))
    def _():
        acc_scr[...] = jnp.zeros_like(acc_scr)

    u = u_ref[...]
    a = _dot(u, wa_ref[...])
    g = _dot(u, wg_ref[...])
    y = _dot(_bf(a * _sigmoid(a) * g), wd_ref[...])
    cmb = cmb_ref[...]
    lane = lax.broadcasted_iota(jnp.int32, cmb.shape, 1)
    ce = jnp.sum(jnp.where(lane == e, cmb, 0.0), axis=-1, keepdims=True)
    acc_scr[...] += ce * y

    @pl.when((e == pl.num_programs(1) - 1) & (j == pl.num_programs(2) - 1))
    def _():
        z = ALPHA * x1_ref[...] + g2_ref[...] * acc_scr[...]
        o_ref[...] = _layer_norm_rows(z, lg_ref[...], lb_ref[...])


def _ffn_experts(u2, x1, gate2, combine, w_in, w_out, ln_g, ln_b, tm, tiles_per_group, th):
    n = u2.shape[0]
    r = gate2.shape[1]
    nh = D_EXPERT // th
    return pl.pallas_call(
        _moe_kernel,
        out_shape=jax.ShapeDtypeStruct((n, D_MODEL), F32),
        grid=(n // tm, N_EXPERTS, nh),
        in_specs=[
            pl.BlockSpec((tm, D_MODEL), lambda i, e, j: (i, 0)),
            pl.BlockSpec((tm, D_MODEL), lambda i, e, j: (i, 0)),
            pl.BlockSpec((None, r, D_MODEL), lambda i, e, j: (i // tiles_per_group, 0, 0)),
            pl.BlockSpec((tm, LANES), lambda i, e, j: (i, 0)),
            pl.BlockSpec((None, D_MODEL, th), lambda i, e, j: (e, 0, j)),
            pl.BlockSpec((None, D_MODEL, th), lambda i, e, j: (e, 0, nh + j)),
            pl.BlockSpec((None, th, D_MODEL), lambda i, e, j: (e, j, 0)),
            pl.BlockSpec((1, D_MODEL), lambda i, e, j: (0, 0)),
            pl.BlockSpec((1, D_MODEL), lambda i, e, j: (0, 0)),
        ],
        out_specs=pl.BlockSpec((tm, D_MODEL), lambda i, e, j: (i, 0)),
        scratch_shapes=[pltpu.VMEM((tm, D_MODEL), F32)],
        compiler_params=_params(3),
        name="ffn_experts",
    )(u2, x1, gate2, combine, w_in, w_in, w_out, ln_g.reshape(1, D_MODEL), ln_b.reshape(1, D_MODEL))


EXPERT_TILE = 512


def _rank_kernel(rr_ref, rank_ref, cnt_ref, carry_scr):
    b = pl.program_id(0)

    @pl.when(b == 0)
    def _():
        carry_scr[...] = jnp.zeros_like(carry_scr)

    rr = rr_ref[...]
    tm = rr.shape[1]
    i1 = rr[2:3]
    i2 = rr[3:4]
    e = lax.broadcasted_iota(jnp.int32, (N_EXPERTS, tm), 0).astype(F32)
    oh1 = jnp.where(e == i1, 1.0, 0.0)
    oh2 = jnp.where(e == i2, 1.0, 0.0)
    sel = oh1 + oh2
    r = lax.broadcasted_iota(jnp.int32, (tm, tm), 0)
    c = lax.broadcasted_iota(jnp.int32, (tm, tm), 1)
    triu = jnp.where(r <= c, 1.0, 0.0).astype(BF16)
    csum = _dot(_bf(sel), triu)
    carry = carry_scr[:, 0:1]
    rank = csum - sel + carry
    r1 = jnp.sum(oh1 * rank, axis=0, keepdims=True)
    r2 = jnp.sum(oh2 * rank, axis=0, keepdims=True)
    rec = jnp.concatenate([i1, i2, r1, r2, jnp.zeros((4, tm), F32)], axis=0)
    rank_ref[...] = rec.astype(jnp.int32)
    total = carry + csum[:, tm - 1:tm]
    carry_scr[...] = jnp.broadcast_to(total, carry_scr.shape)
    cnt_ref[...] = jnp.broadcast_to(total, cnt_ref.shape).astype(jnp.int32)


def _route_ranks(route_row):
    nblk, _, tm = route_row.shape
    return pl.pallas_call(
        _rank_kernel,
        out_shape=(jax.ShapeDtypeStruct((nblk, 8, tm), jnp.int32),
                   jax.ShapeDtypeStruct((N_EXPERTS, LANES), jnp.int32)),
        grid=(nblk,),
        in_specs=[pl.BlockSpec((None, 8, tm), lambda b: (b, 0, 0))],
        out_specs=(pl.BlockSpec((None, 8, tm), lambda b: (b, 0, 0)),
                   pl.BlockSpec((N_EXPERTS, LANES), lambda b: (0, 0))),
        scratch_shapes=[pltpu.VMEM((N_EXPERTS, LANES), F32)],
        compiler_params=_params(1),
        name="route_ranks",
    )(route_row)


def _row_copy(src_ref, src_row, dst_ref, dst_row, sem):
    return pltpu.make_async_copy(src_ref.at[pl.ds(src_row, 1)], dst_ref.at[pl.ds(dst_row, 1)], sem)


def _dispatch_kernel(off_ref, rank_ref, u_ref, xs_in_ref, xs_ref, sem):
    del xs_in_ref
    tm = u_ref.shape[0]

    def start(r, carry):
        for k in range(2):
            slot = off_ref[rank_ref[k, r]] + rank_ref[2 + k, r]
            _row_copy(u_ref, r, xs_ref, slot, sem).start()
        return carry

    lax.fori_loop(0, tm, start, 0, unroll=8)

    def wait(r, carry):
        for k in range(2):
            _row_copy(u_ref, 0, xs_ref, 0, sem).wait()
        return carry

    lax.fori_loop(0, tm, wait, 0, unroll=8)


def _dispatch(u2, ranks, off, n_slots):
    n, d = u2.shape
    nblk, _, tm = ranks.shape
    grid_spec = pltpu.PrefetchScalarGridSpec(
        num_scalar_prefetch=1,
        grid=(nblk,),
        in_specs=[
            pl.BlockSpec((None, 8, tm), lambda b, off: (b, 0, 0), memory_space=pltpu.SMEM),
            pl.BlockSpec((tm, d), lambda b, off: (b, 0)),
            pl.BlockSpec(memory_space=pl.ANY),
        ],
        out_specs=pl.BlockSpec(memory_space=pl.ANY),
        scratch_shapes=[pltpu.SemaphoreType.DMA],
    )
    return pl.pallas_call(
        _dispatch_kernel,
        out_shape=jax.ShapeDtypeStruct((n_slots, d), F32),
        grid_spec=grid_spec,
        input_output_aliases={3: 0},
        compiler_params=_params(1),
        name="expert_dispatch",
    )(off, ranks, u2, jnp.zeros((n_slots, d), F32))


def _grouped_kernel(toff_ref, nt_ref, xs_ref, wa_ref, wg_ref, wd_ref, ys_ref, xb_scr, acc_scr):
    i = pl.program_id(0)
    j = pl.program_id(1)

    @pl.when(i < nt_ref[0])
    def _():
        @pl.when(j == 0)
        def _():
            xb_scr[...] = _bf(xs_ref[...])
            acc_scr[...] = jnp.zeros_like(acc_scr)

        u = xb_scr[...]
        a = _dot(u, wa_ref[...])
        g = _dot(u, wg_ref[...])
        acc_scr[...] += _dot(_bf(a * _sigmoid(a) * g), wd_ref[...])

        @pl.when(j == pl.num_programs(1) - 1)
        def _():
            ys_ref[...] = acc_scr[...]


def _grouped_experts(xs, tile_off, n_tiles, w_in, w_out, th):
    n_slots, d = xs.shape
    tm = EXPERT_TILE
    nh = D_EXPERT // th

    def expert(i, toff):
        e = 0
        for k in range(1, N_EXPERTS):
            e = e + (i >= toff[k]).astype(jnp.int32)
        return e

    def tile(i, nt):
        return jnp.minimum(i, nt[0] - 1)

    grid_spec = pltpu.PrefetchScalarGridSpec(
        num_scalar_prefetch=2,
        grid=(n_slots // tm, nh),
        in_specs=[
            pl.BlockSpec((tm, d), lambda i, j, toff, nt: (tile(i, nt), 0)),
            pl.BlockSpec((None, d, th), lambda i, j, toff, nt: (expert(tile(i, nt), toff), 0, j)),
            pl.BlockSpec((None, d, th), lambda i, j, toff, nt: (expert(tile(i, nt), toff), 0, nh + j)),
            pl.BlockSpec((None, th, d), lambda i, j, toff, nt: (expert(tile(i, nt), toff), j, 0)),
        ],
        out_specs=pl.BlockSpec((tm, d), lambda i, j, toff, nt: (tile(i, nt), 0)),
        scratch_shapes=[pltpu.VMEM((tm, d), BF16), pltpu.VMEM((tm, d), F32)],
    )
    return pl.pallas_call(
        _grouped_kernel,
        out_shape=jax.ShapeDtypeStruct((n_slots, d), F32),
        grid_spec=grid_spec,
        compiler_params=_params(2),
        name="grouped_experts",
    )(tile_off, n_tiles, xs, w_in, w_in, w_out)


def _combine_kernel(off_ref, rank_ref, ys_ref, x1_ref, g2_ref, rcol_ref, lg_ref, lb_ref, o_ref,
                    y1_scr, y2_scr, sem):
    tm = x1_ref.shape[0]
    bufs = (y1_scr, y2_scr)

    def start(r, carry):
        for k in range(2):
            slot = off_ref[rank_ref[k, r]] + rank_ref[2 + k, r]
            _row_copy(ys_ref, slot, bufs[k], r, sem).start()
        return carry

    lax.fori_loop(0, tm, start, 0, unroll=8)

    def wait(r, carry):
        for k in range(2):
            _row_copy(ys_ref, 0, bufs[k], 0, sem).wait()
        return carry

    lax.fori_loop(0, tm, wait, 0, unroll=8)
    rec = rcol_ref[...]
    f = rec[:, ROUTE_W1:ROUTE_W1 + 1] * y1_scr[...] + rec[:, ROUTE_W2:ROUTE_W2 + 1] * y2_scr[...]
    z = ALPHA * x1_ref[...] + g2_ref[...] * f
    o_ref[...] = _layer_norm_rows(z, lg_ref[...], lb_ref[...])


def _combine(ys, ranks, off, x1, gate2, route_col, ln_g, ln_b, tiles_per_group):
    n, d = x1.shape
    nblk, _, tm = ranks.shape
    r = gate2.shape[1]
    grid_spec = pltpu.PrefetchScalarGridSpec(
        num_scalar_prefetch=1,
        grid=(nblk,),
        in_specs=[
            pl.BlockSpec((None, 8, tm), lambda b, off: (b, 0, 0), memory_space=pltpu.SMEM),
            pl.BlockSpec(memory_space=pl.ANY),
            pl.BlockSpec((tm, d), lambda b, off: (b, 0)),
            pl.BlockSpec((None, r, d), lambda b, off: (b // tiles_per_group, 0, 0)),
            pl.BlockSpec((tm, LANES), lambda b, off: (b, 0)),
            pl.BlockSpec((1, d), lambda b, off: (0, 0)),
            pl.BlockSpec((1, d), lambda b, off: (0, 0)),
        ],
        out_specs=pl.BlockSpec((tm, d), lambda b, off: (b, 0)),
        scratch_shapes=[pltpu.VMEM((tm, d), F32), pltpu.VMEM((tm, d), F32), pltpu.SemaphoreType.DMA],
    )
    return pl.pallas_call(
        _combine_kernel,
        out_shape=jax.ShapeDtypeStruct((n, d), F32),
        grid_spec=grid_spec,
        compiler_params=_params(1),
        name="expert_combine_ln",
    )(off, ranks, ys, x1, gate2, route_col, ln_g.reshape(1, d), ln_b.reshape(1, d))


def _ffn_routed(u2, x1, gate2, route_col, route_row, w_in, w_out, ln_g, ln_b, tiles_per_group):
    n = u2.shape[0]
    tm = EXPERT_TILE
    assert route_row.shape[2] == tm
    ranks, counts = _route_ranks(route_row)
    cnt = counts[:, 0]
    padded = ((cnt + tm - 1) // tm) * tm
    off = jnp.cumsum(padded) - padded
    n_slots = 2 * n + N_EXPERTS * tm
    xs = _dispatch(u2, ranks, off.astype(jnp.int32), n_slots)
    ys = _grouped_experts(xs, (off // tm).astype(jnp.int32), (jnp.sum(padded) // tm).astype(jnp.int32).reshape(1),
                          w_in, w_out, 512)
    return _combine(ys, ranks, off.astype(jnp.int32), x1, gate2, route_col, ln_g, ln_b, tiles_per_group)


def _split_w_in(w_in_l):
    w = BRANCH_W
    col = lambda off, n: w_in_l[:, off:off + n]
    o = 0
    ret_q, ret_k, ret_v, ret_g = (col(o + i * w, w) for i in range(4)); o += 4 * w
    fox_q, fox_k, fox_v = (col(o + i * w, w) for i in range(3)); o += 3 * w
    fox_f = col(o, N_HEADS); o += N_HEADS
    conv = col(o, 2 * w); o += 2 * w
    diff_q, diff_k, diff_v = (col(o + i * w, w) for i in range(3)); o += 3 * w
    gate = col(o, 4 * D_MODEL)
    rot_r = lambda m: _rotated_columns(m, HEAD_DIM, HEAD_DIM)
    rot_d = lambda m: _rotated_columns(m, DIFF_SUB, ROT_DIM)
    w_ret = jnp.concatenate([ret_q, rot_r(ret_q), ret_k, rot_r(ret_k), ret_v, ret_g], axis=1)
    w_fox = jnp.concatenate([fox_q, fox_k, fox_v, fox_f, jnp.zeros((D_MODEL, LANES - N_HEADS), F32)], axis=1)
    w_diff = jnp.concatenate([diff_q, rot_d(diff_q), diff_k, rot_d(diff_k), diff_v], axis=1)
    return _bf(w_ret), _bf(w_fox), _bf(conv), _bf(w_diff), _bf(gate)


def _block_diag_state(s):
    b = s.shape[0]
    eye = jnp.eye(N_HEADS, dtype=s.dtype)
    return jnp.einsum('bhde,hg->bhdge', s, eye).reshape(b, BRANCH_W, BRANCH_W)


def _diag_blocks(s_bd):
    b = s_bd.shape[0]
    s4 = s_bd.reshape(b, N_HEADS, HEAD_DIM, N_HEADS, HEAD_DIM)
    return jnp.stack([s4[:, h, :, h, :] for h in range(N_HEADS)], axis=1)


def _pad_rows(a, rows):
    return jnp.concatenate([a, jnp.zeros((a.shape[0], rows - a.shape[1]) + a.shape[2:], a.dtype)], axis=1)


def _trunk_layer(x, mod, pos, past, l, lw, sample):
    b, t, _ = x.shape
    d = D_MODEL
    w = BRANCH_W
    shift1, scale1, gate1 = (mod[0][:, None, i * d:(i + 1) * d] for i in range(3))
    shift2, scale2, gate2 = (mod[1][:, None, i * d:(i + 1) * d] for i in range(3))
    lam_init = 0.8 - 0.6 * math.exp(-0.3 * l)
    chunk = min(t, KV_BLOCK)
    tm_proj = min(t, 1024)

    s0 = jnp.zeros((b, w, w), F32) if past is None else _block_diag_state(past[5])
    h_ret, s_bd = _retention(x, scale1, shift1, lw['w_ret'], pos, s0, chunk)
    ret_state = _diag_blocks(s_bd)

    if past is None:
        hist = jnp.zeros((b, HIST_ROWS, w), F32)
    else:
        hist = jnp.concatenate([jnp.zeros((b, HIST_ROWS - (CONV_W - 1), w), F32), past[6]], axis=1)
    h_conv, tail = _conv_mixer(x, scale1, shift1, lw['w_conv_in'], hist, lw['w_conv'], lw['b_conv'],
                               lw['conv_ln_g'], lw['conv_ln_b'], chunk)
    conv_buf = tail[:, HIST_ROWS - (CONV_W - 1):, :]

    fq, fk, fv, fkb, fvb, lf = _fox_proj(x, scale1, shift1, lw['w_fox'], lw['b_fox_f'], tm_proj)
    if past is None:
        off, tk_valid = 0, t
        k_all, v_all, lf_all = fkb, fvb, lf
    else:
        p_len = past[0].shape[1]
        off, tk_valid = p_len, p_len + t
        tk_pad = p_len + KV_BLOCK
        k_all = _pad_rows(jnp.concatenate([_bf(past[0].reshape(b, p_len, w)), fkb], axis=1), tk_pad)
        v_all = _pad_rows(jnp.concatenate([_bf(past[1].reshape(b, p_len, w)), fvb], axis=1), tk_pad)
        past_lf = jnp.concatenate([past[2].astype(F32), jnp.zeros((b, p_len, LANES - N_HEADS), F32)], axis=2)
        lf_all = _pad_rows(jnp.concatenate([past_lf, lf], axis=1), tk_pad)
    fcol, frow = _logf_cumsum(lf_all)
    tq = min(t, KV_BLOCK)
    h_fox = _fox_attention(fq, k_all, v_all, fcol, frow, tq, off, tk_valid)

    dq, dk, dv, dkb, dvb = _diff_proj(x, scale1, shift1, lw['w_diff'], pos, tm_proj)
    if past is None:
        dk_all, dv_all = dkb, dvb
    else:
        dk_all = _pad_rows(jnp.concatenate([_bf(past[3].reshape(b, p_len, w)), dkb], axis=1), tk_pad)
        dv_all = _pad_rows(jnp.concatenate([_bf(past[4].reshape(b, p_len, w)), dvb], axis=1), tk_pad)
    h_diff = _diff_attention(dq, dk_all, dv_all, lw['diff_lambda'], lw['diff_subln_g'], tq, off, tk_valid, lam_init)

    n = b * t
    if sample:
        tm_merge = tm_ffn = min(n, 256)
        tpg_merge = tpg_ffn = 1
        rows_mod = lambda m, tm: jnp.repeat(m, t, axis=1).reshape(n // tm, tm, d)
    else:
        tm_merge, tm_ffn = min(t, 512), min(t, 1024)
        tpg_merge, tpg_ffn = t // tm_merge, t // tm_ffn
        rows_mod = lambda m, tm: m
    mods = tuple(rows_mod(m, tm_merge) for m in (scale1, shift1, gate1, scale2, shift2))
    branches = tuple(h.reshape(n, w) for h in (h_ret, h_fox, h_conv, h_diff))
    merged = _merge(x.reshape(n, d), mods, branches, lw['w_gate'], lw['w_branch'], lw['w_out'],
                    lw['ln_g'][0], lw['ln_b'][0], lw.get('router'), tm_merge, tpg_merge)
    g2 = rows_mod(gate2, tm_ffn)
    if 'router' in lw and not sample:
        x1, u2, route_col, route_row = merged
        x2 = _ffn_routed(u2, x1, rows_mod(gate2, tm_merge), route_col, route_row, lw['w_exp_in'], lw['w_exp_out'],
                         lw['ln_g'][1], lw['ln_b'][1], tpg_merge)
    elif 'router' in lw:
        x1, u2, route_col, _ = merged
        x2 = _ffn_experts(_bf(u2), x1, g2, route_col, lw['w_exp_in'], lw['w_exp_out'], lw['ln_g'][1], lw['ln_b'][1],
                          tm_ffn, tpg_ffn, 512)
    else:
        x1, u2 = merged
        x2 = _ffn_dense(u2, x1, g2, lw['w_ffn_in'], lw['w_ffn_out'], lw['ln_g'][1], lw['ln_b'][1],
                        tm_ffn, tpg_ffn, 1408)
    heads = lambda a, nh: a.reshape(b, t, nh, w // nh)
    new_state = (heads(fk, N_HEADS), heads(fv, N_HEADS), lf[:, :, :N_HEADS], heads(dk, N_SUB), heads(dv, N_HEADS),
                 ret_state, conv_buf)
    return x2.reshape(b, t, d), new_state


def kernel(x_prompt, x_sample, c_prompt, c_sample, cache_fox_k, cache_fox_v, cache_fox_logf, cache_diff_k, cache_diff_v, state_ret, state_conv, w_in, b_fox_f, w_conv, b_conv, conv_ln_g, conv_ln_b, diff_lambda, diff_subln_g, w_branch, w_out, w_ada, b_ada, ln_g, ln_b, w_ffn_in, w_ffn_out, w_router, b_router, w_exp_in, w_exp_out):
    depth = w_in.shape[0]
    bp = x_prompt.shape[0]
    past_len = cache_fox_k.shape[2]
    pos_p = jnp.arange(x_prompt.shape[1], dtype=jnp.int32)
    pos_s = past_len + jnp.arange(x_sample.shape[1], dtype=jnp.int32)
    mod = _modulation(jnp.concatenate([c_prompt, c_sample], axis=0), w_ada, b_ada)
    yp, ys = x_prompt, x_sample
    new_p, new_s = [], []
    for l in range(depth):
        w_ret, w_fox, w_conv_in, w_diff, w_gate = _split_w_in(w_in[l])
        lw = dict(w_ret=w_ret, w_fox=w_fox, w_conv_in=w_conv_in, w_diff=w_diff, w_gate=w_gate,
                  b_fox_f=b_fox_f[l], w_conv=w_conv[l], b_conv=b_conv[l], conv_ln_g=conv_ln_g[l],
                  conv_ln_b=conv_ln_b[l], diff_lambda=diff_lambda[l], diff_subln_g=diff_subln_g[l],
                  w_branch=_bf(w_branch[l]), w_out=_bf(w_out[l]), ln_g=ln_g[l], ln_b=ln_b[l])
        if l % 2 == 0:
            lw['w_ffn_in'] = _bf(w_ffn_in[l // 2])
            lw['w_ffn_out'] = _bf(w_ffn_out[l // 2])
        else:
            wr = jnp.concatenate([w_router[l // 2], jnp.zeros((D_MODEL, LANES - N_EXPERTS), F32)], axis=1)
            br = jnp.concatenate([b_router[l // 2].astype(F32), jnp.zeros((LANES - N_EXPERTS,), F32)]).reshape(1, LANES)
            lw['router'] = (_bf(wr), br)
            lw['w_exp_in'] = _bf(w_exp_in[l // 2])
            lw['w_exp_out'] = _bf(w_exp_out[l // 2])
        past_l = (cache_fox_k[l], cache_fox_v[l], cache_fox_logf[l], cache_diff_k[l], cache_diff_v[l],
                  state_ret[l], state_conv[l])
        yp, st_p = _trunk_layer(yp, mod[l][:, :bp], pos_p, None, l, lw, sample=False)
        ys, st_s = _trunk_layer(ys, mod[l][:, bp:], pos_s, past_l, l, lw, sample=True)
        new_p.append(st_p)
        new_s.append(st_s)
    outs_p = tuple(jnp.stack(a) for a in zip(*new_p))
    outs_s = tuple(jnp.stack(a) for a in zip(*new_s))
    return (yp, ys) + outs_p + outs_s
```

```python
import functools
import math

import jax
import jax.numpy as jnp
from jax import lax
from jax.experimental import pallas as pl
from jax.experimental.pallas import tpu as pltpu

D_MODEL = 1024
BRANCH_W = 256
HEAD_DIM = 64
N_HEADS = 4
DIFF_SUB = 32
N_SUB = 8
ROT_DIM = DIFF_SUB // 4
RET_THETA = 10000.0
ROPE_THETA = 500000.0
CHUNK = 64
CONV_W = 31
D_FF = 2816
N_EXPERTS = 8
D_EXPERT = 3584
DEPTH = 2
ALPHA = (2.0 * DEPTH) ** 0.25
EPS = 1e-5
NEG = -1e30
LOG2E = math.log2(math.e)

LANES = 128
BF16_ROWS = 16
KV_BLOCK = 256
HIST_ROWS = 32
VMEM_LIMIT = 56 * 1024 * 1024

F32 = jnp.float32
BF16 = jnp.bfloat16


def _bf(x):
    return x.astype(BF16)


def _dot(a, b):
    return jnp.dot(a, b, preferred_element_type=F32)


def _dot_nt(a, b):
    return lax.dot_general(a, b, (((1,), (1,)), ((), ())), preferred_element_type=F32)


def _dot_tn(a, b):
    return lax.dot_general(a, b, (((0,), (0,)), ((), ())), preferred_element_type=F32)


def _sigmoid(x):
    return 1.0 / (1.0 + jnp.exp(-x))


def _params(n_axes):
    return pltpu.CompilerParams(dimension_semantics=("arbitrary",) * n_axes,
                                vmem_limit_bytes=VMEM_LIMIT)


def _head_sum(y, width):
    n = y.shape[-1]
    r = lax.broadcasted_iota(jnp.int32, (n, n), 0) // width
    c = lax.broadcasted_iota(jnp.int32, (n, n), 1) // width
    bd = jnp.where(r == c, 1.0, 0.0).astype(BF16)
    hi = _bf(y)
    lo = _bf(y - hi.astype(F32))
    return _dot(hi, bd) + _dot(lo, bd)


def _layer_norm_rows(z, g, b):
    mu = jnp.mean(z, axis=-1, keepdims=True)
    d = z - mu
    var = jnp.mean(d * d, axis=-1, keepdims=True)
    return d * lax.rsqrt(var + EPS) * g + b


def _mod_kernel(c_ref, w_ref, b_ref, o_ref):
    c = c_ref[...]
    sc = _bf(c * _sigmoid(c))
    o_ref[...] = _dot(sc, _bf(w_ref[...])) + b_ref[...]


def _modulation(c_all, w_ada, b_ada):
    rows = c_all.shape[0]
    depth = w_ada.shape[0]
    d3 = w_ada.shape[-1]
    nj = d3 // D_MODEL
    return pl.pallas_call(
        _mod_kernel,
        out_shape=jax.ShapeDtypeStruct((depth, 2, rows, d3), F32),
        grid=(depth * 2, nj),
        in_specs=[
            pl.BlockSpec((rows, D_MODEL), lambda i, j: (0, 0)),
            pl.BlockSpec((None, None, D_MODEL, D_MODEL), lambda i, j: (i // 2, i % 2, 0, j)),
            pl.BlockSpec((None, None, 1, D_MODEL), lambda i, j: (i // 2, i % 2, 0, j)),
        ],
        out_specs=pl.BlockSpec((None, None, rows, D_MODEL), lambda i, j: (i // 2, i % 2, 0, j)),
        compiler_params=_params(2),
        name="adaln_modulation",
    )(c_all, w_ada, b_ada.reshape(depth, 2, 1, d3))


def _ret_kernel(x_ref, sc_ref, sh_ref, w_ref, cos_ref, sin_ref, dmask_ref, qdec_ref, kdec_ref,
                cdec_ref, s0_ref, h_ref, sout_ref, s_scr):
    c = pl.program_id(1)

    @pl.when(c == 0)
    def _():
        s_scr[...] = s0_ref[...]

    u = x_ref[...] * (1.0 + sc_ref[...]) + sh_ref[...]
    p = _dot(_bf(u), w_ref[...])
    w = BRANCH_W
    cos = cos_ref[...]
    sin = sin_ref[...]
    q = p[:, 0:w] * cos + p[:, w:2 * w] * sin
    k = (p[:, 2 * w:3 * w] * cos + p[:, 3 * w:4 * w] * sin) * (HEAD_DIM ** -0.5)
    v = p[:, 4 * w:5 * w]
    g = p[:, 5 * w:6 * w]
    rows = q.shape[0]
    lane_head = lax.broadcasted_iota(jnp.int32, (1, w), 1) // HEAD_DIM
    kb = _bf(k)
    vb = _bf(v)
    y = jnp.zeros((rows, w), F32)
    for h in range(N_HEADS):
        mh = lane_head == h
        qh = _bf(jnp.where(mh, q, 0.0))
        a = _dot_nt(qh, kb) * dmask_ref[h]
        y = y + jnp.where(mh, _dot(_bf(a), vb), 0.0)
    s_prev = s_scr[...]
    y = y + _dot(_bf(q * qdec_ref[...]), _bf(s_prev))
    kv = _dot_tn(_bf(k * kdec_ref[...]), vb)
    r = lax.broadcasted_iota(jnp.int32, (w, w), 0) // HEAD_DIM
    cc = lax.broadcasted_iota(jnp.int32, (w, w), 1) // HEAD_DIM
    s_new = cdec_ref[...] * s_prev + jnp.where(r == cc, kv, 0.0)
    s_scr[...] = s_new
    sout_ref[...] = s_new
    mu = _head_sum(y, HEAD_DIM) * (1.0 / HEAD_DIM)
    d = y - mu
    var = _head_sum(d * d, HEAD_DIM) * (1.0 / HEAD_DIM)
    hn = d * lax.rsqrt(var + EPS)
    h_ref[...] = _bf(hn * (g * _sigmoid(g)))


def _retention_tables(chunk):
    log_g = jnp.log1p(-jnp.exp2(-5.0 - jnp.arange(N_HEADS, dtype=F32)))
    idx = jnp.arange(chunk, dtype=F32)
    dist = jnp.abs(idx[:, None] - idx[None, :])
    sub = jnp.arange(chunk) // CHUNK
    vis = sub[None, :] <= sub[:, None]
    dmask = jnp.where(vis[None], jnp.exp(log_g[:, None, None] * dist[None]), 0.0)
    lg_lane = jnp.repeat(log_g, HEAD_DIM)[None, :]
    qdec = jnp.exp(lg_lane * (idx[:, None] + 1.0))
    kdec = jnp.exp(lg_lane * (chunk - 1.0 - idx[:, None]))
    cdec = jnp.exp(lg_lane * chunk)
    return dmask.astype(F32), qdec, kdec, cdec


def _rope_tables(pos, dim, rot_dim, theta, n_rep):
    half = rot_dim // 2
    inv_freq = jnp.exp(-math.log(theta) * jnp.arange(half, dtype=F32) / half)
    ang = pos.astype(F32)[:, None] * inv_freq[None, :]
    t = pos.shape[0]
    cos = jnp.concatenate([jnp.cos(ang), jnp.cos(ang), jnp.ones((t, dim - rot_dim), F32)], axis=1)
    sin = jnp.concatenate([jnp.sin(ang), jnp.sin(ang), jnp.zeros((t, dim - rot_dim), F32)], axis=1)
    return jnp.tile(cos, (1, n_rep)), jnp.tile(sin, (1, n_rep))


def _rotated_columns(w, dim, rot_dim):
    half = rot_dim // 2
    k, n = w.shape
    wh = w.reshape(k, n // dim, dim)
    rot = jnp.concatenate([-wh[..., half:rot_dim], wh[..., :half], jnp.zeros_like(wh[..., rot_dim:])], axis=-1)
    return rot.reshape(k, n)


def _retention(x, scale, shift, w_ret, pos, s0_bd, chunk):
    b, t, _ = x.shape
    nc = t // chunk
    cos, sin = _rope_tables(pos, HEAD_DIM, HEAD_DIM, RET_THETA, N_HEADS)
    dmask, qdec, kdec, cdec = _retention_tables(chunk)
    w = BRANCH_W
    full = lambda shape: pl.BlockSpec(shape, lambda i, j: (0,) * len(shape))
    return pl.pallas_call(
        _ret_kernel,
        out_shape=(jax.ShapeDtypeStruct((b, t, w), BF16), jax.ShapeDtypeStruct((b, w, w), F32)),
        grid=(b, nc),
        in_specs=[
            pl.BlockSpec((None, chunk, D_MODEL), lambda i, j: (i, j, 0)),
            pl.BlockSpec((None, 1, D_MODEL), lambda i, j: (i, 0, 0)),
            pl.BlockSpec((None, 1, D_MODEL), lambda i, j: (i, 0, 0)),
            full((D_MODEL, 6 * w)),
            pl.BlockSpec((chunk, w), lambda i, j: (j, 0)),
            pl.BlockSpec((chunk, w), lambda i, j: (j, 0)),
            full((N_HEADS, chunk, chunk)),
            full((chunk, w)),
            full((chunk, w)),
            full((1, w)),
            pl.BlockSpec((None, w, w), lambda i, j: (i, 0, 0)),
        ],
        out_specs=(pl.BlockSpec((None, chunk, w), lambda i, j: (i, j, 0)),
                   pl.BlockSpec((None, w, w), lambda i, j: (i, 0, 0))),
        scratch_shapes=[pltpu.VMEM((w, w), F32)],
        compiler_params=_params(2),
        name="retention_mixer",
    )(x, scale, shift, w_ret, cos, sin, dmask, qdec, kdec, cdec, s0_bd)


def _conv_kernel(x_ref, sc_ref, sh_ref, w_ref, hist_ref, wc_ref, bc_ref, lg_ref, lb_ref,
                 h_ref, tail_ref, xp_scr):
    c = pl.program_id(1)
    rows = x_ref.shape[0]
    w = BRANCH_W
    pad = HIST_ROWS - (CONV_W - 1)

    @pl.when(c == 0)
    def _():
        xp_scr[0:HIST_ROWS, :] = hist_ref[...]

    u = x_ref[...] * (1.0 + sc_ref[...]) + sh_ref[...]
    p = _dot(_bf(u), w_ref[...])
    xp_scr[HIST_ROWS:HIST_ROWS + rows, :] = p[:, :w] * _sigmoid(p[:, w:])
    sub = 8
    acc = jnp.zeros((rows, w), F32) + bc_ref[...]
    for rho in range(sub):
        z = None
        for m in range((pad + CONV_W - 1) // sub + 1):
            j = sub * m + rho - pad
            if not 0 <= j < CONV_W:
                continue
            span = rows if rho == 0 else rows + sub
            term = xp_scr[sub * m:sub * m + span, :] * wc_ref[j:j + 1, :]
            z = term if z is None else z + term
        acc = acc + z[rho:rho + rows, :]
    y = _layer_norm_rows(acc, lg_ref[...], lb_ref[...])
    h_ref[...] = _bf(y * _sigmoid(y))
    tail = xp_scr[rows:rows + HIST_ROWS, :]
    tail_ref[...] = tail
    xp_scr[0:HIST_ROWS, :] = tail


def _conv_mixer(x, scale, shift, w_cv, hist, w_conv, b_conv, ln_g, ln_b, chunk):
    b, t, _ = x.shape
    nc = t // chunk
    w = BRANCH_W
    full = lambda shape: pl.BlockSpec(shape, lambda i, j: (0,) * len(shape))
    wc = jnp.concatenate([w_conv, jnp.zeros((HIST_ROWS - CONV_W, w), F32)], axis=0)
    return pl.pallas_call(
        _conv_kernel,
        out_shape=(jax.ShapeDtypeStruct((b, t, w), BF16), jax.ShapeDtypeStruct((b, HIST_ROWS, w), F32)),
        grid=(b, nc),
        in_specs=[
            pl.BlockSpec((None, chunk, D_MODEL), lambda i, j: (i, j, 0)),
            pl.BlockSpec((None, 1, D_MODEL), lambda i, j: (i, 0, 0)),
            pl.BlockSpec((None, 1, D_MODEL), lambda i, j: (i, 0, 0)),
            full((D_MODEL, 2 * w)),
            pl.BlockSpec((None, HIST_ROWS, w), lambda i, j: (i, 0, 0)),
            full((HIST_ROWS, w)),
            full((1, w)),
            full((1, w)),
            full((1, w)),
        ],
        out_specs=(pl.BlockSpec((None, chunk, w), lambda i, j: (i, j, 0)),
                   pl.BlockSpec((None, HIST_ROWS, w), lambda i, j: (i, 0, 0))),
        scratch_shapes=[pltpu.VMEM((HIST_ROWS + chunk, w), F32)],
        compiler_params=_params(2),
        name="conv_mixer",
    )(x, scale, shift, w_cv, hist, wc, b_conv.reshape(1, w), ln_g.reshape(1, w), ln_b.reshape(1, w))


def _fox_proj_kernel(x_ref, sc_ref, sh_ref, w_ref, bf_ref, q_ref, k_ref, v_ref, kb_ref, vb_ref, lf_ref):
    u = x_ref[...] * (1.0 + sc_ref[...]) + sh_ref[...]
    p = _dot(_bf(u), w_ref[...])
    w = BRANCH_W
    q_ref[...] = _bf(p[:, 0:w] * (HEAD_DIM ** -0.5 * LOG2E))
    k = p[:, w:2 * w]
    v = p[:, 2 * w:3 * w]
    k_ref[...] = k
    v_ref[...] = v
    kb_ref[...] = _bf(k)
    vb_ref[...] = _bf(v)
    z = p[:, 3 * w:] + bf_ref[...]
    lf = jnp.minimum(z, 0.0) - jnp.log(1.0 + jnp.exp(-jnp.abs(z)))
    lane = lax.broadcasted_iota(jnp.int32, lf.shape, 1)
    lf_ref[...] = jnp.where(lane < N_HEADS, lf, 0.0)


def _fox_proj(x, scale, shift, w_fox, b_f, tm):
    b, t, _ = x.shape
    w = BRANCH_W
    nw = 3 * w + LANES
    bfp = jnp.concatenate([b_f.astype(F32), jnp.zeros((LANES - N_HEADS,), F32)]).reshape(1, LANES)
    row = lambda width: pl.BlockSpec((None, tm, width), lambda i, j: (i, j, 0))
    sds = lambda width, dt: jax.ShapeDtypeStruct((b, t, width), dt)
    return pl.pallas_call(
        _fox_proj_kernel,
        out_shape=(sds(w, BF16), sds(w, F32), sds(w, F32), sds(w, BF16), sds(w, BF16), sds(LANES, F32)),
        grid=(b, t // tm),
        in_specs=[
            row(D_MODEL),
            pl.BlockSpec((None, 1, D_MODEL), lambda i, j: (i, 0, 0)),
            pl.BlockSpec((None, 1, D_MODEL), lambda i, j: (i, 0, 0)),
            pl.BlockSpec((D_MODEL, nw), lambda i, j: (0, 0)),
            pl.BlockSpec((1, LANES), lambda i, j: (0, 0)),
        ],
        out_specs=(row(w), row(w), row(w), row(w), row(w), row(LANES)),
        compiler_params=_params(2),
        name="fox_projection",
    )(x, scale, shift, w_fox, bfp)


def _cumsum_kernel(lf_ref, col_ref, row_ref):
    n = KV_BLOCK
    nb = lf_ref.shape[0] // n
    r = lax.broadcasted_iota(jnp.int32, (n, n), 0)
    c = lax.broadcasted_iota(jnp.int32, (n, n), 1)
    tri = jnp.where(c <= r, 1.0, 0.0).astype(BF16)
    carry = jnp.zeros((1, LANES), F32)
    for jb in range(nb):
        x = lf_ref[jb * n:(jb + 1) * n, :]
        hi = _bf(x)
        r1 = x - hi.astype(F32)
        mid = _bf(r1)
        lo = _bf(r1 - mid.astype(F32))
        cs = _dot(tri, hi) + _dot(tri, mid) + _dot(tri, lo) + carry
        carry = cs[n - 1:n, :]
        scaled = cs * LOG2E
        col_ref[jb * n:(jb + 1) * n, :] = scaled
        row_ref[jb] = scaled.T[0:8, :]


def _logf_cumsum(lf):
    b, tk, _ = lf.shape
    nb = tk // KV_BLOCK
    return pl.pallas_call(
        _cumsum_kernel,
        out_shape=(jax.ShapeDtypeStruct((b, tk, LANES), F32),
                   jax.ShapeDtypeStruct((b, nb, 8, KV_BLOCK), F32)),
        grid=(b,),
        in_specs=[pl.BlockSpec((None, tk, LANES), lambda i: (i, 0, 0))],
        out_specs=(pl.BlockSpec((None, tk, LANES), lambda i: (i, 0, 0)),
                   pl.BlockSpec((None, nb, 8, KV_BLOCK), lambda i: (i, 0, 0, 0))),
        compiler_params=_params(1),
        name="logf_cumsum",
    )(lf)


def _attn_kernel(*refs, fox, off, tk_valid, tq, lam_init):
    if fox:
        q_ref, k_ref, v_ref, fc_ref, fr_ref, o_ref, vt_scr, qm_scr, mb_scr, m_scr, r_scr, acc_scr, s_scr = refs
    else:
        q_ref, k_ref, v_ref, lam_ref, g_ref, o_ref, vt_scr, qm_scr, mb_scr, m_scr, r_scr, acc_scr, s_scr = refs
    i = pl.program_id(1)
    w = BRANCH_W
    n_sub, tqp, _ = qm_scr.shape
    sub_w = w // n_sub
    subs_per_head = n_sub // N_HEADS
    nb = vt_scr.shape[0]
    hd = HEAD_DIM
    q0 = off + i * tq
    nfull = q0 // KV_BLOCK

    @pl.when(i == 0)
    def _():
        for jb in range(nb):
            vt = v_ref[jb * KV_BLOCK:(jb + 1) * KV_BLOCK, :].T
            for h in range(N_HEADS):
                vt_scr[jb, h, 0:hd, :] = vt[h * hd:(h + 1) * hd, :]
                vt_scr[jb, h, hd:, :] = jnp.ones((vt_scr.shape[2] - hd, KV_BLOCK), BF16)
        kpos = lax.broadcasted_iota(jnp.int32, (KV_BLOCK, tqp), 0)
        qpos = lax.broadcasted_iota(jnp.int32, (KV_BLOCK, tqp), 1)
        if fox:
            vis = kpos <= qpos
        else:
            vis = (kpos // CHUNK) <= (qpos // CHUNK)
        vis = vis & (kpos < tk_valid - nfull * KV_BLOCK)
        mb_scr[...] = jnp.where(vis, 0.0, NEG)

    q = q_ref[...]
    lane_sub = lax.broadcasted_iota(jnp.int32, (1, w), 1) // sub_w
    if tqp > tq:
        qm_scr[...] = jnp.zeros_like(qm_scr)
    for n in range(n_sub):
        qm_scr[n, 0:tq, :] = jnp.where(lane_sub == n, q, jnp.zeros_like(q))
    m_scr[...] = jnp.full(m_scr.shape, NEG, F32)
    r_scr[...] = jnp.full(r_scr.shape, NEG, F32)
    acc_scr[...] = jnp.zeros_like(acc_scr)

    def scores(j, n, masked):
        start = pl.multiple_of(j * KV_BLOCK, KV_BLOCK)
        h = n // subs_per_head
        s = _dot_nt(k_ref[pl.ds(start, KV_BLOCK), :], qm_scr[n])
        if fox:
            s = s + (fr_ref[h:h + 1, 0:tqp] - fc_ref[pl.ds(start, KV_BLOCK), h:h + 1])
        if masked:
            s = s + mb_scr[...]
        s_scr[n] = s
        m_scr[n] = jnp.maximum(m_scr[n], jnp.max(s, axis=0, keepdims=True))

    def values(j, n):
        h = n // subs_per_head
        m = m_scr[n]
        alpha = jnp.exp2(r_scr[n] - m)
        r_scr[n] = m
        p = jnp.exp2(s_scr[n] - m)
        acc_scr[n] = alpha * acc_scr[n] + _dot(vt_scr[j, h], _bf(p))

    @pl.when(nfull == 0)
    def _():
        for n in range(n_sub):
            scores(0, n, True)

    @pl.when(nfull > 0)
    def _():
        for n in range(n_sub):
            scores(0, n, False)

    def body(j, carry):
        for n in range(n_sub):
            values(j - 1, n)
            scores(j, n, False)
        return carry

    lax.fori_loop(1, nfull, body, 0)

    @pl.when(nfull > 0)
    def _():
        for n in range(n_sub):
            values(nfull - 1, n)
            scores(nfull, n, True)

    for n in range(n_sub):
        values(nfull, n)

    def normalised(n):
        acc = acc_scr[n]
        return acc[0:hd] / acc[hd:hd + 1]

    if fox:
        out_t = jnp.concatenate([normalised(h) for h in range(N_HEADS)], axis=0)
        o_ref[...] = _bf(out_t.T[0:tq, :])
    else:
        lamv = lam_ref[...]
        lam = (jnp.exp(jnp.sum(lamv[0:1] * lamv[1:2], axis=-1, keepdims=True))
               - jnp.exp(jnp.sum(lamv[2:3] * lamv[3:4], axis=-1, keepdims=True)) + lam_init)
        parts = [normalised(2 * h) - lam * normalised(2 * h + 1) for h in range(N_HEADS)]
        dy = jnp.concatenate(parts, axis=0).T[0:tq, :]
        ms = _head_sum(dy * dy, HEAD_DIM) * (1.0 / HEAD_DIM)
        o_ref[...] = _bf(dy * lax.rsqrt(ms + EPS) * g_ref[...] * (1.0 - lam_init))


def _attention(q, kb, vb, extras, *, fox, tq, off, tk_valid, lam_init=0.0):
    b, t, w = q.shape
    tk = kb.shape[1]
    nb = tk // KV_BLOCK
    tqp = max(tq, LANES)
    n_sub = N_HEADS if fox else N_SUB
    assert off % KV_BLOCK == 0 and (tq == KV_BLOCK or t == tq)
    kernel = functools.partial(_attn_kernel, fox=fox, off=off, tk_valid=tk_valid, tq=tq, lam_init=lam_init)
    in_specs = [
        pl.BlockSpec((None, tq, w), lambda i, j: (i, j, 0)),
        pl.BlockSpec((None, tk, w), lambda i, j: (i, 0, 0)),
        pl.BlockSpec((None, tk, w), lambda i, j: (i, 0, 0)),
    ]
    if fox:
        in_specs += [
            pl.BlockSpec((None, tk, LANES), lambda i, j: (i, 0, 0)),
            pl.BlockSpec((None, None, 8, KV_BLOCK), lambda i, j: (i, (off + j * tq) // KV_BLOCK, 0, 0)),
        ]
    else:
        in_specs += [
            pl.BlockSpec((4, DIFF_SUB), lambda i, j: (0, 0)),
            pl.BlockSpec((1, w), lambda i, j: (0, 0)),
        ]
    return pl.pallas_call(
        kernel,
        out_shape=jax.ShapeDtypeStruct((b, t, w), BF16),
        grid=(b, t // tq),
        in_specs=in_specs,
        out_specs=pl.BlockSpec((None, tq, w), lambda i, j: (i, j, 0)),
        scratch_shapes=[
            pltpu.VMEM((nb, N_HEADS, HEAD_DIM + BF16_ROWS, KV_BLOCK), BF16),
            pltpu.VMEM((n_sub, tqp, w), BF16),
            pltpu.VMEM((KV_BLOCK, tqp), F32),
            pltpu.VMEM((n_sub, 1, tqp), F32),
            pltpu.VMEM((n_sub, 1, tqp), F32),
            pltpu.VMEM((n_sub, HEAD_DIM + BF16_ROWS, tqp), F32),
            pltpu.VMEM((n_sub, KV_BLOCK, tqp), F32),
        ],
        compiler_params=_params(2),
        name="fox_attention" if fox else "diff_attention",
    )(q, kb, vb, *extras)


def _fox_attention(q, kb, vb, fcol, frow, tq, off, tk_valid):
    return _attention(q, kb, vb, (fcol, frow), fox=True, tq=tq, off=off, tk_valid=tk_valid)


def _diff_proj_kernel(x_ref, sc_ref, sh_ref, w_ref, cos_ref, sin_ref, q_ref, k_ref, v_ref, kb_ref, vb_ref):
    u = x_ref[...] * (1.0 + sc_ref[...]) + sh_ref[...]
    p = _dot(_bf(u), w_ref[...])
    w = BRANCH_W
    cos = cos_ref[...]
    sin = sin_ref[...]
    q_ref[...] = _bf((p[:, 0:w] * cos + p[:, w:2 * w] * sin) * (DIFF_SUB ** -0.5 * LOG2E))
    k = p[:, 2 * w:3 * w] * cos + p[:, 3 * w:4 * w] * sin
    v = p[:, 4 * w:5 * w]
    k_ref[...] = k
    v_ref[...] = v
    kb_ref[...] = _bf(k)
    vb_ref[...] = _bf(v)


def _diff_proj(x, scale, shift, w_diff, pos, tm):
    b, t, _ = x.shape
    w = BRANCH_W
    cos, sin = _rope_tables(pos, DIFF_SUB, ROT_DIM, ROPE_THETA, N_SUB)
    row = lambda width: pl.BlockSpec((None, tm, width), lambda i, j: (i, j, 0))
    sds = lambda dt: jax.ShapeDtypeStruct((b, t, w), dt)
    return pl.pallas_call(
        _diff_proj_kernel,
        out_shape=(sds(BF16), sds(F32), sds(F32), sds(BF16), sds(BF16)),
        grid=(b, t // tm),
        in_specs=[
            row(D_MODEL),
            pl.BlockSpec((None, 1, D_MODEL), lambda i, j: (i, 0, 0)),
            pl.BlockSpec((None, 1, D_MODEL), lambda i, j: (i, 0, 0)),
            pl.BlockSpec((D_MODEL, 5 * w), lambda i, j: (0, 0)),
            pl.BlockSpec((tm, w), lambda i, j: (j, 0)),
            pl.BlockSpec((tm, w), lambda i, j: (j, 0)),
        ],
        out_specs=(row(w), row(w), row(w), row(w), row(w)),
        compiler_params=_params(2),
        name="diff_projection",
    )(x, scale, shift, w_diff, cos, sin)


def _diff_attention(q, kb, vb, diff_lambda, subln_g, tq, off, tk_valid, lam_init):
    g = jnp.tile(subln_g.astype(F32), N_HEADS).reshape(1, BRANCH_W)
    return _attention(q, kb, vb, (diff_lambda.astype(F32), g), fox=False, tq=tq, off=off,
                      tk_valid=tk_valid, lam_init=lam_init)


ROUTE_W1, ROUTE_W2, ROUTE_I1, ROUTE_I2 = 8, 9, 10, 11


def _top2_route(logits):
    lane = lax.broadcasted_iota(jnp.int32, logits.shape, 1).astype(F32)
    lg = jnp.where(lane < N_EXPERTS, logits, -jnp.inf)
    v1 = jnp.max(lg, axis=-1, keepdims=True)
    i1 = jnp.min(jnp.where(lg == v1, lane, float(LANES)), axis=-1, keepdims=True)
    lg2 = jnp.where(lane == i1, -jnp.inf, lg)
    v2 = jnp.max(lg2, axis=-1, keepdims=True)
    i2 = jnp.min(jnp.where(lg2 == v2, lane, float(LANES)), axis=-1, keepdims=True)
    e2 = jnp.exp(v2 - v1)
    w1 = 1.0 / (1.0 + e2)
    w2 = e2 / (1.0 + e2)
    rec = jnp.where(lane == i1, w1, 0.0) + jnp.where(lane == i2, w2, 0.0)
    for slot, val in ((ROUTE_W1, w1), (ROUTE_W2, w2), (ROUTE_I1, i1), (ROUTE_I2, i2)):
        rec = rec + jnp.where(lane == float(slot), val, 0.0)
    return rec


def _merge_kernel(*refs, with_router):
    if with_router:
        (x_ref, sc1_ref, sh1_ref, g1_ref, sc2_ref, sh2_ref, hr_ref, hf_ref, hc_ref, hd_ref,
         wg_ref, wb_ref, wo_ref, lg_ref, lb_ref, wr_ref, br_ref, x1_ref, u2_ref, rcol_ref, rrow_ref) = refs
    else:
        (x_ref, sc1_ref, sh1_ref, g1_ref, sc2_ref, sh2_ref, hr_ref, hf_ref, hc_ref, hd_ref,
         wg_ref, wb_ref, wo_ref, lg_ref, lb_ref, x1_ref, u2_ref) = refs
    x = x_ref[...]
    u = _bf(x * (1.0 + sc1_ref[...]) + sh1_ref[...])
    merged = None
    for n, h_ref in enumerate((hr_ref, hf_ref, hc_ref, hd_ref)):
        gate = _dot(u, wg_ref[:, n * D_MODEL:(n + 1) * D_MODEL])
        term = _sigmoid(gate) * _dot(h_ref[...], wb_ref[n])
        merged = term if merged is None else merged + term
    mix = _dot(_bf(merged), wo_ref[...])
    x1 = _layer_norm_rows(ALPHA * x + g1_ref[...] * mix, lg_ref[...], lb_ref[...])
    x1_ref[...] = x1
    u2 = x1 * (1.0 + sc2_ref[...]) + sh2_ref[...]
    u2_ref[...] = u2.astype(u2_ref.dtype)
    if with_router:
        rec = _top2_route(_dot(_bf(u2), wr_ref[...]) + br_ref[...])
        rcol_ref[...] = rec
        rrow_ref[...] = rec.T[8:16, :]


def _merge(x2d, mods, branches, w_gate, w_branch, w_out, ln_g, ln_b, router, tm, tiles_per_group):
    n = x2d.shape[0]
    r = mods[0].shape[1]
    w = BRANCH_W
    with_router = router is not None
    row = lambda width: pl.BlockSpec((tm, width), lambda i: (i, 0))
    mod_spec = pl.BlockSpec((None, r, D_MODEL), lambda i: (i // tiles_per_group, 0, 0))
    full = lambda shape: pl.BlockSpec(shape, lambda i: (0,) * len(shape))
    in_specs = ([row(D_MODEL)] + [mod_spec] * 5 + [row(w)] * 4
                + [full((D_MODEL, 4 * D_MODEL)), full((4, w, D_MODEL)), full((D_MODEL, D_MODEL)),
                   full((1, D_MODEL)), full((1, D_MODEL))])
    args = [x2d, *mods, *branches, w_gate, w_branch, w_out, ln_g.reshape(1, D_MODEL), ln_b.reshape(1, D_MODEL)]
    out_shape = [jax.ShapeDtypeStruct((n, D_MODEL), F32),
                 jax.ShapeDtypeStruct((n, D_MODEL), F32 if with_router else BF16)]
    out_specs = [row(D_MODEL), row(D_MODEL)]
    if with_router:
        in_specs += [full((D_MODEL, LANES)), full((1, LANES))]
        args += list(router)
        out_shape += [jax.ShapeDtypeStruct((n, LANES), F32), jax.ShapeDtypeStruct((n // tm, 8, tm), F32)]
        out_specs += [row(LANES), pl.BlockSpec((None, 8, tm), lambda i: (i, 0, 0))]
    return pl.pallas_call(
        functools.partial(_merge_kernel, with_router=with_router),
        out_shape=tuple(out_shape),
        grid=(n // tm,),
        in_specs=in_specs,
        out_specs=tuple(out_specs),
        compiler_params=_params(1),
        name="merge_outproj_ln",
    )(*args)


def _ffn_kernel(u_ref, x1_ref, g2_ref, wa_ref, wg_ref, wd_ref, lg_ref, lb_ref, o_ref, acc_scr):
    j = pl.program_id(1)

    @pl.when(j == 0)
    def _():
        acc_scr[...] = jnp.zeros_like(acc_scr)

    u = u_ref[...]
    a = _dot(u, wa_ref[...])
    g = _dot(u, wg_ref[...])
    acc_scr[...] += _dot(_bf(a * _sigmoid(a) * g), wd_ref[...])

    @pl.when(j == pl.num_programs(1) - 1)
    def _():
        z = ALPHA * x1_ref[...] + g2_ref[...] * acc_scr[...]
        o_ref[...] = _layer_norm_rows(z, lg_ref[...], lb_ref[...])


def _ffn_dense(u2, x1, gate2, w_up, w_down, ln_g, ln_b, tm, tiles_per_group, th):
    n = u2.shape[0]
    r = gate2.shape[1]
    nh = D_FF // th
    return pl.pallas_call(
        _ffn_kernel,
        out_shape=jax.ShapeDtypeStruct((n, D_MODEL), F32),
        grid=(n // tm, nh),
        in_specs=[
            pl.BlockSpec((tm, D_MODEL), lambda i, j: (i, 0)),
            pl.BlockSpec((tm, D_MODEL), lambda i, j: (i, 0)),
            pl.BlockSpec((None, r, D_MODEL), lambda i, j: (i // tiles_per_group, 0, 0)),
            pl.BlockSpec((D_MODEL, th), lambda i, j: (0, j)),
            pl.BlockSpec((D_MODEL, th), lambda i, j: (0, nh + j)),
            pl.BlockSpec((th, D_MODEL), lambda i, j: (j, 0)),
            pl.BlockSpec((1, D_MODEL), lambda i, j: (0, 0)),
            pl.BlockSpec((1, D_MODEL), lambda i, j: (0, 0)),
        ],
        out_specs=pl.BlockSpec((tm, D_MODEL), lambda i, j: (i, 0)),
        scratch_shapes=[pltpu.VMEM((tm, D_MODEL), F32)],
        compiler_params=_params(2),
        name="ffn_dense",
    )(u2, x1, gate2, w_up, w_up, w_down, ln_g.reshape(1, D_MODEL), ln_b.reshape(1, D_MODEL))


def _moe_kernel(u_ref, x1_ref, g2_ref, cmb_ref, wa_ref, wg_ref, wd_ref, lg_ref, lb_ref, o_ref, acc_scr):
    e = pl.program_id(1)
    j = pl.program_id(2)

    @pl.when((e == 0) & (j == 0))
    def _():
        acc_scr[...] = jnp.zeros_like(acc_scr)

    u = u_ref[...]
    a = _dot(u, wa_ref[...])
    g = _dot(u, wg_ref[...])
    y = _dot(_bf(a * _sigmoid(a) * g), wd_ref[...])
    cmb = cmb_ref[...]
    lane = lax.broadcasted_iota(jnp.int32, cmb.shape, 1)
    ce = jnp.sum(jnp.where(lane == e, cmb, 0.0), axis=-1, keepdims=True)
    acc_scr[...] += ce * y

    @pl.when((e == pl.num_programs(1) - 1) & (j == pl.num_programs(2) - 1))
    def _():
        z = ALPHA * x1_ref[...] + g2_ref[...] * acc_scr[...]
        o_ref[...] = _layer_norm_rows(z, lg_ref[...], lb_ref[...])


def _ffn_experts(u2, x1, gate2, combine, w_in, w_out, ln_g, ln_b, tm, tiles_per_group, th):
    n = u2.shape[0]
    r = gate2.shape[1]
    nh = D_EXPERT // th
    return pl.pallas_call(
        _moe_kernel,
        out_shape=jax.ShapeDtypeStruct((n, D_MODEL), F32),
        grid=(n // tm, N_EXPERTS, nh),
        in_specs=[
            pl.BlockSpec((tm, D_MODEL), lambda i, e, j: (i, 0)),
            pl.BlockSpec((tm, D_MODEL), lambda i, e, j: (i, 0)),
            pl.BlockSpec((None, r, D_MODEL), lambda i, e, j: (i // tiles_per_group, 0, 0)),
            pl.BlockSpec((tm, LANES), lambda i, e, j: (i, 0)),
            pl.BlockSpec((None, D_MODEL, th), lambda i, e, j: (e, 0, j)),
            pl.BlockSpec((None, D_MODEL, th), lambda i, e, j: (e, 0, nh + j)),
            pl.BlockSpec((None, th, D_MODEL), lambda i, e, j: (e, j, 0)),
            pl.BlockSpec((1, D_MODEL), lambda i, e, j: (0, 0)),
            pl.BlockSpec((1, D_MODEL), lambda i, e, j: (0, 0)),
        ],
        out_specs=pl.BlockSpec((tm, D_MODEL), lambda i, e, j: (i, 0)),
        scratch_shapes=[pltpu.VMEM((tm, D_MODEL), F32)],
        compiler_params=_params(3),
        name="ffn_experts",
    )(u2, x1, gate2, combine, w_in, w_in, w_out, ln_g.reshape(1, D_MODEL), ln_b.reshape(1, D_MODEL))


EXPERT_TILE = 1024


def _rank_kernel(rr_ref, rank_ref, cnt_ref, carry_scr):
    b = pl.program_id(0)

    @pl.when(b == 0)
    def _():
        carry_scr[...] = jnp.zeros_like(carry_scr)

    rr = rr_ref[...]
    tm = rr.shape[1]
    i1 = rr[2:3]
    i2 = rr[3:4]
    e = lax.broadcasted_iota(jnp.int32, (N_EXPERTS, tm), 0).astype(F32)
    oh1 = jnp.where(e == i1, 1.0, 0.0)
    oh2 = jnp.where(e == i2, 1.0, 0.0)
    sel = oh1 + oh2
    r = lax.broadcasted_iota(jnp.int32, (tm, tm), 0)
    c = lax.broadcasted_iota(jnp.int32, (tm, tm), 1)
    triu = jnp.where(r <= c, 1.0, 0.0).astype(BF16)
    csum = _dot(_bf(sel), triu)
    carry = carry_scr[:, 0:1]
    rank = csum - sel + carry
    r1 = jnp.sum(oh1 * rank, axis=0, keepdims=True)
    r2 = jnp.sum(oh2 * rank, axis=0, keepdims=True)
    rec = jnp.concatenate([i1, i2, r1, r2, jnp.zeros((4, tm), F32)], axis=0)
    rank_ref[...] = rec.astype(jnp.int32)
    total = carry + csum[:, tm - 1:tm]
    carry_scr[...] = jnp.broadcast_to(total, carry_scr.shape)
    cnt_ref[...] = jnp.broadcast_to(total, cnt_ref.shape).astype(jnp.int32)


def _route_ranks(route_row):
    nblk, _, tm = route_row.shape
    return pl.pallas_call(
        _rank_kernel,
        out_shape=(jax.ShapeDtypeStruct((nblk, 8, tm), jnp.int32),
                   jax.ShapeDtypeStruct((N_EXPERTS, LANES), jnp.int32)),
        grid=(nblk,),
        in_specs=[pl.BlockSpec((None, 8, tm), lambda b: (b, 0, 0))],
        out_specs=(pl.BlockSpec((None, 8, tm), lambda b: (b, 0, 0)),
                   pl.BlockSpec((N_EXPERTS, LANES), lambda b: (0, 0))),
        scratch_shapes=[pltpu.VMEM((N_EXPERTS, LANES), F32)],
        compiler_params=_params(1),
        name="route_ranks",
    )(route_row)


def _row_copy(src_ref, src_row, dst_ref, dst_row, sem):
    return pltpu.make_async_copy(src_ref.at[pl.ds(src_row, 1)], dst_ref.at[pl.ds(dst_row, 1)], sem)


def _dispatch_kernel(off_ref, rank_ref, u_ref, xs_in_ref, xs_ref, sem):
    del xs_in_ref
    tm = u_ref.shape[0]

    def start(r, carry):
        for k in range(2):
            slot = off_ref[rank_ref[k, r]] + rank_ref[2 + k, r]
            _row_copy(u_ref, r, xs_ref, slot, sem).start()
        return carry

    lax.fori_loop(0, tm, start, 0, unroll=8)

    def wait(r, carry):
        for k in range(2):
            _row_copy(u_ref, 0, xs_ref, 0, sem).wait()
        return carry

    lax.fori_loop(0, tm, wait, 0, unroll=8)


def _dispatch(u2, ranks, off, n_slots):
    n, d = u2.shape
    nblk, _, tm = ranks.shape
    grid_spec = pltpu.PrefetchScalarGridSpec(
        num_scalar_prefetch=1,
        grid=(nblk,),
        in_specs=[
            pl.BlockSpec((None, 8, tm), lambda b, off: (b, 0, 0), memory_space=pltpu.SMEM),
            pl.BlockSpec((tm, d), lambda b, off: (b, 0)),
            pl.BlockSpec(memory_space=pl.ANY),
        ],
        out_specs=pl.BlockSpec(memory_space=pl.ANY),
        scratch_shapes=[pltpu.SemaphoreType.DMA],
    )
    return pl.pallas_call(
        _dispatch_kernel,
        out_shape=jax.ShapeDtypeStruct((n_slots, d), F32),
        grid_spec=grid_spec,
        input_output_aliases={3: 0},
        compiler_params=_params(1),
        name="expert_dispatch",
    )(off, ranks, u2, jnp.zeros((n_slots, d), F32))


def _grouped_kernel(toff_ref, nt_ref, xs_ref, wa_ref, wg_ref, wd_ref, ys_ref, xb_scr, acc_scr):
    i = pl.program_id(0)
    j = pl.program_id(1)

    @pl.when(i < nt_ref[0])
    def _():
        @pl.when(j == 0)
        def _():
            xb_scr[...] = _bf(xs_ref[...])
            acc_scr[...] = jnp.zeros_like(acc_scr)

        u = xb_scr[...]
        a = _dot(u, wa_ref[...])
        g = _dot(u, wg_ref[...])
        acc_scr[...] += _dot(_bf(a * _sigmoid(a) * g), wd_ref[...])

        @pl.when(j == pl.num_programs(1) - 1)
        def _():
            ys_ref[...] = acc_scr[...]


def _grouped_experts(xs, tile_off, n_tiles, w_in, w_out, tm, th):
    n_slots, d = xs.shape
    nh = D_EXPERT // th

    def expert(i, toff):
        e = 0
        for k in range(1, N_EXPERTS):
            e = e + (i >= toff[k]).astype(jnp.int32)
        return e

    def tile(i, nt):
        return jnp.minimum(i, nt[0] - 1)

    grid_spec = pltpu.PrefetchScalarGridSpec(
        num_scalar_prefetch=2,
        grid=(n_slots // tm, nh),
        in_specs=[
            pl.BlockSpec((tm, d), lambda i, j, toff, nt: (tile(i, nt), 0)),
            pl.BlockSpec((None, d, th), lambda i, j, toff, nt: (expert(tile(i, nt), toff), 0, j)),
            pl.BlockSpec((None, d, th), lambda i, j, toff, nt: (expert(tile(i, nt), toff), 0, nh + j)),
            pl.BlockSpec((None, th, d), lambda i, j, toff, nt: (expert(tile(i, nt), toff), j, 0)),
        ],
        out_specs=pl.BlockSpec((tm, d), lambda i, j, toff, nt: (tile(i, nt), 0)),
        scratch_shapes=[pltpu.VMEM((tm, d), BF16), pltpu.VMEM((tm, d), F32)],
    )
    return pl.pallas_call(
        _grouped_kernel,
        out_shape=jax.ShapeDtypeStruct((n_slots, d), F32),
        grid_spec=grid_spec,
        compiler_params=_params(2),
        name="grouped_experts",
    )(tile_off, n_tiles, xs, w_in, w_in, w_out)


def _combine_kernel(off_ref, rank_ref, ys_ref, x1_ref, g2_ref, rcol_ref, lg_ref, lb_ref, o_ref,
                    y1_scr, y2_scr, sem):
    tm = x1_ref.shape[0]
    bufs = (y1_scr, y2_scr)

    def start(r, carry):
        for k in range(2):
            slot = off_ref[rank_ref[k, r]] + rank_ref[2 + k, r]
            _row_copy(ys_ref, slot, bufs[k], r, sem).start()
        return carry

    lax.fori_loop(0, tm, start, 0, unroll=8)

    def wait(r, carry):
        for k in range(2):
            _row_copy(ys_ref, 0, bufs[k], 0, sem).wait()
        return carry

    lax.fori_loop(0, tm, wait, 0, unroll=8)
    rec = rcol_ref[...]
    f = rec[:, ROUTE_W1:ROUTE_W1 + 1] * y1_scr[...] + rec[:, ROUTE_W2:ROUTE_W2 + 1] * y2_scr[...]
    z = ALPHA * x1_ref[...] + g2_ref[...] * f
    o_ref[...] = _layer_norm_rows(z, lg_ref[...], lb_ref[...])


def _combine(ys, ranks, off, x1, gate2, route_col, ln_g, ln_b, tiles_per_group):
    n, d = x1.shape
    nblk, _, tm = ranks.shape
    r = gate2.shape[1]
    grid_spec = pltpu.PrefetchScalarGridSpec(
        num_scalar_prefetch=1,
        grid=(nblk,),
        in_specs=[
            pl.BlockSpec((None, 8, tm), lambda b, off: (b, 0, 0), memory_space=pltpu.SMEM),
            pl.BlockSpec(memory_space=pl.ANY),
            pl.BlockSpec((tm, d), lambda b, off: (b, 0)),
            pl.BlockSpec((None, r, d), lambda b, off: (b // tiles_per_group, 0, 0)),
            pl.BlockSpec((tm, LANES), lambda b, off: (b, 0)),
            pl.BlockSpec((1, d), lambda b, off: (0, 0)),
            pl.BlockSpec((1, d), lambda b, off: (0, 0)),
        ],
        out_specs=pl.BlockSpec((tm, d), lambda b, off: (b, 0)),
        scratch_shapes=[pltpu.VMEM((tm, d), F32), pltpu.VMEM((tm, d), F32), pltpu.SemaphoreType.DMA],
    )
    return pl.pallas_call(
        _combine_kernel,
        out_shape=jax.ShapeDtypeStruct((n, d), F32),
        grid_spec=grid_spec,
        compiler_params=_params(1),
        name="expert_combine_ln",
    )(off, ranks, ys, x1, gate2, route_col, ln_g.reshape(1, d), ln_b.reshape(1, d))


def _ffn_routed(u2, x1, gate2, route_col, route_row, w_in, w_out, ln_g, ln_b, tiles_per_group):
    n = u2.shape[0]
    tm = min(EXPERT_TILE, n)
    assert n % tm == 0
    ranks, counts = _route_ranks(route_row)
    cnt = counts[:, 0]
    padded = ((cnt + tm - 1) // tm) * tm
    off = jnp.cumsum(padded) - padded
    n_slots = 2 * n + N_EXPERTS * tm
    xs = _dispatch(u2, ranks, off.astype(jnp.int32), n_slots)
    ys = _grouped_experts(xs, (off // tm).astype(jnp.int32), (jnp.sum(padded) // tm).astype(jnp.int32).reshape(1),
                          w_in, w_out, tm, 512)
    return _combine(ys, ranks, off.astype(jnp.int32), x1, gate2, route_col, ln_g, ln_b, tiles_per_group)


def _split_w_in(w_in_l):
    w = BRANCH_W
    col = lambda off, n: w_in_l[:, off:off + n]
    o = 0
    ret_q, ret_k, ret_v, ret_g = (col(o + i * w, w) for i in range(4)); o += 4 * w
    fox_q, fox_k, fox_v = (col(o + i * w, w) for i in range(3)); o += 3 * w
    fox_f = col(o, N_HEADS); o += N_HEADS
    conv = col(o, 2 * w); o += 2 * w
    diff_q, diff_k, diff_v = (col(o + i * w, w) for i in range(3)); o += 3 * w
    gate = col(o, 4 * D_MODEL)
    rot_r = lambda m: _rotated_columns(m, HEAD_DIM, HEAD_DIM)
    rot_d = lambda m: _rotated_columns(m, DIFF_SUB, ROT_DIM)
    w_ret = jnp.concatenate([ret_q, rot_r(ret_q), ret_k, rot_r(ret_k), ret_v, ret_g], axis=1)
    w_fox = jnp.concatenate([fox_q, fox_k, fox_v, fox_f, jnp.zeros((D_MODEL, LANES - N_HEADS), F32)], axis=1)
    w_diff = jnp.concatenate([diff_q, rot_d(diff_q), diff_k, rot_d(diff_k), diff_v], axis=1)
    return _bf(w_ret), _bf(w_fox), _bf(conv), _bf(w_diff), _bf(gate)


def _block_diag_state(s):
    b = s.shape[0]
    eye = jnp.eye(N_HEADS, dtype=s.dtype)
    return jnp.einsum('bhde,hg->bhdge', s, eye).reshape(b, BRANCH_W, BRANCH_W)


def _diag_blocks(s_bd):
    b = s_bd.shape[0]
    s4 = s_bd.reshape(b, N_HEADS, HEAD_DIM, N_HEADS, HEAD_DIM)
    return jnp.stack([s4[:, h, :, h, :] for h in range(N_HEADS)], axis=1)


def _pad_rows(a, rows):
    return jnp.concatenate([a, jnp.zeros((a.shape[0], rows - a.shape[1]) + a.shape[2:], a.dtype)], axis=1)


def _trunk_layer(x, mod, pos, past, l, lw, sample):
    b, t, _ = x.shape
    d = D_MODEL
    w = BRANCH_W
    shift1, scale1, gate1 = (mod[0][:, None, i * d:(i + 1) * d] for i in range(3))
    shift2, scale2, gate2 = (mod[1][:, None, i * d:(i + 1) * d] for i in range(3))
    lam_init = 0.8 - 0.6 * math.exp(-0.3 * l)
    chunk = min(t, KV_BLOCK)
    tm_proj = min(t, 1024)

    s0 = jnp.zeros((b, w, w), F32) if past is None else _block_diag_state(past[5])
    h_ret, s_bd = _retention(x, scale1, shift1, lw['w_ret'], pos, s0, chunk)
    ret_state = _diag_blocks(s_bd)

    if past is None:
        hist = jnp.zeros((b, HIST_ROWS, w), F32)
    else:
        hist = jnp.concatenate([jnp.zeros((b, HIST_ROWS - (CONV_W - 1), w), F32), past[6]], axis=1)
    h_conv, tail = _conv_mixer(x, scale1, shift1, lw['w_conv_in'], hist, lw['w_conv'], lw['b_conv'],
                               lw['conv_ln_g'], lw['conv_ln_b'], chunk)
    conv_buf = tail[:, HIST_ROWS - (CONV_W - 1):, :]

    fq, fk, fv, fkb, fvb, lf = _fox_proj(x, scale1, shift1, lw['w_fox'], lw['b_fox_f'], tm_proj)
    if past is None:
        off, tk_valid = 0, t
        k_all, v_all, lf_all = fkb, fvb, lf
    else:
        p_len = past[0].shape[1]
        off, tk_valid = p_len, p_len + t
        tk_pad = p_len + KV_BLOCK
        k_all = _pad_rows(jnp.concatenate([_bf(past[0].reshape(b, p_len, w)), fkb], axis=1), tk_pad)
        v_all = _pad_rows(jnp.concatenate([_bf(past[1].reshape(b, p_len, w)), fvb], axis=1), tk_pad)
        past_lf = jnp.concatenate([past[2].astype(F32), jnp.zeros((b, p_len, LANES - N_HEADS), F32)], axis=2)
        lf_all = _pad_rows(jnp.concatenate([past_lf, lf], axis=1), tk_pad)
    fcol, frow = _logf_cumsum(lf_all)
    tq = min(t, KV_BLOCK)
    h_fox = _fox_attention(fq, k_all, v_all, fcol, frow, tq, off, tk_valid)

    dq, dk, dv, dkb, dvb = _diff_proj(x, scale1, shift1, lw['w_diff'], pos, tm_proj)
    if past is None:
        dk_all, dv_all = dkb, dvb
    else:
        dk_all = _pad_rows(jnp.concatenate([_bf(past[3].reshape(b, p_len, w)), dkb], axis=1), tk_pad)
        dv_all = _pad_rows(jnp.concatenate([_bf(past[4].reshape(b, p_len, w)), dvb], axis=1), tk_pad)
    h_diff = _diff_attention(dq, dk_all, dv_all, lw['diff_lambda'], lw['diff_subln_g'], tq, off, tk_valid, lam_init)

    n = b * t
    if sample:
        tm_merge = tm_ffn = min(n, 256)
        tpg_merge = tpg_ffn = 1
        rows_mod = lambda m, tm: jnp.repeat(m, t, axis=1).reshape(n // tm, tm, d)
    else:
        tm_merge, tm_ffn = min(t, 512), min(t, 1024)
        tpg_merge, tpg_ffn = t // tm_merge, t // tm_ffn
        rows_mod = lambda m, tm: m
    mods = tuple(rows_mod(m, tm_merge) for m in (scale1, shift1, gate1, scale2, shift2))
    branches = tuple(h.reshape(n, w) for h in (h_ret, h_fox, h_conv, h_diff))
    merged = _merge(x.reshape(n, d), mods, branches, lw['w_gate'], lw['w_branch'], lw['w_out'],
                    lw['ln_g'][0], lw['ln_b'][0], lw.get('router'), tm_merge, tpg_merge)
    g2 = rows_mod(gate2, tm_ffn)
    if 'router' in lw and not sample:
        x1, u2, route_col, route_row = merged
        x2 = _ffn_routed(u2, x1, rows_mod(gate2, tm_merge), route_col, route_row, lw['w_exp_in'], lw['w_exp_out'],
                         lw['ln_g'][1], lw['ln_b'][1], tpg_merge)
    elif 'router' in lw:
        x1, u2, route_col, _ = merged
        x2 = _ffn_experts(_bf(u2), x1, g2, route_col, lw['w_exp_in'], lw['w_exp_out'], lw['ln_g'][1], lw['ln_b'][1],
                          tm_ffn, tpg_ffn, 512)
    else:
        x1, u2 = merged
        x2 = _ffn_dense(u2, x1, g2, lw['w_ffn_in'], lw['w_ffn_out'], lw['ln_g'][1], lw['ln_b'][1],
                        tm_ffn, tpg_ffn, 1408)
    heads = lambda a, nh: a.reshape(b, t, nh, w // nh)
    new_state = (heads(fk, N_HEADS), heads(fv, N_HEADS), lf[:, :, :N_HEADS], heads(dk, N_SUB), heads(dv, N_HEADS),
                 ret_state, conv_buf)
    return x2.reshape(b, t, d), new_state


def kernel(x_prompt, x_sample, c_prompt, c_sample, cache_fox_k, cache_fox_v, cache_fox_logf, cache_diff_k, cache_diff_v, state_ret, state_conv, w_in, b_fox_f, w_conv, b_conv, conv_ln_g, conv_ln_b, diff_lambda, diff_subln_g, w_branch, w_out, w_ada, b_ada, ln_g, ln_b, w_ffn_in, w_ffn_out, w_router, b_router, w_exp_in, w_exp_out):
    depth = w_in.shape[0]
    bp = x_prompt.shape[0]
    past_len = cache_fox_k.shape[2]
    pos_p = jnp.arange(x_prompt.shape[1], dtype=jnp.int32)
    pos_s = past_len + jnp.arange(x_sample.shape[1], dtype=jnp.int32)
    mod = _modulation(jnp.concatenate([c_prompt, c_sample], axis=0), w_ada, b_ada)
    yp, ys = x_prompt, x_sample
    new_p, new_s = [], []
    for l in range(depth):
        w_ret, w_fox, w_conv_in, w_diff, w_gate = _split_w_in(w_in[l])
        lw = dict(w_ret=w_ret, w_fox=w_fox, w_conv_in=w_conv_in, w_diff=w_diff, w_gate=w_gate,
                  b_fox_f=b_fox_f[l], w_conv=w_conv[l], b_conv=b_conv[l], conv_ln_g=conv_ln_g[l],
                  conv_ln_b=conv_ln_b[l], diff_lambda=diff_lambda[l], diff_subln_g=diff_subln_g[l],
                  w_branch=_bf(w_branch[l]), w_out=_bf(w_out[l]), ln_g=ln_g[l], ln_b=ln_b[l])
        if l % 2 == 0:
            lw['w_ffn_in'] = _bf(w_ffn_in[l // 2])
            lw['w_ffn_out'] = _bf(w_ffn_out[l // 2])
        else:
            wr = jnp.concatenate([w_router[l // 2], jnp.zeros((D_MODEL, LANES - N_EXPERTS), F32)], axis=1)
            br = jnp.concatenate([b_router[l // 2].astype(F32), jnp.zeros((LANES - N_EXPERTS,), F32)]).reshape(1, LANES)
            lw['router'] = (_bf(wr), br)
            lw['w_exp_in'] = _bf(w_exp_in[l // 2])
            lw['w_exp_out'] = _bf(w_exp_out[l // 2])
        past_l = (cache_fox_k[l], cache_fox_v[l], cache_fox_logf[l], cache_diff_k[l], cache_diff_v[l],
                  state_ret[l], state_conv[l])
        yp, st_p = _trunk_layer(yp, mod[l][:, :bp], pos_p, None, l, lw, sample=False)
        ys, st_s = _trunk_layer(ys, mod[l][:, bp:], pos_s, past_l, l, lw, sample=True)
        new_p.append(st_p)
        new_s.append(st_s)
    outs_p = tuple(jnp.stack(a) for a in zip(*new_p))
    outs_s = tuple(jnp.stack(a) for a in zip(*new_s))
    return (yp, ys) + outs_p + outs_s
```

```python
import functools
import math

import jax
import jax.numpy as jnp
from jax import lax
from jax.experimental import pallas as pl
from jax.experimental.pallas import tpu as pltpu

D_MODEL = 1024
BRANCH_W = 256
HEAD_DIM = 64
N_HEADS = 4
DIFF_SUB = 32
N_SUB = 8
ROT_DIM = DIFF_SUB // 4
RET_THETA = 10000.0
ROPE_THETA = 500000.0
CHUNK = 64
CONV_W = 31
D_FF = 2816
N_EXPERTS = 8
D_EXPERT = 3584
DEPTH = 2
ALPHA = (2.0 * DEPTH) ** 0.25
EPS = 1e-5
NEG = -1e30
LOG2E = math.log2(math.e)

LANES = 128
BF16_ROWS = 16
KV_BLOCK = 256
HIST_ROWS = 32
VMEM_LIMIT = 56 * 1024 * 1024

F32 = jnp.float32
BF16 = jnp.bfloat16


def _bf(x):
    return x.astype(BF16)


def _dot(a, b):
    return jnp.dot(a, b, preferred_element_type=F32)


def _dot_nt(a, b):
    return lax.dot_general(a, b, (((1,), (1,)), ((), ())), preferred_element_type=F32)


def _dot_tn(a, b):
    return lax.dot_general(a, b, (((0,), (0,)), ((), ())), preferred_element_type=F32)


def _sigmoid(x):
    return 1.0 / (1.0 + jnp.exp(-x))


def _params(n_axes):
    return pltpu.CompilerParams(dimension_semantics=("arbitrary",) * n_axes,
                                vmem_limit_bytes=VMEM_LIMIT)


def _head_sum(y, width):
    n = y.shape[-1]
    r = lax.broadcasted_iota(jnp.int32, (n, n), 0) // width
    c = lax.broadcasted_iota(jnp.int32, (n, n), 1) // width
    bd = jnp.where(r == c, 1.0, 0.0).astype(BF16)
    hi = _bf(y)
    lo = _bf(y - hi.astype(F32))
    return _dot(hi, bd) + _dot(lo, bd)


def _layer_norm_rows(z, g, b):
    mu = jnp.mean(z, axis=-1, keepdims=True)
    d = z - mu
    var = jnp.mean(d * d, axis=-1, keepdims=True)
    return d * lax.rsqrt(var + EPS) * g + b


def _mod_kernel(c_ref, w_ref, b_ref, o_ref):
    c = c_ref[...]
    sc = _bf(c * _sigmoid(c))
    o_ref[...] = _dot(sc, _bf(w_ref[...])) + b_ref[...]


def _modulation(c_all, w_ada, b_ada):
    rows = c_all.shape[0]
    depth = w_ada.shape[0]
    d3 = w_ada.shape[-1]
    nj = d3 // D_MODEL
    return pl.pallas_call(
        _mod_kernel,
        out_shape=jax.ShapeDtypeStruct((depth, 2, rows, d3), F32),
        grid=(depth * 2, nj),
        in_specs=[
            pl.BlockSpec((rows, D_MODEL), lambda i, j: (0, 0)),
            pl.BlockSpec((None, None, D_MODEL, D_MODEL), lambda i, j: (i // 2, i % 2, 0, j)),
            pl.BlockSpec((None, None, 1, D_MODEL), lambda i, j: (i // 2, i % 2, 0, j)),
        ],
        out_specs=pl.BlockSpec((None, None, rows, D_MODEL), lambda i, j: (i // 2, i % 2, 0, j)),
        compiler_params=_params(2),
        name="adaln_modulation",
    )(c_all, w_ada, b_ada.reshape(depth, 2, 1, d3))


def _ret_kernel(x_ref, sc_ref, sh_ref, w_ref, cos_ref, sin_ref, dmask_ref, qdec_ref, kdec_ref,
                cdec_ref, s0_ref, h_ref, sout_ref, s_scr):
    c = pl.program_id(1)

    @pl.when(c == 0)
    def _():
        s_scr[...] = s0_ref[...]

    u = x_ref[...] * (1.0 + sc_ref[...]) + sh_ref[...]
    p = _dot(_bf(u), w_ref[...])
    w = BRANCH_W
    cos = cos_ref[...]
    sin = sin_ref[...]
    q = p[:, 0:w] * cos + p[:, w:2 * w] * sin
    k = (p[:, 2 * w:3 * w] * cos + p[:, 3 * w:4 * w] * sin) * (HEAD_DIM ** -0.5)
    v = p[:, 4 * w:5 * w]
    g = p[:, 5 * w:6 * w]
    rows = q.shape[0]
    lane_head = lax.broadcasted_iota(jnp.int32, (1, w), 1) // HEAD_DIM
    kb = _bf(k)
    vb = _bf(v)
    y = jnp.zeros((rows, w), F32)
    for h in range(N_HEADS):
        mh = lane_head == h
        qh = _bf(jnp.where(mh, q, 0.0))
        a = _dot_nt(qh, kb) * dmask_ref[h]
        y = y + jnp.where(mh, _dot(_bf(a), vb), 0.0)
    s_prev = s_scr[...]
    y = y + _dot(_bf(q * qdec_ref[...]), _bf(s_prev))
    kv = _dot_tn(_bf(k * kdec_ref[...]), vb)
    r = lax.broadcasted_iota(jnp.int32, (w, w), 0) // HEAD_DIM
    cc = lax.broadcasted_iota(jnp.int32, (w, w), 1) // HEAD_DIM
    s_new = cdec_ref[...] * s_prev + jnp.where(r == cc, kv, 0.0)
    s_scr[...] = s_new
    sout_ref[...] = s_new
    mu = _head_sum(y, HEAD_DIM) * (1.0 / HEAD_DIM)
    d = y - mu
    var = _head_sum(d * d, HEAD_DIM) * (1.0 / HEAD_DIM)
    hn = d * lax.rsqrt(var + EPS)
    h_ref[...] = _bf(hn * (g * _sigmoid(g)))


def _retention_tables(chunk):
    log_g = jnp.log1p(-jnp.exp2(-5.0 - jnp.arange(N_HEADS, dtype=F32)))
    idx = jnp.arange(chunk, dtype=F32)
    dist = jnp.abs(idx[:, None] - idx[None, :])
    sub = jnp.arange(chunk) // CHUNK
    vis = sub[None, :] <= sub[:, None]
    dmask = jnp.where(vis[None], jnp.exp(log_g[:, None, None] * dist[None]), 0.0)
    lg_lane = jnp.repeat(log_g, HEAD_DIM)[None, :]
    qdec = jnp.exp(lg_lane * (idx[:, None] + 1.0))
    kdec = jnp.exp(lg_lane * (chunk - 1.0 - idx[:, None]))
    cdec = jnp.exp(lg_lane * chunk)
    return dmask.astype(F32), qdec, kdec, cdec


def _rope_tables(pos, dim, rot_dim, theta, n_rep):
    half = rot_dim // 2
    inv_freq = jnp.exp(-math.log(theta) * jnp.arange(half, dtype=F32) / half)
    ang = pos.astype(F32)[:, None] * inv_freq[None, :]
    t = pos.shape[0]
    cos = jnp.concatenate([jnp.cos(ang), jnp.cos(ang), jnp.ones((t, dim - rot_dim), F32)], axis=1)
    sin = jnp.concatenate([jnp.sin(ang), jnp.sin(ang), jnp.zeros((t, dim - rot_dim), F32)], axis=1)
    return jnp.tile(cos, (1, n_rep)), jnp.tile(sin, (1, n_rep))


def _rotated_columns(w, dim, rot_dim):
    half = rot_dim // 2
    k, n = w.shape
    wh = w.reshape(k, n // dim, dim)
    rot = jnp.concatenate([-wh[..., half:rot_dim], wh[..., :half], jnp.zeros_like(wh[..., rot_dim:])], axis=-1)
    return rot.reshape(k, n)


def _retention(x, scale, shift, w_ret, pos, s0_bd, chunk):
    b, t, _ = x.shape
    nc = t // chunk
    cos, sin = _rope_tables(pos, HEAD_DIM, HEAD_DIM, RET_THETA, N_HEADS)
    dmask, qdec, kdec, cdec = _retention_tables(chunk)
    w = BRANCH_W
    full = lambda shape: pl.BlockSpec(shape, lambda i, j: (0,) * len(shape))
    return pl.pallas_call(
        _ret_kernel,
        out_shape=(jax.ShapeDtypeStruct((b, t, w), BF16), jax.ShapeDtypeStruct((b, w, w), F32)),
        grid=(b, nc),
        in_specs=[
            pl.BlockSpec((None, chunk, D_MODEL), lambda i, j: (i, j, 0)),
            pl.BlockSpec((None, 1, D_MODEL), lambda i, j: (i, 0, 0)),
            pl.BlockSpec((None, 1, D_MODEL), lambda i, j: (i, 0, 0)),
            full((D_MODEL, 6 * w)),
            pl.BlockSpec((chunk, w), lambda i, j: (j, 0)),
            pl.BlockSpec((chunk, w), lambda i, j: (j, 0)),
            full((N_HEADS, chunk, chunk)),
            full((chunk, w)),
            full((chunk, w)),
            full((1, w)),
            pl.BlockSpec((None, w, w), lambda i, j: (i, 0, 0)),
        ],
        out_specs=(pl.BlockSpec((None, chunk, w), lambda i, j: (i, j, 0)),
                   pl.BlockSpec((None, w, w), lambda i, j: (i, 0, 0))),
        scratch_shapes=[pltpu.VMEM((w, w), F32)],
        compiler_params=_params(2),
        name="retention_mixer",
    )(x, scale, shift, w_ret, cos, sin, dmask, qdec, kdec, cdec, s0_bd)


def _conv_kernel(x_ref, sc_ref, sh_ref, w_ref, hist_ref, wc_ref, bc_ref, lg_ref, lb_ref,
                 h_ref, tail_ref, xp_scr):
    c = pl.program_id(1)
    rows = x_ref.shape[0]
    w = BRANCH_W
    pad = HIST_ROWS - (CONV_W - 1)

    @pl.when(c == 0)
    def _():
        xp_scr[0:HIST_ROWS, :] = hist_ref[...]

    u = x_ref[...] * (1.0 + sc_ref[...]) + sh_ref[...]
    p = _dot(_bf(u), w_ref[...])
    xp_scr[HIST_ROWS:HIST_ROWS + rows, :] = p[:, :w] * _sigmoid(p[:, w:])
    sub = 8
    acc = jnp.zeros((rows, w), F32) + bc_ref[...]
    for rho in range(sub):
        z = None
        for m in range((pad + CONV_W - 1) // sub + 1):
            j = sub * m + rho - pad
            if not 0 <= j < CONV_W:
                continue
            span = rows if rho == 0 else rows + sub
            term = xp_scr[sub * m:sub * m + span, :] * wc_ref[j:j + 1, :]
            z = term if z is None else z + term
        acc = acc + z[rho:rho + rows, :]
    y = _layer_norm_rows(acc, lg_ref[...], lb_ref[...])
    h_ref[...] = _bf(y * _sigmoid(y))
    tail = xp_scr[rows:rows + HIST_ROWS, :]
    tail_ref[...] = tail
    xp_scr[0:HIST_ROWS, :] = tail


def _conv_mixer(x, scale, shift, w_cv, hist, w_conv, b_conv, ln_g, ln_b, chunk):
    b, t, _ = x.shape
    nc = t // chunk
    w = BRANCH_W
    full = lambda shape: pl.BlockSpec(shape, lambda i, j: (0,) * len(shape))
    wc = jnp.concatenate([w_conv, jnp.zeros((HIST_ROWS - CONV_W, w), F32)], axis=0)
    return pl.pallas_call(
        _conv_kernel,
        out_shape=(jax.ShapeDtypeStruct((b, t, w), BF16), jax.ShapeDtypeStruct((b, HIST_ROWS, w), F32)),
        grid=(b, nc),
        in_specs=[
            pl.BlockSpec((None, chunk, D_MODEL), lambda i, j: (i, j, 0)),
            pl.BlockSpec((None, 1, D_MODEL), lambda i, j: (i, 0, 0)),
            pl.BlockSpec((None, 1, D_MODEL), lambda i, j: (i, 0, 0)),
            full((D_MODEL, 2 * w)),
            pl.BlockSpec((None, HIST_ROWS, w), lambda i, j: (i, 0, 0)),
            full((HIST_ROWS, w)),
            full((1, w)),
            full((1, w)),
            full((1, w)),
        ],
        out_specs=(pl.BlockSpec((None, chunk, w), lambda i, j: (i, j, 0)),
                   pl.BlockSpec((None, HIST_ROWS, w), lambda i, j: (i, 0, 0))),
        scratch_shapes=[pltpu.VMEM((HIST_ROWS + chunk, w), F32)],
        compiler_params=_params(2),
        name="conv_mixer",
    )(x, scale, shift, w_cv, hist, wc, b_conv.reshape(1, w), ln_g.reshape(1, w), ln_b.reshape(1, w))


def _fox_proj_kernel(x_ref, sc_ref, sh_ref, w_ref, bf_ref, *refs):
    q_ref, k_ref, v_ref, kb_ref, vb_ref, lf_ref = refs[-6:]
    u = x_ref[...] * (1.0 + sc_ref[...]) + sh_ref[...]
    p = _dot(_bf(u), w_ref[...])
    w = BRANCH_W
    q_ref[...] = _bf(p[:, 0:w] * (HEAD_DIM ** -0.5 * LOG2E))
    k = p[:, w:2 * w]
    v = p[:, 2 * w:3 * w]
    _store_layer_rows(k_ref, k)
    _store_layer_rows(v_ref, v)
    kb_ref[...] = _bf(k)
    vb_ref[...] = _bf(v)
    z = p[:, 3 * w:] + bf_ref[...]
    lf = jnp.minimum(z, 0.0) - jnp.log(1.0 + jnp.exp(-jnp.abs(z)))
    lane = lax.broadcasted_iota(jnp.int32, lf.shape, 1)
    _store_layer_rows(lf_ref, jnp.where(lane < N_HEADS, lf, 0.0))


def _layer_slot(depth, layer, b, t, tm, width, dt, first):
    shape = jax.ShapeDtypeStruct((depth, b, t, width), dt)
    if first:
        return shape, pl.BlockSpec((depth, None, tm, width), lambda i, j: (0, i, j, 0))
    return shape, pl.BlockSpec((None, None, tm, width), lambda i, j: (layer, i, j, 0))


def _store_layer_rows(ref, rows):
    if len(ref.shape) == 3:
        for d in range(ref.shape[0]):
            ref[d] = rows
    else:
        ref[...] = rows


def _fox_proj(x, scale, shift, w_fox, b_f, tm, depth, layer, state):
    b, t, _ = x.shape
    w = BRANCH_W
    nw = 3 * w + LANES
    bfp = jnp.concatenate([b_f.astype(F32), jnp.zeros((LANES - N_HEADS,), F32)]).reshape(1, LANES)
    row = lambda width: pl.BlockSpec((None, tm, width), lambda i, j: (i, j, 0))
    sds = lambda width, dt: jax.ShapeDtypeStruct((b, t, width), dt)
    k_out, v_out, lf_out = (_layer_slot(depth, layer, b, t, tm, width, F32, state is None) for width in (w, w, LANES))
    prior = () if state is None else tuple(state)
    n_in = 5
    q, k, v, kb, vb, lf = pl.pallas_call(
        _fox_proj_kernel,
        out_shape=(sds(w, BF16), k_out[0], v_out[0], sds(w, BF16), sds(w, BF16), lf_out[0]),
        grid=(b, t // tm),
        in_specs=[
            row(D_MODEL),
            pl.BlockSpec((None, 1, D_MODEL), lambda i, j: (i, 0, 0)),
            pl.BlockSpec((None, 1, D_MODEL), lambda i, j: (i, 0, 0)),
            pl.BlockSpec((D_MODEL, nw), lambda i, j: (0, 0)),
            pl.BlockSpec((1, LANES), lambda i, j: (0, 0)),
        ] + [pl.BlockSpec(memory_space=pl.ANY)] * len(prior),
        out_specs=(row(w), k_out[1], v_out[1], row(w), row(w), lf_out[1]),
        input_output_aliases={n_in: 1, n_in + 1: 2, n_in + 2: 5} if prior else {},
        compiler_params=_params(2),
        name="fox_projection",
    )(x, scale, shift, w_fox, bfp, *prior)
    return q, kb, vb, lf[layer], (k, v, lf)


def _cumsum_kernel(lf_ref, col_ref, row_ref):
    n = KV_BLOCK
    nb = lf_ref.shape[0] // n
    r = lax.broadcasted_iota(jnp.int32, (n, n), 0)
    c = lax.broadcasted_iota(jnp.int32, (n, n), 1)
    tri = jnp.where(c <= r, 1.0, 0.0).astype(BF16)
    carry = jnp.zeros((1, LANES), F32)
    for jb in range(nb):
        x = lf_ref[jb * n:(jb + 1) * n, :]
        hi = _bf(x)
        r1 = x - hi.astype(F32)
        mid = _bf(r1)
        lo = _bf(r1 - mid.astype(F32))
        cs = _dot(tri, hi) + _dot(tri, mid) + _dot(tri, lo) + carry
        carry = cs[n - 1:n, :]
        scaled = cs * LOG2E
        col_ref[jb * n:(jb + 1) * n, :] = scaled
        row_ref[jb] = scaled.T[0:8, :]


def _logf_cumsum(lf):
    b, tk, _ = lf.shape
    nb = tk // KV_BLOCK
    return pl.pallas_call(
        _cumsum_kernel,
        out_shape=(jax.ShapeDtypeStruct((b, tk, LANES), F32),
                   jax.ShapeDtypeStruct((b, nb, 8, KV_BLOCK), F32)),
        grid=(b,),
        in_specs=[pl.BlockSpec((None, tk, LANES), lambda i: (i, 0, 0))],
        out_specs=(pl.BlockSpec((None, tk, LANES), lambda i: (i, 0, 0)),
                   pl.BlockSpec((None, nb, 8, KV_BLOCK), lambda i: (i, 0, 0, 0))),
        compiler_params=_params(1),
        name="logf_cumsum",
    )(lf)


def _attn_kernel(*refs, fox, off, tk_valid, tq, lam_init, split_kv):
    q_ref, k_ref, v_ref = refs[:3]
    refs = refs[3:]
    if split_kv:
        kn_ref, vn_ref = refs[:2]
        refs = refs[2:]
    if fox:
        fc_ref, fr_ref, o_ref, vt_scr, qm_scr, mb_scr, m_scr, r_scr, acc_scr, s_scr = refs
    else:
        lam_ref, g_ref, o_ref, vt_scr, qm_scr, mb_scr, m_scr, r_scr, acc_scr, s_scr = refs
    i = pl.program_id(1)
    w = BRANCH_W
    n_sub, tqp, _ = qm_scr.shape
    sub_w = w // n_sub
    subs_per_head = n_sub // N_HEADS
    nb = vt_scr.shape[0]
    hd = HEAD_DIM
    q0 = off + i * tq
    nfull = q0 // KV_BLOCK

    @pl.when(i == 0)
    def _():
        for jb in range(nb):
            if split_kv and jb == nb - 1:
                vt = vn_ref[...].T
            else:
                vt = _bf(v_ref[jb * KV_BLOCK:(jb + 1) * KV_BLOCK, :]).T
            for h in range(N_HEADS):
                vt_scr[jb, h, 0:hd, :] = vt[h * hd:(h + 1) * hd, :]
                vt_scr[jb, h, hd:, :] = jnp.ones((vt_scr.shape[2] - hd, KV_BLOCK), BF16)
        kpos = lax.broadcasted_iota(jnp.int32, (KV_BLOCK, tqp), 0)
        qpos = lax.broadcasted_iota(jnp.int32, (KV_BLOCK, tqp), 1)
        if fox:
            vis = kpos <= qpos
        else:
            vis = (kpos // CHUNK) <= (qpos // CHUNK)
        vis = vis & (kpos < tk_valid - nfull * KV_BLOCK)
        mb_scr[...] = jnp.where(vis, 0.0, NEG)

    q = q_ref[...]
    lane_sub = lax.broadcasted_iota(jnp.int32, (1, w), 1) // sub_w
    if tqp > tq:
        qm_scr[...] = jnp.zeros_like(qm_scr)
    for n in range(n_sub):
        qm_scr[n, 0:tq, :] = jnp.where(lane_sub == n, q, jnp.zeros_like(q))
    m_scr[...] = jnp.full(m_scr.shape, NEG, F32)
    r_scr[...] = jnp.full(r_scr.shape, NEG, F32)
    acc_scr[...] = jnp.zeros_like(acc_scr)

    def scores(j, n, masked):
        start = pl.multiple_of(j * KV_BLOCK, KV_BLOCK)
        h = n // subs_per_head
        kb = kn_ref[...] if (split_kv and masked) else _bf(k_ref[pl.ds(start, KV_BLOCK), :])
        s = _dot_nt(kb, qm_scr[n])
        if fox:
            s = s + (fr_ref[h:h + 1, 0:tqp] - fc_ref[pl.ds(start, KV_BLOCK), h:h + 1])
        if masked:
            s = s + mb_scr[...]
        s_scr[n] = s
        m_scr[n] = jnp.maximum(m_scr[n], jnp.max(s, axis=0, keepdims=True))

    def values(j, n):
        h = n // subs_per_head
        m = m_scr[n]
        alpha = jnp.exp2(r_scr[n] - m)
        r_scr[n] = m
        p = jnp.exp2(s_scr[n] - m)
        acc_scr[n] = alpha * acc_scr[n] + _dot(vt_scr[j, h], _bf(p))

    @pl.when(nfull == 0)
    def _():
        for n in range(n_sub):
            scores(0, n, True)

    @pl.when(nfull > 0)
    def _():
        for n in range(n_sub):
            scores(0, n, False)

    def body(j, carry):
        for n in range(n_sub):
            values(j - 1, n)
            scores(j, n, False)
        return carry

    lax.fori_loop(1, nfull, body, 0)

    @pl.when(nfull > 0)
    def _():
        for n in range(n_sub):
            values(nfull - 1, n)
            scores(nfull, n, True)

    for n in range(n_sub):
        values(nfull, n)

    def normalised(n):
        acc = acc_scr[n]
        return acc[0:hd] / acc[hd:hd + 1]

    if fox:
        out_t = jnp.concatenate([normalised(h) for h in range(N_HEADS)], axis=0)
        o_ref[...] = _bf(out_t.T[0:tq, :])
    else:
        lamv = lam_ref[...]
        lam = (jnp.exp(jnp.sum(lamv[0:1] * lamv[1:2], axis=-1, keepdims=True))
               - jnp.exp(jnp.sum(lamv[2:3] * lamv[3:4], axis=-1, keepdims=True)) + lam_init)
        parts = [normalised(2 * h) - lam * normalised(2 * h + 1) for h in range(N_HEADS)]
        dy = jnp.concatenate(parts, axis=0).T[0:tq, :]
        ms = _head_sum(dy * dy, HEAD_DIM) * (1.0 / HEAD_DIM)
        o_ref[...] = _bf(dy * lax.rsqrt(ms + EPS) * g_ref[...] * (1.0 - lam_init))


def _attention(q, kb, vb, extras, *, fox, tq, off, tk_valid, lam_init=0.0, new_kv=None):
    b, t, w = q.shape
    tk = kb.shape[1]
    split_kv = new_kv is not None
    nb = tk // KV_BLOCK + (1 if split_kv else 0)
    tqp = max(tq, LANES)
    n_sub = N_HEADS if fox else N_SUB
    assert off % KV_BLOCK == 0 and (tq == KV_BLOCK or t == tq)
    kernel = functools.partial(_attn_kernel, fox=fox, off=off, tk_valid=tk_valid, tq=tq, lam_init=lam_init,
                               split_kv=split_kv)
    in_specs = [
        pl.BlockSpec((None, tq, w), lambda i, j: (i, j, 0)),
        pl.BlockSpec((None, tk, w), lambda i, j: (i, 0, 0)),
        pl.BlockSpec((None, tk, w), lambda i, j: (i, 0, 0)),
    ]
    if split_kv:
        in_specs += [pl.BlockSpec((None, KV_BLOCK, w), lambda i, j: (i, 0, 0))] * 2
        extras = tuple(new_kv) + tuple(extras)
    if fox:
        in_specs += [
            pl.BlockSpec((None, nb * KV_BLOCK, LANES), lambda i, j: (i, 0, 0)),
            pl.BlockSpec((None, None, 8, KV_BLOCK), lambda i, j: (i, (off + j * tq) // KV_BLOCK, 0, 0)),
        ]
    else:
        in_specs += [
            pl.BlockSpec((4, DIFF_SUB), lambda i, j: (0, 0)),
            pl.BlockSpec((1, w), lambda i, j: (0, 0)),
        ]
    return pl.pallas_call(
        kernel,
        out_shape=jax.ShapeDtypeStruct((b, t, w), BF16),
        grid=(b, t // tq),
        in_specs=in_specs,
        out_specs=pl.BlockSpec((None, tq, w), lambda i, j: (i, j, 0)),
        scratch_shapes=[
            pltpu.VMEM((nb, N_HEADS, HEAD_DIM + BF16_ROWS, KV_BLOCK), BF16),
            pltpu.VMEM((n_sub, tqp, w), BF16),
            pltpu.VMEM((KV_BLOCK, tqp), F32),
            pltpu.VMEM((n_sub, 1, tqp), F32),
            pltpu.VMEM((n_sub, 1, tqp), F32),
            pltpu.VMEM((n_sub, HEAD_DIM + BF16_ROWS, tqp), F32),
            pltpu.VMEM((n_sub, KV_BLOCK, tqp), F32),
        ],
        compiler_params=_params(2),
        name="fox_attention" if fox else "diff_attention",
    )(q, kb, vb, *extras)


def _fox_attention(q, kb, vb, fcol, frow, tq, off, tk_valid, new_kv=None):
    return _attention(q, kb, vb, (fcol, frow), fox=True, tq=tq, off=off, tk_valid=tk_valid, new_kv=new_kv)


def _diff_proj_kernel(x_ref, sc_ref, sh_ref, w_ref, cos_ref, sin_ref, *refs):
    q_ref, k_ref, v_ref, kb_ref, vb_ref = refs[-5:]
    u = x_ref[...] * (1.0 + sc_ref[...]) + sh_ref[...]
    p = _dot(_bf(u), w_ref[...])
    w = BRANCH_W
    cos = cos_ref[...]
    sin = sin_ref[...]
    q_ref[...] = _bf((p[:, 0:w] * cos + p[:, w:2 * w] * sin) * (DIFF_SUB ** -0.5 * LOG2E))
    k = p[:, 2 * w:3 * w] * cos + p[:, 3 * w:4 * w] * sin
    v = p[:, 4 * w:5 * w]
    _store_layer_rows(k_ref, k)
    _store_layer_rows(v_ref, v)
    kb_ref[...] = _bf(k)
    vb_ref[...] = _bf(v)


def _diff_proj(x, scale, shift, w_diff, pos, tm, depth, layer, state):
    b, t, _ = x.shape
    w = BRANCH_W
    cos, sin = _rope_tables(pos, DIFF_SUB, ROT_DIM, ROPE_THETA, N_SUB)
    row = lambda width: pl.BlockSpec((None, tm, width), lambda i, j: (i, j, 0))
    sds = lambda dt: jax.ShapeDtypeStruct((b, t, w), dt)
    k_out, v_out = (_layer_slot(depth, layer, b, t, tm, w, F32, state is None) for _ in range(2))
    prior = () if state is None else tuple(state)
    n_in = 6
    q, k, v, kb, vb = pl.pallas_call(
        _diff_proj_kernel,
        out_shape=(sds(BF16), k_out[0], v_out[0], sds(BF16), sds(BF16)),
        grid=(b, t // tm),
        in_specs=[
            row(D_MODEL),
            pl.BlockSpec((None, 1, D_MODEL), lambda i, j: (i, 0, 0)),
            pl.BlockSpec((None, 1, D_MODEL), lambda i, j: (i, 0, 0)),
            pl.BlockSpec((D_MODEL, 5 * w), lambda i, j: (0, 0)),
            pl.BlockSpec((tm, w), lambda i, j: (j, 0)),
            pl.BlockSpec((tm, w), lambda i, j: (j, 0)),
        ] + [pl.BlockSpec(memory_space=pl.ANY)] * len(prior),
        out_specs=(row(w), k_out[1], v_out[1], row(w), row(w)),
        input_output_aliases={n_in: 1, n_in + 1: 2} if prior else {},
        compiler_params=_params(2),
        name="diff_projection",
    )(x, scale, shift, w_diff, cos, sin, *prior)
    return q, kb, vb, (k, v)


def _diff_attention(q, kb, vb, diff_lambda, subln_g, tq, off, tk_valid, lam_init, new_kv=None):
    g = jnp.tile(subln_g.astype(F32), N_HEADS).reshape(1, BRANCH_W)
    return _attention(q, kb, vb, (diff_lambda.astype(F32), g), fox=False, tq=tq, off=off,
                      tk_valid=tk_valid, lam_init=lam_init, new_kv=new_kv)


ROUTE_W1, ROUTE_W2, ROUTE_I1, ROUTE_I2 = 8, 9, 10, 11


def _top2_route(logits):
    lane = lax.broadcasted_iota(jnp.int32, logits.shape, 1).astype(F32)
    lg = jnp.where(lane < N_EXPERTS, logits, -jnp.inf)
    v1 = jnp.max(lg, axis=-1, keepdims=True)
    i1 = jnp.min(jnp.where(lg == v1, lane, float(LANES)), axis=-1, keepdims=True)
    lg2 = jnp.where(lane == i1, -jnp.inf, lg)
    v2 = jnp.max(lg2, axis=-1, keepdims=True)
    i2 = jnp.min(jnp.where(lg2 == v2, lane, float(LANES)), axis=-1, keepdims=True)
    e2 = jnp.exp(v2 - v1)
    w1 = 1.0 / (1.0 + e2)
    w2 = e2 / (1.0 + e2)
    rec = jnp.where(lane == i1, w1, 0.0) + jnp.where(lane == i2, w2, 0.0)
    for slot, val in ((ROUTE_W1, w1), (ROUTE_W2, w2), (ROUTE_I1, i1), (ROUTE_I2, i2)):
        rec = rec + jnp.where(lane == float(slot), val, 0.0)
    return rec


def _merge_kernel(*refs, with_router):
    if with_router:
        (x_ref, sc1_ref, sh1_ref, g1_ref, sc2_ref, sh2_ref, hr_ref, hf_ref, hc_ref, hd_ref,
         wg_ref, wb_ref, wo_ref, lg_ref, lb_ref, wr_ref, br_ref, x1_ref, u2_ref, rcol_ref, rrow_ref) = refs
    else:
        (x_ref, sc1_ref, sh1_ref, g1_ref, sc2_ref, sh2_ref, hr_ref, hf_ref, hc_ref, hd_ref,
         wg_ref, wb_ref, wo_ref, lg_ref, lb_ref, x1_ref, u2_ref) = refs
    x = x_ref[...]
    u = _bf(x * (1.0 + sc1_ref[...]) + sh1_ref[...])
    merged = None
    for n, h_ref in enumerate((hr_ref, hf_ref, hc_ref, hd_ref)):
        gate = _dot(u, wg_ref[:, n * D_MODEL:(n + 1) * D_MODEL])
        term = _sigmoid(gate) * _dot(h_ref[...], wb_ref[n])
        merged = term if merged is None else merged + term
    mix = _dot(_bf(merged), wo_ref[...])
    x1 = _layer_norm_rows(ALPHA * x + g1_ref[...] * mix, lg_ref[...], lb_ref[...])
    x1_ref[...] = x1
    u2 = x1 * (1.0 + sc2_ref[...]) + sh2_ref[...]
    u2_ref[...] = u2.astype(u2_ref.dtype)
    if with_router:
        rec = _top2_route(_dot(_bf(u2), wr_ref[...]) + br_ref[...])
        rcol_ref[...] = rec
        rrow_ref[...] = rec.T[8:16, :]


def _merge(x2d, mods, branches, w_gate, w_branch, w_out, ln_g, ln_b, router, tm, tiles_per_group):
    n = x2d.shape[0]
    r = mods[0].shape[1]
    w = BRANCH_W
    with_router = router is not None
    row = lambda width: pl.BlockSpec((tm, width), lambda i: (i, 0))
    mod_spec = pl.BlockSpec((None, r, D_MODEL), lambda i: (i // tiles_per_group, 0, 0))
    full = lambda shape: pl.BlockSpec(shape, lambda i: (0,) * len(shape))
    in_specs = ([row(D_MODEL)] + [mod_spec] * 5 + [row(w)] * 4
                + [full((D_MODEL, 4 * D_MODEL)), full((4, w, D_MODEL)), full((D_MODEL, D_MODEL)),
                   full((1, D_MODEL)), full((1, D_MODEL))])
    args = [x2d, *mods, *branches, w_gate, w_branch, w_out, ln_g.reshape(1, D_MODEL), ln_b.reshape(1, D_MODEL)]
    out_shape = [jax.ShapeDtypeStruct((n, D_MODEL), F32),
                 jax.ShapeDtypeStruct((n, D_MODEL), F32 if with_router else BF16)]
    out_specs = [row(D_MODEL), row(D_MODEL)]
    if with_router:
        in_specs += [full((D_MODEL, LANES)), full((1, LANES))]
        args += list(router)
        out_shape += [jax.ShapeDtypeStruct((n, LANES), F32), jax.ShapeDtypeStruct((n // tm, 8, tm), F32)]
        out_specs += [row(LANES), pl.BlockSpec((None, 8, tm), lambda i: (i, 0, 0))]
    return pl.pallas_call(
        functools.partial(_merge_kernel, with_router=with_router),
        out_shape=tuple(out_shape),
        grid=(n // tm,),
        in_specs=in_specs,
        out_specs=tuple(out_specs),
        compiler_params=_params(1),
        name="merge_outproj_ln",
    )(*args)


def _ffn_kernel(u_ref, x1_ref, g2_ref, wa_ref, wg_ref, wd_ref, lg_ref, lb_ref, o_ref, acc_scr):
    j = pl.program_id(1)

    @pl.when(j == 0)
    def _():
        acc_scr[...] = jnp.zeros_like(acc_scr)

    u = u_ref[...]
    a = _dot(u, wa_ref[...])
    g = _dot(u, wg_ref[...])
    acc_scr[...] += _dot(_bf(a * _sigmoid(a) * g), wd_ref[...])

    @pl.when(j == pl.num_programs(1) - 1)
    def _():
        z = ALPHA * x1_ref[...] + g2_ref[...] * acc_scr[...]
        o_ref[...] = _layer_norm_rows(z, lg_ref[...], lb_ref[...])


def _ffn_dense(u2, x1, gate2, w_up, w_down, ln_g, ln_b, tm, tiles_per_group, th):
    n = u2.shape[0]
    r = gate2.shape[1]
    nh = D_FF // th
    return pl.pallas_call(
        _ffn_kernel,
        out_shape=jax.ShapeDtypeStruct((n, D_MODEL), F32),
        grid=(n // tm, nh),
        in_specs=[
            pl.BlockSpec((tm, D_MODEL), lambda i, j: (i, 0)),
            pl.BlockSpec((tm, D_MODEL), lambda i, j: (i, 0)),
            pl.BlockSpec((None, r, D_MODEL), lambda i, j: (i // tiles_per_group, 0, 0)),
            pl.BlockSpec((D_MODEL, th), lambda i, j: (0, j)),
            pl.BlockSpec((D_MODEL, th), lambda i, j: (0, nh + j)),
            pl.BlockSpec((th, D_MODEL), lambda i, j: (j, 0)),
            pl.BlockSpec((1, D_MODEL), lambda i, j: (0, 0)),
            pl.BlockSpec((1, D_MODEL), lambda i, j: (0, 0)),
        ],
        out_specs=pl.BlockSpec((tm, D_MODEL), lambda i, j: (i, 0)),
        scratch_shapes=[pltpu.VMEM((tm, D_MODEL), F32)],
        compiler_params=_params(2),
        name="ffn_dense",
    )(u2, x1, gate2, w_up, w_up, w_down, ln_g.reshape(1, D_MODEL), ln_b.reshape(1, D_MODEL))


def _moe_kernel(u_ref, x1_ref, g2_ref, cmb_ref, wa_ref, wg_ref, wd_ref, lg_ref, lb_ref, o_ref, acc_scr):
    e = pl.program_id(1)
    j = pl.program_id(2)

    @pl.when((e == 0) & (j == 0))
    def _():
        acc_scr[...] = jnp.zeros_like(acc_scr)

    u = u_ref[...]
    a = _dot(u, wa_ref[...])
    g = _dot(u, wg_ref[...])
    y = _dot(_bf(a * _sigmoid(a) * g), wd_ref[...])
    cmb = cmb_ref[...]
    lane = lax.broadcasted_iota(jnp.int32, cmb.shape, 1)
    ce = jnp.sum(jnp.where(lane == e, cmb, 0.0), axis=-1, keepdims=True)
    acc_scr[...] += ce * y

    @pl.when((e == pl.num_programs(1) - 1) & (j == pl.num_programs(2) - 1))
    def _():
        z = ALPHA * x1_ref[...] + g2_ref[...] * acc_scr[...]
        o_ref[...] = _layer_norm_rows(z, lg_ref[...], lb_ref[...])


def _ffn_experts(u2, x1, gate2, combine, w_in, w_out, ln_g, ln_b, tm, tiles_per_group, th):
    n = u2.shape[0]
    r = gate2.shape[1]
    nh = D_EXPERT // th
    return pl.pallas_call(
        _moe_kernel,
        out_shape=jax.ShapeDtypeStruct((n, D_MODEL), F32),
        grid=(n // tm, N_EXPERTS, nh),
        in_specs=[
            pl.BlockSpec((tm, D_MODEL), lambda i, e, j: (i, 0)),
            pl.BlockSpec((tm, D_MODEL), lambda i, e, j: (i, 0)),
            pl.BlockSpec((None, r, D_MODEL), lambda i, e, j: (i // tiles_per_group, 0, 0)),
            pl.BlockSpec((tm, LANES), lambda i, e, j: (i, 0)),
            pl.BlockSpec((None, D_MODEL, th), lambda i, e, j: (e, 0, j)),
            pl.BlockSpec((None, D_MODEL, th), lambda i, e, j: (e, 0, nh + j)),
            pl.BlockSpec((None, th, D_MODEL), lambda i, e, j: (e, j, 0)),
            pl.BlockSpec((1, D_MODEL), lambda i, e, j: (0, 0)),
            pl.BlockSpec((1, D_MODEL), lambda i, e, j: (0, 0)),
        ],
        out_specs=pl.BlockSpec((tm, D_MODEL), lambda i, e, j: (i, 0)),
        scratch_shapes=[pltpu.VMEM((tm, D_MODEL), F32)],
        compiler_params=_params(3),
        name="ffn_experts",
    )(u2, x1, gate2, combine, w_in, w_in, w_out, ln_g.reshape(1, D_MODEL), ln_b.reshape(1, D_MODEL))


EXPERT_TILE = 1024


def _rank_kernel(rr_ref, rank_ref, cnt_ref, carry_scr):
    b = pl.program_id(0)

    @pl.when(b == 0)
    def _():
        carry_scr[...] = jnp.zeros_like(carry_scr)

    rr = rr_ref[...]
    tm = rr.shape[1]
    i1 = rr[2:3]
    i2 = rr[3:4]
    e = lax.broadcasted_iota(jnp.int32, (N_EXPERTS, tm), 0).astype(F32)
    oh1 = jnp.where(e == i1, 1.0, 0.0)
    oh2 = jnp.where(e == i2, 1.0, 0.0)
    sel = oh1 + oh2
    r = lax.broadcasted_iota(jnp.int32, (tm, tm), 0)
    c = lax.broadcasted_iota(jnp.int32, (tm, tm), 1)
    triu = jnp.where(r <= c, 1.0, 0.0).astype(BF16)
    csum = _dot(_bf(sel), triu)
    carry = carry_scr[:, 0:1]
    rank = csum - sel + carry
    r1 = jnp.sum(oh1 * rank, axis=0, keepdims=True)
    r2 = jnp.sum(oh2 * rank, axis=0, keepdims=True)
    rec = jnp.concatenate([i1, i2, r1, r2, jnp.zeros((4, tm), F32)], axis=0)
    rank_ref[...] = rec.astype(jnp.int32)
    total = carry + csum[:, tm - 1:tm]
    carry_scr[...] = jnp.broadcast_to(total, carry_scr.shape)
    cnt_ref[...] = jnp.broadcast_to(total, cnt_ref.shape).astype(jnp.int32)


def _route_ranks(route_row):
    nblk, _, tm = route_row.shape
    return pl.pallas_call(
        _rank_kernel,
        out_shape=(jax.ShapeDtypeStruct((nblk, 8, tm), jnp.int32),
                   jax.ShapeDtypeStruct((N_EXPERTS, LANES), jnp.int32)),
        grid=(nblk,),
        in_specs=[pl.BlockSpec((None, 8, tm), lambda b: (b, 0, 0))],
        out_specs=(pl.BlockSpec((None, 8, tm), lambda b: (b, 0, 0)),
                   pl.BlockSpec((N_EXPERTS, LANES), lambda b: (0, 0))),
        scratch_shapes=[pltpu.VMEM((N_EXPERTS, LANES), F32)],
        compiler_params=_params(1),
        name="route_ranks",
    )(route_row)


def _row_copy(src_ref, src_row, dst_ref, dst_row, sem):
    return pltpu.make_async_copy(src_ref.at[pl.ds(src_row, 1)], dst_ref.at[pl.ds(dst_row, 1)], sem)


def _dispatch_kernel(off_ref, last_ref, rank_ref, u_ref, xs_ref, zero_scr, sem):
    tm = u_ref.shape[0]
    gt = zero_scr.shape[0]

    @pl.when(pl.program_id(0) == 0)
    def _():
        zero_scr[...] = jnp.zeros_like(zero_scr)
        for e in range(N_EXPERTS):
            @pl.when(last_ref[e] >= 0)
            def _():
                first = pl.multiple_of(last_ref[e], gt)
                fill = pltpu.make_async_copy(zero_scr, xs_ref.at[pl.ds(first, gt)], sem)
                fill.start()
                fill.wait()

    def start(r, carry):
        for k in range(2):
            slot = off_ref[rank_ref[k, r]] + rank_ref[2 + k, r]
            _row_copy(u_ref, r, xs_ref, slot, sem).start()
        return carry

    lax.fori_loop(0, tm, start, 0, unroll=8)

    def wait(r, carry):
        for k in range(2):
            _row_copy(u_ref, 0, xs_ref, 0, sem).wait()
        return carry

    lax.fori_loop(0, tm, wait, 0, unroll=8)


def _dispatch(u2, ranks, off, last_tile, n_slots, group_tile):
    n, d = u2.shape
    nblk, _, tm = ranks.shape
    grid_spec = pltpu.PrefetchScalarGridSpec(
        num_scalar_prefetch=2,
        grid=(nblk,),
        in_specs=[
            pl.BlockSpec((None, 8, tm), lambda b, off, last: (b, 0, 0), memory_space=pltpu.SMEM),
            pl.BlockSpec((tm, d), lambda b, off, last: (b, 0)),
        ],
        out_specs=pl.BlockSpec(memory_space=pl.ANY),
        scratch_shapes=[pltpu.VMEM((group_tile, d), F32), pltpu.SemaphoreType.DMA],
    )
    return pl.pallas_call(
        _dispatch_kernel,
        out_shape=jax.ShapeDtypeStruct((n_slots, d), F32),
        grid_spec=grid_spec,
        compiler_params=_params(1),
        name="expert_dispatch",
    )(off, last_tile, ranks, u2)


def _grouped_kernel(toff_ref, nt_ref, xs_ref, wa_ref, wg_ref, wd_ref, ys_ref, xb_scr, acc_scr):
    i = pl.program_id(0)
    j = pl.program_id(1)

    @pl.when(i < nt_ref[0])
    def _():
        @pl.when(j == 0)
        def _():
            xb_scr[...] = _bf(xs_ref[...])
            acc_scr[...] = jnp.zeros_like(acc_scr)

        u = xb_scr[...]
        a = _dot(u, wa_ref[...])
        g = _dot(u, wg_ref[...])
        acc_scr[...] += _dot(_bf(a * _sigmoid(a) * g), wd_ref[...])

        @pl.when(j == pl.num_programs(1) - 1)
        def _():
            ys_ref[...] = acc_scr[...]


def _grouped_experts(xs, tile_off, n_tiles, w_in, w_out, tm, th):
    n_slots, d = xs.shape
    nh = D_EXPERT // th

    def expert(i, toff):
        e = 0
        for k in range(1, N_EXPERTS):
            e = e + (i >= toff[k]).astype(jnp.int32)
        return e

    def tile(i, nt):
        return jnp.minimum(i, nt[0] - 1)

    grid_spec = pltpu.PrefetchScalarGridSpec(
        num_scalar_prefetch=2,
        grid=(n_slots // tm, nh),
        in_specs=[
            pl.BlockSpec((tm, d), lambda i, j, toff, nt: (tile(i, nt), 0)),
            pl.BlockSpec((None, d, th), lambda i, j, toff, nt: (expert(tile(i, nt), toff), 0, j)),
            pl.BlockSpec((None, d, th), lambda i, j, toff, nt: (expert(tile(i, nt), toff), 0, nh + j)),
            pl.BlockSpec((None, th, d), lambda i, j, toff, nt: (expert(tile(i, nt), toff), j, 0)),
        ],
        out_specs=pl.BlockSpec((tm, d), lambda i, j, toff, nt: (tile(i, nt), 0)),
        scratch_shapes=[pltpu.VMEM((tm, d), BF16), pltpu.VMEM((tm, d), F32)],
    )
    return pl.pallas_call(
        _grouped_kernel,
        out_shape=jax.ShapeDtypeStruct((n_slots, d), F32),
        grid_spec=grid_spec,
        compiler_params=_params(2),
        name="grouped_experts",
    )(tile_off, n_tiles, xs, w_in, w_in, w_out)


def _combine_kernel(off_ref, rank_ref, ys_ref, x1_ref, g2_ref, rcol_ref, lg_ref, lb_ref, o_ref,
                    y1_scr, y2_scr, sem):
    tm = x1_ref.shape[0]
    bufs = (y1_scr, y2_scr)

    def start(r, carry):
        for k in range(2):
            slot = off_ref[rank_ref[k, r]] + rank_ref[2 + k, r]
            _row_copy(ys_ref, slot, bufs[k], r, sem).start()
        return carry

    lax.fori_loop(0, tm, start, 0, unroll=8)

    def wait(r, carry):
        for k in range(2):
            _row_copy(ys_ref, 0, bufs[k], 0, sem).wait()
        return carry

    lax.fori_loop(0, tm, wait, 0, unroll=8)
    rec = rcol_ref[...]
    f = rec[:, ROUTE_W1:ROUTE_W1 + 1] * y1_scr[...] + rec[:, ROUTE_W2:ROUTE_W2 + 1] * y2_scr[...]
    z = ALPHA * x1_ref[...] + g2_ref[...] * f
    o_ref[...] = _layer_norm_rows(z, lg_ref[...], lb_ref[...])


def _combine(ys, ranks, off, x1, gate2, route_col, ln_g, ln_b, tiles_per_group):
    n, d = x1.shape
    nblk, _, tm = ranks.shape
    r = gate2.shape[1]
    grid_spec = pltpu.PrefetchScalarGridSpec(
        num_scalar_prefetch=1,
        grid=(nblk,),
        in_specs=[
            pl.BlockSpec((None, 8, tm), lambda b, off: (b, 0, 0), memory_space=pltpu.SMEM),
            pl.BlockSpec(memory_space=pl.ANY),
            pl.BlockSpec((tm, d), lambda b, off: (b, 0)),
            pl.BlockSpec((None, r, d), lambda b, off: (b // tiles_per_group, 0, 0)),
            pl.BlockSpec((tm, LANES), lambda b, off: (b, 0)),
            pl.BlockSpec((1, d), lambda b, off: (0, 0)),
            pl.BlockSpec((1, d), lambda b, off: (0, 0)),
        ],
        out_specs=pl.BlockSpec((tm, d), lambda b, off: (b, 0)),
        scratch_shapes=[pltpu.VMEM((tm, d), F32), pltpu.VMEM((tm, d), F32), pltpu.SemaphoreType.DMA],
    )
    return pl.pallas_call(
        _combine_kernel,
        out_shape=jax.ShapeDtypeStruct((n, d), F32),
        grid_spec=grid_spec,
        compiler_params=_params(1),
        name="expert_combine_ln",
    )(off, ranks, ys, x1, gate2, route_col, ln_g.reshape(1, d), ln_b.reshape(1, d))


def _ffn_routed(u2, x1, gate2, route_col, route_row, w_in, w_out, ln_g, ln_b, tiles_per_group):
    n = u2.shape[0]
    tm = min(EXPERT_TILE, n)
    assert n % tm == 0
    ranks, counts = _route_ranks(route_row)
    cnt = counts[:, 0]
    padded = ((cnt + tm - 1) // tm) * tm
    off = (jnp.cumsum(padded) - padded).astype(jnp.int32)
    last_tile = jnp.where(padded > 0, off + padded - tm, -1).astype(jnp.int32)
    n_slots = 2 * n + N_EXPERTS * tm
    xs = _dispatch(u2, ranks, off, last_tile, n_slots, tm)
    ys = _grouped_experts(xs, (off // tm).astype(jnp.int32), (jnp.sum(padded) // tm).astype(jnp.int32).reshape(1),
                          w_in, w_out, tm, 896)
    return _combine(ys, ranks, off.astype(jnp.int32), x1, gate2, route_col, ln_g, ln_b, tiles_per_group)


def _split_w_in(w_in_l):
    w = BRANCH_W
    col = lambda off, n: w_in_l[:, off:off + n]
    o = 0
    ret_q, ret_k, ret_v, ret_g = (col(o + i * w, w) for i in range(4)); o += 4 * w
    fox_q, fox_k, fox_v = (col(o + i * w, w) for i in range(3)); o += 3 * w
    fox_f = col(o, N_HEADS); o += N_HEADS
    conv = col(o, 2 * w); o += 2 * w
    diff_q, diff_k, diff_v = (col(o + i * w, w) for i in range(3)); o += 3 * w
    gate = col(o, 4 * D_MODEL)
    rot_r = lambda m: _rotated_columns(m, HEAD_DIM, HEAD_DIM)
    rot_d = lambda m: _rotated_columns(m, DIFF_SUB, ROT_DIM)
    w_ret = jnp.concatenate([ret_q, rot_r(ret_q), ret_k, rot_r(ret_k), ret_v, ret_g], axis=1)
    w_fox = jnp.concatenate([fox_q, fox_k, fox_v, fox_f, jnp.zeros((D_MODEL, LANES - N_HEADS), F32)], axis=1)
    w_diff = jnp.concatenate([diff_q, rot_d(diff_q), diff_k, rot_d(diff_k), diff_v], axis=1)
    return _bf(w_ret), _bf(w_fox), _bf(conv), _bf(w_diff), _bf(gate)


def _block_diag_state(s):
    b = s.shape[0]
    eye = jnp.eye(N_HEADS, dtype=s.dtype)
    return jnp.einsum('bhde,hg->bhdge', s, eye).reshape(b, BRANCH_W, BRANCH_W)


def _diag_blocks(s_bd):
    b = s_bd.shape[0]
    s4 = s_bd.reshape(b, N_HEADS, HEAD_DIM, N_HEADS, HEAD_DIM)
    return jnp.stack([s4[:, h, :, h, :] for h in range(N_HEADS)], axis=1)


def _pad_rows(a, rows):
    return jnp.concatenate([a, jnp.zeros((a.shape[0], rows - a.shape[1]) + a.shape[2:], a.dtype)], axis=1)


def _trunk_layer(x, mod, pos, past, l, lw, sample, depth, kv_state):
    b, t, _ = x.shape
    d = D_MODEL
    w = BRANCH_W
    shift1, scale1, gate1 = (mod[0][:, None, i * d:(i + 1) * d] for i in range(3))
    shift2, scale2, gate2 = (mod[1][:, None, i * d:(i + 1) * d] for i in range(3))
    lam_init = 0.8 - 0.6 * math.exp(-0.3 * l)
    chunk = min(t, KV_BLOCK)
    tm_proj = min(t, 1024)

    s0 = jnp.zeros((b, w, w), F32) if past is None else _block_diag_state(past[5])
    h_ret, s_bd = _retention(x, scale1, shift1, lw['w_ret'], pos, s0, chunk)
    ret_state = _diag_blocks(s_bd)

    if past is None:
        hist = jnp.zeros((b, HIST_ROWS, w), F32)
    else:
        hist = jnp.concatenate([jnp.zeros((b, HIST_ROWS - (CONV_W - 1), w), F32), past[6]], axis=1)
    h_conv, tail = _conv_mixer(x, scale1, shift1, lw['w_conv_in'], hist, lw['w_conv'], lw['b_conv'],
                               lw['conv_ln_g'], lw['conv_ln_b'], chunk)
    conv_buf = tail[:, HIST_ROWS - (CONV_W - 1):, :]

    fox_state, diff_state = (None, None) if kv_state is None else kv_state
    fq, fkb, fvb, lf, fox_state = _fox_proj(x, scale1, shift1, lw['w_fox'], lw['b_fox_f'], tm_proj, depth, l, fox_state)
    if past is None:
        off, tk_valid = 0, t
        k_all, v_all, lf_all, fox_new = fkb, fvb, lf, None
    else:
        p_len = past[0].shape[1]
        off, tk_valid = p_len, p_len + t
        tk_pad = p_len + KV_BLOCK
        k_all, v_all = past[0].reshape(b, p_len, w), past[1].reshape(b, p_len, w)
        fox_new = (_pad_rows(fkb, KV_BLOCK), _pad_rows(fvb, KV_BLOCK))
        past_lf = jnp.concatenate([past[2].astype(F32), jnp.zeros((b, p_len, LANES - N_HEADS), F32)], axis=2)
        lf_all = _pad_rows(jnp.concatenate([past_lf, lf], axis=1), tk_pad)
    fcol, frow = _logf_cumsum(lf_all)
    tq = min(t, KV_BLOCK)
    h_fox = _fox_attention(fq, k_all, v_all, fcol, frow, tq, off, tk_valid, fox_new)

    dq, dkb, dvb, diff_state = _diff_proj(x, scale1, shift1, lw['w_diff'], pos, tm_proj, depth, l, diff_state)
    if past is None:
        dk_all, dv_all, diff_new = dkb, dvb, None
    else:
        dk_all, dv_all = past[3].reshape(b, p_len, w), past[4].reshape(b, p_len, w)
        diff_new = (_pad_rows(dkb, KV_BLOCK), _pad_rows(dvb, KV_BLOCK))
    h_diff = _diff_attention(dq, dk_all, dv_all, lw['diff_lambda'], lw['diff_subln_g'], tq, off, tk_valid, lam_init,
                             diff_new)

    n = b * t
    if sample:
        tm_merge = tm_ffn = min(n, 256)
        tpg_merge = tpg_ffn = 1
        rows_mod = lambda m, tm: jnp.repeat(m, t, axis=1).reshape(n // tm, tm, d)
    else:
        tm_merge, tm_ffn = min(t, 512), min(t, 1024)
        tpg_merge, tpg_ffn = t // tm_merge, t // tm_ffn
        rows_mod = lambda m, tm: m
    mods = tuple(rows_mod(m, tm_merge) for m in (scale1, shift1, gate1, scale2, shift2))
    branches = tuple(h.reshape(n, w) for h in (h_ret, h_fox, h_conv, h_diff))
    merged = _merge(x.reshape(n, d), mods, branches, lw['w_gate'], lw['w_branch'], lw['w_out'],
                    lw['ln_g'][0], lw['ln_b'][0], lw.get('router'), tm_merge, tpg_merge)
    g2 = rows_mod(gate2, tm_ffn)
    if 'router' in lw and not sample:
        x1, u2, route_col, route_row = merged
        x2 = _ffn_routed(u2, x1, rows_mod(gate2, tm_merge), route_col, route_row, lw['w_exp_in'], lw['w_exp_out'],
                         lw['ln_g'][1], lw['ln_b'][1], tpg_merge)
    elif 'router' in lw:
        x1, u2, route_col, _ = merged
        x2 = _ffn_experts(_bf(u2), x1, g2, route_col, lw['w_exp_in'], lw['w_exp_out'], lw['ln_g'][1], lw['ln_b'][1],
                          tm_ffn, tpg_ffn, 512)
    else:
        x1, u2 = merged
        x2 = _ffn_dense(u2, x1, g2, lw['w_ffn_in'], lw['w_ffn_out'], lw['ln_g'][1], lw['ln_b'][1],
                        tm_ffn, tpg_ffn, 1408)
    return x2.reshape(b, t, d), (fox_state, diff_state), (ret_state, conv_buf)


def _state_outputs(kv_state, small):
    (fk, fv, lf), (dk, dv) = kv_state
    depth, b, t, w = fk.shape
    heads = lambda a, nh: a.reshape(depth, b, t, nh, w // nh)
    ret_state, conv_buf = (jnp.stack(a) for a in zip(*small))
    return (heads(fk, N_HEADS), heads(fv, N_HEADS), lf[..., :N_HEADS], heads(dk, N_SUB), heads(dv, N_HEADS),
            ret_state, conv_buf)


def kernel(x_prompt, x_sample, c_prompt, c_sample, cache_fox_k, cache_fox_v, cache_fox_logf, cache_diff_k, cache_diff_v, state_ret, state_conv, w_in, b_fox_f, w_conv, b_conv, conv_ln_g, conv_ln_b, diff_lambda, diff_subln_g, w_branch, w_out, w_ada, b_ada, ln_g, ln_b, w_ffn_in, w_ffn_out, w_router, b_router, w_exp_in, w_exp_out):
    depth = w_in.shape[0]
    bp = x_prompt.shape[0]
    past_len = cache_fox_k.shape[2]
    pos_p = jnp.arange(x_prompt.shape[1], dtype=jnp.int32)
    pos_s = past_len + jnp.arange(x_sample.shape[1], dtype=jnp.int32)
    mod = _modulation(jnp.concatenate([c_prompt, c_sample], axis=0), w_ada, b_ada)
    yp, ys = x_prompt, x_sample
    kv_p = kv_s = None
    small_p, small_s = [], []
    for l in range(depth):
        w_ret, w_fox, w_conv_in, w_diff, w_gate = _split_w_in(w_in[l])
        lw = dict(w_ret=w_ret, w_fox=w_fox, w_conv_in=w_conv_in, w_diff=w_diff, w_gate=w_gate,
                  b_fox_f=b_fox_f[l], w_conv=w_conv[l], b_conv=b_conv[l], conv_ln_g=conv_ln_g[l],
                  conv_ln_b=conv_ln_b[l], diff_lambda=diff_lambda[l], diff_subln_g=diff_subln_g[l],
                  w_branch=_bf(w_branch[l]), w_out=_bf(w_out[l]), ln_g=ln_g[l], ln_b=ln_b[l])
        if l % 2 == 0:
            lw['w_ffn_in'] = _bf(w_ffn_in[l // 2])
            lw['w_ffn_out'] = _bf(w_ffn_out[l // 2])
        else:
            wr = jnp.concatenate([w_router[l // 2], jnp.zeros((D_MODEL, LANES - N_EXPERTS), F32)], axis=1)
            br = jnp.concatenate([b_router[l // 2].astype(F32), jnp.zeros((LANES - N_EXPERTS,), F32)]).reshape(1, LANES)
            lw['router'] = (_bf(wr), br)
            lw['w_exp_in'] = _bf(w_exp_in[l // 2])
            lw['w_exp_out'] = _bf(w_exp_out[l // 2])
        past_l = (cache_fox_k[l], cache_fox_v[l], cache_fox_logf[l], cache_diff_k[l], cache_diff_v[l],
                  state_ret[l], state_conv[l])
        yp, kv_p, st_p = _trunk_layer(yp, mod[l][:, :bp], pos_p, None, l, lw, False, depth, kv_p)
        ys, kv_s, st_s = _trunk_layer(ys, mod[l][:, bp:], pos_s, past_l, l, lw, True, depth, kv_s)
        small_p.append(st_p)
        small_s.append(st_s)
    return (yp, ys) + _state_outputs(kv_p, small_p) + _state_outputs(kv_s, small_s)
```

```python
import functools
import math

import jax
import jax.numpy as jnp
from jax import lax
from jax.experimental import pallas as pl
from jax.experimental.pallas import tpu as pltpu

D_MODEL = 1024
BRANCH_W = 256
HEAD_DIM = 64
N_HEADS = 4
DIFF_SUB = 32
N_SUB = 8
ROT_DIM = DIFF_SUB // 4
RET_THETA = 10000.0
ROPE_THETA = 500000.0
CHUNK = 64
CONV_W = 31
D_FF = 2816
N_EXPERTS = 8
D_EXPERT = 3584
DEPTH = 2
ALPHA = (2.0 * DEPTH) ** 0.25
EPS = 1e-5
NEG = -1e30
LOG2E = math.log2(math.e)

LANES = 128
BF16_ROWS = 16
KV_BLOCK = 256
HIST_ROWS = 32
VMEM_LIMIT = 56 * 1024 * 1024

F32 = jnp.float32
BF16 = jnp.bfloat16


def _bf(x):
    return x.astype(BF16)


def _dot(a, b):
    return jnp.dot(a, b, preferred_element_type=F32)


def _dot_nt(a, b):
    return lax.dot_general(a, b, (((1,), (1,)), ((), ())), preferred_element_type=F32)


def _dot_tn(a, b):
    return lax.dot_general(a, b, (((0,), (0,)), ((), ())), preferred_element_type=F32)


def _sigmoid(x):
    return 1.0 / (1.0 + jnp.exp(-x))


def _params(n_axes):
    return pltpu.CompilerParams(dimension_semantics=("arbitrary",) * n_axes,
                                vmem_limit_bytes=VMEM_LIMIT)


def _head_sum(y, width):
    n = y.shape[-1]
    r = lax.broadcasted_iota(jnp.int32, (n, n), 0) // width
    c = lax.broadcasted_iota(jnp.int32, (n, n), 1) // width
    bd = jnp.where(r == c, 1.0, 0.0).astype(BF16)
    hi = _bf(y)
    lo = _bf(y - hi.astype(F32))
    return _dot(hi, bd) + _dot(lo, bd)


def _layer_norm_rows(z, g, b):
    mu = jnp.mean(z, axis=-1, keepdims=True)
    d = z - mu
    var = jnp.mean(d * d, axis=-1, keepdims=True)
    return d * lax.rsqrt(var + EPS) * g + b


def _mod_kernel(c_ref, w_ref, b_ref, o_ref):
    c = c_ref[...]
    sc = _bf(c * _sigmoid(c))
    o_ref[...] = _dot(sc, _bf(w_ref[...])) + b_ref[...]


def _modulation(c_all, w_ada, b_ada):
    rows = c_all.shape[0]
    depth = w_ada.shape[0]
    d3 = w_ada.shape[-1]
    nj = d3 // D_MODEL
    return pl.pallas_call(
        _mod_kernel,
        out_shape=jax.ShapeDtypeStruct((depth, 2, rows, d3), F32),
        grid=(depth * 2, nj),
        in_specs=[
            pl.BlockSpec((rows, D_MODEL), lambda i, j: (0, 0)),
            pl.BlockSpec((None, None, D_MODEL, D_MODEL), lambda i, j: (i // 2, i % 2, 0, j)),
            pl.BlockSpec((None, None, 1, D_MODEL), lambda i, j: (i // 2, i % 2, 0, j)),
        ],
        out_specs=pl.BlockSpec((None, None, rows, D_MODEL), lambda i, j: (i // 2, i % 2, 0, j)),
        compiler_params=_params(2),
        name="adaln_modulation",
    )(c_all, w_ada, b_ada.reshape(depth, 2, 1, d3))


def _ret_kernel(x_ref, sc_ref, sh_ref, w_ref, cos_ref, sin_ref, dmask_ref, qdec_ref, kdec_ref,
                cdec_ref, s0_ref, h_ref, sout_ref, s_scr):
    c = pl.program_id(1)

    @pl.when(c == 0)
    def _():
        s_scr[...] = s0_ref[...]

    u = x_ref[...] * (1.0 + sc_ref[...]) + sh_ref[...]
    p = _dot(_bf(u), w_ref[...])
    w = BRANCH_W
    cos = cos_ref[...]
    sin = sin_ref[...]
    q = p[:, 0:w] * cos + p[:, w:2 * w] * sin
    k = (p[:, 2 * w:3 * w] * cos + p[:, 3 * w:4 * w] * sin) * (HEAD_DIM ** -0.5)
    v = p[:, 4 * w:5 * w]
    g = p[:, 5 * w:6 * w]
    rows = q.shape[0]
    lane_head = lax.broadcasted_iota(jnp.int32, (1, w), 1) // HEAD_DIM
    kb = _bf(k)
    vb = _bf(v)
    y = jnp.zeros((rows, w), F32)
    for h in range(N_HEADS):
        mh = lane_head == h
        qh = _bf(jnp.where(mh, q, 0.0))
        a = _dot_nt(qh, kb) * dmask_ref[h]
        y = y + jnp.where(mh, _dot(_bf(a), vb), 0.0)
    s_prev = s_scr[...]
    y = y + _dot(_bf(q * qdec_ref[...]), _bf(s_prev))
    kv = _dot_tn(_bf(k * kdec_ref[...]), vb)
    r = lax.broadcasted_iota(jnp.int32, (w, w), 0) // HEAD_DIM
    cc = lax.broadcasted_iota(jnp.int32, (w, w), 1) // HEAD_DIM
    s_new = cdec_ref[...] * s_prev + jnp.where(r == cc, kv, 0.0)
    s_scr[...] = s_new
    sout_ref[...] = s_new
    mu = _head_sum(y, HEAD_DIM) * (1.0 / HEAD_DIM)
    d = y - mu
    var = _head_sum(d * d, HEAD_DIM) * (1.0 / HEAD_DIM)
    hn = d * lax.rsqrt(var + EPS)
    h_ref[...] = _bf(hn * (g * _sigmoid(g)))


def _retention_tables(chunk):
    log_g = jnp.log1p(-jnp.exp2(-5.0 - jnp.arange(N_HEADS, dtype=F32)))
    idx = jnp.arange(chunk, dtype=F32)
    dist = jnp.abs(idx[:, None] - idx[None, :])
    sub = jnp.arange(chunk) // CHUNK
    vis = sub[None, :] <= sub[:, None]
    dmask = jnp.where(vis[None], jnp.exp(log_g[:, None, None] * dist[None]), 0.0)
    lg_lane = jnp.repeat(log_g, HEAD_DIM)[None, :]
    qdec = jnp.exp(lg_lane * (idx[:, None] + 1.0))
    kdec = jnp.exp(lg_lane * (chunk - 1.0 - idx[:, None]))
    cdec = jnp.exp(lg_lane * chunk)
    return dmask.astype(F32), qdec, kdec, cdec


def _rope_tables(pos, dim, rot_dim, theta, n_rep):
    half = rot_dim // 2
    inv_freq = jnp.exp(-math.log(theta) * jnp.arange(half, dtype=F32) / half)
    ang = pos.astype(F32)[:, None] * inv_freq[None, :]
    t = pos.shape[0]
    cos = jnp.concatenate([jnp.cos(ang), jnp.cos(ang), jnp.ones((t, dim - rot_dim), F32)], axis=1)
    sin = jnp.concatenate([jnp.sin(ang), jnp.sin(ang), jnp.zeros((t, dim - rot_dim), F32)], axis=1)
    return jnp.tile(cos, (1, n_rep)), jnp.tile(sin, (1, n_rep))


def _rotated_columns(w, dim, rot_dim):
    half = rot_dim // 2
    k, n = w.shape
    wh = w.reshape(k, n // dim, dim)
    rot = jnp.concatenate([-wh[..., half:rot_dim], wh[..., :half], jnp.zeros_like(wh[..., rot_dim:])], axis=-1)
    return rot.reshape(k, n)


def _retention(x, scale, shift, w_ret, pos, s0_bd, chunk):
    b, t, _ = x.shape
    nc = t // chunk
    cos, sin = _rope_tables(pos, HEAD_DIM, HEAD_DIM, RET_THETA, N_HEADS)
    dmask, qdec, kdec, cdec = _retention_tables(chunk)
    w = BRANCH_W
    full = lambda shape: pl.BlockSpec(shape, lambda i, j: (0,) * len(shape))
    return pl.pallas_call(
        _ret_kernel,
        out_shape=(jax.ShapeDtypeStruct((b, t, w), BF16), jax.ShapeDtypeStruct((b, w, w), F32)),
        grid=(b, nc),
        in_specs=[
            pl.BlockSpec((None, chunk, D_MODEL), lambda i, j: (i, j, 0)),
            pl.BlockSpec((None, 1, D_MODEL), lambda i, j: (i, 0, 0)),
            pl.BlockSpec((None, 1, D_MODEL), lambda i, j: (i, 0, 0)),
            full((D_MODEL, 6 * w)),
            pl.BlockSpec((chunk, w), lambda i, j: (j, 0)),
            pl.BlockSpec((chunk, w), lambda i, j: (j, 0)),
            full((N_HEADS, chunk, chunk)),
            full((chunk, w)),
            full((chunk, w)),
            full((1, w)),
            pl.BlockSpec((None, w, w), lambda i, j: (i, 0, 0)),
        ],
        out_specs=(pl.BlockSpec((None, chunk, w), lambda i, j: (i, j, 0)),
                   pl.BlockSpec((None, w, w), lambda i, j: (i, 0, 0))),
        scratch_shapes=[pltpu.VMEM((w, w), F32)],
        compiler_params=_params(2),
        name="retention_mixer",
    )(x, scale, shift, w_ret, cos, sin, dmask, qdec, kdec, cdec, s0_bd)


def _conv_kernel(x_ref, sc_ref, sh_ref, w_ref, hist_ref, wc_ref, bc_ref, lg_ref, lb_ref,
                 h_ref, tail_ref, xp_scr):
    c = pl.program_id(1)
    rows = x_ref.shape[0]
    w = BRANCH_W
    pad = HIST_ROWS - (CONV_W - 1)

    @pl.when(c == 0)
    def _():
        xp_scr[0:HIST_ROWS, :] = hist_ref[...]

    u = x_ref[...] * (1.0 + sc_ref[...]) + sh_ref[...]
    p = _dot(_bf(u), w_ref[...])
    xp_scr[HIST_ROWS:HIST_ROWS + rows, :] = p[:, :w] * _sigmoid(p[:, w:])
    sub = 8
    acc = jnp.zeros((rows, w), F32) + bc_ref[...]
    for rho in range(sub):
        z = None
        for m in range((pad + CONV_W - 1) // sub + 1):
            j = sub * m + rho - pad
            if not 0 <= j < CONV_W:
                continue
            span = rows if rho == 0 else rows + sub
            term = xp_scr[sub * m:sub * m + span, :] * wc_ref[j:j + 1, :]
            z = term if z is None else z + term
        acc = acc + z[rho:rho + rows, :]
    y = _layer_norm_rows(acc, lg_ref[...], lb_ref[...])
    h_ref[...] = _bf(y * _sigmoid(y))
    tail = xp_scr[rows:rows + HIST_ROWS, :]
    tail_ref[...] = tail
    xp_scr[0:HIST_ROWS, :] = tail


def _conv_mixer(x, scale, shift, w_cv, hist, w_conv, b_conv, ln_g, ln_b, chunk):
    b, t, _ = x.shape
    nc = t // chunk
    w = BRANCH_W
    full = lambda shape: pl.BlockSpec(shape, lambda i, j: (0,) * len(shape))
    wc = jnp.concatenate([w_conv, jnp.zeros((HIST_ROWS - CONV_W, w), F32)], axis=0)
    return pl.pallas_call(
        _conv_kernel,
        out_shape=(jax.ShapeDtypeStruct((b, t, w), BF16), jax.ShapeDtypeStruct((b, HIST_ROWS, w), F32)),
        grid=(b, nc),
        in_specs=[
            pl.BlockSpec((None, chunk, D_MODEL), lambda i, j: (i, j, 0)),
            pl.BlockSpec((None, 1, D_MODEL), lambda i, j: (i, 0, 0)),
            pl.BlockSpec((None, 1, D_MODEL), lambda i, j: (i, 0, 0)),
            full((D_MODEL, 2 * w)),
            pl.BlockSpec((None, HIST_ROWS, w), lambda i, j: (i, 0, 0)),
            full((HIST_ROWS, w)),
            full((1, w)),
            full((1, w)),
            full((1, w)),
        ],
        out_specs=(pl.BlockSpec((None, chunk, w), lambda i, j: (i, j, 0)),
                   pl.BlockSpec((None, HIST_ROWS, w), lambda i, j: (i, 0, 0))),
        scratch_shapes=[pltpu.VMEM((HIST_ROWS + chunk, w), F32)],
        compiler_params=_params(2),
        name="conv_mixer",
    )(x, scale, shift, w_cv, hist, wc, b_conv.reshape(1, w), ln_g.reshape(1, w), ln_b.reshape(1, w))


def _fox_proj_kernel(x_ref, sc_ref, sh_ref, w_ref, bf_ref, *refs):
    q_ref, k_ref, v_ref, kb_ref, vb_ref, lf_ref = refs[-6:]
    u = x_ref[...] * (1.0 + sc_ref[...]) + sh_ref[...]
    p = _dot(_bf(u), w_ref[...])
    w = BRANCH_W
    q_ref[...] = _bf(p[:, 0:w] * (HEAD_DIM ** -0.5 * LOG2E))
    k = p[:, w:2 * w]
    v = p[:, 2 * w:3 * w]
    _store_layer_rows(k_ref, k)
    _store_layer_rows(v_ref, v)
    kb_ref[...] = _bf(k)
    vb_ref[...] = _bf(v)
    z = p[:, 3 * w:] + bf_ref[...]
    lf = jnp.minimum(z, 0.0) - jnp.log(1.0 + jnp.exp(-jnp.abs(z)))
    lane = lax.broadcasted_iota(jnp.int32, lf.shape, 1)
    _store_layer_rows(lf_ref, jnp.where(lane < N_HEADS, lf, 0.0))


def _layer_slot(depth, layer, b, t, tm, width, dt, first):
    shape = jax.ShapeDtypeStruct((depth, b, t, width), dt)
    if first:
        return shape, pl.BlockSpec((depth, None, tm, width), lambda i, j: (0, i, j, 0))
    return shape, pl.BlockSpec((None, None, tm, width), lambda i, j: (layer, i, j, 0))


def _store_layer_rows(ref, rows):
    if len(ref.shape) == 3:
        for d in range(ref.shape[0]):
            ref[d] = rows
    else:
        ref[...] = rows


def _fox_proj(x, scale, shift, w_fox, b_f, tm, depth, layer, state):
    b, t, _ = x.shape
    w = BRANCH_W
    nw = 3 * w + LANES
    bfp = jnp.concatenate([b_f.astype(F32), jnp.zeros((LANES - N_HEADS,), F32)]).reshape(1, LANES)
    row = lambda width: pl.BlockSpec((None, tm, width), lambda i, j: (i, j, 0))
    sds = lambda width, dt: jax.ShapeDtypeStruct((b, t, width), dt)
    k_out, v_out, lf_out = (_layer_slot(depth, layer, b, t, tm, width, F32, state is None) for width in (w, w, LANES))
    prior = () if state is None else tuple(state)
    n_in = 5
    q, k, v, kb, vb, lf = pl.pallas_call(
        _fox_proj_kernel,
        out_shape=(sds(w, BF16), k_out[0], v_out[0], sds(w, BF16), sds(w, BF16), lf_out[0]),
        grid=(b, t // tm),
        in_specs=[
            row(D_MODEL),
            pl.BlockSpec((None, 1, D_MODEL), lambda i, j: (i, 0, 0)),
            pl.BlockSpec((None, 1, D_MODEL), lambda i, j: (i, 0, 0)),
            pl.BlockSpec((D_MODEL, nw), lambda i, j: (0, 0)),
            pl.BlockSpec((1, LANES), lambda i, j: (0, 0)),
        ] + [pl.BlockSpec(memory_space=pl.ANY)] * len(prior),
        out_specs=(row(w), k_out[1], v_out[1], row(w), row(w), lf_out[1]),
        input_output_aliases={n_in: 1, n_in + 1: 2, n_in + 2: 5} if prior else {},
        compiler_params=_params(2),
        name="fox_projection",
    )(x, scale, shift, w_fox, bfp, *prior)
    return q, kb, vb, lf[layer], (k, v, lf)


def _cumsum_kernel(lf_ref, col_ref, row_ref):
    n = KV_BLOCK
    nb = lf_ref.shape[0] // n
    r = lax.broadcasted_iota(jnp.int32, (n, n), 0)
    c = lax.broadcasted_iota(jnp.int32, (n, n), 1)
    tri = jnp.where(c <= r, 1.0, 0.0).astype(BF16)
    carry = jnp.zeros((1, LANES), F32)
    for jb in range(nb):
        x = lf_ref[jb * n:(jb + 1) * n, :]
        hi = _bf(x)
        r1 = x - hi.astype(F32)
        mid = _bf(r1)
        lo = _bf(r1 - mid.astype(F32))
        cs = _dot(tri, hi) + _dot(tri, mid) + _dot(tri, lo) + carry
        carry = cs[n - 1:n, :]
        scaled = cs * LOG2E
        col_ref[jb * n:(jb + 1) * n, :] = scaled
        row_ref[jb] = scaled.T[0:8, :]


def _logf_cumsum(lf):
    b, tk, _ = lf.shape
    nb = tk // KV_BLOCK
    return pl.pallas_call(
        _cumsum_kernel,
        out_shape=(jax.ShapeDtypeStruct((b, tk, LANES), F32),
                   jax.ShapeDtypeStruct((b, nb, 8, KV_BLOCK), F32)),
        grid=(b,),
        in_specs=[pl.BlockSpec((None, tk, LANES), lambda i: (i, 0, 0))],
        out_specs=(pl.BlockSpec((None, tk, LANES), lambda i: (i, 0, 0)),
                   pl.BlockSpec((None, nb, 8, KV_BLOCK), lambda i: (i, 0, 0, 0))),
        compiler_params=_params(1),
        name="logf_cumsum",
    )(lf)


def _attn_kernel(*refs, fox, off, tk_valid, tq, lam_init, split_kv):
    q_ref, k_ref, v_ref = refs[:3]
    refs = refs[3:]
    if split_kv:
        kn_ref, vn_ref = refs[:2]
        refs = refs[2:]
    if fox:
        fc_ref, fr_ref, o_ref, vt_scr, qm_scr, mb_scr, m_scr, r_scr, acc_scr, s_scr = refs
    else:
        lam_ref, g_ref, o_ref, vt_scr, qm_scr, mb_scr, m_scr, r_scr, acc_scr, s_scr = refs
    i = pl.program_id(1)
    w = BRANCH_W
    n_sub, tqp, _ = qm_scr.shape
    sub_w = w // n_sub
    subs_per_head = n_sub // N_HEADS
    nb = vt_scr.shape[0]
    hd = HEAD_DIM
    q0 = off + i * tq
    nfull = q0 // KV_BLOCK

    @pl.when(i == 0)
    def _():
        for jb in range(nb):
            if split_kv and jb == nb - 1:
                vt = vn_ref[...].T
            else:
                vt = _bf(v_ref[jb * KV_BLOCK:(jb + 1) * KV_BLOCK, :]).T
            for h in range(N_HEADS):
                vt_scr[jb, h, 0:hd, :] = vt[h * hd:(h + 1) * hd, :]
                vt_scr[jb, h, hd:, :] = jnp.ones((vt_scr.shape[2] - hd, KV_BLOCK), BF16)
        kpos = lax.broadcasted_iota(jnp.int32, (KV_BLOCK, tqp), 0)
        qpos = lax.broadcasted_iota(jnp.int32, (KV_BLOCK, tqp), 1)
        if fox:
            vis = kpos <= qpos
        else:
            vis = (kpos // CHUNK) <= (qpos // CHUNK)
        vis = vis & (kpos < tk_valid - nfull * KV_BLOCK)
        mb_scr[...] = jnp.where(vis, 0.0, NEG)

    q = q_ref[...]
    lane_sub = lax.broadcasted_iota(jnp.int32, (1, w), 1) // sub_w
    if tqp > tq:
        qm_scr[...] = jnp.zeros_like(qm_scr)
    for n in range(n_sub):
        qm_scr[n, 0:tq, :] = jnp.where(lane_sub == n, q, jnp.zeros_like(q))
    m_scr[...] = jnp.full(m_scr.shape, NEG, F32)
    r_scr[...] = jnp.full(r_scr.shape, NEG, F32)
    acc_scr[...] = jnp.zeros_like(acc_scr)

    def scores(j, n, masked):
        start = pl.multiple_of(j * KV_BLOCK, KV_BLOCK)
        h = n // subs_per_head
        kb = kn_ref[...] if (split_kv and masked) else _bf(k_ref[pl.ds(start, KV_BLOCK), :])
        s = _dot_nt(kb, qm_scr[n])
        if fox:
            s = s + (fr_ref[h:h + 1, 0:tqp] - fc_ref[pl.ds(start, KV_BLOCK), h:h + 1])
        if masked:
            s = s + mb_scr[...]
        s_scr[n] = s
        m_scr[n] = jnp.maximum(m_scr[n], jnp.max(s, axis=0, keepdims=True))

    def values(j, n):
        h = n // subs_per_head
        m = m_scr[n]
        alpha = jnp.exp2(r_scr[n] - m)
        r_scr[n] = m
        p = jnp.exp2(s_scr[n] - m)
        acc_scr[n] = alpha * acc_scr[n] + _dot(vt_scr[j, h], _bf(p))

    @pl.when(nfull == 0)
    def _():
        for n in range(n_sub):
            scores(0, n, True)

    @pl.when(nfull > 0)
    def _():
        for n in range(n_sub):
            scores(0, n, False)

    def body(j, carry):
        for n in range(n_sub):
            values(j - 1, n)
            scores(j, n, False)
        return carry

    lax.fori_loop(1, nfull, body, 0)

    @pl.when(nfull > 0)
    def _():
        for n in range(n_sub):
            values(nfull - 1, n)
            scores(nfull, n, True)

    for n in range(n_sub):
        values(nfull, n)

    def normalised(n):
        acc = acc_scr[n]
        return acc[0:hd] / acc[hd:hd + 1]

    if fox:
        out_t = jnp.concatenate([normalised(h) for h in range(N_HEADS)], axis=0)
        o_ref[...] = _bf(out_t.T[0:tq, :])
    else:
        lamv = lam_ref[...]
        lam = (jnp.exp(jnp.sum(lamv[0:1] * lamv[1:2], axis=-1, keepdims=True))
               - jnp.exp(jnp.sum(lamv[2:3] * lamv[3:4], axis=-1, keepdims=True)) + lam_init)
        parts = [normalised(2 * h) - lam * normalised(2 * h + 1) for h in range(N_HEADS)]
        dy = jnp.concatenate(parts, axis=0).T[0:tq, :]
        ms = _head_sum(dy * dy, HEAD_DIM) * (1.0 / HEAD_DIM)
        o_ref[...] = _bf(dy * lax.rsqrt(ms + EPS) * g_ref[...] * (1.0 - lam_init))


def _attention(q, kb, vb, extras, *, fox, tq, off, tk_valid, lam_init=0.0, new_kv=None):
    b, t, w = q.shape
    tk = kb.shape[1]
    split_kv = new_kv is not None
    nb = tk // KV_BLOCK + (1 if split_kv else 0)
    tqp = max(tq, LANES)
    n_sub = N_HEADS if fox else N_SUB
    assert off % KV_BLOCK == 0 and (tq == KV_BLOCK or t == tq)
    kernel = functools.partial(_attn_kernel, fox=fox, off=off, tk_valid=tk_valid, tq=tq, lam_init=lam_init,
                               split_kv=split_kv)
    in_specs = [
        pl.BlockSpec((None, tq, w), lambda i, j: (i, j, 0)),
        pl.BlockSpec((None, tk, w), lambda i, j: (i, 0, 0)),
        pl.BlockSpec((None, tk, w), lambda i, j: (i, 0, 0)),
    ]
    if split_kv:
        in_specs += [pl.BlockSpec((None, KV_BLOCK, w), lambda i, j: (i, 0, 0))] * 2
        extras = tuple(new_kv) + tuple(extras)
    if fox:
        in_specs += [
            pl.BlockSpec((None, nb * KV_BLOCK, LANES), lambda i, j: (i, 0, 0)),
            pl.BlockSpec((None, None, 8, KV_BLOCK), lambda i, j: (i, (off + j * tq) // KV_BLOCK, 0, 0)),
        ]
    else:
        in_specs += [
            pl.BlockSpec((4, DIFF_SUB), lambda i, j: (0, 0)),
            pl.BlockSpec((1, w), lambda i, j: (0, 0)),
        ]
    return pl.pallas_call(
        kernel,
        out_shape=jax.ShapeDtypeStruct((b, t, w), BF16),
        grid=(b, t // tq),
        in_specs=in_specs,
        out_specs=pl.BlockSpec((None, tq, w), lambda i, j: (i, j, 0)),
        scratch_shapes=[
            pltpu.VMEM((nb, N_HEADS, HEAD_DIM + BF16_ROWS, KV_BLOCK), BF16),
            pltpu.VMEM((n_sub, tqp, w), BF16),
            pltpu.VMEM((KV_BLOCK, tqp), F32),
            pltpu.VMEM((n_sub, 1, tqp), F32),
            pltpu.VMEM((n_sub, 1, tqp), F32),
            pltpu.VMEM((n_sub, HEAD_DIM + BF16_ROWS, tqp), F32),
            pltpu.VMEM((n_sub, KV_BLOCK, tqp), F32),
        ],
        compiler_params=_params(2),
        name="fox_attention" if fox else "diff_attention",
    )(q, kb, vb, *extras)


def _fox_attention(q, kb, vb, fcol, frow, tq, off, tk_valid, new_kv=None):
    return _attention(q, kb, vb, (fcol, frow), fox=True, tq=tq, off=off, tk_valid=tk_valid, new_kv=new_kv)


def _diff_proj_kernel(x_ref, sc_ref, sh_ref, w_ref, cos_ref, sin_ref, *refs):
    q_ref, k_ref, v_ref, kb_ref, vb_ref = refs[-5:]
    u = x_ref[...] * (1.0 + sc_ref[...]) + sh_ref[...]
    p = _dot(_bf(u), w_ref[...])
    w = BRANCH_W
    cos = cos_ref[...]
    sin = sin_ref[...]
    q_ref[...] = _bf((p[:, 0:w] * cos + p[:, w:2 * w] * sin) * (DIFF_SUB ** -0.5 * LOG2E))
    k = p[:, 2 * w:3 * w] * cos + p[:, 3 * w:4 * w] * sin
    v = p[:, 4 * w:5 * w]
    _store_layer_rows(k_ref, k)
    _store_layer_rows(v_ref, v)
    kb_ref[...] = _bf(k)
    vb_ref[...] = _bf(v)


def _diff_proj(x, scale, shift, w_diff, pos, tm, depth, layer, state):
    b, t, _ = x.shape
    w = BRANCH_W
    cos, sin = _rope_tables(pos, DIFF_SUB, ROT_DIM, ROPE_THETA, N_SUB)
    row = lambda width: pl.BlockSpec((None, tm, width), lambda i, j: (i, j, 0))
    sds = lambda dt: jax.ShapeDtypeStruct((b, t, w), dt)
    k_out, v_out = (_layer_slot(depth, layer, b, t, tm, w, F32, state is None) for _ in range(2))
    prior = () if state is None else tuple(state)
    n_in = 6
    q, k, v, kb, vb = pl.pallas_call(
        _diff_proj_kernel,
        out_shape=(sds(BF16), k_out[0], v_out[0], sds(BF16), sds(BF16)),
        grid=(b, t // tm),
        in_specs=[
            row(D_MODEL),
            pl.BlockSpec((None, 1, D_MODEL), lambda i, j: (i, 0, 0)),
            pl.BlockSpec((None, 1, D_MODEL), lambda i, j: (i, 0, 0)),
            pl.BlockSpec((D_MODEL, 5 * w), lambda i, j: (0, 0)),
            pl.BlockSpec((tm, w), lambda i, j: (j, 0)),
            pl.BlockSpec((tm, w), lambda i, j: (j, 0)),
        ] + [pl.BlockSpec(memory_space=pl.ANY)] * len(prior),
        out_specs=(row(w), k_out[1], v_out[1], row(w), row(w)),
        input_output_aliases={n_in: 1, n_in + 1: 2} if prior else {},
        compiler_params=_params(2),
        name="diff_projection",
    )(x, scale, shift, w_diff, cos, sin, *prior)
    return q, kb, vb, (k, v)


def _diff_attention(q, kb, vb, diff_lambda, subln_g, tq, off, tk_valid, lam_init, new_kv=None):
    g = jnp.tile(subln_g.astype(F32), N_HEADS).reshape(1, BRANCH_W)
    return _attention(q, kb, vb, (diff_lambda.astype(F32), g), fox=False, tq=tq, off=off,
                      tk_valid=tk_valid, lam_init=lam_init, new_kv=new_kv)


ROUTE_W1, ROUTE_W2, ROUTE_I1, ROUTE_I2 = 8, 9, 10, 11


def _top2_route(logits):
    lane = lax.broadcasted_iota(jnp.int32, logits.shape, 1).astype(F32)
    lg = jnp.where(lane < N_EXPERTS, logits, -jnp.inf)
    v1 = jnp.max(lg, axis=-1, keepdims=True)
    i1 = jnp.min(jnp.where(lg == v1, lane, float(LANES)), axis=-1, keepdims=True)
    lg2 = jnp.where(lane == i1, -jnp.inf, lg)
    v2 = jnp.max(lg2, axis=-1, keepdims=True)
    i2 = jnp.min(jnp.where(lg2 == v2, lane, float(LANES)), axis=-1, keepdims=True)
    e2 = jnp.exp(v2 - v1)
    w1 = 1.0 / (1.0 + e2)
    w2 = e2 / (1.0 + e2)
    rec = jnp.where(lane == i1, w1, 0.0) + jnp.where(lane == i2, w2, 0.0)
    for slot, val in ((ROUTE_W1, w1), (ROUTE_W2, w2), (ROUTE_I1, i1), (ROUTE_I2, i2)):
        rec = rec + jnp.where(lane == float(slot), val, 0.0)
    return rec


def _merge_kernel(*refs, with_router):
    if with_router:
        (x_ref, sc1_ref, sh1_ref, g1_ref, sc2_ref, sh2_ref, hr_ref, hf_ref, hc_ref, hd_ref,
         wg_ref, wb_ref, wo_ref, lg_ref, lb_ref, wr_ref, br_ref, x1_ref, u2_ref, rcol_ref, rrow_ref) = refs
    else:
        (x_ref, sc1_ref, sh1_ref, g1_ref, sc2_ref, sh2_ref, hr_ref, hf_ref, hc_ref, hd_ref,
         wg_ref, wb_ref, wo_ref, lg_ref, lb_ref, x1_ref, u2_ref) = refs
    x = x_ref[...]
    u = _bf(x * (1.0 + sc1_ref[...]) + sh1_ref[...])
    merged = None
    for n, h_ref in enumerate((hr_ref, hf_ref, hc_ref, hd_ref)):
        gate = _dot(u, wg_ref[:, n * D_MODEL:(n + 1) * D_MODEL])
        term = _sigmoid(gate) * _dot(h_ref[...], wb_ref[n])
        merged = term if merged is None else merged + term
    mix = _dot(_bf(merged), wo_ref[...])
    x1 = _layer_norm_rows(ALPHA * x + g1_ref[...] * mix, lg_ref[...], lb_ref[...])
    x1_ref[...] = x1
    u2 = x1 * (1.0 + sc2_ref[...]) + sh2_ref[...]
    u2_ref[...] = u2.astype(u2_ref.dtype)
    if with_router:
        rec = _top2_route(_dot(_bf(u2), wr_ref[...]) + br_ref[...])
        rcol_ref[...] = rec
        rrow_ref[...] = rec.T[8:16, :]


def _merge(x2d, mods, branches, w_gate, w_branch, w_out, ln_g, ln_b, router, tm, tiles_per_group):
    n = x2d.shape[0]
    r = mods[0].shape[1]
    w = BRANCH_W
    with_router = router is not None
    row = lambda width: pl.BlockSpec((tm, width), lambda i: (i, 0))
    mod_spec = pl.BlockSpec((None, r, D_MODEL), lambda i: (i // tiles_per_group, 0, 0))
    full = lambda shape: pl.BlockSpec(shape, lambda i: (0,) * len(shape))
    resident = lambda shape: pl.BlockSpec(shape, lambda i: (0,) * len(shape), pipeline_mode=pl.Buffered(1))
    in_specs = ([row(D_MODEL)] + [mod_spec] * 5 + [row(w)] * 4
                + [resident((D_MODEL, 4 * D_MODEL)), resident((4, w, D_MODEL)), resident((D_MODEL, D_MODEL)),
                   full((1, D_MODEL)), full((1, D_MODEL))])
    args = [x2d, *mods, *branches, w_gate, w_branch, w_out, ln_g.reshape(1, D_MODEL), ln_b.reshape(1, D_MODEL)]
    out_shape = [jax.ShapeDtypeStruct((n, D_MODEL), F32),
                 jax.ShapeDtypeStruct((n, D_MODEL), F32 if with_router else BF16)]
    out_specs = [row(D_MODEL), row(D_MODEL)]
    if with_router:
        in_specs += [full((D_MODEL, LANES)), full((1, LANES))]
        args += list(router)
        out_shape += [jax.ShapeDtypeStruct((n, LANES), F32), jax.ShapeDtypeStruct((n // tm, 8, tm), F32)]
        out_specs += [row(LANES), pl.BlockSpec((None, 8, tm), lambda i: (i, 0, 0))]
    return pl.pallas_call(
        functools.partial(_merge_kernel, with_router=with_router),
        out_shape=tuple(out_shape),
        grid=(n // tm,),
        in_specs=in_specs,
        out_specs=tuple(out_specs),
        compiler_params=_params(1),
        name="merge_outproj_ln",
    )(*args)


def _swiglu_chunks(u, w_up_ref, w_down_ref, hidden, chunk):
    acc = None
    for c0 in range(0, hidden, chunk):
        c1 = min(c0 + chunk, hidden)
        a = _dot(u, w_up_ref[:, c0:c1])
        g = _dot(u, w_up_ref[:, hidden + c0:hidden + c1])
        y = _dot(_bf(a * _sigmoid(a) * g), w_down_ref[c0:c1, :])
        acc = y if acc is None else acc + y
    return acc


def _ffn_kernel(u_ref, x1_ref, g2_ref, wu_ref, wd_ref, lg_ref, lb_ref, o_ref, *, chunk):
    f = _swiglu_chunks(u_ref[...], wu_ref, wd_ref, D_FF, chunk)
    z = ALPHA * x1_ref[...] + g2_ref[...] * f
    o_ref[...] = _layer_norm_rows(z, lg_ref[...], lb_ref[...])


def _ffn_dense(u2, x1, gate2, w_up, w_down, ln_g, ln_b, tm, tiles_per_group, chunk):
    n = u2.shape[0]
    r = gate2.shape[1]
    resident = lambda shape: pl.BlockSpec(shape, lambda i: (0,) * len(shape), pipeline_mode=pl.Buffered(1))
    return pl.pallas_call(
        functools.partial(_ffn_kernel, chunk=chunk),
        out_shape=jax.ShapeDtypeStruct((n, D_MODEL), F32),
        grid=(n // tm,),
        in_specs=[
            pl.BlockSpec((tm, D_MODEL), lambda i: (i, 0)),
            pl.BlockSpec((tm, D_MODEL), lambda i: (i, 0)),
            pl.BlockSpec((None, r, D_MODEL), lambda i: (i // tiles_per_group, 0, 0)),
            resident((D_MODEL, 2 * D_FF)),
            resident((D_FF, D_MODEL)),
            pl.BlockSpec((1, D_MODEL), lambda i: (0, 0)),
            pl.BlockSpec((1, D_MODEL), lambda i: (0, 0)),
        ],
        out_specs=pl.BlockSpec((tm, D_MODEL), lambda i: (i, 0)),
        compiler_params=_params(1),
        name="ffn_dense",
    )(u2, x1, gate2, w_up, w_down, ln_g.reshape(1, D_MODEL), ln_b.reshape(1, D_MODEL))


def _moe_kernel(u_ref, x1_ref, g2_ref, cmb_ref, wa_ref, wg_ref, wd_ref, lg_ref, lb_ref, o_ref, acc_scr):
    e = pl.program_id(1)
    j = pl.program_id(2)

    @pl.when((e == 0) & (j == 0))
    def _():
        acc_scr[...] = jnp.zeros_like(acc_scr)

    u = u_ref[...]
    a = _dot(u, wa_ref[...])
    g = _dot(u, wg_ref[...])
    y = _dot(_bf(a * _sigmoid(a) * g), wd_ref[...])
    cmb = cmb_ref[...]
    lane = lax.broadcasted_iota(jnp.int32, cmb.shape, 1)
    ce = jnp.sum(jnp.where(lane == e, cmb, 0.0), axis=-1, keepdims=True)
    acc_scr[...] += ce * y

    @pl.when((e == pl.num_programs(1) - 1) & (j == pl.num_programs(2) - 1))
    def _():
        z = ALPHA * x1_ref[...] + g2_ref[...] * acc_scr[...]
        o_ref[...] = _layer_norm_rows(z, lg_ref[...], lb_ref[...])


def _ffn_experts(u2, x1, gate2, combine, w_in, w_out, ln_g, ln_b, tm, tiles_per_group, th):
    n = u2.shape[0]
    r = gate2.shape[1]
    nh = D_EXPERT // th
    return pl.pallas_call(
        _moe_kernel,
        out_shape=jax.ShapeDtypeStruct((n, D_MODEL), F32),
        grid=(n // tm, N_EXPERTS, nh),
        in_specs=[
            pl.BlockSpec((tm, D_MODEL), lambda i, e, j: (i, 0)),
            pl.BlockSpec((tm, D_MODEL), lambda i, e, j: (i, 0)),
            pl.BlockSpec((None, r, D_MODEL), lambda i, e, j: (i // tiles_per_group, 0, 0)),
            pl.BlockSpec((tm, LANES), lambda i, e, j: (i, 0)),
            pl.BlockSpec((None, D_MODEL, th), lambda i, e, j: (e, 0, j)),
            pl.BlockSpec((None, D_MODEL, th), lambda i, e, j: (e, 0, nh + j)),
            pl.BlockSpec((None, th, D_MODEL), lambda i, e, j: (e, j, 0)),
            pl.BlockSpec((1, D_MODEL), lambda i, e, j: (0, 0)),
            pl.BlockSpec((1, D_MODEL), lambda i, e, j: (0, 0)),
        ],
        out_specs=pl.BlockSpec((tm, D_MODEL), lambda i, e, j: (i, 0)),
        scratch_shapes=[pltpu.VMEM((tm, D_MODEL), F32)],
        compiler_params=_params(3),
        name="ffn_experts",
    )(u2, x1, gate2, combine, w_in, w_in, w_out, ln_g.reshape(1, D_MODEL), ln_b.reshape(1, D_MODEL))


EXPERT_TILE = 1024


def _rank_kernel(rr_ref, rank_ref, cnt_ref, carry_scr):
    b = pl.program_id(0)

    @pl.when(b == 0)
    def _():
        carry_scr[...] = jnp.zeros_like(carry_scr)

    rr = rr_ref[...]
    tm = rr.shape[1]
    i1 = rr[2:3]
    i2 = rr[3:4]
    e = lax.broadcasted_iota(jnp.int32, (N_EXPERTS, tm), 0).astype(F32)
    oh1 = jnp.where(e == i1, 1.0, 0.0)
    oh2 = jnp.where(e == i2, 1.0, 0.0)
    sel = oh1 + oh2
    r = lax.broadcasted_iota(jnp.int32, (tm, tm), 0)
    c = lax.broadcasted_iota(jnp.int32, (tm, tm), 1)
    triu = jnp.where(r <= c, 1.0, 0.0).astype(BF16)
    csum = _dot(_bf(sel), triu)
    carry = carry_scr[:, 0:1]
    rank = csum - sel + carry
    r1 = jnp.sum(oh1 * rank, axis=0, keepdims=True)
    r2 = jnp.sum(oh2 * rank, axis=0, keepdims=True)
    rec = jnp.concatenate([i1, i2, r1, r2, jnp.zeros((4, tm), F32)], axis=0)
    rank_ref[...] = rec.astype(jnp.int32)
    total = carry + csum[:, tm - 1:tm]
    carry_scr[...] = jnp.broadcast_to(total, carry_scr.shape)
    cnt_ref[...] = jnp.broadcast_to(total, cnt_ref.shape).astype(jnp.int32)


def _route_ranks(route_row):
    nblk, _, tm = route_row.shape
    return pl.pallas_call(
        _rank_kernel,
        out_shape=(jax.ShapeDtypeStruct((nblk, 8, tm), jnp.int32),
                   jax.ShapeDtypeStruct((N_EXPERTS, LANES), jnp.int32)),
        grid=(nblk,),
        in_specs=[pl.BlockSpec((None, 8, tm), lambda b: (b, 0, 0))],
        out_specs=(pl.BlockSpec((None, 8, tm), lambda b: (b, 0, 0)),
                   pl.BlockSpec((N_EXPERTS, LANES), lambda b: (0, 0))),
        scratch_shapes=[pltpu.VMEM((N_EXPERTS, LANES), F32)],
        compiler_params=_params(1),
        name="route_ranks",
    )(route_row)


def _row_copy(src_ref, src_row, dst_ref, dst_row, sem):
    return pltpu.make_async_copy(src_ref.at[pl.ds(src_row, 1)], dst_ref.at[pl.ds(dst_row, 1)], sem)


def _dispatch_kernel(off_ref, last_ref, rank_ref, u_ref, xs_ref, zero_scr, sem):
    tm = u_ref.shape[0]
    gt = zero_scr.shape[0]

    @pl.when(pl.program_id(0) == 0)
    def _():
        zero_scr[...] = jnp.zeros_like(zero_scr)
        for e in range(N_EXPERTS):
            @pl.when(last_ref[e] >= 0)
            def _():
                first = pl.multiple_of(last_ref[e], gt)
                fill = pltpu.make_async_copy(zero_scr, xs_ref.at[pl.ds(first, gt)], sem)
                fill.start()
                fill.wait()

    def start(r, carry):
        for k in range(2):
            slot = off_ref[rank_ref[k, r]] + rank_ref[2 + k, r]
            _row_copy(u_ref, r, xs_ref, slot, sem).start()
        return carry

    lax.fori_loop(0, tm, start, 0, unroll=8)

    def wait(r, carry):
        for k in range(2):
            _row_copy(u_ref, 0, xs_ref, 0, sem).wait()
        return carry

    lax.fori_loop(0, tm, wait, 0, unroll=8)


def _dispatch(u2, ranks, off, last_tile, n_slots, group_tile):
    n, d = u2.shape
    nblk, _, tm = ranks.shape
    grid_spec = pltpu.PrefetchScalarGridSpec(
        num_scalar_prefetch=2,
        grid=(nblk,),
        in_specs=[
            pl.BlockSpec((None, 8, tm), lambda b, off, last: (b, 0, 0), memory_space=pltpu.SMEM),
            pl.BlockSpec((tm, d), lambda b, off, last: (b, 0)),
        ],
        out_specs=pl.BlockSpec(memory_space=pl.ANY),
        scratch_shapes=[pltpu.VMEM((group_tile, d), F32), pltpu.SemaphoreType.DMA],
    )
    return pl.pallas_call(
        _dispatch_kernel,
        out_shape=jax.ShapeDtypeStruct((n_slots, d), F32),
        grid_spec=grid_spec,
        compiler_params=_params(1),
        name="expert_dispatch",
    )(off, last_tile, ranks, u2)


def _grouped_kernel(toff_ref, nt_ref, xs_ref, wi_ref, wo_ref, ys_ref, *, chunk):
    @pl.when(pl.program_id(0) < nt_ref[0])
    def _():
        ys_ref[...] = _swiglu_chunks(_bf(xs_ref[...]), wi_ref, wo_ref, D_EXPERT, chunk)


def _grouped_experts(xs, tile_off, n_tiles, w_in, w_out, tm, chunk):
    n_slots, d = xs.shape

    def expert(i, toff):
        e = 0
        for k in range(1, N_EXPERTS):
            e = e + (i >= toff[k]).astype(jnp.int32)
        return e

    def tile(i, nt):
        return jnp.minimum(i, nt[0] - 1)

    grid_spec = pltpu.PrefetchScalarGridSpec(
        num_scalar_prefetch=2,
        grid=(n_slots // tm,),
        in_specs=[
            pl.BlockSpec((tm, d), lambda i, toff, nt: (tile(i, nt), 0)),
            pl.BlockSpec((None, d, 2 * D_EXPERT), lambda i, toff, nt: (expert(tile(i, nt), toff), 0, 0),
                         pipeline_mode=pl.Buffered(1)),
            pl.BlockSpec((None, D_EXPERT, d), lambda i, toff, nt: (expert(tile(i, nt), toff), 0, 0),
                         pipeline_mode=pl.Buffered(1)),
        ],
        out_specs=pl.BlockSpec((tm, d), lambda i, toff, nt: (tile(i, nt), 0)),
    )
    return pl.pallas_call(
        functools.partial(_grouped_kernel, chunk=chunk),
        out_shape=jax.ShapeDtypeStruct((n_slots, d), F32),
        grid_spec=grid_spec,
        compiler_params=_params(1),
        name="grouped_experts",
    )(tile_off, n_tiles, xs, w_in, w_out)


def _combine_kernel(off_ref, rank_ref, ys_ref, x1_ref, g2_ref, rcol_ref, lg_ref, lb_ref, o_ref,
                    y1_scr, y2_scr, sem):
    tm = x1_ref.shape[0]
    bufs = (y1_scr, y2_scr)

    def start(r, carry):
        for k in range(2):
            slot = off_ref[rank_ref[k, r]] + rank_ref[2 + k, r]
            _row_copy(ys_ref, slot, bufs[k], r, sem).start()
        return carry

    lax.fori_loop(0, tm, start, 0, unroll=8)

    def wait(r, carry):
        for k in range(2):
            _row_copy(ys_ref, 0, bufs[k], 0, sem).wait()
        return carry

    lax.fori_loop(0, tm, wait, 0, unroll=8)
    rec = rcol_ref[...]
    f = rec[:, ROUTE_W1:ROUTE_W1 + 1] * y1_scr[...] + rec[:, ROUTE_W2:ROUTE_W2 + 1] * y2_scr[...]
    z = ALPHA * x1_ref[...] + g2_ref[...] * f
    o_ref[...] = _layer_norm_rows(z, lg_ref[...], lb_ref[...])


def _combine(ys, ranks, off, x1, gate2, route_col, ln_g, ln_b, tiles_per_group):
    n, d = x1.shape
    nblk, _, tm = ranks.shape
    r = gate2.shape[1]
    grid_spec = pltpu.PrefetchScalarGridSpec(
        num_scalar_prefetch=1,
        grid=(nblk,),
        in_specs=[
            pl.BlockSpec((None, 8, tm), lambda b, off: (b, 0, 0), memory_space=pltpu.SMEM),
            pl.BlockSpec(memory_space=pl.ANY),
            pl.BlockSpec((tm, d), lambda b, off: (b, 0)),
            pl.BlockSpec((None, r, d), lambda b, off: (b // tiles_per_group, 0, 0)),
            pl.BlockSpec((tm, LANES), lambda b, off: (b, 0)),
            pl.BlockSpec((1, d), lambda b, off: (0, 0)),
            pl.BlockSpec((1, d), lambda b, off: (0, 0)),
        ],
        out_specs=pl.BlockSpec((tm, d), lambda b, off: (b, 0)),
        scratch_shapes=[pltpu.VMEM((tm, d), F32), pltpu.VMEM((tm, d), F32), pltpu.SemaphoreType.DMA],
    )
    return pl.pallas_call(
        _combine_kernel,
        out_shape=jax.ShapeDtypeStruct((n, d), F32),
        grid_spec=grid_spec,
        compiler_params=_params(1),
        name="expert_combine_ln",
    )(off, ranks, ys, x1, gate2, route_col, ln_g.reshape(1, d), ln_b.reshape(1, d))


def _ffn_routed(u2, x1, gate2, route_col, route_row, w_in, w_out, ln_g, ln_b, tiles_per_group):
    n = u2.shape[0]
    tm = min(EXPERT_TILE, n)
    assert n % tm == 0
    ranks, counts = _route_ranks(route_row)
    cnt = counts[:, 0]
    padded = ((cnt + tm - 1) // tm) * tm
    off = (jnp.cumsum(padded) - padded).astype(jnp.int32)
    last_tile = jnp.where(padded > 0, off + padded - tm, -1).astype(jnp.int32)
    n_slots = 2 * n + N_EXPERTS * tm
    xs = _dispatch(u2, ranks, off, last_tile, n_slots, tm)
    ys = _grouped_experts(xs, (off // tm).astype(jnp.int32), (jnp.sum(padded) // tm).astype(jnp.int32).reshape(1),
                          w_in, w_out, tm, 512)
    return _combine(ys, ranks, off.astype(jnp.int32), x1, gate2, route_col, ln_g, ln_b, tiles_per_group)


def _split_w_in(w_in_l):
    w = BRANCH_W
    col = lambda off, n: w_in_l[:, off:off + n]
    o = 0
    ret_q, ret_k, ret_v, ret_g = (col(o + i * w, w) for i in range(4)); o += 4 * w
    fox_q, fox_k, fox_v = (col(o + i * w, w) for i in range(3)); o += 3 * w
    fox_f = col(o, N_HEADS); o += N_HEADS
    conv = col(o, 2 * w); o += 2 * w
    diff_q, diff_k, diff_v = (col(o + i * w, w) for i in range(3)); o += 3 * w
    gate = col(o, 4 * D_MODEL)
    rot_r = lambda m: _rotated_columns(m, HEAD_DIM, HEAD_DIM)
    rot_d = lambda m: _rotated_columns(m, DIFF_SUB, ROT_DIM)
    w_ret = jnp.concatenate([ret_q, rot_r(ret_q), ret_k, rot_r(ret_k), ret_v, ret_g], axis=1)
    w_fox = jnp.concatenate([fox_q, fox_k, fox_v, fox_f, jnp.zeros((D_MODEL, LANES - N_HEADS), F32)], axis=1)
    w_diff = jnp.concatenate([diff_q, rot_d(diff_q), diff_k, rot_d(diff_k), diff_v], axis=1)
    return _bf(w_ret), _bf(w_fox), _bf(conv), _bf(w_diff), _bf(gate)


def _block_diag_state(s):
    b = s.shape[0]
    eye = jnp.eye(N_HEADS, dtype=s.dtype)
    return jnp.einsum('bhde,hg->bhdge', s, eye).reshape(b, BRANCH_W, BRANCH_W)


def _diag_blocks(s_bd):
    b = s_bd.shape[0]
    s4 = s_bd.reshape(b, N_HEADS, HEAD_DIM, N_HEADS, HEAD_DIM)
    return jnp.stack([s4[:, h, :, h, :] for h in range(N_HEADS)], axis=1)


def _pad_rows(a, rows):
    return jnp.concatenate([a, jnp.zeros((a.shape[0], rows - a.shape[1]) + a.shape[2:], a.dtype)], axis=1)


def _trunk_layer(x, mod, pos, past, l, lw, sample, depth, kv_state):
    b, t, _ = x.shape
    d = D_MODEL
    w = BRANCH_W
    shift1, scale1, gate1 = (mod[0][:, None, i * d:(i + 1) * d] for i in range(3))
    shift2, scale2, gate2 = (mod[1][:, None, i * d:(i + 1) * d] for i in range(3))
    lam_init = 0.8 - 0.6 * math.exp(-0.3 * l)
    chunk = min(t, KV_BLOCK)
    tm_proj = min(t, 1024)

    s0 = jnp.zeros((b, w, w), F32) if past is None else _block_diag_state(past[5])
    h_ret, s_bd = _retention(x, scale1, shift1, lw['w_ret'], pos, s0, chunk)
    ret_state = _diag_blocks(s_bd)

    if past is None:
        hist = jnp.zeros((b, HIST_ROWS, w), F32)
    else:
        hist = jnp.concatenate([jnp.zeros((b, HIST_ROWS - (CONV_W - 1), w), F32), past[6]], axis=1)
    h_conv, tail = _conv_mixer(x, scale1, shift1, lw['w_conv_in'], hist, lw['w_conv'], lw['b_conv'],
                               lw['conv_ln_g'], lw['conv_ln_b'], chunk)
    conv_buf = tail[:, HIST_ROWS - (CONV_W - 1):, :]

    fox_state, diff_state = (None, None) if kv_state is None else kv_state
    fq, fkb, fvb, lf, fox_state = _fox_proj(x, scale1, shift1, lw['w_fox'], lw['b_fox_f'], tm_proj, depth, l, fox_state)
    if past is None:
        off, tk_valid = 0, t
        k_all, v_all, lf_all, fox_new = fkb, fvb, lf, None
    else:
        p_len = past[0].shape[1]
        off, tk_valid = p_len, p_len + t
        tk_pad = p_len + KV_BLOCK
        k_all, v_all = past[0].reshape(b, p_len, w), past[1].reshape(b, p_len, w)
        fox_new = (_pad_rows(fkb, KV_BLOCK), _pad_rows(fvb, KV_BLOCK))
        past_lf = jnp.concatenate([past[2].astype(F32), jnp.zeros((b, p_len, LANES - N_HEADS), F32)], axis=2)
        lf_all = _pad_rows(jnp.concatenate([past_lf, lf], axis=1), tk_pad)
    fcol, frow = _logf_cumsum(lf_all)
    tq = min(t, KV_BLOCK)
    h_fox = _fox_attention(fq, k_all, v_all, fcol, frow, tq, off, tk_valid, fox_new)

    dq, dkb, dvb, diff_state = _diff_proj(x, scale1, shift1, lw['w_diff'], pos, tm_proj, depth, l, diff_state)
    if past is None:
        dk_all, dv_all, diff_new = dkb, dvb, None
    else:
        dk_all, dv_all = past[3].reshape(b, p_len, w), past[4].reshape(b, p_len, w)
        diff_new = (_pad_rows(dkb, KV_BLOCK), _pad_rows(dvb, KV_BLOCK))
    h_diff = _diff_attention(dq, dk_all, dv_all, lw['diff_lambda'], lw['diff_subln_g'], tq, off, tk_valid, lam_init,
                             diff_new)

    n = b * t
    if sample:
        tm_merge = tm_ffn = min(n, 256)
        tpg_merge = tpg_ffn = 1
        rows_mod = lambda m, tm: jnp.repeat(m, t, axis=1).reshape(n // tm, tm, d)
    else:
        tm_merge, tm_ffn = min(t, 512), min(t, 1024)
        tpg_merge, tpg_ffn = t // tm_merge, t // tm_ffn
        rows_mod = lambda m, tm: m
    mods = tuple(rows_mod(m, tm_merge) for m in (scale1, shift1, gate1, scale2, shift2))
    branches = tuple(h.reshape(n, w) for h in (h_ret, h_fox, h_conv, h_diff))
    merged = _merge(x.reshape(n, d), mods, branches, lw['w_gate'], lw['w_branch'], lw['w_out'],
                    lw['ln_g'][0], lw['ln_b'][0], lw.get('router'), tm_merge, tpg_merge)
    g2 = rows_mod(gate2, tm_ffn)
    if 'router' in lw and not sample:
        x1, u2, route_col, route_row = merged
        x2 = _ffn_routed(u2, x1, rows_mod(gate2, tm_merge), route_col, route_row, lw['w_exp_in'], lw['w_exp_out'],
                         lw['ln_g'][1], lw['ln_b'][1], tpg_merge)
    elif 'router' in lw:
        x1, u2, route_col, _ = merged
        x2 = _ffn_experts(_bf(u2), x1, g2, route_col, lw['w_exp_in'], lw['w_exp_out'], lw['ln_g'][1], lw['ln_b'][1],
                          tm_ffn, tpg_ffn, 512)
    else:
        x1, u2 = merged
        x2 = _ffn_dense(u2, x1, g2, lw['w_ffn_in'], lw['w_ffn_out'], lw['ln_g'][1], lw['ln_b'][1],
                        tm_ffn, tpg_ffn, 512)
    return x2.reshape(b, t, d), (fox_state, diff_state), (ret_state, conv_buf)


def _state_outputs(kv_state, small):
    (fk, fv, lf), (dk, dv) = kv_state
    depth, b, t, w = fk.shape
    heads = lambda a, nh: a.reshape(depth, b, t, nh, w // nh)
    ret_state, conv_buf = (jnp.stack(a) for a in zip(*small))
    return (heads(fk, N_HEADS), heads(fv, N_HEADS), lf[..., :N_HEADS], heads(dk, N_SUB), heads(dv, N_HEADS),
            ret_state, conv_buf)


def kernel(x_prompt, x_sample, c_prompt, c_sample, cache_fox_k, cache_fox_v, cache_fox_logf, cache_diff_k, cache_diff_v, state_ret, state_conv, w_in, b_fox_f, w_conv, b_conv, conv_ln_g, conv_ln_b, diff_lambda, diff_subln_g, w_branch, w_out, w_ada, b_ada, ln_g, ln_b, w_ffn_in, w_ffn_out, w_router, b_router, w_exp_in, w_exp_out):
    depth = w_in.shape[0]
    bp = x_prompt.shape[0]
    past_len = cache_fox_k.shape[2]
    pos_p = jnp.arange(x_prompt.shape[1], dtype=jnp.int32)
    pos_s = past_len + jnp.arange(x_sample.shape[1], dtype=jnp.int32)
    mod = _modulation(jnp.concatenate([c_prompt, c_sample], axis=0), w_ada, b_ada)
    yp, ys = x_prompt, x_sample
    kv_p = kv_s = None
    small_p, small_s = [], []
    for l in range(depth):
        w_ret, w_fox, w_conv_in, w_diff, w_gate = _split_w_in(w_in[l])
        lw = dict(w_ret=w_ret, w_fox=w_fox, w_conv_in=w_conv_in, w_diff=w_diff, w_gate=w_gate,
                  b_fox_f=b_fox_f[l], w_conv=w_conv[l], b_conv=b_conv[l], conv_ln_g=conv_ln_g[l],
                  conv_ln_b=conv_ln_b[l], diff_lambda=diff_lambda[l], diff_subln_g=diff_subln_g[l],
                  w_branch=_bf(w_branch[l]), w_out=_bf(w_out[l]), ln_g=ln_g[l], ln_b=ln_b[l])
        if l % 2 == 0:
            lw['w_ffn_in'] = _bf(w_ffn_in[l // 2])
            lw['w_ffn_out'] = _bf(w_ffn_out[l // 2])
        else:
            wr = jnp.concatenate([w_router[l // 2], jnp.zeros((D_MODEL, LANES - N_EXPERTS), F32)], axis=1)
            br = jnp.concatenate([b_router[l // 2].astype(F32), jnp.zeros((LANES - N_EXPERTS,), F32)]).reshape(1, LANES)
            lw['router'] = (_bf(wr), br)
            lw['w_exp_in'] = _bf(w_exp_in[l // 2])
            lw['w_exp_out'] = _bf(w_exp_out[l // 2])
        past_l = (cache_fox_k[l], cache_fox_v[l], cache_fox_logf[l], cache_diff_k[l], cache_diff_v[l],
                  state_ret[l], state_conv[l])
        yp, kv_p, st_p = _trunk_layer(yp, mod[l][:, :bp], pos_p, None, l, lw, False, depth, kv_p)
        ys, kv_s, st_s = _trunk_layer(ys, mod[l][:, bp:], pos_s, past_l, l, lw, True, depth, kv_s)
        small_p.append(st_p)
        small_s.append(st_s)
    return (yp, ys) + _state_outputs(kv_p, small_p) + _state_outputs(kv_s, small_s)
```

```python
import functools
import math

import jax
import jax.numpy as jnp
from jax import lax
from jax.experimental import pallas as pl
from jax.experimental.pallas import tpu as pltpu

D_MODEL = 1024
BRANCH_W = 256
HEAD_DIM = 64
N_HEADS = 4
DIFF_SUB = 32
N_SUB = 8
ROT_DIM = DIFF_SUB // 4
RET_THETA = 10000.0
ROPE_THETA = 500000.0
CHUNK = 64
CONV_W = 31
D_FF = 2816
N_EXPERTS = 8
D_EXPERT = 3584
DEPTH = 2
ALPHA = (2.0 * DEPTH) ** 0.25
EPS = 1e-5
NEG = -1e30
LOG2E = math.log2(math.e)

LANES = 128
BF16_ROWS = 16
KV_BLOCK = 256
HIST_ROWS = 32
VMEM_LIMIT = 56 * 1024 * 1024

F32 = jnp.float32
BF16 = jnp.bfloat16


def _bf(x):
    return x.astype(BF16)


def _dot(a, b):
    return jnp.dot(a, b, preferred_element_type=F32)


def _dot_nt(a, b):
    return lax.dot_general(a, b, (((1,), (1,)), ((), ())), preferred_element_type=F32)


def _dot_tn(a, b):
    return lax.dot_general(a, b, (((0,), (0,)), ((), ())), preferred_element_type=F32)


def _sigmoid(x):
    return 1.0 / (1.0 + jnp.exp(-x))


def _params(n_axes):
    return pltpu.CompilerParams(dimension_semantics=("arbitrary",) * n_axes,
                                vmem_limit_bytes=VMEM_LIMIT)


def _head_sum(y, width):
    n = y.shape[-1]
    r = lax.broadcasted_iota(jnp.int32, (n, n), 0) // width
    c = lax.broadcasted_iota(jnp.int32, (n, n), 1) // width
    bd = jnp.where(r == c, 1.0, 0.0).astype(BF16)
    hi = _bf(y)
    lo = _bf(y - hi.astype(F32))
    return _dot(hi, bd) + _dot(lo, bd)


def _layer_norm_rows(z, g, b):
    mu = jnp.mean(z, axis=-1, keepdims=True)
    d = z - mu
    var = jnp.mean(d * d, axis=-1, keepdims=True)
    return d * lax.rsqrt(var + EPS) * g + b


def _mod_kernel(c_ref, w_ref, b_ref, o_ref):
    c = c_ref[...]
    sc = _bf(c * _sigmoid(c))
    o_ref[...] = _dot(sc, _bf(w_ref[...])) + b_ref[...]


def _modulation(c_all, w_ada, b_ada):
    rows = c_all.shape[0]
    depth = w_ada.shape[0]
    d3 = w_ada.shape[-1]
    nj = d3 // D_MODEL
    return pl.pallas_call(
        _mod_kernel,
        out_shape=jax.ShapeDtypeStruct((depth, 2, rows, d3), F32),
        grid=(depth * 2, nj),
        in_specs=[
            pl.BlockSpec((rows, D_MODEL), lambda i, j: (0, 0)),
            pl.BlockSpec((None, None, D_MODEL, D_MODEL), lambda i, j: (i // 2, i % 2, 0, j)),
            pl.BlockSpec((None, None, 1, D_MODEL), lambda i, j: (i // 2, i % 2, 0, j)),
        ],
        out_specs=pl.BlockSpec((None, None, rows, D_MODEL), lambda i, j: (i // 2, i % 2, 0, j)),
        compiler_params=_params(2),
        name="adaln_modulation",
    )(c_all, w_ada, b_ada.reshape(depth, 2, 1, d3))


def _ret_kernel(x_ref, sc_ref, sh_ref, w_ref, cos_ref, sin_ref, dmask_ref, qdec_ref, kdec_ref,
                cdec_ref, s0_ref, h_ref, sout_ref, s_scr):
    c = pl.program_id(1)

    @pl.when(c == 0)
    def _():
        s_scr[...] = s0_ref[...]

    u = x_ref[...] * (1.0 + sc_ref[...]) + sh_ref[...]
    p = _dot(_bf(u), w_ref[...])
    w = BRANCH_W
    cos = cos_ref[...]
    sin = sin_ref[...]
    q = p[:, 0:w] * cos + p[:, w:2 * w] * sin
    k = (p[:, 2 * w:3 * w] * cos + p[:, 3 * w:4 * w] * sin) * (HEAD_DIM ** -0.5)
    v = p[:, 4 * w:5 * w]
    g = p[:, 5 * w:6 * w]
    rows = q.shape[0]
    lane_head = lax.broadcasted_iota(jnp.int32, (1, w), 1) // HEAD_DIM
    kb = _bf(k)
    vb = _bf(v)
    y = jnp.zeros((rows, w), F32)
    for h in range(N_HEADS):
        mh = lane_head == h
        qh = _bf(jnp.where(mh, q, 0.0))
        a = _dot_nt(qh, kb) * dmask_ref[h]
        y = y + jnp.where(mh, _dot(_bf(a), vb), 0.0)
    s_prev = s_scr[...]
    y = y + _dot(_bf(q * qdec_ref[...]), _bf(s_prev))
    kv = _dot_tn(_bf(k * kdec_ref[...]), vb)
    r = lax.broadcasted_iota(jnp.int32, (w, w), 0) // HEAD_DIM
    cc = lax.broadcasted_iota(jnp.int32, (w, w), 1) // HEAD_DIM
    s_new = cdec_ref[...] * s_prev + jnp.where(r == cc, kv, 0.0)
    s_scr[...] = s_new
    sout_ref[...] = s_new
    mu = _head_sum(y, HEAD_DIM) * (1.0 / HEAD_DIM)
    d = y - mu
    var = _head_sum(d * d, HEAD_DIM) * (1.0 / HEAD_DIM)
    hn = d * lax.rsqrt(var + EPS)
    h_ref[...] = _bf(hn * (g * _sigmoid(g)))


def _retention_tables(chunk):
    log_g = jnp.log1p(-jnp.exp2(-5.0 - jnp.arange(N_HEADS, dtype=F32)))
    idx = jnp.arange(chunk, dtype=F32)
    dist = jnp.abs(idx[:, None] - idx[None, :])
    sub = jnp.arange(chunk) // CHUNK
    vis = sub[None, :] <= sub[:, None]
    dmask = jnp.where(vis[None], jnp.exp(log_g[:, None, None] * dist[None]), 0.0)
    lg_lane = jnp.repeat(log_g, HEAD_DIM)[None, :]
    qdec = jnp.exp(lg_lane * (idx[:, None] + 1.0))
    kdec = jnp.exp(lg_lane * (chunk - 1.0 - idx[:, None]))
    cdec = jnp.exp(lg_lane * chunk)
    return dmask.astype(F32), qdec, kdec, cdec


def _rope_tables(pos, dim, rot_dim, theta, n_rep):
    half = rot_dim // 2
    inv_freq = jnp.exp(-math.log(theta) * jnp.arange(half, dtype=F32) / half)
    ang = pos.astype(F32)[:, None] * inv_freq[None, :]
    t = pos.shape[0]
    cos = jnp.concatenate([jnp.cos(ang), jnp.cos(ang), jnp.ones((t, dim - rot_dim), F32)], axis=1)
    sin = jnp.concatenate([jnp.sin(ang), jnp.sin(ang), jnp.zeros((t, dim - rot_dim), F32)], axis=1)
    return jnp.tile(cos, (1, n_rep)), jnp.tile(sin, (1, n_rep))


def _rotated_columns(w, dim, rot_dim):
    half = rot_dim // 2
    k, n = w.shape
    wh = w.reshape(k, n // dim, dim)
    rot = jnp.concatenate([-wh[..., half:rot_dim], wh[..., :half], jnp.zeros_like(wh[..., rot_dim:])], axis=-1)
    return rot.reshape(k, n)


def _retention(x, scale, shift, w_ret, pos, s0_bd, chunk):
    b, t, _ = x.shape
    nc = t // chunk
    cos, sin = _rope_tables(pos, HEAD_DIM, HEAD_DIM, RET_THETA, N_HEADS)
    dmask, qdec, kdec, cdec = _retention_tables(chunk)
    w = BRANCH_W
    full = lambda shape: pl.BlockSpec(shape, lambda i, j: (0,) * len(shape))
    return pl.pallas_call(
        _ret_kernel,
        out_shape=(jax.ShapeDtypeStruct((b, t, w), BF16), jax.ShapeDtypeStruct((b, w, w), F32)),
        grid=(b, nc),
        in_specs=[
            pl.BlockSpec((None, chunk, D_MODEL), lambda i, j: (i, j, 0)),
            pl.BlockSpec((None, 1, D_MODEL), lambda i, j: (i, 0, 0)),
            pl.BlockSpec((None, 1, D_MODEL), lambda i, j: (i, 0, 0)),
            full((D_MODEL, 6 * w)),
            pl.BlockSpec((chunk, w), lambda i, j: (j, 0)),
            pl.BlockSpec((chunk, w), lambda i, j: (j, 0)),
            full((N_HEADS, chunk, chunk)),
            full((chunk, w)),
            full((chunk, w)),
            full((1, w)),
            pl.BlockSpec((None, w, w), lambda i, j: (i, 0, 0)),
        ],
        out_specs=(pl.BlockSpec((None, chunk, w), lambda i, j: (i, j, 0)),
                   pl.BlockSpec((None, w, w), lambda i, j: (i, 0, 0))),
        scratch_shapes=[pltpu.VMEM((w, w), F32)],
        compiler_params=_params(2),
        name="retention_mixer",
    )(x, scale, shift, w_ret, cos, sin, dmask, qdec, kdec, cdec, s0_bd)


def _conv_kernel(x_ref, sc_ref, sh_ref, w_ref, hist_ref, wc_ref, bc_ref, lg_ref, lb_ref,
                 h_ref, tail_ref, xp_scr):
    c = pl.program_id(1)
    rows = x_ref.shape[0]
    w = BRANCH_W
    pad = HIST_ROWS - (CONV_W - 1)

    @pl.when(c == 0)
    def _():
        xp_scr[0:HIST_ROWS, :] = hist_ref[...]

    u = x_ref[...] * (1.0 + sc_ref[...]) + sh_ref[...]
    p = _dot(_bf(u), w_ref[...])
    xp_scr[HIST_ROWS:HIST_ROWS + rows, :] = p[:, :w] * _sigmoid(p[:, w:])
    sub = 8
    acc = jnp.zeros((rows, w), F32) + bc_ref[...]
    for rho in range(sub):
        z = None
        for m in range((pad + CONV_W - 1) // sub + 1):
            j = sub * m + rho - pad
            if not 0 <= j < CONV_W:
                continue
            span = rows if rho == 0 else rows + sub
            term = xp_scr[sub * m:sub * m + span, :] * wc_ref[j:j + 1, :]
            z = term if z is None else z + term
        acc = acc + z[rho:rho + rows, :]
    y = _layer_norm_rows(acc, lg_ref[...], lb_ref[...])
    h_ref[...] = _bf(y * _sigmoid(y))
    tail = xp_scr[rows:rows + HIST_ROWS, :]
    tail_ref[...] = tail
    xp_scr[0:HIST_ROWS, :] = tail


def _conv_mixer(x, scale, shift, w_cv, hist, w_conv, b_conv, ln_g, ln_b, chunk):
    b, t, _ = x.shape
    nc = t // chunk
    w = BRANCH_W
    full = lambda shape: pl.BlockSpec(shape, lambda i, j: (0,) * len(shape))
    wc = jnp.concatenate([w_conv, jnp.zeros((HIST_ROWS - CONV_W, w), F32)], axis=0)
    return pl.pallas_call(
        _conv_kernel,
        out_shape=(jax.ShapeDtypeStruct((b, t, w), BF16), jax.ShapeDtypeStruct((b, HIST_ROWS, w), F32)),
        grid=(b, nc),
        in_specs=[
            pl.BlockSpec((None, chunk, D_MODEL), lambda i, j: (i, j, 0)),
            pl.BlockSpec((None, 1, D_MODEL), lambda i, j: (i, 0, 0)),
            pl.BlockSpec((None, 1, D_MODEL), lambda i, j: (i, 0, 0)),
            full((D_MODEL, 2 * w)),
            pl.BlockSpec((None, HIST_ROWS, w), lambda i, j: (i, 0, 0)),
            full((HIST_ROWS, w)),
            full((1, w)),
            full((1, w)),
            full((1, w)),
        ],
        out_specs=(pl.BlockSpec((None, chunk, w), lambda i, j: (i, j, 0)),
                   pl.BlockSpec((None, HIST_ROWS, w), lambda i, j: (i, 0, 0))),
        scratch_shapes=[pltpu.VMEM((HIST_ROWS + chunk, w), F32)],
        compiler_params=_params(2),
        name="conv_mixer",
    )(x, scale, shift, w_cv, hist, wc, b_conv.reshape(1, w), ln_g.reshape(1, w), ln_b.reshape(1, w))


def _fox_proj_kernel(x_ref, sc_ref, sh_ref, w_ref, bf_ref, *refs):
    q_ref, k_ref, v_ref, kb_ref, vb_ref, lf_ref = refs[-6:]
    u = x_ref[...] * (1.0 + sc_ref[...]) + sh_ref[...]
    p = _dot(_bf(u), w_ref[...])
    w = BRANCH_W
    q_ref[...] = _bf(p[:, 0:w] * (HEAD_DIM ** -0.5 * LOG2E))
    k = p[:, w:2 * w]
    v = p[:, 2 * w:3 * w]
    _store_layer_rows(k_ref, k)
    _store_layer_rows(v_ref, v)
    kb_ref[...] = _bf(k)
    vb_ref[...] = _bf(v)
    z = p[:, 3 * w:] + bf_ref[...]
    lf = jnp.minimum(z, 0.0) - jnp.log(1.0 + jnp.exp(-jnp.abs(z)))
    lane = lax.broadcasted_iota(jnp.int32, lf.shape, 1)
    _store_layer_rows(lf_ref, jnp.where(lane < N_HEADS, lf, 0.0))


def _layer_slot(depth, layer, b, t, tm, width, dt, first):
    shape = jax.ShapeDtypeStruct((depth, b, t, width), dt)
    if first:
        return shape, pl.BlockSpec((depth, None, tm, width), lambda i, j: (0, i, j, 0))
    return shape, pl.BlockSpec((None, None, tm, width), lambda i, j: (layer, i, j, 0))


def _store_layer_rows(ref, rows):
    if len(ref.shape) == 3:
        for d in range(ref.shape[0]):
            ref[d] = rows
    else:
        ref[...] = rows


def _fox_proj(x, scale, shift, w_fox, b_f, tm, depth, layer, state):
    b, t, _ = x.shape
    w = BRANCH_W
    nw = 3 * w + LANES
    bfp = jnp.concatenate([b_f.astype(F32), jnp.zeros((LANES - N_HEADS,), F32)]).reshape(1, LANES)
    row = lambda width: pl.BlockSpec((None, tm, width), lambda i, j: (i, j, 0))
    sds = lambda width, dt: jax.ShapeDtypeStruct((b, t, width), dt)
    k_out, v_out, lf_out = (_layer_slot(depth, layer, b, t, tm, width, F32, state is None) for width in (w, w, LANES))
    prior = () if state is None else tuple(state)
    n_in = 5
    q, k, v, kb, vb, lf = pl.pallas_call(
        _fox_proj_kernel,
        out_shape=(sds(w, BF16), k_out[0], v_out[0], sds(w, BF16), sds(w, BF16), lf_out[0]),
        grid=(b, t // tm),
        in_specs=[
            row(D_MODEL),
            pl.BlockSpec((None, 1, D_MODEL), lambda i, j: (i, 0, 0)),
            pl.BlockSpec((None, 1, D_MODEL), lambda i, j: (i, 0, 0)),
            pl.BlockSpec((D_MODEL, nw), lambda i, j: (0, 0)),
            pl.BlockSpec((1, LANES), lambda i, j: (0, 0)),
        ] + [pl.BlockSpec(memory_space=pl.ANY)] * len(prior),
        out_specs=(row(w), k_out[1], v_out[1], row(w), row(w), lf_out[1]),
        input_output_aliases={n_in: 1, n_in + 1: 2, n_in + 2: 5} if prior else {},
        compiler_params=_params(2),
        name="fox_projection",
    )(x, scale, shift, w_fox, bfp, *prior)
    return q, kb, vb, lf[layer], (k, v, lf)


def _cumsum_kernel(lf_ref, col_ref, row_ref):
    n = KV_BLOCK
    nb = lf_ref.shape[0] // n
    r = lax.broadcasted_iota(jnp.int32, (n, n), 0)
    c = lax.broadcasted_iota(jnp.int32, (n, n), 1)
    tri = jnp.where(c <= r, 1.0, 0.0).astype(BF16)
    carry = jnp.zeros((1, LANES), F32)
    for jb in range(nb):
        x = lf_ref[jb * n:(jb + 1) * n, :]
        hi = _bf(x)
        r1 = x - hi.astype(F32)
        mid = _bf(r1)
        lo = _bf(r1 - mid.astype(F32))
        cs = _dot(tri, hi) + _dot(tri, mid) + _dot(tri, lo) + carry
        carry = cs[n - 1:n, :]
        scaled = cs * LOG2E
        col_ref[jb * n:(jb + 1) * n, :] = scaled
        row_ref[jb] = scaled.T[0:8, :]


def _logf_cumsum(lf):
    b, tk, _ = lf.shape
    nb = tk // KV_BLOCK
    return pl.pallas_call(
        _cumsum_kernel,
        out_shape=(jax.ShapeDtypeStruct((b, tk, LANES), F32),
                   jax.ShapeDtypeStruct((b, nb, 8, KV_BLOCK), F32)),
        grid=(b,),
        in_specs=[pl.BlockSpec((None, tk, LANES), lambda i: (i, 0, 0))],
        out_specs=(pl.BlockSpec((None, tk, LANES), lambda i: (i, 0, 0)),
                   pl.BlockSpec((None, nb, 8, KV_BLOCK), lambda i: (i, 0, 0, 0))),
        compiler_params=_params(1),
        name="logf_cumsum",
    )(lf)


def _attn_kernel(*refs, fox, off, tk_valid, tq, lam_init, split_kv):
    q_ref, k_ref, v_ref = refs[:3]
    refs = refs[3:]
    if split_kv:
        kn_ref, vn_ref = refs[:2]
        refs = refs[2:]
    if fox:
        fc_ref, fr_ref, o_ref, vt_scr, qm_scr, mb_scr, m_scr, r_scr, acc_scr, s_scr = refs
    else:
        lam_ref, g_ref, o_ref, vt_scr, qm_scr, mb_scr, m_scr, r_scr, acc_scr, s_scr = refs
    i = pl.program_id(1)
    w = BRANCH_W
    n_sub, tqp, _ = qm_scr.shape
    sub_w = w // n_sub
    subs_per_head = n_sub // N_HEADS
    nb = vt_scr.shape[0]
    hd = HEAD_DIM
    q0 = off + i * tq
    nfull = q0 // KV_BLOCK

    @pl.when(i == 0)
    def _():
        for jb in range(nb):
            if split_kv and jb == nb - 1:
                vt = vn_ref[...].T
            else:
                vt = _bf(v_ref[jb * KV_BLOCK:(jb + 1) * KV_BLOCK, :]).T
            for h in range(N_HEADS):
                vt_scr[jb, h, 0:hd, :] = vt[h * hd:(h + 1) * hd, :]
                vt_scr[jb, h, hd:, :] = jnp.ones((vt_scr.shape[2] - hd, KV_BLOCK), BF16)
        kpos = lax.broadcasted_iota(jnp.int32, (KV_BLOCK, tqp), 0)
        qpos = lax.broadcasted_iota(jnp.int32, (KV_BLOCK, tqp), 1)
        if fox:
            vis = kpos <= qpos
        else:
            vis = (kpos // CHUNK) <= (qpos // CHUNK)
        vis = vis & (kpos < tk_valid - nfull * KV_BLOCK)
        mb_scr[...] = jnp.where(vis, 0.0, NEG)

    q = q_ref[...]
    lane_sub = lax.broadcasted_iota(jnp.int32, (1, w), 1) // sub_w
    if tqp > tq:
        qm_scr[...] = jnp.zeros_like(qm_scr)
    for n in range(n_sub):
        qm_scr[n, 0:tq, :] = jnp.where(lane_sub == n, q, jnp.zeros_like(q))
    m_scr[...] = jnp.full(m_scr.shape, NEG, F32)
    r_scr[...] = jnp.full(r_scr.shape, NEG, F32)
    acc_scr[...] = jnp.zeros_like(acc_scr)

    def scores(j, n, masked):
        start = pl.multiple_of(j * KV_BLOCK, KV_BLOCK)
        h = n // subs_per_head
        kb = kn_ref[...] if (split_kv and masked) else _bf(k_ref[pl.ds(start, KV_BLOCK), :])
        s = _dot_nt(kb, qm_scr[n])
        if fox:
            s = s + (fr_ref[h:h + 1, 0:tqp] - fc_ref[pl.ds(start, KV_BLOCK), h:h + 1])
        if masked:
            s = s + mb_scr[...]
        s_scr[n] = s
        m_scr[n] = jnp.maximum(m_scr[n], jnp.max(s, axis=0, keepdims=True))

    def values(j, n):
        h = n // subs_per_head
        m = m_scr[n]
        alpha = jnp.exp2(r_scr[n] - m)
        r_scr[n] = m
        p = jnp.exp2(s_scr[n] - m)
        acc_scr[n] = alpha * acc_scr[n] + _dot(vt_scr[j, h], _bf(p))

    @pl.when(nfull == 0)
    def _():
        for n in range(n_sub):
            scores(0, n, True)

    @pl.when(nfull > 0)
    def _():
        for n in range(n_sub):
            scores(0, n, False)

    def body(j, carry):
        for n in range(n_sub):
            values(j - 1, n)
            scores(j, n, False)
        return carry

    lax.fori_loop(1, nfull, body, 0)

    @pl.when(nfull > 0)
    def _():
        for n in range(n_sub):
            values(nfull - 1, n)
            scores(nfull, n, True)

    for n in range(n_sub):
        values(nfull, n)

    def normalised(n):
        acc = acc_scr[n]
        return acc[0:hd] / acc[hd:hd + 1]

    if fox:
        out_t = jnp.concatenate([normalised(h) for h in range(N_HEADS)], axis=0)
        o_ref[...] = _bf(out_t.T[0:tq, :])
    else:
        lamv = lam_ref[...]
        lam = (jnp.exp(jnp.sum(lamv[0:1] * lamv[1:2], axis=-1, keepdims=True))
               - jnp.exp(jnp.sum(lamv[2:3] * lamv[3:4], axis=-1, keepdims=True)) + lam_init)
        parts = [normalised(2 * h) - lam * normalised(2 * h + 1) for h in range(N_HEADS)]
        dy = jnp.concatenate(parts, axis=0).T[0:tq, :]
        ms = _head_sum(dy * dy, HEAD_DIM) * (1.0 / HEAD_DIM)
        o_ref[...] = _bf(dy * lax.rsqrt(ms + EPS) * g_ref[...] * (1.0 - lam_init))


def _attention(q, kb, vb, extras, *, fox, tq, off, tk_valid, lam_init=0.0, new_kv=None):
    b, t, w = q.shape
    tk = kb.shape[1]
    split_kv = new_kv is not None
    nb = tk // KV_BLOCK + (1 if split_kv else 0)
    tqp = max(tq, LANES)
    n_sub = N_HEADS if fox else N_SUB
    assert off % KV_BLOCK == 0 and (tq == KV_BLOCK or t == tq)
    kernel = functools.partial(_attn_kernel, fox=fox, off=off, tk_valid=tk_valid, tq=tq, lam_init=lam_init,
                               split_kv=split_kv)
    in_specs = [
        pl.BlockSpec((None, tq, w), lambda i, j: (i, j, 0)),
        pl.BlockSpec((None, tk, w), lambda i, j: (i, 0, 0)),
        pl.BlockSpec((None, tk, w), lambda i, j: (i, 0, 0)),
    ]
    if split_kv:
        in_specs += [pl.BlockSpec((None, KV_BLOCK, w), lambda i, j: (i, 0, 0))] * 2
        extras = tuple(new_kv) + tuple(extras)
    if fox:
        in_specs += [
            pl.BlockSpec((None, nb * KV_BLOCK, LANES), lambda i, j: (i, 0, 0)),
            pl.BlockSpec((None, None, 8, KV_BLOCK), lambda i, j: (i, (off + j * tq) // KV_BLOCK, 0, 0)),
        ]
    else:
        in_specs += [
            pl.BlockSpec((4, DIFF_SUB), lambda i, j: (0, 0)),
            pl.BlockSpec((1, w), lambda i, j: (0, 0)),
        ]
    return pl.pallas_call(
        kernel,
        out_shape=jax.ShapeDtypeStruct((b, t, w), BF16),
        grid=(b, t // tq),
        in_specs=in_specs,
        out_specs=pl.BlockSpec((None, tq, w), lambda i, j: (i, j, 0)),
        scratch_shapes=[
            pltpu.VMEM((nb, N_HEADS, HEAD_DIM + BF16_ROWS, KV_BLOCK), BF16),
            pltpu.VMEM((n_sub, tqp, w), BF16),
            pltpu.VMEM((KV_BLOCK, tqp), F32),
            pltpu.VMEM((n_sub, 1, tqp), F32),
            pltpu.VMEM((n_sub, 1, tqp), F32),
            pltpu.VMEM((n_sub, HEAD_DIM + BF16_ROWS, tqp), F32),
            pltpu.VMEM((n_sub, KV_BLOCK, tqp), F32),
        ],
        compiler_params=_params(2),
        name="fox_attention" if fox else "diff_attention",
    )(q, kb, vb, *extras)


def _fox_attention(q, kb, vb, fcol, frow, tq, off, tk_valid, new_kv=None):
    return _attention(q, kb, vb, (fcol, frow), fox=True, tq=tq, off=off, tk_valid=tk_valid, new_kv=new_kv)


def _diff_proj_kernel(x_ref, sc_ref, sh_ref, w_ref, cos_ref, sin_ref, *refs):
    q_ref, k_ref, v_ref, kb_ref, vb_ref = refs[-5:]
    u = x_ref[...] * (1.0 + sc_ref[...]) + sh_ref[...]
    p = _dot(_bf(u), w_ref[...])
    w = BRANCH_W
    cos = cos_ref[...]
    sin = sin_ref[...]
    q_ref[...] = _bf((p[:, 0:w] * cos + p[:, w:2 * w] * sin) * (DIFF_SUB ** -0.5 * LOG2E))
    k = p[:, 2 * w:3 * w] * cos + p[:, 3 * w:4 * w] * sin
    v = p[:, 4 * w:5 * w]
    _store_layer_rows(k_ref, k)
    _store_layer_rows(v_ref, v)
    kb_ref[...] = _bf(k)
    vb_ref[...] = _bf(v)


def _diff_proj(x, scale, shift, w_diff, pos, tm, depth, layer, state):
    b, t, _ = x.shape
    w = BRANCH_W
    cos, sin = _rope_tables(pos, DIFF_SUB, ROT_DIM, ROPE_THETA, N_SUB)
    row = lambda width: pl.BlockSpec((None, tm, width), lambda i, j: (i, j, 0))
    sds = lambda dt: jax.ShapeDtypeStruct((b, t, w), dt)
    k_out, v_out = (_layer_slot(depth, layer, b, t, tm, w, F32, state is None) for _ in range(2))
    prior = () if state is None else tuple(state)
    n_in = 6
    q, k, v, kb, vb = pl.pallas_call(
        _diff_proj_kernel,
        out_shape=(sds(BF16), k_out[0], v_out[0], sds(BF16), sds(BF16)),
        grid=(b, t // tm),
        in_specs=[
            row(D_MODEL),
            pl.BlockSpec((None, 1, D_MODEL), lambda i, j: (i, 0, 0)),
            pl.BlockSpec((None, 1, D_MODEL), lambda i, j: (i, 0, 0)),
            pl.BlockSpec((D_MODEL, 5 * w), lambda i, j: (0, 0)),
            pl.BlockSpec((tm, w), lambda i, j: (j, 0)),
            pl.BlockSpec((tm, w), lambda i, j: (j, 0)),
        ] + [pl.BlockSpec(memory_space=pl.ANY)] * len(prior),
        out_specs=(row(w), k_out[1], v_out[1], row(w), row(w)),
        input_output_aliases={n_in: 1, n_in + 1: 2} if prior else {},
        compiler_params=_params(2),
        name="diff_projection",
    )(x, scale, shift, w_diff, cos, sin, *prior)
    return q, kb, vb, (k, v)


def _diff_attention(q, kb, vb, diff_lambda, subln_g, tq, off, tk_valid, lam_init, new_kv=None):
    g = jnp.tile(subln_g.astype(F32), N_HEADS).reshape(1, BRANCH_W)
    return _attention(q, kb, vb, (diff_lambda.astype(F32), g), fox=False, tq=tq, off=off,
                      tk_valid=tk_valid, lam_init=lam_init, new_kv=new_kv)


ROUTE_W1, ROUTE_W2, ROUTE_I1, ROUTE_I2 = 8, 9, 10, 11


def _top2_route(logits):
    lane = lax.broadcasted_iota(jnp.int32, logits.shape, 1).astype(F32)
    lg = jnp.where(lane < N_EXPERTS, logits, -jnp.inf)
    v1 = jnp.max(lg, axis=-1, keepdims=True)
    i1 = jnp.min(jnp.where(lg == v1, lane, float(LANES)), axis=-1, keepdims=True)
    lg2 = jnp.where(lane == i1, -jnp.inf, lg)
    v2 = jnp.max(lg2, axis=-1, keepdims=True)
    i2 = jnp.min(jnp.where(lg2 == v2, lane, float(LANES)), axis=-1, keepdims=True)
    e2 = jnp.exp(v2 - v1)
    w1 = 1.0 / (1.0 + e2)
    w2 = e2 / (1.0 + e2)
    rec = jnp.where(lane == i1, w1, 0.0) + jnp.where(lane == i2, w2, 0.0)
    for slot, val in ((ROUTE_W1, w1), (ROUTE_W2, w2), (ROUTE_I1, i1), (ROUTE_I2, i2)):
        rec = rec + jnp.where(lane == float(slot), val, 0.0)
    return rec


def _merge_kernel(*refs, with_router):
    if with_router:
        (x_ref, sc1_ref, sh1_ref, g1_ref, sc2_ref, sh2_ref, hr_ref, hf_ref, hc_ref, hd_ref,
         wg_ref, wb_ref, wo_ref, lg_ref, lb_ref, wr_ref, br_ref, x1_ref, u2_ref, rcol_ref, rrow_ref) = refs
    else:
        (x_ref, sc1_ref, sh1_ref, g1_ref, sc2_ref, sh2_ref, hr_ref, hf_ref, hc_ref, hd_ref,
         wg_ref, wb_ref, wo_ref, lg_ref, lb_ref, x1_ref, u2_ref) = refs
    x = x_ref[...]
    u = _bf(x * (1.0 + sc1_ref[...]) + sh1_ref[...])
    merged = None
    for n, h_ref in enumerate((hr_ref, hf_ref, hc_ref, hd_ref)):
        gate = _dot(u, wg_ref[:, n * D_MODEL:(n + 1) * D_MODEL])
        term = _sigmoid(gate) * _dot(h_ref[...], wb_ref[n])
        merged = term if merged is None else merged + term
    mix = _dot(_bf(merged), wo_ref[...])
    x1 = _layer_norm_rows(ALPHA * x + g1_ref[...] * mix, lg_ref[...], lb_ref[...])
    x1_ref[...] = x1
    u2 = x1 * (1.0 + sc2_ref[...]) + sh2_ref[...]
    u2_ref[...] = u2.astype(u2_ref.dtype)
    if with_router:
        rec = _top2_route(_dot(_bf(u2), wr_ref[...]) + br_ref[...])
        rcol_ref[...] = rec
        rrow_ref[...] = rec.T[8:16, :]


def _merge(x2d, mods, branches, w_gate, w_branch, w_out, ln_g, ln_b, router, tm, tiles_per_group):
    n = x2d.shape[0]
    r = mods[0].shape[1]
    w = BRANCH_W
    with_router = router is not None
    row = lambda width: pl.BlockSpec((tm, width), lambda i: (i, 0))
    mod_spec = pl.BlockSpec((None, r, D_MODEL), lambda i: (i // tiles_per_group, 0, 0))
    full = lambda shape: pl.BlockSpec(shape, lambda i: (0,) * len(shape))
    resident = lambda shape: pl.BlockSpec(shape, lambda i: (0,) * len(shape), pipeline_mode=pl.Buffered(1))
    in_specs = ([row(D_MODEL)] + [mod_spec] * 5 + [row(w)] * 4
                + [resident((D_MODEL, 4 * D_MODEL)), resident((4, w, D_MODEL)), resident((D_MODEL, D_MODEL)),
                   full((1, D_MODEL)), full((1, D_MODEL))])
    args = [x2d, *mods, *branches, w_gate, w_branch, w_out, ln_g.reshape(1, D_MODEL), ln_b.reshape(1, D_MODEL)]
    out_shape = [jax.ShapeDtypeStruct((n, D_MODEL), F32),
                 jax.ShapeDtypeStruct((n, D_MODEL), F32 if with_router else BF16)]
    out_specs = [row(D_MODEL), row(D_MODEL)]
    if with_router:
        in_specs += [full((D_MODEL, LANES)), full((1, LANES))]
        args += list(router)
        out_shape += [jax.ShapeDtypeStruct((n, LANES), F32), jax.ShapeDtypeStruct((n // tm, 8, tm), F32)]
        out_specs += [row(LANES), pl.BlockSpec((None, 8, tm), lambda i: (i, 0, 0))]
    return pl.pallas_call(
        functools.partial(_merge_kernel, with_router=with_router),
        out_shape=tuple(out_shape),
        grid=(n // tm,),
        in_specs=in_specs,
        out_specs=tuple(out_specs),
        compiler_params=_params(1),
        name="merge_outproj_ln",
    )(*args)


def _swiglu_chunks(u, w_up_ref, w_down_ref, hidden, chunk):
    acc = None
    for c0 in range(0, hidden, chunk):
        c1 = min(c0 + chunk, hidden)
        a = _dot(u, w_up_ref[:, c0:c1])
        g = _dot(u, w_up_ref[:, hidden + c0:hidden + c1])
        y = _dot(_bf(a * _sigmoid(a) * g), w_down_ref[c0:c1, :])
        acc = y if acc is None else acc + y
    return acc


def _ffn_kernel(u_ref, x1_ref, g2_ref, wu_ref, wd_ref, lg_ref, lb_ref, o_ref, *, chunk):
    f = _swiglu_chunks(u_ref[...], wu_ref, wd_ref, D_FF, chunk)
    z = ALPHA * x1_ref[...] + g2_ref[...] * f
    o_ref[...] = _layer_norm_rows(z, lg_ref[...], lb_ref[...])


def _ffn_dense(u2, x1, gate2, w_up, w_down, ln_g, ln_b, tm, tiles_per_group, chunk):
    n = u2.shape[0]
    r = gate2.shape[1]
    resident = lambda shape: pl.BlockSpec(shape, lambda i: (0,) * len(shape), pipeline_mode=pl.Buffered(1))
    return pl.pallas_call(
        functools.partial(_ffn_kernel, chunk=chunk),
        out_shape=jax.ShapeDtypeStruct((n, D_MODEL), F32),
        grid=(n // tm,),
        in_specs=[
            pl.BlockSpec((tm, D_MODEL), lambda i: (i, 0)),
            pl.BlockSpec((tm, D_MODEL), lambda i: (i, 0)),
            pl.BlockSpec((None, r, D_MODEL), lambda i: (i // tiles_per_group, 0, 0)),
            resident((D_MODEL, 2 * D_FF)),
            resident((D_FF, D_MODEL)),
            pl.BlockSpec((1, D_MODEL), lambda i: (0, 0)),
            pl.BlockSpec((1, D_MODEL), lambda i: (0, 0)),
        ],
        out_specs=pl.BlockSpec((tm, D_MODEL), lambda i: (i, 0)),
        compiler_params=_params(1),
        name="ffn_dense",
    )(u2, x1, gate2, w_up, w_down, ln_g.reshape(1, D_MODEL), ln_b.reshape(1, D_MODEL))


def _moe_kernel(u_ref, x1_ref, g2_ref, cmb_ref, wa_ref, wg_ref, wd_ref, lg_ref, lb_ref, o_ref, acc_scr):
    e = pl.program_id(1)
    j = pl.program_id(2)

    @pl.when((e == 0) & (j == 0))
    def _():
        acc_scr[...] = jnp.zeros_like(acc_scr)

    u = u_ref[...]
    a = _dot(u, wa_ref[...])
    g = _dot(u, wg_ref[...])
    y = _dot(_bf(a * _sigmoid(a) * g), wd_ref[...])
    cmb = cmb_ref[...]
    lane = lax.broadcasted_iota(jnp.int32, cmb.shape, 1)
    ce = jnp.sum(jnp.where(lane == e, cmb, 0.0), axis=-1, keepdims=True)
    acc_scr[...] += ce * y

    @pl.when((e == pl.num_programs(1) - 1) & (j == pl.num_programs(2) - 1))
    def _():
        z = ALPHA * x1_ref[...] + g2_ref[...] * acc_scr[...]
        o_ref[...] = _layer_norm_rows(z, lg_ref[...], lb_ref[...])


def _ffn_experts(u2, x1, gate2, combine, w_in, w_out, ln_g, ln_b, tm, tiles_per_group, th):
    n = u2.shape[0]
    r = gate2.shape[1]
    nh = D_EXPERT // th
    return pl.pallas_call(
        _moe_kernel,
        out_shape=jax.ShapeDtypeStruct((n, D_MODEL), F32),
        grid=(n // tm, N_EXPERTS, nh),
        in_specs=[
            pl.BlockSpec((tm, D_MODEL), lambda i, e, j: (i, 0)),
            pl.BlockSpec((tm, D_MODEL), lambda i, e, j: (i, 0)),
            pl.BlockSpec((None, r, D_MODEL), lambda i, e, j: (i // tiles_per_group, 0, 0)),
            pl.BlockSpec((tm, LANES), lambda i, e, j: (i, 0)),
            pl.BlockSpec((None, D_MODEL, th), lambda i, e, j: (e, 0, j)),
            pl.BlockSpec((None, D_MODEL, th), lambda i, e, j: (e, 0, nh + j)),
            pl.BlockSpec((None, th, D_MODEL), lambda i, e, j: (e, j, 0)),
            pl.BlockSpec((1, D_MODEL), lambda i, e, j: (0, 0)),
            pl.BlockSpec((1, D_MODEL), lambda i, e, j: (0, 0)),
        ],
        out_specs=pl.BlockSpec((tm, D_MODEL), lambda i, e, j: (i, 0)),
        scratch_shapes=[pltpu.VMEM((tm, D_MODEL), F32)],
        compiler_params=_params(3),
        name="ffn_experts",
    )(u2, x1, gate2, combine, w_in, w_in, w_out, ln_g.reshape(1, D_MODEL), ln_b.reshape(1, D_MODEL))


EXPERT_TILE = 1024


def _rank_kernel(rr_ref, rank_ref, cnt_ref, carry_scr):
    b = pl.program_id(0)

    @pl.when(b == 0)
    def _():
        carry_scr[...] = jnp.zeros_like(carry_scr)

    rr = rr_ref[...]
    tm = rr.shape[1]
    i1 = rr[2:3]
    i2 = rr[3:4]
    e = lax.broadcasted_iota(jnp.int32, (N_EXPERTS, tm), 0).astype(F32)
    oh1 = jnp.where(e == i1, 1.0, 0.0)
    oh2 = jnp.where(e == i2, 1.0, 0.0)
    sel = oh1 + oh2
    r = lax.broadcasted_iota(jnp.int32, (tm, tm), 0)
    c = lax.broadcasted_iota(jnp.int32, (tm, tm), 1)
    triu = jnp.where(r <= c, 1.0, 0.0).astype(BF16)
    csum = _dot(_bf(sel), triu)
    carry = carry_scr[:, 0:1]
    rank = csum - sel + carry
    r1 = jnp.sum(oh1 * rank, axis=0, keepdims=True)
    r2 = jnp.sum(oh2 * rank, axis=0, keepdims=True)
    rec = jnp.concatenate([i1, i2, r1, r2, jnp.zeros((4, tm), F32)], axis=0)
    rank_ref[...] = rec.astype(jnp.int32)
    total = carry + csum[:, tm - 1:tm]
    carry_scr[...] = jnp.broadcast_to(total, carry_scr.shape)
    cnt_ref[...] = jnp.broadcast_to(total, cnt_ref.shape).astype(jnp.int32)


def _route_ranks(route_row):
    nblk, _, tm = route_row.shape
    return pl.pallas_call(
        _rank_kernel,
        out_shape=(jax.ShapeDtypeStruct((nblk, 8, tm), jnp.int32),
                   jax.ShapeDtypeStruct((N_EXPERTS, LANES), jnp.int32)),
        grid=(nblk,),
        in_specs=[pl.BlockSpec((None, 8, tm), lambda b: (b, 0, 0))],
        out_specs=(pl.BlockSpec((None, 8, tm), lambda b: (b, 0, 0)),
                   pl.BlockSpec((N_EXPERTS, LANES), lambda b: (0, 0))),
        scratch_shapes=[pltpu.VMEM((N_EXPERTS, LANES), F32)],
        compiler_params=_params(1),
        name="route_ranks",
    )(route_row)


def _row_copy(src_ref, src_row, dst_ref, dst_row, sem):
    return pltpu.make_async_copy(src_ref.at[pl.ds(src_row, 1)], dst_ref.at[pl.ds(dst_row, 1)], sem)


def _dispatch_kernel(last_ref, slot_ref, u_ref, xs_ref, zero_scr, sem):
    tm = u_ref.shape[0]
    gt = zero_scr.shape[0]

    @pl.when(pl.program_id(0) == 0)
    def _():
        zero_scr[...] = jnp.zeros_like(zero_scr)
        for e in range(N_EXPERTS):
            @pl.when(last_ref[e] >= 0)
            def _():
                first = pl.multiple_of(last_ref[e], gt)
                fill = pltpu.make_async_copy(zero_scr, xs_ref.at[pl.ds(first, gt)], sem)
                fill.start()
                fill.wait()

    def start(r, carry):
        for k in range(2):
            _row_copy(u_ref, r, xs_ref, slot_ref[k, r], sem).start()
        return carry

    lax.fori_loop(0, tm, start, 0, unroll=8)

    def wait(r, carry):
        for k in range(2):
            _row_copy(u_ref, 0, xs_ref, 0, sem).wait()
        return carry

    lax.fori_loop(0, tm, wait, 0, unroll=8)


def _dispatch(u2, slots, last_tile, n_slots, group_tile):
    n, d = u2.shape
    nblk, _, tm = slots.shape
    grid_spec = pltpu.PrefetchScalarGridSpec(
        num_scalar_prefetch=1,
        grid=(nblk,),
        in_specs=[
            pl.BlockSpec((None, 2, tm), lambda b, last: (b, 0, 0), memory_space=pltpu.SMEM),
            pl.BlockSpec((tm, d), lambda b, last: (b, 0)),
        ],
        out_specs=pl.BlockSpec(memory_space=pl.ANY),
        scratch_shapes=[pltpu.VMEM((group_tile, d), F32), pltpu.SemaphoreType.DMA],
    )
    return pl.pallas_call(
        _dispatch_kernel,
        out_shape=jax.ShapeDtypeStruct((n_slots, d), F32),
        grid_spec=grid_spec,
        compiler_params=_params(1),
        name="expert_dispatch",
    )(last_tile, slots, u2)


def _grouped_kernel(toff_ref, nt_ref, xs_ref, wi_ref, wo_ref, ys_ref, *, chunk):
    @pl.when(pl.program_id(0) < nt_ref[0])
    def _():
        ys_ref[...] = _swiglu_chunks(_bf(xs_ref[...]), wi_ref, wo_ref, D_EXPERT, chunk)


def _grouped_experts(xs, tile_off, n_tiles, w_in, w_out, tm, chunk):
    n_slots, d = xs.shape

    def expert(i, toff):
        e = 0
        for k in range(1, N_EXPERTS):
            e = e + (i >= toff[k]).astype(jnp.int32)
        return e

    def tile(i, nt):
        return jnp.minimum(i, nt[0] - 1)

    grid_spec = pltpu.PrefetchScalarGridSpec(
        num_scalar_prefetch=2,
        grid=(n_slots // tm,),
        in_specs=[
            pl.BlockSpec((tm, d), lambda i, toff, nt: (tile(i, nt), 0)),
            pl.BlockSpec((None, d, 2 * D_EXPERT), lambda i, toff, nt: (expert(tile(i, nt), toff), 0, 0),
                         pipeline_mode=pl.Buffered(1)),
            pl.BlockSpec((None, D_EXPERT, d), lambda i, toff, nt: (expert(tile(i, nt), toff), 0, 0),
                         pipeline_mode=pl.Buffered(1)),
        ],
        out_specs=pl.BlockSpec((tm, d), lambda i, toff, nt: (tile(i, nt), 0)),
    )
    return pl.pallas_call(
        functools.partial(_grouped_kernel, chunk=chunk),
        out_shape=jax.ShapeDtypeStruct((n_slots, d), F32),
        grid_spec=grid_spec,
        compiler_params=_params(1),
        name="grouped_experts",
    )(tile_off, n_tiles, xs, w_in, w_out)


def _combine_kernel(slot_ref, ys_ref, x1_ref, g2_ref, rcol_ref, lg_ref, lb_ref, o_ref, y1_scr, y2_scr, sem):
    tm = x1_ref.shape[0]
    bufs = (y1_scr, y2_scr)

    def start(r, carry):
        for k in range(2):
            _row_copy(ys_ref, slot_ref[k, r], bufs[k], r, sem).start()
        return carry

    lax.fori_loop(0, tm, start, 0, unroll=8)

    def wait(r, carry):
        for k in range(2):
            _row_copy(ys_ref, 0, bufs[k], 0, sem).wait()
        return carry

    lax.fori_loop(0, tm, wait, 0, unroll=8)
    rec = rcol_ref[...]
    f = rec[:, ROUTE_W1:ROUTE_W1 + 1] * y1_scr[...] + rec[:, ROUTE_W2:ROUTE_W2 + 1] * y2_scr[...]
    z = ALPHA * x1_ref[...] + g2_ref[...] * f
    o_ref[...] = _layer_norm_rows(z, lg_ref[...], lb_ref[...])


def _combine(ys, slots, x1, gate2, route_col, ln_g, ln_b, tiles_per_group):
    n, d = x1.shape
    nblk, _, tm = slots.shape
    r = gate2.shape[1]
    return pl.pallas_call(
        _combine_kernel,
        out_shape=jax.ShapeDtypeStruct((n, d), F32),
        grid=(nblk,),
        in_specs=[
            pl.BlockSpec((None, 2, tm), lambda b: (b, 0, 0), memory_space=pltpu.SMEM),
            pl.BlockSpec(memory_space=pl.ANY),
            pl.BlockSpec((tm, d), lambda b: (b, 0)),
            pl.BlockSpec((None, r, d), lambda b: (b // tiles_per_group, 0, 0)),
            pl.BlockSpec((tm, LANES), lambda b: (b, 0)),
            pl.BlockSpec((1, d), lambda b: (0, 0)),
            pl.BlockSpec((1, d), lambda b: (0, 0)),
        ],
        out_specs=pl.BlockSpec((tm, d), lambda b: (b, 0)),
        scratch_shapes=[pltpu.VMEM((tm, d), F32), pltpu.VMEM((tm, d), F32), pltpu.SemaphoreType.DMA],
        compiler_params=_params(1),
        name="expert_combine_ln",
    )(slots, ys, x1, gate2, route_col, ln_g.reshape(1, d), ln_b.reshape(1, d))


def _ffn_routed(u2, x1, gate2, route_col, route_row, w_in, w_out, ln_g, ln_b, tiles_per_group):
    n = u2.shape[0]
    tm = min(EXPERT_TILE, n)
    assert n % tm == 0
    ranks, counts = _route_ranks(route_row)
    cnt = counts[:, 0]
    padded = ((cnt + tm - 1) // tm) * tm
    off = (jnp.cumsum(padded) - padded).astype(jnp.int32)
    last_tile = jnp.where(padded > 0, off + padded - tm, -1).astype(jnp.int32)
    n_slots = 2 * n + N_EXPERTS * tm
    slots = off[ranks[:, 0:2, :]] + ranks[:, 2:4, :]
    xs = _dispatch(u2, slots, last_tile, n_slots, tm)
    ys = _grouped_experts(xs, (off // tm).astype(jnp.int32), (jnp.sum(padded) // tm).astype(jnp.int32).reshape(1),
                          w_in, w_out, tm, 512)
    return _combine(ys, slots, x1, gate2, route_col, ln_g, ln_b, tiles_per_group)


def _split_w_in(w_in_l):
    w = BRANCH_W
    col = lambda off, n: w_in_l[:, off:off + n]
    o = 0
    ret_q, ret_k, ret_v, ret_g = (col(o + i * w, w) for i in range(4)); o += 4 * w
    fox_q, fox_k, fox_v = (col(o + i * w, w) for i in range(3)); o += 3 * w
    fox_f = col(o, N_HEADS); o += N_HEADS
    conv = col(o, 2 * w); o += 2 * w
    diff_q, diff_k, diff_v = (col(o + i * w, w) for i in range(3)); o += 3 * w
    gate = col(o, 4 * D_MODEL)
    rot_r = lambda m: _rotated_columns(m, HEAD_DIM, HEAD_DIM)
    rot_d = lambda m: _rotated_columns(m, DIFF_SUB, ROT_DIM)
    w_ret = jnp.concatenate([ret_q, rot_r(ret_q), ret_k, rot_r(ret_k), ret_v, ret_g], axis=1)
    w_fox = jnp.concatenate([fox_q, fox_k, fox_v, fox_f, jnp.zeros((D_MODEL, LANES - N_HEADS), F32)], axis=1)
    w_diff = jnp.concatenate([diff_q, rot_d(diff_q), diff_k, rot_d(diff_k), diff_v], axis=1)
    return _bf(w_ret), _bf(w_fox), _bf(conv), _bf(w_diff), _bf(gate)


def _block_diag_state(s):
    b = s.shape[0]
    eye = jnp.eye(N_HEADS, dtype=s.dtype)
    return jnp.einsum('bhde,hg->bhdge', s, eye).reshape(b, BRANCH_W, BRANCH_W)


def _diag_blocks(s_bd):
    b = s_bd.shape[0]
    s4 = s_bd.reshape(b, N_HEADS, HEAD_DIM, N_HEADS, HEAD_DIM)
    return jnp.stack([s4[:, h, :, h, :] for h in range(N_HEADS)], axis=1)


def _pad_rows(a, rows):
    return jnp.concatenate([a, jnp.zeros((a.shape[0], rows - a.shape[1]) + a.shape[2:], a.dtype)], axis=1)


def _trunk_layer(x, mod, pos, past, l, lw, sample, depth, kv_state):
    b, t, _ = x.shape
    d = D_MODEL
    w = BRANCH_W
    shift1, scale1, gate1 = (mod[0][:, None, i * d:(i + 1) * d] for i in range(3))
    shift2, scale2, gate2 = (mod[1][:, None, i * d:(i + 1) * d] for i in range(3))
    lam_init = 0.8 - 0.6 * math.exp(-0.3 * l)
    chunk = min(t, KV_BLOCK)
    tm_proj = min(t, 1024)

    s0 = jnp.zeros((b, w, w), F32) if past is None else _block_diag_state(past[5])
    h_ret, s_bd = _retention(x, scale1, shift1, lw['w_ret'], pos, s0, chunk)
    ret_state = _diag_blocks(s_bd)

    if past is None:
        hist = jnp.zeros((b, HIST_ROWS, w), F32)
    else:
        hist = jnp.concatenate([jnp.zeros((b, HIST_ROWS - (CONV_W - 1), w), F32), past[6]], axis=1)
    h_conv, tail = _conv_mixer(x, scale1, shift1, lw['w_conv_in'], hist, lw['w_conv'], lw['b_conv'],
                               lw['conv_ln_g'], lw['conv_ln_b'], chunk)
    conv_buf = tail[:, HIST_ROWS - (CONV_W - 1):, :]

    fox_state, diff_state = (None, None) if kv_state is None else kv_state
    fq, fkb, fvb, lf, fox_state = _fox_proj(x, scale1, shift1, lw['w_fox'], lw['b_fox_f'], tm_proj, depth, l, fox_state)
    if past is None:
        off, tk_valid = 0, t
        k_all, v_all, lf_all, fox_new = fkb, fvb, lf, None
    else:
        p_len = past[0].shape[1]
        off, tk_valid = p_len, p_len + t
        tk_pad = p_len + KV_BLOCK
        k_all, v_all = past[0].reshape(b, p_len, w), past[1].reshape(b, p_len, w)
        fox_new = (_pad_rows(fkb, KV_BLOCK), _pad_rows(fvb, KV_BLOCK))
        past_lf = jnp.concatenate([past[2].astype(F32), jnp.zeros((b, p_len, LANES - N_HEADS), F32)], axis=2)
        lf_all = _pad_rows(jnp.concatenate([past_lf, lf], axis=1), tk_pad)
    fcol, frow = _logf_cumsum(lf_all)
    tq = min(t, KV_BLOCK)
    h_fox = _fox_attention(fq, k_all, v_all, fcol, frow, tq, off, tk_valid, fox_new)

    dq, dkb, dvb, diff_state = _diff_proj(x, scale1, shift1, lw['w_diff'], pos, tm_proj, depth, l, diff_state)
    if past is None:
        dk_all, dv_all, diff_new = dkb, dvb, None
    else:
        dk_all, dv_all = past[3].reshape(b, p_len, w), past[4].reshape(b, p_len, w)
        diff_new = (_pad_rows(dkb, KV_BLOCK), _pad_rows(dvb, KV_BLOCK))
    h_diff = _diff_attention(dq, dk_all, dv_all, lw['diff_lambda'], lw['diff_subln_g'], tq, off, tk_valid, lam_init,
                             diff_new)

    n = b * t
    if sample:
        tm_merge = tm_ffn = min(n, 256)
        tpg_merge = tpg_ffn = 1
        rows_mod = lambda m, tm: jnp.repeat(m, t, axis=1).reshape(n // tm, tm, d)
    else:
        tm_merge, tm_ffn = min(t, 512), min(t, 1024)
        tpg_merge, tpg_ffn = t // tm_merge, t // tm_ffn
        rows_mod = lambda m, tm: m
    mods = tuple(rows_mod(m, tm_merge) for m in (scale1, shift1, gate1, scale2, shift2))
    branches = tuple(h.reshape(n, w) for h in (h_ret, h_fox, h_conv, h_diff))
    merged = _merge(x.reshape(n, d), mods, branches, lw['w_gate'], lw['w_branch'], lw['w_out'],
                    lw['ln_g'][0], lw['ln_b'][0], lw.get('router'), tm_merge, tpg_merge)
    g2 = rows_mod(gate2, tm_ffn)
    if 'router' in lw and not sample:
        x1, u2, route_col, route_row = merged
        x2 = _ffn_routed(u2, x1, rows_mod(gate2, tm_merge), route_col, route_row, lw['w_exp_in'], lw['w_exp_out'],
                         lw['ln_g'][1], lw['ln_b'][1], tpg_merge)
    elif 'router' in lw:
        x1, u2, route_col, _ = merged
        x2 = _ffn_experts(_bf(u2), x1, g2, route_col, lw['w_exp_in'], lw['w_exp_out'], lw['ln_g'][1], lw['ln_b'][1],
                          tm_ffn, tpg_ffn, 512)
    else:
        x1, u2 = merged
        x2 = _ffn_dense(u2, x1, g2, lw['w_ffn_in'], lw['w_ffn_out'], lw['ln_g'][1], lw['ln_b'][1],
                        tm_ffn, tpg_ffn, 512)
    return x2.reshape(b, t, d), (fox_state, diff_state), (ret_state, conv_buf)


def _state_outputs(kv_state, small):
    (fk, fv, lf), (dk, dv) = kv_state
    depth, b, t, w = fk.shape
    heads = lambda a, nh: a.reshape(depth, b, t, nh, w // nh)
    ret_state, conv_buf = (jnp.stack(a) for a in zip(*small))
    return (heads(fk, N_HEADS), heads(fv, N_HEADS), lf[..., :N_HEADS], heads(dk, N_SUB), heads(dv, N_HEADS),
            ret_state, conv_buf)


def kernel(x_prompt, x_sample, c_prompt, c_sample, cache_fox_k, cache_fox_v, cache_fox_logf, cache_diff_k, cache_diff_v, state_ret, state_conv, w_in, b_fox_f, w_conv, b_conv, conv_ln_g, conv_ln_b, diff_lambda, diff_subln_g, w_branch, w_out, w_ada, b_ada, ln_g, ln_b, w_ffn_in, w_ffn_out, w_router, b_router, w_exp_in, w_exp_out):
    depth = w_in.shape[0]
    bp = x_prompt.shape[0]
    past_len = cache_fox_k.shape[2]
    pos_p = jnp.arange(x_prompt.shape[1], dtype=jnp.int32)
    pos_s = past_len + jnp.arange(x_sample.shape[1], dtype=jnp.int32)
    mod = _modulation(jnp.concatenate([c_prompt, c_sample], axis=0), w_ada, b_ada)
    yp, ys = x_prompt, x_sample
    kv_p = kv_s = None
    small_p, small_s = [], []
    for l in range(depth):
        w_ret, w_fox, w_conv_in, w_diff, w_gate = _split_w_in(w_in[l])
        lw = dict(w_ret=w_ret, w_fox=w_fox, w_conv_in=w_conv_in, w_diff=w_diff, w_gate=w_gate,
                  b_fox_f=b_fox_f[l], w_conv=w_conv[l], b_conv=b_conv[l], conv_ln_g=conv_ln_g[l],
                  conv_ln_b=conv_ln_b[l], diff_lambda=diff_lambda[l], diff_subln_g=diff_subln_g[l],
                  w_branch=_bf(w_branch[l]), w_out=_bf(w_out[l]), ln_g=ln_g[l], ln_b=ln_b[l])
        if l % 2 == 0:
            lw['w_ffn_in'] = _bf(w_ffn_in[l // 2])
            lw['w_ffn_out'] = _bf(w_ffn_out[l // 2])
        else:
            wr = jnp.concatenate([w_router[l // 2], jnp.zeros((D_MODEL, LANES - N_EXPERTS), F32)], axis=1)
            br = jnp.concatenate([b_router[l // 2].astype(F32), jnp.zeros((LANES - N_EXPERTS,), F32)]).reshape(1, LANES)
            lw['router'] = (_bf(wr), br)
            lw['w_exp_in'] = _bf(w_exp_in[l // 2])
            lw['w_exp_out'] = _bf(w_exp_out[l // 2])
        past_l = (cache_fox_k[l], cache_fox_v[l], cache_fox_logf[l], cache_diff_k[l], cache_diff_v[l],
                  state_ret[l], state_conv[l])
        yp, kv_p, st_p = _trunk_layer(yp, mod[l][:, :bp], pos_p, None, l, lw, False, depth, kv_p)
        ys, kv_s, st_s = _trunk_layer(ys, mod[l][:, bp:], pos_s, past_l, l, lw, True, depth, kv_s)
        small_p.append(st_p)
        small_s.append(st_s)
    return (yp, ys) + _state_outputs(kv_p, small_p) + _state_outputs(kv_s, small_s)
```

```python
import functools
import math

import jax
import jax.numpy as jnp
from jax import lax
from jax.experimental import pallas as pl
from jax.experimental.pallas import tpu as pltpu

D_MODEL = 1024
BRANCH_W = 256
HEAD_DIM = 64
N_HEADS = 4
DIFF_SUB = 32
N_SUB = 8
ROT_DIM = DIFF_SUB // 4
RET_THETA = 10000.0
ROPE_THETA = 500000.0
CHUNK = 64
CONV_W = 31
D_FF = 2816
N_EXPERTS = 8
D_EXPERT = 3584
DEPTH = 2
ALPHA = (2.0 * DEPTH) ** 0.25
EPS = 1e-5
NEG = -1e30
LOG2E = math.log2(math.e)

LANES = 128
BF16_ROWS = 16
KV_BLOCK = 256
HIST_ROWS = 32
VMEM_LIMIT = 56 * 1024 * 1024

F32 = jnp.float32
BF16 = jnp.bfloat16


def _bf(x):
    return x.astype(BF16)


def _dot(a, b):
    return jnp.dot(a, b, preferred_element_type=F32)


def _dot_nt(a, b):
    return lax.dot_general(a, b, (((1,), (1,)), ((), ())), preferred_element_type=F32)


def _dot_tn(a, b):
    return lax.dot_general(a, b, (((0,), (0,)), ((), ())), preferred_element_type=F32)


def _sigmoid(x):
    return 1.0 / (1.0 + jnp.exp(-x))


def _params(n_axes):
    return pltpu.CompilerParams(dimension_semantics=("arbitrary",) * n_axes,
                                vmem_limit_bytes=VMEM_LIMIT)


def _head_sum(y, width):
    n = y.shape[-1]
    r = lax.broadcasted_iota(jnp.int32, (n, n), 0) // width
    c = lax.broadcasted_iota(jnp.int32, (n, n), 1) // width
    bd = jnp.where(r == c, 1.0, 0.0).astype(BF16)
    hi = _bf(y)
    lo = _bf(y - hi.astype(F32))
    return _dot(hi, bd) + _dot(lo, bd)


def _layer_norm_rows(z, g, b):
    mu = jnp.mean(z, axis=-1, keepdims=True)
    d = z - mu
    var = jnp.mean(d * d, axis=-1, keepdims=True)
    return d * lax.rsqrt(var + EPS) * g + b


def _mod_kernel(c_ref, w_ref, b_ref, o_ref):
    c = c_ref[...]
    sc = _bf(c * _sigmoid(c))
    o_ref[...] = _dot(sc, _bf(w_ref[...])) + b_ref[...]


def _modulation(c_all, w_ada, b_ada):
    rows = c_all.shape[0]
    depth = w_ada.shape[0]
    d3 = w_ada.shape[-1]
    nj = d3 // D_MODEL
    return pl.pallas_call(
        _mod_kernel,
        out_shape=jax.ShapeDtypeStruct((depth, 2, rows, d3), F32),
        grid=(depth * 2, nj),
        in_specs=[
            pl.BlockSpec((rows, D_MODEL), lambda i, j: (0, 0)),
            pl.BlockSpec((None, None, D_MODEL, D_MODEL), lambda i, j: (i // 2, i % 2, 0, j)),
            pl.BlockSpec((None, None, 1, D_MODEL), lambda i, j: (i // 2, i % 2, 0, j)),
        ],
        out_specs=pl.BlockSpec((None, None, rows, D_MODEL), lambda i, j: (i // 2, i % 2, 0, j)),
        compiler_params=_params(2),
        name="adaln_modulation",
    )(c_all, w_ada, b_ada.reshape(depth, 2, 1, d3))


def _ret_kernel(q_ref, k_ref, v_ref, g_ref, dmask_ref, qdec_ref, kdec_ref, cdec_ref, s0_ref,
                h_ref, sout_ref, s_scr):
    c = pl.program_id(1)

    @pl.when(c == 0)
    def _():
        s_scr[...] = s0_ref[...]

    w = BRANCH_W
    q = q_ref[...]
    kb = k_ref[...]
    vb = v_ref[...]
    g = g_ref[...]
    rows = q.shape[0]
    lane_head = lax.broadcasted_iota(jnp.int32, (1, w), 1) // HEAD_DIM
    y = jnp.zeros((rows, w), F32)
    for h in range(N_HEADS):
        mh = lane_head == h
        qh = jnp.where(mh, q, jnp.zeros_like(q))
        a = _dot_nt(qh, kb) * dmask_ref[h]
        y = y + jnp.where(mh, _dot(_bf(a), vb), 0.0)
    s_prev = s_scr[...]
    y = y + _dot(_bf(q.astype(F32) * qdec_ref[...]), _bf(s_prev))
    kv = _dot_tn(_bf(kb.astype(F32) * kdec_ref[...]), vb)
    r = lax.broadcasted_iota(jnp.int32, (w, w), 0) // HEAD_DIM
    cc = lax.broadcasted_iota(jnp.int32, (w, w), 1) // HEAD_DIM
    s_new = cdec_ref[...] * s_prev + jnp.where(r == cc, kv, 0.0)
    s_scr[...] = s_new
    sout_ref[...] = s_new
    mu = _head_sum(y, HEAD_DIM) * (1.0 / HEAD_DIM)
    d = y - mu
    var = _head_sum(d * d, HEAD_DIM) * (1.0 / HEAD_DIM)
    hn = d * lax.rsqrt(var + EPS)
    h_ref[...] = _bf(hn * (g * _sigmoid(g)))


def _retention_tables(chunk):
    log_g = jnp.log1p(-jnp.exp2(-5.0 - jnp.arange(N_HEADS, dtype=F32)))
    idx = jnp.arange(chunk, dtype=F32)
    dist = jnp.abs(idx[:, None] - idx[None, :])
    sub = jnp.arange(chunk) // CHUNK
    vis = sub[None, :] <= sub[:, None]
    dmask = jnp.where(vis[None], jnp.exp(log_g[:, None, None] * dist[None]), 0.0)
    lg_lane = jnp.repeat(log_g, HEAD_DIM)[None, :]
    qdec = jnp.exp(lg_lane * (idx[:, None] + 1.0))
    kdec = jnp.exp(lg_lane * (chunk - 1.0 - idx[:, None]))
    cdec = jnp.exp(lg_lane * chunk)
    return dmask.astype(F32), qdec, kdec, cdec


def _rope_tables(pos, dim, rot_dim, theta, n_rep):
    half = rot_dim // 2
    inv_freq = jnp.exp(-math.log(theta) * jnp.arange(half, dtype=F32) / half)
    ang = pos.astype(F32)[:, None] * inv_freq[None, :]
    t = pos.shape[0]
    cos = jnp.concatenate([jnp.cos(ang), jnp.cos(ang), jnp.ones((t, dim - rot_dim), F32)], axis=1)
    sin = jnp.concatenate([jnp.sin(ang), jnp.sin(ang), jnp.zeros((t, dim - rot_dim), F32)], axis=1)
    return jnp.tile(cos, (1, n_rep)), jnp.tile(sin, (1, n_rep))


def _rotated_columns(w, dim, rot_dim):
    half = rot_dim // 2
    k, n = w.shape
    wh = w.reshape(k, n // dim, dim)
    rot = jnp.concatenate([-wh[..., half:rot_dim], wh[..., :half], jnp.zeros_like(wh[..., rot_dim:])], axis=-1)
    return rot.reshape(k, n)


def _retention(q, k, v, g, s0_bd, chunk):
    b, t, w = q.shape
    nc = t // chunk
    dmask, qdec, kdec, cdec = _retention_tables(chunk)
    full = lambda shape: pl.BlockSpec(shape, lambda i, j: (0,) * len(shape))
    row = pl.BlockSpec((None, chunk, w), lambda i, j: (i, j, 0))
    return pl.pallas_call(
        _ret_kernel,
        out_shape=(jax.ShapeDtypeStruct((b, t, w), BF16), jax.ShapeDtypeStruct((b, w, w), F32)),
        grid=(b, nc),
        in_specs=[
            row, row, row, row,
            full((N_HEADS, chunk, chunk)),
            full((chunk, w)),
            full((chunk, w)),
            full((1, w)),
            pl.BlockSpec((None, w, w), lambda i, j: (i, 0, 0)),
        ],
        out_specs=(row, pl.BlockSpec((None, w, w), lambda i, j: (i, 0, 0))),
        scratch_shapes=[pltpu.VMEM((w, w), F32)],
        compiler_params=_params(2),
        name="retention_mixer",
    )(q, k, v, g, dmask, qdec, kdec, cdec, s0_bd)


def _conv_kernel(glu_ref, hist_ref, wc_ref, bc_ref, lg_ref, lb_ref, h_ref, tail_ref, xp_scr):
    c = pl.program_id(1)
    rows = glu_ref.shape[0]
    w = BRANCH_W
    pad = HIST_ROWS - (CONV_W - 1)

    @pl.when(c == 0)
    def _():
        xp_scr[0:HIST_ROWS, :] = hist_ref[...]

    xp_scr[HIST_ROWS:HIST_ROWS + rows, :] = glu_ref[...]
    sub = 8
    acc = jnp.zeros((rows, w), F32) + bc_ref[...]
    for rho in range(sub):
        z = None
        for m in range((pad + CONV_W - 1) // sub + 1):
            j = sub * m + rho - pad
            if not 0 <= j < CONV_W:
                continue
            span = rows if rho == 0 else rows + sub
            term = xp_scr[sub * m:sub * m + span, :] * wc_ref[j:j + 1, :]
            z = term if z is None else z + term
        acc = acc + z[rho:rho + rows, :]
    y = _layer_norm_rows(acc, lg_ref[...], lb_ref[...])
    h_ref[...] = _bf(y * _sigmoid(y))
    tail = xp_scr[rows:rows + HIST_ROWS, :]
    tail_ref[...] = tail
    xp_scr[0:HIST_ROWS, :] = tail


def _conv_mixer(glu, hist, w_conv, b_conv, ln_g, ln_b, chunk):
    b, t, w = glu.shape
    nc = t // chunk
    full = lambda shape: pl.BlockSpec(shape, lambda i, j: (0,) * len(shape))
    wc = jnp.concatenate([w_conv, jnp.zeros((HIST_ROWS - CONV_W, w), F32)], axis=0)
    return pl.pallas_call(
        _conv_kernel,
        out_shape=(jax.ShapeDtypeStruct((b, t, w), BF16), jax.ShapeDtypeStruct((b, HIST_ROWS, w), F32)),
        grid=(b, nc),
        in_specs=[
            pl.BlockSpec((None, chunk, w), lambda i, j: (i, j, 0)),
            pl.BlockSpec((None, HIST_ROWS, w), lambda i, j: (i, 0, 0)),
            full((HIST_ROWS, w)),
            full((1, w)),
            full((1, w)),
            full((1, w)),
        ],
        out_specs=(pl.BlockSpec((None, chunk, w), lambda i, j: (i, j, 0)),
                   pl.BlockSpec((None, HIST_ROWS, w), lambda i, j: (i, 0, 0))),
        scratch_shapes=[pltpu.VMEM((HIST_ROWS + chunk, w), F32)],
        compiler_params=_params(2),
        name="conv_mixer",
    )(glu, hist, wc, b_conv.reshape(1, w), ln_g.reshape(1, w), ln_b.reshape(1, w))


def _layer_slot(depth, layer, b, t, tm, width, dt, first):
    shape = jax.ShapeDtypeStruct((depth, b, t, width), dt)
    if first:
        return shape, pl.BlockSpec((depth, None, tm, width), lambda i, j: (0, i, j, 0))
    return shape, pl.BlockSpec((None, None, tm, width), lambda i, j: (layer, i, j, 0))


def _store_layer_rows(ref, rows):
    if len(ref.shape) == 3:
        for d in range(ref.shape[0]):
            ref[d] = rows
    else:
        ref[...] = rows


def _in_proj_kernel(x_ref, sc_ref, sh_ref, w_ref, bf_ref, rcos_ref, rsin_ref, dcos_ref, dsin_ref, *refs):
    (rq_ref, rk_ref, rv_ref, rg_ref, glu_ref, fq_ref, fk_ref, fv_ref, fkb_ref, fvb_ref, lf_ref,
     dq_ref, dk_ref, dv_ref, dkb_ref, dvb_ref) = refs[-16:]
    w = BRANCH_W
    u = _bf(x_ref[...] * (1.0 + sc_ref[...]) + sh_ref[...])
    o = 0
    p = _dot(u, w_ref[:, o:o + 6 * w])
    cos = rcos_ref[...]
    sin = rsin_ref[...]
    rq_ref[...] = _bf(p[:, 0:w] * cos + p[:, w:2 * w] * sin)
    rk_ref[...] = _bf((p[:, 2 * w:3 * w] * cos + p[:, 3 * w:4 * w] * sin) * (HEAD_DIM ** -0.5))
    rv_ref[...] = _bf(p[:, 4 * w:5 * w])
    rg_ref[...] = p[:, 5 * w:6 * w]
    o += 6 * w
    p = _dot(u, w_ref[:, o:o + 2 * w])
    glu_ref[...] = p[:, :w] * _sigmoid(p[:, w:])
    o += 2 * w
    p = _dot(u, w_ref[:, o:o + 3 * w + LANES])
    fq_ref[...] = _bf(p[:, 0:w] * (HEAD_DIM ** -0.5 * LOG2E))
    k = p[:, w:2 * w]
    v = p[:, 2 * w:3 * w]
    _store_layer_rows(fk_ref, k)
    _store_layer_rows(fv_ref, v)
    fkb_ref[...] = _bf(k)
    fvb_ref[...] = _bf(v)
    z = p[:, 3 * w:] + bf_ref[...]
    lf = jnp.minimum(z, 0.0) - jnp.log(1.0 + jnp.exp(-jnp.abs(z)))
    lane = lax.broadcasted_iota(jnp.int32, lf.shape, 1)
    _store_layer_rows(lf_ref, jnp.where(lane < N_HEADS, lf, 0.0))
    o += 3 * w + LANES
    p = _dot(u, w_ref[:, o:o + 5 * w])
    cos = dcos_ref[...]
    sin = dsin_ref[...]
    dq_ref[...] = _bf((p[:, 0:w] * cos + p[:, w:2 * w] * sin) * (DIFF_SUB ** -0.5 * LOG2E))
    k = p[:, 2 * w:3 * w] * cos + p[:, 3 * w:4 * w] * sin
    v = p[:, 4 * w:5 * w]
    _store_layer_rows(dk_ref, k)
    _store_layer_rows(dv_ref, v)
    dkb_ref[...] = _bf(k)
    dvb_ref[...] = _bf(v)


def _in_projection(x, scale, shift, w_all, b_f, pos, tm, depth, layer, state):
    b, t, _ = x.shape
    w = BRANCH_W
    nw = w_all.shape[1]
    bfp = jnp.concatenate([b_f.astype(F32), jnp.zeros((LANES - N_HEADS,), F32)]).reshape(1, LANES)
    rcos, rsin = _rope_tables(pos, HEAD_DIM, HEAD_DIM, RET_THETA, N_HEADS)
    dcos, dsin = _rope_tables(pos, DIFF_SUB, ROT_DIM, ROPE_THETA, N_SUB)
    row = lambda width: pl.BlockSpec((None, tm, width), lambda i, j: (i, j, 0))
    table = pl.BlockSpec((tm, w), lambda i, j: (j, 0))
    plain = lambda width, dt: (jax.ShapeDtypeStruct((b, t, width), dt), row(width))
    slot = lambda width: _layer_slot(depth, layer, b, t, tm, width, F32, state is None)
    outs = [plain(w, BF16), plain(w, BF16), plain(w, BF16), plain(w, F32), plain(w, F32),
            plain(w, BF16), slot(w), slot(w), plain(w, BF16), plain(w, BF16), slot(LANES),
            plain(w, BF16), slot(w), slot(w), plain(w, BF16), plain(w, BF16)]
    prior = () if state is None else tuple(state)
    n_in = 9
    aliases = {n_in: 6, n_in + 1: 7, n_in + 2: 10, n_in + 3: 12, n_in + 4: 13} if prior else {}
    res = pl.pallas_call(
        _in_proj_kernel,
        out_shape=tuple(o[0] for o in outs),
        grid=(b, t // tm),
        in_specs=[
            row(D_MODEL),
            pl.BlockSpec((None, 1, D_MODEL), lambda i, j: (i, 0, 0)),
            pl.BlockSpec((None, 1, D_MODEL), lambda i, j: (i, 0, 0)),
            pl.BlockSpec((D_MODEL, nw), lambda i, j: (0, 0), pipeline_mode=pl.Buffered(1)),
            pl.BlockSpec((1, LANES), lambda i, j: (0, 0)),
            table, table, table, table,
        ] + [pl.BlockSpec(memory_space=pl.ANY)] * len(prior),
        out_specs=tuple(o[1] for o in outs),
        input_output_aliases=aliases,
        compiler_params=_params(2),
        name="mixer_in_projection",
    )(x, scale, shift, w_all, bfp, rcos, rsin, dcos, dsin, *prior)
    names = ('rq', 'rk', 'rv', 'rg', 'glu', 'fq', 'fk', 'fv', 'fkb', 'fvb', 'lf', 'dq', 'dk', 'dv', 'dkb', 'dvb')
    pr = dict(zip(names, res))
    pr['state'] = (pr['fk'], pr['fv'], pr['lf'], pr['dk'], pr['dv'])
    return pr


def _cumsum_kernel(lf_ref, col_ref, row_ref):
    n = KV_BLOCK
    nb = lf_ref.shape[0] // n
    r = lax.broadcasted_iota(jnp.int32, (n, n), 0)
    c = lax.broadcasted_iota(jnp.int32, (n, n), 1)
    tri = jnp.where(c <= r, 1.0, 0.0).astype(BF16)
    carry = jnp.zeros((1, LANES), F32)
    for jb in range(nb):
        x = lf_ref[jb * n:(jb + 1) * n, :]
        hi = _bf(x)
        r1 = x - hi.astype(F32)
        mid = _bf(r1)
        lo = _bf(r1 - mid.astype(F32))
        cs = _dot(tri, hi) + _dot(tri, mid) + _dot(tri, lo) + carry
        carry = cs[n - 1:n, :]
        scaled = cs * LOG2E
        col_ref[jb * n:(jb + 1) * n, :] = scaled
        row_ref[jb] = scaled.T[0:8, :]


def _logf_cumsum(lf):
    b, tk, _ = lf.shape
    nb = tk // KV_BLOCK
    return pl.pallas_call(
        _cumsum_kernel,
        out_shape=(jax.ShapeDtypeStruct((b, tk, LANES), F32),
                   jax.ShapeDtypeStruct((b, nb, 8, KV_BLOCK), F32)),
        grid=(b,),
        in_specs=[pl.BlockSpec((None, tk, LANES), lambda i: (i, 0, 0))],
        out_specs=(pl.BlockSpec((None, tk, LANES), lambda i: (i, 0, 0)),
                   pl.BlockSpec((None, nb, 8, KV_BLOCK), lambda i: (i, 0, 0, 0))),
        compiler_params=_params(1),
        name="logf_cumsum",
    )(lf)


def _attn_kernel(*refs, fox, off, tk_valid, tq, lam_init, split_kv):
    q_ref, k_ref, v_ref = refs[:3]
    refs = refs[3:]
    if split_kv:
        kn_ref, vn_ref = refs[:2]
        refs = refs[2:]
    if fox:
        fc_ref, fr_ref, o_ref, vt_scr, qm_scr, mb_scr, m_scr, r_scr, acc_scr, s_scr = refs
    else:
        lam_ref, g_ref, o_ref, vt_scr, qm_scr, mb_scr, m_scr, r_scr, acc_scr, s_scr = refs
    i = pl.program_id(1)
    w = BRANCH_W
    n_sub, tqp, _ = qm_scr.shape
    sub_w = w // n_sub
    subs_per_head = n_sub // N_HEADS
    nb = vt_scr.shape[0]
    hd = HEAD_DIM
    q0 = off + i * tq
    nfull = q0 // KV_BLOCK

    @pl.when(i == 0)
    def _():
        for jb in range(nb):
            if split_kv and jb == nb - 1:
                vt = vn_ref[...].T
            else:
                vt = _bf(v_ref[jb * KV_BLOCK:(jb + 1) * KV_BLOCK, :]).T
            for h in range(N_HEADS):
                vt_scr[jb, h, 0:hd, :] = vt[h * hd:(h + 1) * hd, :]
                vt_scr[jb, h, hd:, :] = jnp.ones((vt_scr.shape[2] - hd, KV_BLOCK), BF16)
        kpos = lax.broadcasted_iota(jnp.int32, (KV_BLOCK, tqp), 0)
        qpos = lax.broadcasted_iota(jnp.int32, (KV_BLOCK, tqp), 1)
        if fox:
            vis = kpos <= qpos
        else:
            vis = (kpos // CHUNK) <= (qpos // CHUNK)
        vis = vis & (kpos < tk_valid - nfull * KV_BLOCK)
        mb_scr[...] = jnp.where(vis, 0.0, NEG)

    q = q_ref[...]
    lane_sub = lax.broadcasted_iota(jnp.int32, (1, w), 1) // sub_w
    if tqp > tq:
        qm_scr[...] = jnp.zeros_like(qm_scr)
    for n in range(n_sub):
        qm_scr[n, 0:tq, :] = jnp.where(lane_sub == n, q, jnp.zeros_like(q))
    m_scr[...] = jnp.full(m_scr.shape, NEG, F32)
    r_scr[...] = jnp.full(r_scr.shape, NEG, F32)
    acc_scr[...] = jnp.zeros_like(acc_scr)

    def scores(j, n, masked):
        start = pl.multiple_of(j * KV_BLOCK, KV_BLOCK)
        h = n // subs_per_head
        kb = kn_ref[...] if (split_kv and masked) else _bf(k_ref[pl.ds(start, KV_BLOCK), :])
        s = _dot_nt(kb, qm_scr[n])
        if fox:
            s = s + (fr_ref[h:h + 1, 0:tqp] - fc_ref[pl.ds(start, KV_BLOCK), h:h + 1])
        if masked:
            s = s + mb_scr[...]
        s_scr[n] = s
        m_scr[n] = jnp.maximum(m_scr[n], jnp.max(s, axis=0, keepdims=True))

    def values(j, n):
        h = n // subs_per_head
        m = m_scr[n]
        alpha = jnp.exp2(r_scr[n] - m)
        r_scr[n] = m
        p = jnp.exp2(s_scr[n] - m)
        acc_scr[n] = alpha * acc_scr[n] + _dot(vt_scr[j, h], _bf(p))

    @pl.when(nfull == 0)
    def _():
        for n in range(n_sub):
            scores(0, n, True)

    @pl.when(nfull > 0)
    def _():
        for n in range(n_sub):
            scores(0, n, False)

    def body(j, carry):
        for n in range(n_sub):
            values(j - 1, n)
            scores(j, n, False)
        return carry

    lax.fori_loop(1, nfull, body, 0)

    @pl.when(nfull > 0)
    def _():
        for n in range(n_sub):
            values(nfull - 1, n)
            scores(nfull, n, True)

    for n in range(n_sub):
        values(nfull, n)

    def normalised(n):
        acc = acc_scr[n]
        return acc[0:hd] / acc[hd:hd + 1]

    if fox:
        out_t = jnp.concatenate([normalised(h) for h in range(N_HEADS)], axis=0)
        o_ref[...] = _bf(out_t.T[0:tq, :])
    else:
        lamv = lam_ref[...]
        lam = (jnp.exp(jnp.sum(lamv[0:1] * lamv[1:2], axis=-1, keepdims=True))
               - jnp.exp(jnp.sum(lamv[2:3] * lamv[3:4], axis=-1, keepdims=True)) + lam_init)
        parts = [normalised(2 * h) - lam * normalised(2 * h + 1) for h in range(N_HEADS)]
        dy = jnp.concatenate(parts, axis=0).T[0:tq, :]
        ms = _head_sum(dy * dy, HEAD_DIM) * (1.0 / HEAD_DIM)
        o_ref[...] = _bf(dy * lax.rsqrt(ms + EPS) * g_ref[...] * (1.0 - lam_init))


def _attention(q, kb, vb, extras, *, fox, tq, off, tk_valid, lam_init=0.0, new_kv=None):
    b, t, w = q.shape
    tk = kb.shape[1]
    split_kv = new_kv is not None
    nb = tk // KV_BLOCK + (1 if split_kv else 0)
    tqp = max(tq, LANES)
    n_sub = N_HEADS if fox else N_SUB
    assert off % KV_BLOCK == 0 and (tq == KV_BLOCK or t == tq)
    kernel = functools.partial(_attn_kernel, fox=fox, off=off, tk_valid=tk_valid, tq=tq, lam_init=lam_init,
                               split_kv=split_kv)
    in_specs = [
        pl.BlockSpec((None, tq, w), lambda i, j: (i, j, 0)),
        pl.BlockSpec((None, tk, w), lambda i, j: (i, 0, 0)),
        pl.BlockSpec((None, tk, w), lambda i, j: (i, 0, 0)),
    ]
    if split_kv:
        in_specs += [pl.BlockSpec((None, KV_BLOCK, w), lambda i, j: (i, 0, 0))] * 2
        extras = tuple(new_kv) + tuple(extras)
    if fox:
        in_specs += [
            pl.BlockSpec((None, nb * KV_BLOCK, LANES), lambda i, j: (i, 0, 0)),
            pl.BlockSpec((None, None, 8, KV_BLOCK), lambda i, j: (i, (off + j * tq) // KV_BLOCK, 0, 0)),
        ]
    else:
        in_specs += [
            pl.BlockSpec((4, DIFF_SUB), lambda i, j: (0, 0)),
            pl.BlockSpec((1, w), lambda i, j: (0, 0)),
        ]
    return pl.pallas_call(
        kernel,
        out_shape=jax.ShapeDtypeStruct((b, t, w), BF16),
        grid=(b, t // tq),
        in_specs=in_specs,
        out_specs=pl.BlockSpec((None, tq, w), lambda i, j: (i, j, 0)),
        scratch_shapes=[
            pltpu.VMEM((nb, N_HEADS, HEAD_DIM + BF16_ROWS, KV_BLOCK), BF16),
            pltpu.VMEM((n_sub, tqp, w), BF16),
            pltpu.VMEM((KV_BLOCK, tqp), F32),
            pltpu.VMEM((n_sub, 1, tqp), F32),
            pltpu.VMEM((n_sub, 1, tqp), F32),
            pltpu.VMEM((n_sub, HEAD_DIM + BF16_ROWS, tqp), F32),
            pltpu.VMEM((n_sub, KV_BLOCK, tqp), F32),
        ],
        compiler_params=_params(2),
        name="fox_attention" if fox else "diff_attention",
    )(q, kb, vb, *extras)


def _fox_attention(q, kb, vb, fcol, frow, tq, off, tk_valid, new_kv=None):
    return _attention(q, kb, vb, (fcol, frow), fox=True, tq=tq, off=off, tk_valid=tk_valid, new_kv=new_kv)


def _diff_attention(q, kb, vb, diff_lambda, subln_g, tq, off, tk_valid, lam_init, new_kv=None):
    g = jnp.tile(subln_g.astype(F32), N_HEADS).reshape(1, BRANCH_W)
    return _attention(q, kb, vb, (diff_lambda.astype(F32), g), fox=False, tq=tq, off=off,
                      tk_valid=tk_valid, lam_init=lam_init, new_kv=new_kv)


ROUTE_W1, ROUTE_W2, ROUTE_I1, ROUTE_I2 = 8, 9, 10, 11


def _top2_route(logits):
    lane = lax.broadcasted_iota(jnp.int32, logits.shape, 1).astype(F32)
    lg = jnp.where(lane < N_EXPERTS, logits, -jnp.inf)
    v1 = jnp.max(lg, axis=-1, keepdims=True)
    i1 = jnp.min(jnp.where(lg == v1, lane, float(LANES)), axis=-1, keepdims=True)
    lg2 = jnp.where(lane == i1, -jnp.inf, lg)
    v2 = jnp.max(lg2, axis=-1, keepdims=True)
    i2 = jnp.min(jnp.where(lg2 == v2, lane, float(LANES)), axis=-1, keepdims=True)
    e2 = jnp.exp(v2 - v1)
    w1 = 1.0 / (1.0 + e2)
    w2 = e2 / (1.0 + e2)
    rec = jnp.where(lane == i1, w1, 0.0) + jnp.where(lane == i2, w2, 0.0)
    for slot, val in ((ROUTE_W1, w1), (ROUTE_W2, w2), (ROUTE_I1, i1), (ROUTE_I2, i2)):
        rec = rec + jnp.where(lane == float(slot), val, 0.0)
    return rec


def _merge_kernel(*refs, with_router):
    if with_router:
        (x_ref, sc1_ref, sh1_ref, g1_ref, sc2_ref, sh2_ref, hr_ref, hf_ref, hc_ref, hd_ref,
         wg_ref, wb_ref, wo_ref, lg_ref, lb_ref, wr_ref, br_ref, x1_ref, u2_ref, rcol_ref, rrow_ref) = refs
    else:
        (x_ref, sc1_ref, sh1_ref, g1_ref, sc2_ref, sh2_ref, hr_ref, hf_ref, hc_ref, hd_ref,
         wg_ref, wb_ref, wo_ref, lg_ref, lb_ref, x1_ref, u2_ref) = refs
    x = x_ref[...]
    u = _bf(x * (1.0 + sc1_ref[...]) + sh1_ref[...])
    merged = None
    for n, h_ref in enumerate((hr_ref, hf_ref, hc_ref, hd_ref)):
        gate = _dot(u, wg_ref[:, n * D_MODEL:(n + 1) * D_MODEL])
        term = _sigmoid(gate) * _dot(h_ref[...], wb_ref[n])
        merged = term if merged is None else merged + term
    mix = _dot(_bf(merged), wo_ref[...])
    x1 = _layer_norm_rows(ALPHA * x + g1_ref[...] * mix, lg_ref[...], lb_ref[...])
    x1_ref[...] = x1
    u2 = x1 * (1.0 + sc2_ref[...]) + sh2_ref[...]
    u2_ref[...] = u2.astype(u2_ref.dtype)
    if with_router:
        rec = _top2_route(_dot(_bf(u2), wr_ref[...]) + br_ref[...])
        rcol_ref[...] = rec
        rrow_ref[...] = rec.T[8:16, :]


def _merge(x2d, mods, branches, w_gate, w_branch, w_out, ln_g, ln_b, router, tm, tiles_per_group):
    n = x2d.shape[0]
    r = mods[0].shape[1]
    w = BRANCH_W
    with_router = router is not None
    row = lambda width: pl.BlockSpec((tm, width), lambda i: (i, 0))
    mod_spec = pl.BlockSpec((None, r, D_MODEL), lambda i: (i // tiles_per_group, 0, 0))
    full = lambda shape: pl.BlockSpec(shape, lambda i: (0,) * len(shape))
    resident = lambda shape: pl.BlockSpec(shape, lambda i: (0,) * len(shape), pipeline_mode=pl.Buffered(1))
    in_specs = ([row(D_MODEL)] + [mod_spec] * 5 + [row(w)] * 4
                + [resident((D_MODEL, 4 * D_MODEL)), resident((4, w, D_MODEL)), resident((D_MODEL, D_MODEL)),
                   full((1, D_MODEL)), full((1, D_MODEL))])
    args = [x2d, *mods, *branches, w_gate, w_branch, w_out, ln_g.reshape(1, D_MODEL), ln_b.reshape(1, D_MODEL)]
    out_shape = [jax.ShapeDtypeStruct((n, D_MODEL), F32),
                 jax.ShapeDtypeStruct((n, D_MODEL), F32 if with_router else BF16)]
    out_specs = [row(D_MODEL), row(D_MODEL)]
    if with_router:
        in_specs += [full((D_MODEL, LANES)), full((1, LANES))]
        args += list(router)
        out_shape += [jax.ShapeDtypeStruct((n, LANES), F32), jax.ShapeDtypeStruct((n // tm, 8, tm), F32)]
        out_specs += [row(LANES), pl.BlockSpec((None, 8, tm), lambda i: (i, 0, 0))]
    return pl.pallas_call(
        functools.partial(_merge_kernel, with_router=with_router),
        out_shape=tuple(out_shape),
        grid=(n // tm,),
        in_specs=in_specs,
        out_specs=tuple(out_specs),
        compiler_params=_params(1),
        name="merge_outproj_ln",
    )(*args)


def _swiglu_chunks(u, w_up_ref, w_down_ref, hidden, chunk):
    acc = None
    for c0 in range(0, hidden, chunk):
        c1 = min(c0 + chunk, hidden)
        a = _dot(u, w_up_ref[:, c0:c1])
        g = _dot(u, w_up_ref[:, hidden + c0:hidden + c1])
        y = _dot(_bf(a * _sigmoid(a) * g), w_down_ref[c0:c1, :])
        acc = y if acc is None else acc + y
    return acc


def _ffn_kernel(u_ref, x1_ref, g2_ref, wu_ref, wd_ref, lg_ref, lb_ref, o_ref, *, chunk):
    f = _swiglu_chunks(u_ref[...], wu_ref, wd_ref, D_FF, chunk)
    z = ALPHA * x1_ref[...] + g2_ref[...] * f
    o_ref[...] = _layer_norm_rows(z, lg_ref[...], lb_ref[...])


def _ffn_dense(u2, x1, gate2, w_up, w_down, ln_g, ln_b, tm, tiles_per_group, chunk):
    n = u2.shape[0]
    r = gate2.shape[1]
    resident = lambda shape: pl.BlockSpec(shape, lambda i: (0,) * len(shape), pipeline_mode=pl.Buffered(1))
    return pl.pallas_call(
        functools.partial(_ffn_kernel, chunk=chunk),
        out_shape=jax.ShapeDtypeStruct((n, D_MODEL), F32),
        grid=(n // tm,),
        in_specs=[
            pl.BlockSpec((tm, D_MODEL), lambda i: (i, 0)),
            pl.BlockSpec((tm, D_MODEL), lambda i: (i, 0)),
            pl.BlockSpec((None, r, D_MODEL), lambda i: (i // tiles_per_group, 0, 0)),
            resident((D_MODEL, 2 * D_FF)),
            resident((D_FF, D_MODEL)),
            pl.BlockSpec((1, D_MODEL), lambda i: (0, 0)),
            pl.BlockSpec((1, D_MODEL), lambda i: (0, 0)),
        ],
        out_specs=pl.BlockSpec((tm, D_MODEL), lambda i: (i, 0)),
        compiler_params=_params(1),
        name="ffn_dense",
    )(u2, x1, gate2, w_up, w_down, ln_g.reshape(1, D_MODEL), ln_b.reshape(1, D_MODEL))


def _moe_kernel(u_ref, x1_ref, g2_ref, cmb_ref, wa_ref, wg_ref, wd_ref, lg_ref, lb_ref, o_ref, acc_scr):
    e = pl.program_id(1)
    j = pl.program_id(2)

    @pl.when((e == 0) & (j == 0))
    def _():
        acc_scr[...] = jnp.zeros_like(acc_scr)

    u = u_ref[...]
    a = _dot(u, wa_ref[...])
    g = _dot(u, wg_ref[...])
    y = _dot(_bf(a * _sigmoid(a) * g), wd_ref[...])
    cmb = cmb_ref[...]
    lane = lax.broadcasted_iota(jnp.int32, cmb.shape, 1)
    ce = jnp.sum(jnp.where(lane == e, cmb, 0.0), axis=-1, keepdims=True)
    acc_scr[...] += ce * y

    @pl.when((e == pl.num_programs(1) - 1) & (j == pl.num_programs(2) - 1))
    def _():
        z = ALPHA * x1_ref[...] + g2_ref[...] * acc_scr[...]
        o_ref[...] = _layer_norm_rows(z, lg_ref[...], lb_ref[...])


def _ffn_experts(u2, x1, gate2, combine, w_in, w_out, ln_g, ln_b, tm, tiles_per_group, th):
    n = u2.shape[0]
    r = gate2.shape[1]
    nh = D_EXPERT // th
    return pl.pallas_call(
        _moe_kernel,
        out_shape=jax.ShapeDtypeStruct((n, D_MODEL), F32),
        grid=(n // tm, N_EXPERTS, nh),
        in_specs=[
            pl.BlockSpec((tm, D_MODEL), lambda i, e, j: (i, 0)),
            pl.BlockSpec((tm, D_MODEL), lambda i, e, j: (i, 0)),
            pl.BlockSpec((None, r, D_MODEL), lambda i, e, j: (i // tiles_per_group, 0, 0)),
            pl.BlockSpec((tm, LANES), lambda i, e, j: (i, 0)),
            pl.BlockSpec((None, D_MODEL, th), lambda i, e, j: (e, 0, j)),
            pl.BlockSpec((None, D_MODEL, th), lambda i, e, j: (e, 0, nh + j)),
            pl.BlockSpec((None, th, D_MODEL), lambda i, e, j: (e, j, 0)),
            pl.BlockSpec((1, D_MODEL), lambda i, e, j: (0, 0)),
            pl.BlockSpec((1, D_MODEL), lambda i, e, j: (0, 0)),
        ],
        out_specs=pl.BlockSpec((tm, D_MODEL), lambda i, e, j: (i, 0)),
        scratch_shapes=[pltpu.VMEM((tm, D_MODEL), F32)],
        compiler_params=_params(3),
        name="ffn_experts",
    )(u2, x1, gate2, combine, w_in, w_in, w_out, ln_g.reshape(1, D_MODEL), ln_b.reshape(1, D_MODEL))


EXPERT_TILE = 1024


def _rank_kernel(rr_ref, rank_ref, cnt_ref, carry_scr):
    b = pl.program_id(0)

    @pl.when(b == 0)
    def _():
        carry_scr[...] = jnp.zeros_like(carry_scr)

    rr = rr_ref[...]
    tm = rr.shape[1]
    i1 = rr[2:3]
    i2 = rr[3:4]
    e = lax.broadcasted_iota(jnp.int32, (N_EXPERTS, tm), 0).astype(F32)
    oh1 = jnp.where(e == i1, 1.0, 0.0)
    oh2 = jnp.where(e == i2, 1.0, 0.0)
    sel = oh1 + oh2
    r = lax.broadcasted_iota(jnp.int32, (tm, tm), 0)
    c = lax.broadcasted_iota(jnp.int32, (tm, tm), 1)
    triu = jnp.where(r <= c, 1.0, 0.0).astype(BF16)
    csum = _dot(_bf(sel), triu)
    carry = carry_scr[:, 0:1]
    rank = csum - sel + carry
    r1 = jnp.sum(oh1 * rank, axis=0, keepdims=True)
    r2 = jnp.sum(oh2 * rank, axis=0, keepdims=True)
    rec = jnp.concatenate([i1, i2, r1, r2, jnp.zeros((4, tm), F32)], axis=0)
    rank_ref[...] = rec.astype(jnp.int32)
    total = carry + csum[:, tm - 1:tm]
    carry_scr[...] = jnp.broadcast_to(total, carry_scr.shape)
    cnt_ref[...] = jnp.broadcast_to(total, cnt_ref.shape).astype(jnp.int32)


def _route_ranks(route_row):
    nblk, _, tm = route_row.shape
    return pl.pallas_call(
        _rank_kernel,
        out_shape=(jax.ShapeDtypeStruct((nblk, 8, tm), jnp.int32),
                   jax.ShapeDtypeStruct((N_EXPERTS, LANES), jnp.int32)),
        grid=(nblk,),
        in_specs=[pl.BlockSpec((None, 8, tm), lambda b: (b, 0, 0))],
        out_specs=(pl.BlockSpec((None, 8, tm), lambda b: (b, 0, 0)),
                   pl.BlockSpec((N_EXPERTS, LANES), lambda b: (0, 0))),
        scratch_shapes=[pltpu.VMEM((N_EXPERTS, LANES), F32)],
        compiler_params=_params(1),
        name="route_ranks",
    )(route_row)


def _row_copy(src_ref, src_row, dst_ref, dst_row, sem):
    return pltpu.make_async_copy(src_ref.at[pl.ds(src_row, 1)], dst_ref.at[pl.ds(dst_row, 1)], sem)


def _dispatch_kernel(last_ref, slot_ref, u_ref, xs_ref, zero_scr, sem):
    tm = u_ref.shape[0]
    gt = zero_scr.shape[0]

    @pl.when(pl.program_id(0) == 0)
    def _():
        zero_scr[...] = jnp.zeros_like(zero_scr)
        for e in range(N_EXPERTS):
            @pl.when(last_ref[e] >= 0)
            def _():
                first = pl.multiple_of(last_ref[e], gt)
                fill = pltpu.make_async_copy(zero_scr, xs_ref.at[pl.ds(first, gt)], sem)
                fill.start()
                fill.wait()

    def start(r, carry):
        for k in range(2):
            _row_copy(u_ref, r, xs_ref, slot_ref[k, r], sem).start()
        return carry

    lax.fori_loop(0, tm, start, 0, unroll=8)

    def wait(r, carry):
        for k in range(2):
            _row_copy(u_ref, 0, xs_ref, 0, sem).wait()
        return carry

    lax.fori_loop(0, tm, wait, 0, unroll=8)


def _dispatch(u2, slots, last_tile, n_slots, group_tile):
    n, d = u2.shape
    nblk, _, tm = slots.shape
    grid_spec = pltpu.PrefetchScalarGridSpec(
        num_scalar_prefetch=1,
        grid=(nblk,),
        in_specs=[
            pl.BlockSpec((None, 2, tm), lambda b, last: (b, 0, 0), memory_space=pltpu.SMEM),
            pl.BlockSpec((tm, d), lambda b, last: (b, 0)),
        ],
        out_specs=pl.BlockSpec(memory_space=pl.ANY),
        scratch_shapes=[pltpu.VMEM((group_tile, d), F32), pltpu.SemaphoreType.DMA],
    )
    return pl.pallas_call(
        _dispatch_kernel,
        out_shape=jax.ShapeDtypeStruct((n_slots, d), F32),
        grid_spec=grid_spec,
        compiler_params=_params(1),
        name="expert_dispatch",
    )(last_tile, slots, u2)


def _grouped_kernel(toff_ref, nt_ref, xs_ref, wi_ref, wo_ref, ys_ref, *, chunk):
    @pl.when(pl.program_id(0) < nt_ref[0])
    def _():
        ys_ref[...] = _swiglu_chunks(_bf(xs_ref[...]), wi_ref, wo_ref, D_EXPERT, chunk)


def _grouped_experts(xs, tile_off, n_tiles, w_in, w_out, tm, chunk):
    n_slots, d = xs.shape

    def expert(i, toff):
        e = 0
        for k in range(1, N_EXPERTS):
            e = e + (i >= toff[k]).astype(jnp.int32)
        return e

    def tile(i, nt):
        return jnp.minimum(i, nt[0] - 1)

    grid_spec = pltpu.PrefetchScalarGridSpec(
        num_scalar_prefetch=2,
        grid=(n_slots // tm,),
        in_specs=[
            pl.BlockSpec((tm, d), lambda i, toff, nt: (tile(i, nt), 0)),
            pl.BlockSpec((None, d, 2 * D_EXPERT), lambda i, toff, nt: (expert(tile(i, nt), toff), 0, 0),
                         pipeline_mode=pl.Buffered(1)),
            pl.BlockSpec((None, D_EXPERT, d), lambda i, toff, nt: (expert(tile(i, nt), toff), 0, 0),
                         pipeline_mode=pl.Buffered(1)),
        ],
        out_specs=pl.BlockSpec((tm, d), lambda i, toff, nt: (tile(i, nt), 0)),
    )
    return pl.pallas_call(
        functools.partial(_grouped_kernel, chunk=chunk),
        out_shape=jax.ShapeDtypeStruct((n_slots, d), F32),
        grid_spec=grid_spec,
        compiler_params=_params(1),
        name="grouped_experts",
    )(tile_off, n_tiles, xs, w_in, w_out)


def _combine_kernel(slot_ref, ys_ref, x1_ref, g2_ref, rcol_ref, lg_ref, lb_ref, o_ref, y1_scr, y2_scr, sem):
    tm = x1_ref.shape[0]
    bufs = (y1_scr, y2_scr)

    def start(r, carry):
        for k in range(2):
            _row_copy(ys_ref, slot_ref[k, r], bufs[k], r, sem).start()
        return carry

    lax.fori_loop(0, tm, start, 0, unroll=8)

    def wait(r, carry):
        for k in range(2):
            _row_copy(ys_ref, 0, bufs[k], 0, sem).wait()
        return carry

    lax.fori_loop(0, tm, wait, 0, unroll=8)
    rec = rcol_ref[...]
    f = rec[:, ROUTE_W1:ROUTE_W1 + 1] * y1_scr[...] + rec[:, ROUTE_W2:ROUTE_W2 + 1] * y2_scr[...]
    z = ALPHA * x1_ref[...] + g2_ref[...] * f
    o_ref[...] = _layer_norm_rows(z, lg_ref[...], lb_ref[...])


def _combine(ys, slots, x1, gate2, route_col, ln_g, ln_b, tiles_per_group):
    n, d = x1.shape
    nblk, _, tm = slots.shape
    r = gate2.shape[1]
    return pl.pallas_call(
        _combine_kernel,
        out_shape=jax.ShapeDtypeStruct((n, d), F32),
        grid=(nblk,),
        in_specs=[
            pl.BlockSpec((None, 2, tm), lambda b: (b, 0, 0), memory_space=pltpu.SMEM),
            pl.BlockSpec(memory_space=pl.ANY),
            pl.BlockSpec((tm, d), lambda b: (b, 0)),
            pl.BlockSpec((None, r, d), lambda b: (b // tiles_per_group, 0, 0)),
            pl.BlockSpec((tm, LANES), lambda b: (b, 0)),
            pl.BlockSpec((1, d), lambda b: (0, 0)),
            pl.BlockSpec((1, d), lambda b: (0, 0)),
        ],
        out_specs=pl.BlockSpec((tm, d), lambda b: (b, 0)),
        scratch_shapes=[pltpu.VMEM((tm, d), F32), pltpu.VMEM((tm, d), F32), pltpu.SemaphoreType.DMA],
        compiler_params=_params(1),
        name="expert_combine_ln",
    )(slots, ys, x1, gate2, route_col, ln_g.reshape(1, d), ln_b.reshape(1, d))


def _ffn_routed(u2, x1, gate2, route_col, route_row, w_in, w_out, ln_g, ln_b, tiles_per_group):
    n = u2.shape[0]
    tm = min(EXPERT_TILE, n)
    assert n % tm == 0
    ranks, counts = _route_ranks(route_row)
    cnt = counts[:, 0]
    padded = ((cnt + tm - 1) // tm) * tm
    off = (jnp.cumsum(padded) - padded).astype(jnp.int32)
    last_tile = jnp.where(padded > 0, off + padded - tm, -1).astype(jnp.int32)
    n_slots = 2 * n + N_EXPERTS * tm
    slots = off[ranks[:, 0:2, :]] + ranks[:, 2:4, :]
    xs = _dispatch(u2, slots, last_tile, n_slots, tm)
    ys = _grouped_experts(xs, (off // tm).astype(jnp.int32), (jnp.sum(padded) // tm).astype(jnp.int32).reshape(1),
                          w_in, w_out, tm, 512)
    return _combine(ys, slots, x1, gate2, route_col, ln_g, ln_b, tiles_per_group)


def _split_w_in(w_in_l):
    w = BRANCH_W
    col = lambda off, n: w_in_l[:, off:off + n]
    o = 0
    ret_q, ret_k, ret_v, ret_g = (col(o + i * w, w) for i in range(4)); o += 4 * w
    fox_q, fox_k, fox_v = (col(o + i * w, w) for i in range(3)); o += 3 * w
    fox_f = col(o, N_HEADS); o += N_HEADS
    conv = col(o, 2 * w); o += 2 * w
    diff_q, diff_k, diff_v = (col(o + i * w, w) for i in range(3)); o += 3 * w
    gate = col(o, 4 * D_MODEL)
    rot_r = lambda m: _rotated_columns(m, HEAD_DIM, HEAD_DIM)
    rot_d = lambda m: _rotated_columns(m, DIFF_SUB, ROT_DIM)
    w_ret = jnp.concatenate([ret_q, rot_r(ret_q), ret_k, rot_r(ret_k), ret_v, ret_g], axis=1)
    w_fox = jnp.concatenate([fox_q, fox_k, fox_v, fox_f, jnp.zeros((D_MODEL, LANES - N_HEADS), F32)], axis=1)
    w_diff = jnp.concatenate([diff_q, rot_d(diff_q), diff_k, rot_d(diff_k), diff_v], axis=1)
    return _bf(jnp.concatenate([w_ret, conv, w_fox, w_diff], axis=1)), _bf(gate)


def _block_diag_state(s):
    b = s.shape[0]
    eye = jnp.eye(N_HEADS, dtype=s.dtype)
    return jnp.einsum('bhde,hg->bhdge', s, eye).reshape(b, BRANCH_W, BRANCH_W)


def _diag_blocks(s_bd):
    b = s_bd.shape[0]
    s4 = s_bd.reshape(b, N_HEADS, HEAD_DIM, N_HEADS, HEAD_DIM)
    return jnp.stack([s4[:, h, :, h, :] for h in range(N_HEADS)], axis=1)


def _pad_rows(a, rows):
    return jnp.concatenate([a, jnp.zeros((a.shape[0], rows - a.shape[1]) + a.shape[2:], a.dtype)], axis=1)


def _trunk_layer(x, mod, pos, past, l, lw, sample, depth, kv_state):
    b, t, _ = x.shape
    d = D_MODEL
    w = BRANCH_W
    shift1, scale1, gate1 = (mod[0][:, None, i * d:(i + 1) * d] for i in range(3))
    shift2, scale2, gate2 = (mod[1][:, None, i * d:(i + 1) * d] for i in range(3))
    lam_init = 0.8 - 0.6 * math.exp(-0.3 * l)
    chunk = min(t, KV_BLOCK)
    pr = _in_projection(x, scale1, shift1, lw['w_mix'], lw['b_fox_f'], pos, min(t, 512), depth, l, kv_state)
    kv_state = pr['state']

    s0 = jnp.zeros((b, w, w), F32) if past is None else _block_diag_state(past[5])
    h_ret, s_bd = _retention(pr['rq'], pr['rk'], pr['rv'], pr['rg'], s0, chunk)
    ret_state = _diag_blocks(s_bd)

    if past is None:
        hist = jnp.zeros((b, HIST_ROWS, w), F32)
    else:
        hist = jnp.concatenate([jnp.zeros((b, HIST_ROWS - (CONV_W - 1), w), F32), past[6]], axis=1)
    h_conv, tail = _conv_mixer(pr['glu'], hist, lw['w_conv'], lw['b_conv'], lw['conv_ln_g'], lw['conv_ln_b'], chunk)
    conv_buf = tail[:, HIST_ROWS - (CONV_W - 1):, :]

    fq, fkb, fvb, lf = pr['fq'], pr['fkb'], pr['fvb'], pr['lf'][l]
    if past is None:
        off, tk_valid = 0, t
        k_all, v_all, lf_all, fox_new = fkb, fvb, lf, None
    else:
        p_len = past[0].shape[1]
        off, tk_valid = p_len, p_len + t
        tk_pad = p_len + KV_BLOCK
        k_all, v_all = past[0].reshape(b, p_len, w), past[1].reshape(b, p_len, w)
        fox_new = (_pad_rows(fkb, KV_BLOCK), _pad_rows(fvb, KV_BLOCK))
        past_lf = jnp.concatenate([past[2].astype(F32), jnp.zeros((b, p_len, LANES - N_HEADS), F32)], axis=2)
        lf_all = _pad_rows(jnp.concatenate([past_lf, lf], axis=1), tk_pad)
    fcol, frow = _logf_cumsum(lf_all)
    tq = min(t, KV_BLOCK)
    h_fox = _fox_attention(fq, k_all, v_all, fcol, frow, tq, off, tk_valid, fox_new)

    dq, dkb, dvb = pr['dq'], pr['dkb'], pr['dvb']
    if past is None:
        dk_all, dv_all, diff_new = dkb, dvb, None
    else:
        dk_all, dv_all = past[3].reshape(b, p_len, w), past[4].reshape(b, p_len, w)
        diff_new = (_pad_rows(dkb, KV_BLOCK), _pad_rows(dvb, KV_BLOCK))
    h_diff = _diff_attention(dq, dk_all, dv_all, lw['diff_lambda'], lw['diff_subln_g'], tq, off, tk_valid, lam_init,
                             diff_new)

    n = b * t
    if sample:
        tm_merge = tm_ffn = min(n, 256)
        tpg_merge = tpg_ffn = 1
        rows_mod = lambda m, tm: jnp.repeat(m, t, axis=1).reshape(n // tm, tm, d)
    else:
        tm_merge, tm_ffn = min(t, 512), min(t, 1024)
        tpg_merge, tpg_ffn = t // tm_merge, t // tm_ffn
        rows_mod = lambda m, tm: m
    mods = tuple(rows_mod(m, tm_merge) for m in (scale1, shift1, gate1, scale2, shift2))
    branches = tuple(h.reshape(n, w) for h in (h_ret, h_fox, h_conv, h_diff))
    merged = _merge(x.reshape(n, d), mods, branches, lw['w_gate'], lw['w_branch'], lw['w_out'],
                    lw['ln_g'][0], lw['ln_b'][0], lw.get('router'), tm_merge, tpg_merge)
    g2 = rows_mod(gate2, tm_ffn)
    if 'router' in lw and not sample:
        x1, u2, route_col, route_row = merged
        x2 = _ffn_routed(u2, x1, rows_mod(gate2, tm_merge), route_col, route_row, lw['w_exp_in'], lw['w_exp_out'],
                         lw['ln_g'][1], lw['ln_b'][1], tpg_merge)
    elif 'router' in lw:
        x1, u2, route_col, _ = merged
        x2 = _ffn_experts(_bf(u2), x1, g2, route_col, lw['w_exp_in'], lw['w_exp_out'], lw['ln_g'][1], lw['ln_b'][1],
                          tm_ffn, tpg_ffn, 512)
    else:
        x1, u2 = merged
        x2 = _ffn_dense(u2, x1, g2, lw['w_ffn_in'], lw['w_ffn_out'], lw['ln_g'][1], lw['ln_b'][1],
                        tm_ffn, tpg_ffn, 512)
    return x2.reshape(b, t, d), kv_state, (ret_state, conv_buf)


def _state_outputs(kv_state, small):
    fk, fv, lf, dk, dv = kv_state
    depth, b, t, w = fk.shape
    heads = lambda a, nh: a.reshape(depth, b, t, nh, w // nh)
    ret_state, conv_buf = (jnp.stack(a) for a in zip(*small))
    return (heads(fk, N_HEADS), heads(fv, N_HEADS), lf[..., :N_HEADS], heads(dk, N_SUB), heads(dv, N_HEADS),
            ret_state, conv_buf)


def kernel(x_prompt, x_sample, c_prompt, c_sample, cache_fox_k, cache_fox_v, cache_fox_logf, cache_diff_k, cache_diff_v, state_ret, state_conv, w_in, b_fox_f, w_conv, b_conv, conv_ln_g, conv_ln_b, diff_lambda, diff_subln_g, w_branch, w_out, w_ada, b_ada, ln_g, ln_b, w_ffn_in, w_ffn_out, w_router, b_router, w_exp_in, w_exp_out):
    depth = w_in.shape[0]
    bp = x_prompt.shape[0]
    past_len = cache_fox_k.shape[2]
    pos_p = jnp.arange(x_prompt.shape[1], dtype=jnp.int32)
    pos_s = past_len + jnp.arange(x_sample.shape[1], dtype=jnp.int32)
    mod = _modulation(jnp.concatenate([c_prompt, c_sample], axis=0), w_ada, b_ada)
    yp, ys = x_prompt, x_sample
    kv_p = kv_s = None
    small_p, small_s = [], []
    for l in range(depth):
        w_mix, w_gate = _split_w_in(w_in[l])
        lw = dict(w_mix=w_mix, w_gate=w_gate,
                  b_fox_f=b_fox_f[l], w_conv=w_conv[l], b_conv=b_conv[l], conv_ln_g=conv_ln_g[l],
                  conv_ln_b=conv_ln_b[l], diff_lambda=diff_lambda[l], diff_subln_g=diff_subln_g[l],
                  w_branch=_bf(w_branch[l]), w_out=_bf(w_out[l]), ln_g=ln_g[l], ln_b=ln_b[l])
        if l % 2 == 0:
            lw['w_ffn_in'] = _bf(w_ffn_in[l // 2])
            lw['w_ffn_out'] = _bf(w_ffn_out[l // 2])
        else:
            wr = jnp.concatenate([w_router[l // 2], jnp.zeros((D_MODEL, LANES - N_EXPERTS), F32)], axis=1)
            br = jnp.concatenate([b_router[l // 2].astype(F32), jnp.zeros((LANES - N_EXPERTS,), F32)]).reshape(1, LANES)
            lw['router'] = (_bf(wr), br)
            lw['w_exp_in'] = _bf(w_exp_in[l // 2])
            lw['w_exp_out'] = _bf(w_exp_out[l // 2])
        past_l = (cache_fox_k[l], cache_fox_v[l], cache_fox_logf[l], cache_diff_k[l], cache_diff_v[l],
                  state_ret[l], state_conv[l])
        yp, kv_p, st_p = _trunk_layer(yp, mod[l][:, :bp], pos_p, None, l, lw, False, depth, kv_p)
        ys, kv_s, st_s = _trunk_layer(ys, mod[l][:, bp:], pos_s, past_l, l, lw, True, depth, kv_s)
        small_p.append(st_p)
        small_s.append(st_s)
    return (yp, ys) + _state_outputs(kv_p, small_p) + _state_outputs(kv_s, small_s)
```

```python
import functools
import math

import jax
import jax.numpy as jnp
from jax import lax
from jax.experimental import pallas as pl
from jax.experimental.pallas import tpu as pltpu

D_MODEL = 1024
BRANCH_W = 256
HEAD_DIM = 64
N_HEADS = 4
DIFF_SUB = 32
N_SUB = 8
ROT_DIM = DIFF_SUB // 4
RET_THETA = 10000.0
ROPE_THETA = 500000.0
CHUNK = 64
CONV_W = 31
D_FF = 2816
N_EXPERTS = 8
D_EXPERT = 3584
DEPTH = 2
ALPHA = (2.0 * DEPTH) ** 0.25
EPS = 1e-5
NEG = -1e30
LOG2E = math.log2(math.e)

LANES = 128
BF16_ROWS = 16
KV_BLOCK = 256
HIST_ROWS = 32
VMEM_LIMIT = 56 * 1024 * 1024

F32 = jnp.float32
BF16 = jnp.bfloat16


def _bf(x):
    return x.astype(BF16)


def _dot(a, b):
    return jnp.dot(a, b, preferred_element_type=F32)


def _dot_nt(a, b):
    return lax.dot_general(a, b, (((1,), (1,)), ((), ())), preferred_element_type=F32)


def _dot_tn(a, b):
    return lax.dot_general(a, b, (((0,), (0,)), ((), ())), preferred_element_type=F32)


def _sigmoid(x):
    return 1.0 / (1.0 + jnp.exp(-x))


def _params(n_axes):
    return pltpu.CompilerParams(dimension_semantics=("arbitrary",) * n_axes,
                                vmem_limit_bytes=VMEM_LIMIT)


def _head_sum(y, width):
    n = y.shape[-1]
    r = lax.broadcasted_iota(jnp.int32, (n, n), 0) // width
    c = lax.broadcasted_iota(jnp.int32, (n, n), 1) // width
    bd = jnp.where(r == c, 1.0, 0.0).astype(BF16)
    hi = _bf(y)
    lo = _bf(y - hi.astype(F32))
    return _dot(hi, bd) + _dot(lo, bd)


def _layer_norm_rows(z, g, b):
    mu = jnp.mean(z, axis=-1, keepdims=True)
    d = z - mu
    var = jnp.mean(d * d, axis=-1, keepdims=True)
    return d * lax.rsqrt(var + EPS) * g + b


def _mod_kernel(c_ref, w_ref, b_ref, o_ref):
    c = c_ref[...]
    sc = _bf(c * _sigmoid(c))
    o_ref[...] = _dot(sc, _bf(w_ref[...])) + b_ref[...]


def _modulation(c_all, w_ada, b_ada):
    rows = c_all.shape[0]
    depth = w_ada.shape[0]
    d3 = w_ada.shape[-1]
    nj = d3 // D_MODEL
    return pl.pallas_call(
        _mod_kernel,
        out_shape=jax.ShapeDtypeStruct((depth, 2, rows, d3), F32),
        grid=(depth * 2, nj),
        in_specs=[
            pl.BlockSpec((rows, D_MODEL), lambda i, j: (0, 0)),
            pl.BlockSpec((None, None, D_MODEL, D_MODEL), lambda i, j: (i // 2, i % 2, 0, j)),
            pl.BlockSpec((None, None, 1, D_MODEL), lambda i, j: (i // 2, i % 2, 0, j)),
        ],
        out_specs=pl.BlockSpec((None, None, rows, D_MODEL), lambda i, j: (i // 2, i % 2, 0, j)),
        compiler_params=_params(2),
        name="adaln_modulation",
    )(c_all, w_ada, b_ada.reshape(depth, 2, 1, d3))


def _ret_kernel(q_ref, k_ref, v_ref, g_ref, dmask_ref, qdec_ref, kdec_ref, cdec_ref, s0_ref,
                h_ref, sout_ref, s_scr):
    c = pl.program_id(1)

    @pl.when(c == 0)
    def _():
        s_scr[...] = s0_ref[...]

    w = BRANCH_W
    q = q_ref[...]
    kb = k_ref[...]
    vb = v_ref[...]
    g = g_ref[...]
    rows = q.shape[0]
    lane_head = lax.broadcasted_iota(jnp.int32, (1, w), 1) // HEAD_DIM
    y = jnp.zeros((rows, w), F32)
    for h in range(N_HEADS):
        mh = lane_head == h
        qh = jnp.where(mh, q, jnp.zeros_like(q))
        a = _dot_nt(qh, kb) * dmask_ref[h]
        y = y + jnp.where(mh, _dot(_bf(a), vb), 0.0)
    s_prev = s_scr[...]
    y = y + _dot(_bf(q.astype(F32) * qdec_ref[...]), _bf(s_prev))
    kv = _dot_tn(_bf(kb.astype(F32) * kdec_ref[...]), vb)
    r = lax.broadcasted_iota(jnp.int32, (w, w), 0) // HEAD_DIM
    cc = lax.broadcasted_iota(jnp.int32, (w, w), 1) // HEAD_DIM
    s_new = cdec_ref[...] * s_prev + jnp.where(r == cc, kv, 0.0)
    s_scr[...] = s_new
    sout_ref[...] = s_new
    mu = _head_sum(y, HEAD_DIM) * (1.0 / HEAD_DIM)
    d = y - mu
    var = _head_sum(d * d, HEAD_DIM) * (1.0 / HEAD_DIM)
    hn = d * lax.rsqrt(var + EPS)
    h_ref[...] = _bf(hn * (g * _sigmoid(g)))


def _retention_tables(chunk):
    log_g = jnp.log1p(-jnp.exp2(-5.0 - jnp.arange(N_HEADS, dtype=F32)))
    idx = jnp.arange(chunk, dtype=F32)
    dist = jnp.abs(idx[:, None] - idx[None, :])
    sub = jnp.arange(chunk) // CHUNK
    vis = sub[None, :] <= sub[:, None]
    dmask = jnp.where(vis[None], jnp.exp(log_g[:, None, None] * dist[None]), 0.0)
    lg_lane = jnp.repeat(log_g, HEAD_DIM)[None, :]
    qdec = jnp.exp(lg_lane * (idx[:, None] + 1.0))
    kdec = jnp.exp(lg_lane * (chunk - 1.0 - idx[:, None]))
    cdec = jnp.exp(lg_lane * chunk)
    return dmask.astype(F32), qdec, kdec, cdec


def _rope_tables(pos, dim, rot_dim, theta, n_rep):
    half = rot_dim // 2
    inv_freq = jnp.exp(-math.log(theta) * jnp.arange(half, dtype=F32) / half)
    ang = pos.astype(F32)[:, None] * inv_freq[None, :]
    t = pos.shape[0]
    cos = jnp.concatenate([jnp.cos(ang), jnp.cos(ang), jnp.ones((t, dim - rot_dim), F32)], axis=1)
    sin = jnp.concatenate([jnp.sin(ang), jnp.sin(ang), jnp.zeros((t, dim - rot_dim), F32)], axis=1)
    return jnp.tile(cos, (1, n_rep)), jnp.tile(sin, (1, n_rep))


def _rotated_columns(w, dim, rot_dim):
    half = rot_dim // 2
    k, n = w.shape
    wh = w.reshape(k, n // dim, dim)
    rot = jnp.concatenate([-wh[..., half:rot_dim], wh[..., :half], jnp.zeros_like(wh[..., rot_dim:])], axis=-1)
    return rot.reshape(k, n)


def _retention(q, k, v, g, s0_bd, chunk):
    b, t, w = q.shape
    nc = t // chunk
    dmask, qdec, kdec, cdec = _retention_tables(chunk)
    full = lambda shape: pl.BlockSpec(shape, lambda i, j: (0,) * len(shape))
    row = pl.BlockSpec((None, chunk, w), lambda i, j: (i, j, 0))
    return pl.pallas_call(
        _ret_kernel,
        out_shape=(jax.ShapeDtypeStruct((b, t, w), BF16), jax.ShapeDtypeStruct((b, w, w), F32)),
        grid=(b, nc),
        in_specs=[
            row, row, row, row,
            full((N_HEADS, chunk, chunk)),
            full((chunk, w)),
            full((chunk, w)),
            full((1, w)),
            pl.BlockSpec((None, w, w), lambda i, j: (i, 0, 0)),
        ],
        out_specs=(row, pl.BlockSpec((None, w, w), lambda i, j: (i, 0, 0))),
        scratch_shapes=[pltpu.VMEM((w, w), F32)],
        compiler_params=_params(2),
        name="retention_mixer",
    )(q, k, v, g, dmask, qdec, kdec, cdec, s0_bd)


def _conv_kernel(glu_ref, hist_ref, wc_ref, bc_ref, lg_ref, lb_ref, h_ref, tail_ref, xp_scr):
    c = pl.program_id(1)
    rows = glu_ref.shape[0]
    w = BRANCH_W
    pad = HIST_ROWS - (CONV_W - 1)

    @pl.when(c == 0)
    def _():
        xp_scr[0:HIST_ROWS, :] = hist_ref[...]

    xp_scr[HIST_ROWS:HIST_ROWS + rows, :] = glu_ref[...]
    sub = 8
    acc = jnp.zeros((rows, w), F32) + bc_ref[...]
    for rho in range(sub):
        z = None
        for m in range((pad + CONV_W - 1) // sub + 1):
            j = sub * m + rho - pad
            if not 0 <= j < CONV_W:
                continue
            span = rows if rho == 0 else rows + sub
            term = xp_scr[sub * m:sub * m + span, :] * wc_ref[j:j + 1, :]
            z = term if z is None else z + term
        acc = acc + z[rho:rho + rows, :]
    y = _layer_norm_rows(acc, lg_ref[...], lb_ref[...])
    h_ref[...] = _bf(y * _sigmoid(y))
    tail = xp_scr[rows:rows + HIST_ROWS, :]
    tail_ref[...] = tail
    xp_scr[0:HIST_ROWS, :] = tail


def _conv_mixer(glu, hist, w_conv, b_conv, ln_g, ln_b, chunk):
    b, t, w = glu.shape
    nc = t // chunk
    full = lambda shape: pl.BlockSpec(shape, lambda i, j: (0,) * len(shape))
    wc = jnp.concatenate([w_conv, jnp.zeros((HIST_ROWS - CONV_W, w), F32)], axis=0)
    return pl.pallas_call(
        _conv_kernel,
        out_shape=(jax.ShapeDtypeStruct((b, t, w), BF16), jax.ShapeDtypeStruct((b, HIST_ROWS, w), F32)),
        grid=(b, nc),
        in_specs=[
            pl.BlockSpec((None, chunk, w), lambda i, j: (i, j, 0)),
            pl.BlockSpec((None, HIST_ROWS, w), lambda i, j: (i, 0, 0)),
            full((HIST_ROWS, w)),
            full((1, w)),
            full((1, w)),
            full((1, w)),
        ],
        out_specs=(pl.BlockSpec((None, chunk, w), lambda i, j: (i, j, 0)),
                   pl.BlockSpec((None, HIST_ROWS, w), lambda i, j: (i, 0, 0))),
        scratch_shapes=[pltpu.VMEM((HIST_ROWS + chunk, w), F32)],
        compiler_params=_params(2),
        name="conv_mixer",
    )(glu, hist, wc, b_conv.reshape(1, w), ln_g.reshape(1, w), ln_b.reshape(1, w))


def _layer_slot(depth, layer, b, t, tm, width, dt, first):
    shape = jax.ShapeDtypeStruct((depth, b, t, width), dt)
    if first:
        return shape, pl.BlockSpec((depth, None, tm, width), lambda i, j: (0, i, j, 0))
    return shape, pl.BlockSpec((None, None, tm, width), lambda i, j: (layer, i, j, 0))


def _store_layer_rows(ref, rows):
    if len(ref.shape) == 3:
        for d in range(ref.shape[0]):
            ref[d] = rows
    else:
        ref[...] = rows


def _in_proj_kernel(x_ref, sc_ref, sh_ref, w_ref, bf_ref, rcos_ref, rsin_ref, dcos_ref, dsin_ref, *refs):
    (rq_ref, rk_ref, rv_ref, rg_ref, glu_ref, fq_ref, fk_ref, fv_ref, fkb_ref, fvb_ref, lf_ref,
     dq_ref, dk_ref, dv_ref, dkb_ref, dvb_ref) = refs[-16:]
    w = BRANCH_W
    u = _bf(x_ref[...] * (1.0 + sc_ref[...]) + sh_ref[...])
    o = 0
    p = _dot(u, w_ref[:, o:o + 6 * w])
    cos = rcos_ref[...]
    sin = rsin_ref[...]
    rq_ref[...] = _bf(p[:, 0:w] * cos + p[:, w:2 * w] * sin)
    rk_ref[...] = _bf((p[:, 2 * w:3 * w] * cos + p[:, 3 * w:4 * w] * sin) * (HEAD_DIM ** -0.5))
    rv_ref[...] = _bf(p[:, 4 * w:5 * w])
    rg_ref[...] = p[:, 5 * w:6 * w]
    o += 6 * w
    p = _dot(u, w_ref[:, o:o + 2 * w])
    glu_ref[...] = p[:, :w] * _sigmoid(p[:, w:])
    o += 2 * w
    p = _dot(u, w_ref[:, o:o + 3 * w + LANES])
    fq_ref[...] = _bf(p[:, 0:w] * (HEAD_DIM ** -0.5 * LOG2E))
    k = p[:, w:2 * w]
    v = p[:, 2 * w:3 * w]
    _store_layer_rows(fk_ref, k)
    _store_layer_rows(fv_ref, v)
    fkb_ref[...] = _bf(k)
    fvb_ref[...] = _bf(v)
    z = p[:, 3 * w:] + bf_ref[...]
    lf = jnp.minimum(z, 0.0) - jnp.log(1.0 + jnp.exp(-jnp.abs(z)))
    lane = lax.broadcasted_iota(jnp.int32, lf.shape, 1)
    _store_layer_rows(lf_ref, jnp.where(lane < N_HEADS, lf, 0.0))
    o += 3 * w + LANES
    p = _dot(u, w_ref[:, o:o + 5 * w])
    cos = dcos_ref[...]
    sin = dsin_ref[...]
    dq_ref[...] = _bf((p[:, 0:w] * cos + p[:, w:2 * w] * sin) * (DIFF_SUB ** -0.5 * LOG2E))
    k = p[:, 2 * w:3 * w] * cos + p[:, 3 * w:4 * w] * sin
    v = p[:, 4 * w:5 * w]
    _store_layer_rows(dk_ref, k)
    _store_layer_rows(dv_ref, v)
    dkb_ref[...] = _bf(k)
    dvb_ref[...] = _bf(v)


def _in_projection(x, scale, shift, w_all, b_f, pos, tm, depth, layer, state):
    b, t, _ = x.shape
    w = BRANCH_W
    nw = w_all.shape[1]
    bfp = jnp.concatenate([b_f.astype(F32), jnp.zeros((LANES - N_HEADS,), F32)]).reshape(1, LANES)
    rcos, rsin = _rope_tables(pos, HEAD_DIM, HEAD_DIM, RET_THETA, N_HEADS)
    dcos, dsin = _rope_tables(pos, DIFF_SUB, ROT_DIM, ROPE_THETA, N_SUB)
    row = lambda width: pl.BlockSpec((None, tm, width), lambda i, j: (i, j, 0))
    table = pl.BlockSpec((tm, w), lambda i, j: (j, 0))
    plain = lambda width, dt: (jax.ShapeDtypeStruct((b, t, width), dt), row(width))
    slot = lambda width: _layer_slot(depth, layer, b, t, tm, width, F32, state is None)
    outs = [plain(w, BF16), plain(w, BF16), plain(w, BF16), plain(w, F32), plain(w, F32),
            plain(w, BF16), slot(w), slot(w), plain(w, BF16), plain(w, BF16), slot(LANES),
            plain(w, BF16), slot(w), slot(w), plain(w, BF16), plain(w, BF16)]
    prior = () if state is None else tuple(state)
    n_in = 9
    aliases = {n_in: 6, n_in + 1: 7, n_in + 2: 10, n_in + 3: 12, n_in + 4: 13} if prior else {}
    res = pl.pallas_call(
        _in_proj_kernel,
        out_shape=tuple(o[0] for o in outs),
        grid=(b, t // tm),
        in_specs=[
            row(D_MODEL),
            pl.BlockSpec((None, 1, D_MODEL), lambda i, j: (i, 0, 0)),
            pl.BlockSpec((None, 1, D_MODEL), lambda i, j: (i, 0, 0)),
            pl.BlockSpec((D_MODEL, nw), lambda i, j: (0, 0), pipeline_mode=pl.Buffered(1)),
            pl.BlockSpec((1, LANES), lambda i, j: (0, 0)),
            table, table, table, table,
        ] + [pl.BlockSpec(memory_space=pl.ANY)] * len(prior),
        out_specs=tuple(o[1] for o in outs),
        input_output_aliases=aliases,
        compiler_params=_params(2),
        name="mixer_in_projection",
    )(x, scale, shift, w_all, bfp, rcos, rsin, dcos, dsin, *prior)
    names = ('rq', 'rk', 'rv', 'rg', 'glu', 'fq', 'fk', 'fv', 'fkb', 'fvb', 'lf', 'dq', 'dk', 'dv', 'dkb', 'dvb')
    pr = dict(zip(names, res))
    pr['state'] = (pr['fk'], pr['fv'], pr['lf'], pr['dk'], pr['dv'])
    return pr


def _cumsum_kernel(lf_ref, col_ref, row_ref):
    n = KV_BLOCK
    nb = lf_ref.shape[0] // n
    r = lax.broadcasted_iota(jnp.int32, (n, n), 0)
    c = lax.broadcasted_iota(jnp.int32, (n, n), 1)
    tri = jnp.where(c <= r, 1.0, 0.0).astype(BF16)
    carry = jnp.zeros((1, LANES), F32)
    for jb in range(nb):
        x = lf_ref[jb * n:(jb + 1) * n, :]
        hi = _bf(x)
        r1 = x - hi.astype(F32)
        mid = _bf(r1)
        lo = _bf(r1 - mid.astype(F32))
        cs = _dot(tri, hi) + _dot(tri, mid) + _dot(tri, lo) + carry
        carry = cs[n - 1:n, :]
        scaled = cs * LOG2E
        col_ref[jb * n:(jb + 1) * n, :] = scaled
        row_ref[jb] = scaled.T[0:8, :]


def _logf_cumsum(lf):
    b, tk, _ = lf.shape
    nb = tk // KV_BLOCK
    return pl.pallas_call(
        _cumsum_kernel,
        out_shape=(jax.ShapeDtypeStruct((b, tk, LANES), F32),
                   jax.ShapeDtypeStruct((b, nb, 8, KV_BLOCK), F32)),
        grid=(b,),
        in_specs=[pl.BlockSpec((None, tk, LANES), lambda i: (i, 0, 0))],
        out_specs=(pl.BlockSpec((None, tk, LANES), lambda i: (i, 0, 0)),
                   pl.BlockSpec((None, nb, 8, KV_BLOCK), lambda i: (i, 0, 0, 0))),
        compiler_params=_params(1),
        name="logf_cumsum",
    )(lf)


def _attn_kernel(*refs, fox, off, tk_valid, tq, lam_init):
    if fox:
        q_ref, k_ref, v_ref, fc_ref, fr_ref, o_ref, vt_scr, qm_scr, mb_scr, m_scr, r_scr, acc_scr, s_scr = refs
    else:
        q_ref, k_ref, v_ref, lam_ref, g_ref, o_ref, vt_scr, qm_scr, mb_scr, m_scr, r_scr, acc_scr, s_scr = refs
    i = pl.program_id(1)
    w = BRANCH_W
    n_sub, tqp, _ = qm_scr.shape
    sub_w = w // n_sub
    subs_per_head = n_sub // N_HEADS
    nb = vt_scr.shape[0]
    hd = HEAD_DIM
    q0 = off + i * tq
    nfull = q0 // KV_BLOCK

    @pl.when(i == 0)
    def _():
        for jb in range(nb):
            vt = v_ref[jb * KV_BLOCK:(jb + 1) * KV_BLOCK, :].T
            for h in range(N_HEADS):
                vt_scr[jb, h, 0:hd, :] = vt[h * hd:(h + 1) * hd, :]
                vt_scr[jb, h, hd:, :] = jnp.ones((vt_scr.shape[2] - hd, KV_BLOCK), BF16)
        kpos = lax.broadcasted_iota(jnp.int32, (KV_BLOCK, tqp), 0)
        qpos = lax.broadcasted_iota(jnp.int32, (KV_BLOCK, tqp), 1)
        if fox:
            vis = kpos <= qpos
        else:
            vis = (kpos // CHUNK) <= (qpos // CHUNK)
        vis = vis & (kpos < tk_valid - nfull * KV_BLOCK)
        mb_scr[...] = jnp.where(vis, 0.0, NEG)

    q = q_ref[...]
    lane_sub = lax.broadcasted_iota(jnp.int32, (1, w), 1) // sub_w
    if tqp > tq:
        qm_scr[...] = jnp.zeros_like(qm_scr)
    for n in range(n_sub):
        qm_scr[n, 0:tq, :] = jnp.where(lane_sub == n, q, jnp.zeros_like(q))
    m_scr[...] = jnp.full(m_scr.shape, NEG, F32)
    r_scr[...] = jnp.full(r_scr.shape, NEG, F32)
    acc_scr[...] = jnp.zeros_like(acc_scr)

    def scores(j, n, masked):
        start = pl.multiple_of(j * KV_BLOCK, KV_BLOCK)
        h = n // subs_per_head
        s = _dot_nt(k_ref[pl.ds(start, KV_BLOCK), :], qm_scr[n])
        if fox:
            s = s + (fr_ref[h:h + 1, 0:tqp] - fc_ref[pl.ds(start, KV_BLOCK), h:h + 1])
        if masked:
            s = s + mb_scr[...]
        s_scr[n] = s
        m_scr[n] = jnp.maximum(m_scr[n], jnp.max(s, axis=0, keepdims=True))

    def values(j, n):
        h = n // subs_per_head
        m = m_scr[n]
        alpha = jnp.exp2(r_scr[n] - m)
        r_scr[n] = m
        p = jnp.exp2(s_scr[n] - m)
        acc_scr[n] = alpha * acc_scr[n] + _dot(vt_scr[j, h], _bf(p))

    @pl.when(nfull == 0)
    def _():
        for n in range(n_sub):
            scores(0, n, True)

    @pl.when(nfull > 0)
    def _():
        for n in range(n_sub):
            scores(0, n, False)

    def body(j, carry):
        for n in range(n_sub):
            values(j - 1, n)
            scores(j, n, False)
        return carry

    lax.fori_loop(1, nfull, body, 0)

    @pl.when(nfull > 0)
    def _():
        for n in range(n_sub):
            values(nfull - 1, n)
            scores(nfull, n, True)

    for n in range(n_sub):
        values(nfull, n)

    def normalised(n):
        acc = acc_scr[n]
        return acc[0:hd] / acc[hd:hd + 1]

    if fox:
        out_t = jnp.concatenate([normalised(h) for h in range(N_HEADS)], axis=0)
        o_ref[...] = _bf(out_t.T[0:tq, :])
    else:
        lamv = lam_ref[...]
        lam = (jnp.exp(jnp.sum(lamv[0:1] * lamv[1:2], axis=-1, keepdims=True))
               - jnp.exp(jnp.sum(lamv[2:3] * lamv[3:4], axis=-1, keepdims=True)) + lam_init)
        parts = [normalised(2 * h) - lam * normalised(2 * h + 1) for h in range(N_HEADS)]
        dy = jnp.concatenate(parts, axis=0).T[0:tq, :]
        ms = _head_sum(dy * dy, HEAD_DIM) * (1.0 / HEAD_DIM)
        o_ref[...] = _bf(dy * lax.rsqrt(ms + EPS) * g_ref[...] * (1.0 - lam_init))


def _attention(q, kb, vb, extras, *, fox, tq, off, tk_valid, lam_init=0.0):
    b, t, w = q.shape
    tk = kb.shape[1]
    nb = tk // KV_BLOCK
    tqp = max(tq, LANES)
    n_sub = N_HEADS if fox else N_SUB
    assert off % KV_BLOCK == 0 and (tq == KV_BLOCK or t == tq)
    kernel = functools.partial(_attn_kernel, fox=fox, off=off, tk_valid=tk_valid, tq=tq, lam_init=lam_init)
    in_specs = [
        pl.BlockSpec((None, tq, w), lambda i, j: (i, j, 0)),
        pl.BlockSpec((None, tk, w), lambda i, j: (i, 0, 0)),
        pl.BlockSpec((None, tk, w), lambda i, j: (i, 0, 0)),
    ]
    if fox:
        in_specs += [
            pl.BlockSpec((None, tk, LANES), lambda i, j: (i, 0, 0)),
            pl.BlockSpec((None, None, 8, KV_BLOCK), lambda i, j: (i, (off + j * tq) // KV_BLOCK, 0, 0)),
        ]
    else:
        in_specs += [
            pl.BlockSpec((4, DIFF_SUB), lambda i, j: (0, 0)),
            pl.BlockSpec((1, w), lambda i, j: (0, 0)),
        ]
    return pl.pallas_call(
        kernel,
        out_shape=jax.ShapeDtypeStruct((b, t, w), BF16),
        grid=(b, t // tq),
        in_specs=in_specs,
        out_specs=pl.BlockSpec((None, tq, w), lambda i, j: (i, j, 0)),
        scratch_shapes=[
            pltpu.VMEM((nb, N_HEADS, HEAD_DIM + BF16_ROWS, KV_BLOCK), BF16),
            pltpu.VMEM((n_sub, tqp, w), BF16),
            pltpu.VMEM((KV_BLOCK, tqp), F32),
            pltpu.VMEM((n_sub, 1, tqp), F32),
            pltpu.VMEM((n_sub, 1, tqp), F32),
            pltpu.VMEM((n_sub, HEAD_DIM + BF16_ROWS, tqp), F32),
            pltpu.VMEM((n_sub, KV_BLOCK, tqp), F32),
        ],
        compiler_params=_params(2),
        name="fox_attention" if fox else "diff_attention",
    )(q, kb, vb, *extras)


def _fox_attention(q, kb, vb, fcol, frow, tq, off, tk_valid):
    return _attention(q, kb, vb, (fcol, frow), fox=True, tq=tq, off=off, tk_valid=tk_valid)


def _diff_attention(q, kb, vb, diff_lambda, subln_g, tq, off, tk_valid, lam_init):
    g = jnp.tile(subln_g.astype(F32), N_HEADS).reshape(1, BRANCH_W)
    return _attention(q, kb, vb, (diff_lambda.astype(F32), g), fox=False, tq=tq, off=off,
                      tk_valid=tk_valid, lam_init=lam_init)


def _decode_attn_kernel(*refs, fox, p_len, lam_init):
    if fox:
        q_ref, kc_ref, vc_ref, kn_ref, vn_ref, fc_ref, fr_ref, o_ref = refs
    else:
        q_ref, kc_ref, vc_ref, kn_ref, vn_ref, lam_ref, g_ref, o_ref = refs
    w = BRANCH_W
    t = q_ref.shape[0]
    n_sub = N_HEADS if fox else N_SUB
    sub_w = w // n_sub
    rows = n_sub * t
    q = q_ref[...]
    lane = lax.broadcasted_iota(jnp.int32, (1, w), 1)
    qs = jnp.concatenate([jnp.where(lane // sub_w == n, q, jnp.zeros_like(q)) for n in range(n_sub)], axis=0)
    if fox:
        fcn = fc_ref[...]
        fq = jnp.concatenate([fcn[:, h:h + 1] for h in range(N_HEADS)], axis=0)

    def step(kb, vb, fk, mask, carry):
        m, l, acc = carry
        s = _dot_nt(qs, kb)
        if fox:
            fk_rows = jnp.concatenate([jnp.broadcast_to(fk[h:h + 1, :], (t, KV_BLOCK)) for h in range(N_HEADS)],
                                      axis=0)
            s = s + (fq - fk_rows)
        if mask is not None:
            s = jnp.where(mask, s, NEG)
        m_new = jnp.maximum(m, jnp.max(s, axis=-1, keepdims=True))
        alpha = jnp.exp2(m - m_new)
        p = jnp.exp2(s - m_new)
        l = alpha * l + jnp.sum(p, axis=-1, keepdims=True)
        acc = alpha * acc + _dot(_bf(p), vb)
        return m_new, l, acc

    def cached(j, carry):
        start = pl.multiple_of(j * KV_BLOCK, KV_BLOCK)
        fk = fr_ref[j] if fox else None
        return step(_bf(kc_ref[pl.ds(start, KV_BLOCK), :]), _bf(vc_ref[pl.ds(start, KV_BLOCK), :]), fk, None, carry)

    init = (jnp.full((rows, 1), NEG, F32), jnp.zeros((rows, 1), F32), jnp.zeros((rows, w), F32))
    nbc = p_len // KV_BLOCK
    carry = lax.fori_loop(0, nbc, cached, init)
    qpos = p_len + lax.broadcasted_iota(jnp.int32, (rows, KV_BLOCK), 0) % t
    kpos = p_len + lax.broadcasted_iota(jnp.int32, (rows, KV_BLOCK), 1)
    vis = (kpos <= qpos) if fox else ((kpos // CHUNK) <= (qpos // CHUNK))
    vis = vis & (kpos < p_len + t)
    m, l, acc = step(kn_ref[...], vn_ref[...], fr_ref[nbc] if fox else None, vis, carry)
    o = acc / l
    lane_head = lane // HEAD_DIM
    if fox:
        out = sum(jnp.where(lane_head == h, o[h * t:(h + 1) * t], 0.0) for h in range(N_HEADS))
        o_ref[...] = _bf(out)
    else:
        lamv = lam_ref[...]
        lam = (jnp.exp(jnp.sum(lamv[0:1] * lamv[1:2], axis=-1, keepdims=True))
               - jnp.exp(jnp.sum(lamv[2:3] * lamv[3:4], axis=-1, keepdims=True)) + lam_init)
        dy = sum(jnp.where(lane_head == h, o[2 * h * t:(2 * h + 1) * t] - lam * o[(2 * h + 1) * t:(2 * h + 2) * t], 0.0)
                 for h in range(N_HEADS))
        ms = _head_sum(dy * dy, HEAD_DIM) * (1.0 / HEAD_DIM)
        o_ref[...] = _bf(dy * lax.rsqrt(ms + EPS) * g_ref[...] * (1.0 - lam_init))


def _decode_attention(q, k_cache, v_cache, new_kv, extras, *, fox, lam_init=0.0):
    b, t, w = q.shape
    p_len = k_cache.shape[1]
    assert p_len % KV_BLOCK == 0 and t <= KV_BLOCK and t % 8 == 0 and p_len % t == 0
    nb = p_len // KV_BLOCK + 1
    in_specs = [
        pl.BlockSpec((None, t, w), lambda i: (i, 0, 0)),
        pl.BlockSpec((None, p_len, w), lambda i: (i, 0, 0)),
        pl.BlockSpec((None, p_len, w), lambda i: (i, 0, 0)),
        pl.BlockSpec((None, KV_BLOCK, w), lambda i: (i, 0, 0)),
        pl.BlockSpec((None, KV_BLOCK, w), lambda i: (i, 0, 0)),
    ]
    if fox:
        in_specs += [
            pl.BlockSpec((None, t, LANES), lambda i: (i, p_len // t, 0)),
            pl.BlockSpec((None, nb, 8, KV_BLOCK), lambda i: (i, 0, 0, 0)),
        ]
    else:
        in_specs += [pl.BlockSpec((4, DIFF_SUB), lambda i: (0, 0)), pl.BlockSpec((1, w), lambda i: (0, 0))]
    return pl.pallas_call(
        functools.partial(_decode_attn_kernel, fox=fox, p_len=p_len, lam_init=lam_init),
        out_shape=jax.ShapeDtypeStruct((b, t, w), BF16),
        grid=(b,),
        in_specs=in_specs,
        out_specs=pl.BlockSpec((None, t, w), lambda i: (i, 0, 0)),
        compiler_params=_params(1),
        name="fox_decode_attention" if fox else "diff_decode_attention",
    )(q, k_cache, v_cache, *new_kv, *extras)


ROUTE_W1, ROUTE_W2, ROUTE_I1, ROUTE_I2 = 8, 9, 10, 11


def _top2_route(logits):
    lane = lax.broadcasted_iota(jnp.int32, logits.shape, 1).astype(F32)
    lg = jnp.where(lane < N_EXPERTS, logits, -jnp.inf)
    v1 = jnp.max(lg, axis=-1, keepdims=True)
    i1 = jnp.min(jnp.where(lg == v1, lane, float(LANES)), axis=-1, keepdims=True)
    lg2 = jnp.where(lane == i1, -jnp.inf, lg)
    v2 = jnp.max(lg2, axis=-1, keepdims=True)
    i2 = jnp.min(jnp.where(lg2 == v2, lane, float(LANES)), axis=-1, keepdims=True)
    e2 = jnp.exp(v2 - v1)
    w1 = 1.0 / (1.0 + e2)
    w2 = e2 / (1.0 + e2)
    rec = jnp.where(lane == i1, w1, 0.0) + jnp.where(lane == i2, w2, 0.0)
    for slot, val in ((ROUTE_W1, w1), (ROUTE_W2, w2), (ROUTE_I1, i1), (ROUTE_I2, i2)):
        rec = rec + jnp.where(lane == float(slot), val, 0.0)
    return rec


def _merge_kernel(*refs, with_router):
    if with_router:
        (x_ref, sc1_ref, sh1_ref, g1_ref, sc2_ref, sh2_ref, hr_ref, hf_ref, hc_ref, hd_ref,
         wg_ref, wb_ref, wo_ref, lg_ref, lb_ref, wr_ref, br_ref, x1_ref, u2_ref, rcol_ref, rrow_ref) = refs
    else:
        (x_ref, sc1_ref, sh1_ref, g1_ref, sc2_ref, sh2_ref, hr_ref, hf_ref, hc_ref, hd_ref,
         wg_ref, wb_ref, wo_ref, lg_ref, lb_ref, x1_ref, u2_ref) = refs
    x = x_ref[...]
    u = _bf(x * (1.0 + sc1_ref[...]) + sh1_ref[...])
    merged = None
    for n, h_ref in enumerate((hr_ref, hf_ref, hc_ref, hd_ref)):
        gate = _dot(u, wg_ref[:, n * D_MODEL:(n + 1) * D_MODEL])
        term = _sigmoid(gate) * _dot(h_ref[...], wb_ref[n])
        merged = term if merged is None else merged + term
    mix = _dot(_bf(merged), wo_ref[...])
    x1 = _layer_norm_rows(ALPHA * x + g1_ref[...] * mix, lg_ref[...], lb_ref[...])
    x1_ref[...] = x1
    u2 = x1 * (1.0 + sc2_ref[...]) + sh2_ref[...]
    u2_ref[...] = u2.astype(u2_ref.dtype)
    if with_router:
        rec = _top2_route(_dot(_bf(u2), wr_ref[...]) + br_ref[...])
        rcol_ref[...] = rec
        rrow_ref[...] = rec.T[8:16, :]


def _merge(x2d, mods, branches, w_gate, w_branch, w_out, ln_g, ln_b, router, tm, tiles_per_group):
    n = x2d.shape[0]
    r = mods[0].shape[1]
    w = BRANCH_W
    with_router = router is not None
    row = lambda width: pl.BlockSpec((tm, width), lambda i: (i, 0))
    mod_spec = pl.BlockSpec((None, r, D_MODEL), lambda i: (i // tiles_per_group, 0, 0))
    full = lambda shape: pl.BlockSpec(shape, lambda i: (0,) * len(shape))
    resident = lambda shape: pl.BlockSpec(shape, lambda i: (0,) * len(shape), pipeline_mode=pl.Buffered(1))
    in_specs = ([row(D_MODEL)] + [mod_spec] * 5 + [row(w)] * 4
                + [resident((D_MODEL, 4 * D_MODEL)), resident((4, w, D_MODEL)), resident((D_MODEL, D_MODEL)),
                   full((1, D_MODEL)), full((1, D_MODEL))])
    args = [x2d, *mods, *branches, w_gate, w_branch, w_out, ln_g.reshape(1, D_MODEL), ln_b.reshape(1, D_MODEL)]
    out_shape = [jax.ShapeDtypeStruct((n, D_MODEL), F32),
                 jax.ShapeDtypeStruct((n, D_MODEL), F32 if with_router else BF16)]
    out_specs = [row(D_MODEL), row(D_MODEL)]
    if with_router:
        in_specs += [full((D_MODEL, LANES)), full((1, LANES))]
        args += list(router)
        out_shape += [jax.ShapeDtypeStruct((n, LANES), F32), jax.ShapeDtypeStruct((n // tm, 8, tm), F32)]
        out_specs += [row(LANES), pl.BlockSpec((None, 8, tm), lambda i: (i, 0, 0))]
    return pl.pallas_call(
        functools.partial(_merge_kernel, with_router=with_router),
        out_shape=tuple(out_shape),
        grid=(n // tm,),
        in_specs=in_specs,
        out_specs=tuple(out_specs),
        compiler_params=_params(1),
        name="merge_outproj_ln",
    )(*args)


def _swiglu_chunks(u, w_up_ref, w_down_ref, hidden, chunk):
    acc = None
    for c0 in range(0, hidden, chunk):
        c1 = min(c0 + chunk, hidden)
        a = _dot(u, w_up_ref[:, c0:c1])
        g = _dot(u, w_up_ref[:, hidden + c0:hidden + c1])
        y = _dot(_bf(a * _sigmoid(a) * g), w_down_ref[c0:c1, :])
        acc = y if acc is None else acc + y
    return acc


def _ffn_kernel(u_ref, x1_ref, g2_ref, wu_ref, wd_ref, lg_ref, lb_ref, o_ref, *, chunk):
    f = _swiglu_chunks(u_ref[...], wu_ref, wd_ref, D_FF, chunk)
    z = ALPHA * x1_ref[...] + g2_ref[...] * f
    o_ref[...] = _layer_norm_rows(z, lg_ref[...], lb_ref[...])


def _ffn_dense(u2, x1, gate2, w_up, w_down, ln_g, ln_b, tm, tiles_per_group, chunk):
    n = u2.shape[0]
    r = gate2.shape[1]
    resident = lambda shape: pl.BlockSpec(shape, lambda i: (0,) * len(shape), pipeline_mode=pl.Buffered(1))
    return pl.pallas_call(
        functools.partial(_ffn_kernel, chunk=chunk),
        out_shape=jax.ShapeDtypeStruct((n, D_MODEL), F32),
        grid=(n // tm,),
        in_specs=[
            pl.BlockSpec((tm, D_MODEL), lambda i: (i, 0)),
            pl.BlockSpec((tm, D_MODEL), lambda i: (i, 0)),
            pl.BlockSpec((None, r, D_MODEL), lambda i: (i // tiles_per_group, 0, 0)),
            resident((D_MODEL, 2 * D_FF)),
            resident((D_FF, D_MODEL)),
            pl.BlockSpec((1, D_MODEL), lambda i: (0, 0)),
            pl.BlockSpec((1, D_MODEL), lambda i: (0, 0)),
        ],
        out_specs=pl.BlockSpec((tm, D_MODEL), lambda i: (i, 0)),
        compiler_params=_params(1),
        name="ffn_dense",
    )(u2, x1, gate2, w_up, w_down, ln_g.reshape(1, D_MODEL), ln_b.reshape(1, D_MODEL))


def _moe_kernel(u_ref, x1_ref, g2_ref, cmb_ref, wa_ref, wg_ref, wd_ref, lg_ref, lb_ref, o_ref, acc_scr):
    e = pl.program_id(1)
    j = pl.program_id(2)

    @pl.when((e == 0) & (j == 0))
    def _():
        acc_scr[...] = jnp.zeros_like(acc_scr)

    u = u_ref[...]
    a = _dot(u, wa_ref[...])
    g = _dot(u, wg_ref[...])
    y = _dot(_bf(a * _sigmoid(a) * g), wd_ref[...])
    cmb = cmb_ref[...]
    lane = lax.broadcasted_iota(jnp.int32, cmb.shape, 1)
    ce = jnp.sum(jnp.where(lane == e, cmb, 0.0), axis=-1, keepdims=True)
    acc_scr[...] += ce * y

    @pl.when((e == pl.num_programs(1) - 1) & (j == pl.num_programs(2) - 1))
    def _():
        z = ALPHA * x1_ref[...] + g2_ref[...] * acc_scr[...]
        o_ref[...] = _layer_norm_rows(z, lg_ref[...], lb_ref[...])


def _ffn_experts(u2, x1, gate2, combine, w_in, w_out, ln_g, ln_b, tm, tiles_per_group, th):
    n = u2.shape[0]
    r = gate2.shape[1]
    nh = D_EXPERT // th
    return pl.pallas_call(
        _moe_kernel,
        out_shape=jax.ShapeDtypeStruct((n, D_MODEL), F32),
        grid=(n // tm, N_EXPERTS, nh),
        in_specs=[
            pl.BlockSpec((tm, D_MODEL), lambda i, e, j: (i, 0)),
            pl.BlockSpec((tm, D_MODEL), lambda i, e, j: (i, 0)),
            pl.BlockSpec((None, r, D_MODEL), lambda i, e, j: (i // tiles_per_group, 0, 0)),
            pl.BlockSpec((tm, LANES), lambda i, e, j: (i, 0)),
            pl.BlockSpec((None, D_MODEL, th), lambda i, e, j: (e, 0, j)),
            pl.BlockSpec((None, D_MODEL, th), lambda i, e, j: (e, 0, nh + j)),
            pl.BlockSpec((None, th, D_MODEL), lambda i, e, j: (e, j, 0)),
            pl.BlockSpec((1, D_MODEL), lambda i, e, j: (0, 0)),
            pl.BlockSpec((1, D_MODEL), lambda i, e, j: (0, 0)),
        ],
        out_specs=pl.BlockSpec((tm, D_MODEL), lambda i, e, j: (i, 0)),
        scratch_shapes=[pltpu.VMEM((tm, D_MODEL), F32)],
        compiler_params=_params(3),
        name="ffn_experts",
    )(u2, x1, gate2, combine, w_in, w_in, w_out, ln_g.reshape(1, D_MODEL), ln_b.reshape(1, D_MODEL))


EXPERT_TILE = 1024


def _rank_kernel(rr_ref, rank_ref, cnt_ref, carry_scr):
    b = pl.program_id(0)

    @pl.when(b == 0)
    def _():
        carry_scr[...] = jnp.zeros_like(carry_scr)

    rr = rr_ref[...]
    tm = rr.shape[1]
    i1 = rr[2:3]
    i2 = rr[3:4]
    e = lax.broadcasted_iota(jnp.int32, (N_EXPERTS, tm), 0).astype(F32)
    oh1 = jnp.where(e == i1, 1.0, 0.0)
    oh2 = jnp.where(e == i2, 1.0, 0.0)
    sel = oh1 + oh2
    r = lax.broadcasted_iota(jnp.int32, (tm, tm), 0)
    c = lax.broadcasted_iota(jnp.int32, (tm, tm), 1)
    triu = jnp.where(r <= c, 1.0, 0.0).astype(BF16)
    csum = _dot(_bf(sel), triu)
    carry = carry_scr[:, 0:1]
    rank = csum - sel + carry
    r1 = jnp.sum(oh1 * rank, axis=0, keepdims=True)
    r2 = jnp.sum(oh2 * rank, axis=0, keepdims=True)
    rec = jnp.concatenate([i1, i2, r1, r2, jnp.zeros((4, tm), F32)], axis=0)
    rank_ref[...] = rec.astype(jnp.int32)
    total = carry + csum[:, tm - 1:tm]
    carry_scr[...] = jnp.broadcast_to(total, carry_scr.shape)
    cnt_ref[...] = jnp.broadcast_to(total, cnt_ref.shape).astype(jnp.int32)


def _route_ranks(route_row):
    nblk, _, tm = route_row.shape
    return pl.pallas_call(
        _rank_kernel,
        out_shape=(jax.ShapeDtypeStruct((nblk, 8, tm), jnp.int32),
                   jax.ShapeDtypeStruct((N_EXPERTS, LANES), jnp.int32)),
        grid=(nblk,),
        in_specs=[pl.BlockSpec((None, 8, tm), lambda b: (b, 0, 0))],
        out_specs=(pl.BlockSpec((None, 8, tm), lambda b: (b, 0, 0)),
                   pl.BlockSpec((N_EXPERTS, LANES), lambda b: (0, 0))),
        scratch_shapes=[pltpu.VMEM((N_EXPERTS, LANES), F32)],
        compiler_params=_params(1),
        name="route_ranks",
    )(route_row)


def _row_copy(src_ref, src_row, dst_ref, dst_row, sem):
    return pltpu.make_async_copy(src_ref.at[pl.ds(src_row, 1)], dst_ref.at[pl.ds(dst_row, 1)], sem)


def _dispatch_kernel(last_ref, slot_ref, u_ref, xs_ref, zero_scr, sem):
    tm = u_ref.shape[0]
    gt = zero_scr.shape[0]

    @pl.when(pl.program_id(0) == 0)
    def _():
        zero_scr[...] = jnp.zeros_like(zero_scr)
        for e in range(N_EXPERTS):
            @pl.when(last_ref[e] >= 0)
            def _():
                first = pl.multiple_of(last_ref[e], gt)
                fill = pltpu.make_async_copy(zero_scr, xs_ref.at[pl.ds(first, gt)], sem)
                fill.start()
                fill.wait()

    def start(r, carry):
        for k in range(2):
            _row_copy(u_ref, r, xs_ref, slot_ref[k, r], sem).start()
        return carry

    lax.fori_loop(0, tm, start, 0, unroll=8)

    def wait(r, carry):
        for k in range(2):
            _row_copy(u_ref, 0, xs_ref, 0, sem).wait()
        return carry

    lax.fori_loop(0, tm, wait, 0, unroll=8)


def _dispatch(u2, slots, last_tile, n_slots, group_tile):
    n, d = u2.shape
    nblk, _, tm = slots.shape
    grid_spec = pltpu.PrefetchScalarGridSpec(
        num_scalar_prefetch=1,
        grid=(nblk,),
        in_specs=[
            pl.BlockSpec((None, 2, tm), lambda b, last: (b, 0, 0), memory_space=pltpu.SMEM),
            pl.BlockSpec((tm, d), lambda b, last: (b, 0)),
        ],
        out_specs=pl.BlockSpec(memory_space=pl.ANY),
        scratch_shapes=[pltpu.VMEM((group_tile, d), F32), pltpu.SemaphoreType.DMA],
    )
    return pl.pallas_call(
        _dispatch_kernel,
        out_shape=jax.ShapeDtypeStruct((n_slots, d), F32),
        grid_spec=grid_spec,
        compiler_params=_params(1),
        name="expert_dispatch",
    )(last_tile, slots, u2)


def _grouped_kernel(toff_ref, nt_ref, xs_ref, wi_ref, wo_ref, ys_ref, *, chunk):
    @pl.when(pl.program_id(0) < nt_ref[0])
    def _():
        ys_ref[...] = _swiglu_chunks(_bf(xs_ref[...]), wi_ref, wo_ref, D_EXPERT, chunk)


def _grouped_experts(xs, tile_off, n_tiles, w_in, w_out, tm, chunk):
    n_slots, d = xs.shape

    def expert(i, toff):
        e = 0
        for k in range(1, N_EXPERTS):
            e = e + (i >= toff[k]).astype(jnp.int32)
        return e

    def tile(i, nt):
        return jnp.minimum(i, nt[0] - 1)

    grid_spec = pltpu.PrefetchScalarGridSpec(
        num_scalar_prefetch=2,
        grid=(n_slots // tm,),
        in_specs=[
            pl.BlockSpec((tm, d), lambda i, toff, nt: (tile(i, nt), 0)),
            pl.BlockSpec((None, d, 2 * D_EXPERT), lambda i, toff, nt: (expert(tile(i, nt), toff), 0, 0),
                         pipeline_mode=pl.Buffered(1)),
            pl.BlockSpec((None, D_EXPERT, d), lambda i, toff, nt: (expert(tile(i, nt), toff), 0, 0),
                         pipeline_mode=pl.Buffered(1)),
        ],
        out_specs=pl.BlockSpec((tm, d), lambda i, toff, nt: (tile(i, nt), 0)),
    )
    return pl.pallas_call(
        functools.partial(_grouped_kernel, chunk=chunk),
        out_shape=jax.ShapeDtypeStruct((n_slots, d), F32),
        grid_spec=grid_spec,
        compiler_params=_params(1),
        name="grouped_experts",
    )(tile_off, n_tiles, xs, w_in, w_out)


def _combine_kernel(slot_ref, ys_ref, x1_ref, g2_ref, rcol_ref, lg_ref, lb_ref, o_ref, y1_scr, y2_scr, sem):
    tm = x1_ref.shape[0]
    bufs = (y1_scr, y2_scr)

    def start(r, carry):
        for k in range(2):
            _row_copy(ys_ref, slot_ref[k, r], bufs[k], r, sem).start()
        return carry

    lax.fori_loop(0, tm, start, 0, unroll=8)

    def wait(r, carry):
        for k in range(2):
            _row_copy(ys_ref, 0, bufs[k], 0, sem).wait()
        return carry

    lax.fori_loop(0, tm, wait, 0, unroll=8)
    rec = rcol_ref[...]
    f = rec[:, ROUTE_W1:ROUTE_W1 + 1] * y1_scr[...] + rec[:, ROUTE_W2:ROUTE_W2 + 1] * y2_scr[...]
    z = ALPHA * x1_ref[...] + g2_ref[...] * f
    o_ref[...] = _layer_norm_rows(z, lg_ref[...], lb_ref[...])


def _combine(ys, slots, x1, gate2, route_col, ln_g, ln_b, tiles_per_group):
    n, d = x1.shape
    nblk, _, tm = slots.shape
    r = gate2.shape[1]
    return pl.pallas_call(
        _combine_kernel,
        out_shape=jax.ShapeDtypeStruct((n, d), F32),
        grid=(nblk,),
        in_specs=[
            pl.BlockSpec((None, 2, tm), lambda b: (b, 0, 0), memory_space=pltpu.SMEM),
            pl.BlockSpec(memory_space=pl.ANY),
            pl.BlockSpec((tm, d), lambda b: (b, 0)),
            pl.BlockSpec((None, r, d), lambda b: (b // tiles_per_group, 0, 0)),
            pl.BlockSpec((tm, LANES), lambda b: (b, 0)),
            pl.BlockSpec((1, d), lambda b: (0, 0)),
            pl.BlockSpec((1, d), lambda b: (0, 0)),
        ],
        out_specs=pl.BlockSpec((tm, d), lambda b: (b, 0)),
        scratch_shapes=[pltpu.VMEM((tm, d), F32), pltpu.VMEM((tm, d), F32), pltpu.SemaphoreType.DMA],
        compiler_params=_params(1),
        name="expert_combine_ln",
    )(slots, ys, x1, gate2, route_col, ln_g.reshape(1, d), ln_b.reshape(1, d))


def _ffn_routed(u2, x1, gate2, route_col, route_row, w_in, w_out, ln_g, ln_b, tiles_per_group):
    n = u2.shape[0]
    tm = min(EXPERT_TILE, n)
    assert n % tm == 0
    ranks, counts = _route_ranks(route_row)
    cnt = counts[:, 0]
    padded = ((cnt + tm - 1) // tm) * tm
    off = (jnp.cumsum(padded) - padded).astype(jnp.int32)
    last_tile = jnp.where(padded > 0, off + padded - tm, -1).astype(jnp.int32)
    n_slots = 2 * n + N_EXPERTS * tm
    slots = off[ranks[:, 0:2, :]] + ranks[:, 2:4, :]
    xs = _dispatch(u2, slots, last_tile, n_slots, tm)
    ys = _grouped_experts(xs, (off // tm).astype(jnp.int32), (jnp.sum(padded) // tm).astype(jnp.int32).reshape(1),
                          w_in, w_out, tm, 512)
    return _combine(ys, slots, x1, gate2, route_col, ln_g, ln_b, tiles_per_group)


def _split_w_in(w_in_l):
    w = BRANCH_W
    col = lambda off, n: w_in_l[:, off:off + n]
    o = 0
    ret_q, ret_k, ret_v, ret_g = (col(o + i * w, w) for i in range(4)); o += 4 * w
    fox_q, fox_k, fox_v = (col(o + i * w, w) for i in range(3)); o += 3 * w
    fox_f = col(o, N_HEADS); o += N_HEADS
    conv = col(o, 2 * w); o += 2 * w
    diff_q, diff_k, diff_v = (col(o + i * w, w) for i in range(3)); o += 3 * w
    gate = col(o, 4 * D_MODEL)
    rot_r = lambda m: _rotated_columns(m, HEAD_DIM, HEAD_DIM)
    rot_d = lambda m: _rotated_columns(m, DIFF_SUB, ROT_DIM)
    w_ret = jnp.concatenate([ret_q, rot_r(ret_q), ret_k, rot_r(ret_k), ret_v, ret_g], axis=1)
    w_fox = jnp.concatenate([fox_q, fox_k, fox_v, fox_f, jnp.zeros((D_MODEL, LANES - N_HEADS), F32)], axis=1)
    w_diff = jnp.concatenate([diff_q, rot_d(diff_q), diff_k, rot_d(diff_k), diff_v], axis=1)
    return _bf(jnp.concatenate([w_ret, conv, w_fox, w_diff], axis=1)), _bf(gate)


def _block_diag_state(s):
    b = s.shape[0]
    eye = jnp.eye(N_HEADS, dtype=s.dtype)
    return jnp.einsum('bhde,hg->bhdge', s, eye).reshape(b, BRANCH_W, BRANCH_W)


def _diag_blocks(s_bd):
    b = s_bd.shape[0]
    s4 = s_bd.reshape(b, N_HEADS, HEAD_DIM, N_HEADS, HEAD_DIM)
    return jnp.stack([s4[:, h, :, h, :] for h in range(N_HEADS)], axis=1)


def _pad_rows(a, rows):
    return jnp.concatenate([a, jnp.zeros((a.shape[0], rows - a.shape[1]) + a.shape[2:], a.dtype)], axis=1)


def _trunk_layer(x, mod, pos, past, l, lw, sample, depth, kv_state):
    b, t, _ = x.shape
    d = D_MODEL
    w = BRANCH_W
    shift1, scale1, gate1 = (mod[0][:, None, i * d:(i + 1) * d] for i in range(3))
    shift2, scale2, gate2 = (mod[1][:, None, i * d:(i + 1) * d] for i in range(3))
    lam_init = 0.8 - 0.6 * math.exp(-0.3 * l)
    chunk = min(t, KV_BLOCK)
    pr = _in_projection(x, scale1, shift1, lw['w_mix'], lw['b_fox_f'], pos, min(t, 512), depth, l, kv_state)
    kv_state = pr['state']

    s0 = jnp.zeros((b, w, w), F32) if past is None else _block_diag_state(past[5])
    h_ret, s_bd = _retention(pr['rq'], pr['rk'], pr['rv'], pr['rg'], s0, chunk)
    ret_state = _diag_blocks(s_bd)

    if past is None:
        hist = jnp.zeros((b, HIST_ROWS, w), F32)
    else:
        hist = jnp.concatenate([jnp.zeros((b, HIST_ROWS - (CONV_W - 1), w), F32), past[6]], axis=1)
    h_conv, tail = _conv_mixer(pr['glu'], hist, lw['w_conv'], lw['b_conv'], lw['conv_ln_g'], lw['conv_ln_b'], chunk)
    conv_buf = tail[:, HIST_ROWS - (CONV_W - 1):, :]

    fq, fkb, fvb, lf = pr['fq'], pr['fkb'], pr['fvb'], pr['lf'][l]
    if past is None:
        off, tk_valid = 0, t
        k_all, v_all, lf_all, fox_new = fkb, fvb, lf, None
    else:
        p_len = past[0].shape[1]
        off, tk_valid = p_len, p_len + t
        tk_pad = p_len + KV_BLOCK
        k_all, v_all = past[0].reshape(b, p_len, w), past[1].reshape(b, p_len, w)
        fox_new = (_pad_rows(fkb, KV_BLOCK), _pad_rows(fvb, KV_BLOCK))
        past_lf = jnp.concatenate([past[2].astype(F32), jnp.zeros((b, p_len, LANES - N_HEADS), F32)], axis=2)
        lf_all = _pad_rows(jnp.concatenate([past_lf, lf], axis=1), tk_pad)
    fcol, frow = _logf_cumsum(lf_all)
    tq = min(t, KV_BLOCK)
    if past is None:
        h_fox = _fox_attention(fq, k_all, v_all, fcol, frow, tq, off, tk_valid)
    else:
        h_fox = _decode_attention(fq, k_all, v_all, fox_new, (fcol, frow), fox=True)

    dq, dkb, dvb = pr['dq'], pr['dkb'], pr['dvb']
    if past is None:
        dk_all, dv_all, diff_new = dkb, dvb, None
    else:
        dk_all, dv_all = past[3].reshape(b, p_len, w), past[4].reshape(b, p_len, w)
        diff_new = (_pad_rows(dkb, KV_BLOCK), _pad_rows(dvb, KV_BLOCK))
    if past is None:
        h_diff = _diff_attention(dq, dk_all, dv_all, lw['diff_lambda'], lw['diff_subln_g'], tq, off, tk_valid, lam_init)
    else:
        g_sub = jnp.tile(lw['diff_subln_g'].astype(F32), N_HEADS).reshape(1, w)
        h_diff = _decode_attention(dq, dk_all, dv_all, diff_new, (lw['diff_lambda'].astype(F32), g_sub), fox=False,
                                   lam_init=lam_init)

    n = b * t
    if sample:
        tm_merge = tm_ffn = min(n, 256)
        tpg_merge = tpg_ffn = 1
        rows_mod = lambda m, tm: jnp.repeat(m, t, axis=1).reshape(n // tm, tm, d)
    else:
        tm_merge, tm_ffn = min(t, 512), min(t, 1024)
        tpg_merge, tpg_ffn = t // tm_merge, t // tm_ffn
        rows_mod = lambda m, tm: m
    mods = tuple(rows_mod(m, tm_merge) for m in (scale1, shift1, gate1, scale2, shift2))
    branches = tuple(h.reshape(n, w) for h in (h_ret, h_fox, h_conv, h_diff))
    merged = _merge(x.reshape(n, d), mods, branches, lw['w_gate'], lw['w_branch'], lw['w_out'],
                    lw['ln_g'][0], lw['ln_b'][0], lw.get('router'), tm_merge, tpg_merge)
    g2 = rows_mod(gate2, tm_ffn)
    if 'router' in lw and not sample:
        x1, u2, route_col, route_row = merged
        x2 = _ffn_routed(u2, x1, rows_mod(gate2, tm_merge), route_col, route_row, lw['w_exp_in'], lw['w_exp_out'],
                         lw['ln_g'][1], lw['ln_b'][1], tpg_merge)
    elif 'router' in lw:
        x1, u2, route_col, _ = merged
        x2 = _ffn_experts(_bf(u2), x1, g2, route_col, lw['w_exp_in'], lw['w_exp_out'], lw['ln_g'][1], lw['ln_b'][1],
                          tm_ffn, tpg_ffn, 512)
    else:
        x1, u2 = merged
        x2 = _ffn_dense(u2, x1, g2, lw['w_ffn_in'], lw['w_ffn_out'], lw['ln_g'][1], lw['ln_b'][1],
                        tm_ffn, tpg_ffn, 512)
    return x2.reshape(b, t, d), kv_state, (ret_state, conv_buf)


def _state_outputs(kv_state, small):
    fk, fv, lf, dk, dv = kv_state
    depth, b, t, w = fk.shape
    heads = lambda a, nh: a.reshape(depth, b, t, nh, w // nh)
    ret_state, conv_buf = (jnp.stack(a) for a in zip(*small))
    return (heads(fk, N_HEADS), heads(fv, N_HEADS), lf[..., :N_HEADS], heads(dk, N_SUB), heads(dv, N_HEADS),
            ret_state, conv_buf)


def kernel(x_prompt, x_sample, c_prompt, c_sample, cache_fox_k, cache_fox_v, cache_fox_logf, cache_diff_k, cache_diff_v, state_ret, state_conv, w_in, b_fox_f, w_conv, b_conv, conv_ln_g, conv_ln_b, diff_lambda, diff_subln_g, w_branch, w_out, w_ada, b_ada, ln_g, ln_b, w_ffn_in, w_ffn_out, w_router, b_router, w_exp_in, w_exp_out):
    depth = w_in.shape[0]
    bp = x_prompt.shape[0]
    past_len = cache_fox_k.shape[2]
    pos_p = jnp.arange(x_prompt.shape[1], dtype=jnp.int32)
    pos_s = past_len + jnp.arange(x_sample.shape[1], dtype=jnp.int32)
    mod = _modulation(jnp.concatenate([c_prompt, c_sample], axis=0), w_ada, b_ada)
    yp, ys = x_prompt, x_sample
    kv_p = kv_s = None
    small_p, small_s = [], []
    for l in range(depth):
        w_mix, w_gate = _split_w_in(w_in[l])
        lw = dict(w_mix=w_mix, w_gate=w_gate,
                  b_fox_f=b_fox_f[l], w_conv=w_conv[l], b_conv=b_conv[l], conv_ln_g=conv_ln_g[l],
                  conv_ln_b=conv_ln_b[l], diff_lambda=diff_lambda[l], diff_subln_g=diff_subln_g[l],
                  w_branch=_bf(w_branch[l]), w_out=_bf(w_out[l]), ln_g=ln_g[l], ln_b=ln_b[l])
        if l % 2 == 0:
            lw['w_ffn_in'] = _bf(w_ffn_in[l // 2])
            lw['w_ffn_out'] = _bf(w_ffn_out[l // 2])
        else:
            wr = jnp.concatenate([w_router[l // 2], jnp.zeros((D_MODEL, LANES - N_EXPERTS), F32)], axis=1)
            br = jnp.concatenate([b_router[l // 2].astype(F32), jnp.zeros((LANES - N_EXPERTS,), F32)]).reshape(1, LANES)
            lw['router'] = (_bf(wr), br)
            lw['w_exp_in'] = _bf(w_exp_in[l // 2])
            lw['w_exp_out'] = _bf(w_exp_out[l // 2])
        past_l = (cache_fox_k[l], cache_fox_v[l], cache_fox_logf[l], cache_diff_k[l], cache_diff_v[l],
                  state_ret[l], state_conv[l])
        yp, kv_p, st_p = _trunk_layer(yp, mod[l][:, :bp], pos_p, None, l, lw, False, depth, kv_p)
        ys, kv_s, st_s = _trunk_layer(ys, mod[l][:, bp:], pos_s, past_l, l, lw, True, depth, kv_s)
        small_p.append(st_p)
        small_s.append(st_s)
    return (yp, ys) + _state_outputs(kv_p, small_p) + _state_outputs(kv_s, small_s)
```

```python
import functools
import math

import jax
import jax.numpy as jnp
from jax import lax
from jax.experimental import pallas as pl
from jax.experimental.pallas import tpu as pltpu

D_MODEL = 1024
BRANCH_W = 256
HEAD_DIM = 64
N_HEADS = 4
DIFF_SUB = 32
N_SUB = 8
ROT_DIM = DIFF_SUB // 4
RET_THETA = 10000.0
ROPE_THETA = 500000.0
CHUNK = 64
CONV_W = 31
D_FF = 2816
N_EXPERTS = 8
D_EXPERT = 3584
DEPTH = 2
ALPHA = (2.0 * DEPTH) ** 0.25
EPS = 1e-5
NEG = -1e30
LOG2E = math.log2(math.e)

LANES = 128
BF16_ROWS = 16
KV_BLOCK = 256
HIST_ROWS = 32
VMEM_LIMIT = 56 * 1024 * 1024

F32 = jnp.float32
BF16 = jnp.bfloat16


def _bf(x):
    return x.astype(BF16)


def _dot(a, b):
    return jnp.dot(a, b, preferred_element_type=F32)


def _dot_nt(a, b):
    return lax.dot_general(a, b, (((1,), (1,)), ((), ())), preferred_element_type=F32)


def _dot_tn(a, b):
    return lax.dot_general(a, b, (((0,), (0,)), ((), ())), preferred_element_type=F32)


def _sigmoid(x):
    return 1.0 / (1.0 + jnp.exp(-x))


def _params(n_axes):
    return pltpu.CompilerParams(dimension_semantics=("arbitrary",) * n_axes,
                                vmem_limit_bytes=VMEM_LIMIT)


def _head_sum(y, width):
    n = y.shape[-1]
    r = lax.broadcasted_iota(jnp.int32, (n, n), 0) // width
    c = lax.broadcasted_iota(jnp.int32, (n, n), 1) // width
    bd = jnp.where(r == c, 1.0, 0.0).astype(BF16)
    hi = _bf(y)
    lo = _bf(y - hi.astype(F32))
    return _dot(hi, bd) + _dot(lo, bd)


def _layer_norm_rows(z, g, b):
    mu = jnp.mean(z, axis=-1, keepdims=True)
    d = z - mu
    var = jnp.mean(d * d, axis=-1, keepdims=True)
    return d * lax.rsqrt(var + EPS) * g + b


def _mod_kernel(c_ref, w_ref, b_ref, o_ref):
    c = c_ref[...]
    sc = _bf(c * _sigmoid(c))
    o_ref[...] = _dot(sc, _bf(w_ref[...])) + b_ref[...]


def _modulation(c_all, w_ada, b_ada):
    rows = c_all.shape[0]
    depth = w_ada.shape[0]
    d3 = w_ada.shape[-1]
    nj = d3 // D_MODEL
    return pl.pallas_call(
        _mod_kernel,
        out_shape=jax.ShapeDtypeStruct((depth, 2, rows, d3), F32),
        grid=(depth * 2, nj),
        in_specs=[
            pl.BlockSpec((rows, D_MODEL), lambda i, j: (0, 0)),
            pl.BlockSpec((None, None, D_MODEL, D_MODEL), lambda i, j: (i // 2, i % 2, 0, j)),
            pl.BlockSpec((None, None, 1, D_MODEL), lambda i, j: (i // 2, i % 2, 0, j)),
        ],
        out_specs=pl.BlockSpec((None, None, rows, D_MODEL), lambda i, j: (i // 2, i % 2, 0, j)),
        compiler_params=_params(2),
        name="adaln_modulation",
    )(c_all, w_ada, b_ada.reshape(depth, 2, 1, d3))


def _ret_kernel(q_ref, k_ref, v_ref, g_ref, dmask_ref, qdec_ref, kdec_ref, cdec_ref, s0_ref,
                h_ref, sout_ref, s_scr):
    c = pl.program_id(1)

    @pl.when(c == 0)
    def _():
        s_scr[...] = s0_ref[...]

    w = BRANCH_W
    q = q_ref[...]
    kb = k_ref[...]
    vb = v_ref[...]
    g = g_ref[...]
    rows = q.shape[0]
    lane_head = lax.broadcasted_iota(jnp.int32, (1, w), 1) // HEAD_DIM
    y = jnp.zeros((rows, w), F32)
    for h in range(N_HEADS):
        mh = lane_head == h
        qh = jnp.where(mh, q, jnp.zeros_like(q))
        a = _dot_nt(qh, kb) * dmask_ref[h]
        y = y + jnp.where(mh, _dot(_bf(a), vb), 0.0)
    s_prev = s_scr[...]
    y = y + _dot(_bf(q.astype(F32) * qdec_ref[...]), _bf(s_prev))
    kv = _dot_tn(_bf(kb.astype(F32) * kdec_ref[...]), vb)
    r = lax.broadcasted_iota(jnp.int32, (w, w), 0) // HEAD_DIM
    cc = lax.broadcasted_iota(jnp.int32, (w, w), 1) // HEAD_DIM
    s_new = cdec_ref[...] * s_prev + jnp.where(r == cc, kv, 0.0)
    s_scr[...] = s_new
    sout_ref[...] = s_new
    mu = _head_sum(y, HEAD_DIM) * (1.0 / HEAD_DIM)
    d = y - mu
    var = _head_sum(d * d, HEAD_DIM) * (1.0 / HEAD_DIM)
    hn = d * lax.rsqrt(var + EPS)
    h_ref[...] = _bf(hn * (g * _sigmoid(g)))


def _retention_tables(chunk):
    log_g = jnp.log1p(-jnp.exp2(-5.0 - jnp.arange(N_HEADS, dtype=F32)))
    idx = jnp.arange(chunk, dtype=F32)
    dist = jnp.abs(idx[:, None] - idx[None, :])
    sub = jnp.arange(chunk) // CHUNK
    vis = sub[None, :] <= sub[:, None]
    dmask = jnp.where(vis[None], jnp.exp(log_g[:, None, None] * dist[None]), 0.0)
    lg_lane = jnp.repeat(log_g, HEAD_DIM)[None, :]
    qdec = jnp.exp(lg_lane * (idx[:, None] + 1.0))
    kdec = jnp.exp(lg_lane * (chunk - 1.0 - idx[:, None]))
    cdec = jnp.exp(lg_lane * chunk)
    return dmask.astype(F32), qdec, kdec, cdec


def _rope_tables(pos, dim, rot_dim, theta, n_rep):
    half = rot_dim // 2
    inv_freq = jnp.exp(-math.log(theta) * jnp.arange(half, dtype=F32) / half)
    ang = pos.astype(F32)[:, None] * inv_freq[None, :]
    t = pos.shape[0]
    cos = jnp.concatenate([jnp.cos(ang), jnp.cos(ang), jnp.ones((t, dim - rot_dim), F32)], axis=1)
    sin = jnp.concatenate([jnp.sin(ang), jnp.sin(ang), jnp.zeros((t, dim - rot_dim), F32)], axis=1)
    return jnp.tile(cos, (1, n_rep)), jnp.tile(sin, (1, n_rep))


def _rotated_columns(w, dim, rot_dim):
    half = rot_dim // 2
    k, n = w.shape
    wh = w.reshape(k, n // dim, dim)
    rot = jnp.concatenate([-wh[..., half:rot_dim], wh[..., :half], jnp.zeros_like(wh[..., rot_dim:])], axis=-1)
    return rot.reshape(k, n)


def _retention(q, k, v, g, s0_bd, chunk):
    b, t, w = q.shape
    nc = t // chunk
    dmask, qdec, kdec, cdec = _retention_tables(chunk)
    full = lambda shape: pl.BlockSpec(shape, lambda i, j: (0,) * len(shape))
    row = pl.BlockSpec((None, chunk, w), lambda i, j: (i, j, 0))
    return pl.pallas_call(
        _ret_kernel,
        out_shape=(jax.ShapeDtypeStruct((b, t, w), BF16), jax.ShapeDtypeStruct((b, w, w), F32)),
        grid=(b, nc),
        in_specs=[
            row, row, row, row,
            full((N_HEADS, chunk, chunk)),
            full((chunk, w)),
            full((chunk, w)),
            full((1, w)),
            pl.BlockSpec((None, w, w), lambda i, j: (i, 0, 0)),
        ],
        out_specs=(row, pl.BlockSpec((None, w, w), lambda i, j: (i, 0, 0))),
        scratch_shapes=[pltpu.VMEM((w, w), F32)],
        compiler_params=_params(2),
        name="retention_mixer",
    )(q, k, v, g, dmask, qdec, kdec, cdec, s0_bd)


def _conv_kernel(glu_ref, hist_ref, wc_ref, bc_ref, lg_ref, lb_ref, h_ref, tail_ref, xp_scr):
    c = pl.program_id(1)
    rows = glu_ref.shape[0]
    w = BRANCH_W
    pad = HIST_ROWS - (CONV_W - 1)

    @pl.when(c == 0)
    def _():
        xp_scr[0:HIST_ROWS, :] = hist_ref[...]

    xp_scr[HIST_ROWS:HIST_ROWS + rows, :] = glu_ref[...]
    sub = 8
    acc = jnp.zeros((rows, w), F32) + bc_ref[...]
    for rho in range(sub):
        z = None
        for m in range((pad + CONV_W - 1) // sub + 1):
            j = sub * m + rho - pad
            if not 0 <= j < CONV_W:
                continue
            span = rows if rho == 0 else rows + sub
            term = xp_scr[sub * m:sub * m + span, :] * wc_ref[j:j + 1, :]
            z = term if z is None else z + term
        acc = acc + z[rho:rho + rows, :]
    y = _layer_norm_rows(acc, lg_ref[...], lb_ref[...])
    h_ref[...] = _bf(y * _sigmoid(y))
    tail = xp_scr[rows:rows + HIST_ROWS, :]
    tail_ref[...] = tail
    xp_scr[0:HIST_ROWS, :] = tail


def _conv_mixer(glu, hist, w_conv, b_conv, ln_g, ln_b, chunk):
    b, t, w = glu.shape
    nc = t // chunk
    full = lambda shape: pl.BlockSpec(shape, lambda i, j: (0,) * len(shape))
    wc = jnp.concatenate([w_conv, jnp.zeros((HIST_ROWS - CONV_W, w), F32)], axis=0)
    return pl.pallas_call(
        _conv_kernel,
        out_shape=(jax.ShapeDtypeStruct((b, t, w), BF16), jax.ShapeDtypeStruct((b, HIST_ROWS, w), F32)),
        grid=(b, nc),
        in_specs=[
            pl.BlockSpec((None, chunk, w), lambda i, j: (i, j, 0)),
            pl.BlockSpec((None, HIST_ROWS, w), lambda i, j: (i, 0, 0)),
            full((HIST_ROWS, w)),
            full((1, w)),
            full((1, w)),
            full((1, w)),
        ],
        out_specs=(pl.BlockSpec((None, chunk, w), lambda i, j: (i, j, 0)),
                   pl.BlockSpec((None, HIST_ROWS, w), lambda i, j: (i, 0, 0))),
        scratch_shapes=[pltpu.VMEM((HIST_ROWS + chunk, w), F32)],
        compiler_params=_params(2),
        name="conv_mixer",
    )(glu, hist, wc, b_conv.reshape(1, w), ln_g.reshape(1, w), ln_b.reshape(1, w))


def _layer_slot(depth, layer, b, t, tm, width, dt, first):
    shape = jax.ShapeDtypeStruct((depth, b, t, width), dt)
    if first:
        return shape, pl.BlockSpec((depth, None, tm, width), lambda i, j: (0, i, j, 0))
    return shape, pl.BlockSpec((None, None, tm, width), lambda i, j: (layer, i, j, 0))


def _store_layer_rows(ref, rows):
    if len(ref.shape) == 3:
        for d in range(ref.shape[0]):
            ref[d] = rows
    else:
        ref[...] = rows


def _in_proj_kernel(x_ref, sc_ref, sh_ref, w_ref, bf_ref, rcos_ref, rsin_ref, dcos_ref, dsin_ref, *refs):
    (rq_ref, rk_ref, rv_ref, rg_ref, glu_ref, fq_ref, fk_ref, fv_ref, fkb_ref, fvb_ref, lf_ref,
     dq_ref, dk_ref, dv_ref, dkb_ref, dvb_ref) = refs[-16:]
    w = BRANCH_W
    u = _bf(x_ref[...] * (1.0 + sc_ref[...]) + sh_ref[...])
    o = 0
    p = _dot(u, w_ref[:, o:o + 6 * w])
    cos = rcos_ref[...]
    sin = rsin_ref[...]
    rq_ref[...] = _bf(p[:, 0:w] * cos + p[:, w:2 * w] * sin)
    rk_ref[...] = _bf((p[:, 2 * w:3 * w] * cos + p[:, 3 * w:4 * w] * sin) * (HEAD_DIM ** -0.5))
    rv_ref[...] = _bf(p[:, 4 * w:5 * w])
    rg_ref[...] = p[:, 5 * w:6 * w]
    o += 6 * w
    p = _dot(u, w_ref[:, o:o + 2 * w])
    glu_ref[...] = p[:, :w] * _sigmoid(p[:, w:])
    o += 2 * w
    p = _dot(u, w_ref[:, o:o + 3 * w + LANES])
    fq_ref[...] = _bf(p[:, 0:w] * (HEAD_DIM ** -0.5 * LOG2E))
    k = p[:, w:2 * w]
    v = p[:, 2 * w:3 * w]
    _store_layer_rows(fk_ref, k)
    _store_layer_rows(fv_ref, v)
    fkb_ref[...] = _bf(k)
    fvb_ref[...] = _bf(v)
    z = p[:, 3 * w:] + bf_ref[...]
    lf = jnp.minimum(z, 0.0) - jnp.log(1.0 + jnp.exp(-jnp.abs(z)))
    lane = lax.broadcasted_iota(jnp.int32, lf.shape, 1)
    _store_layer_rows(lf_ref, jnp.where(lane < N_HEADS, lf, 0.0))
    o += 3 * w + LANES
    p = _dot(u, w_ref[:, o:o + 5 * w])
    cos = dcos_ref[...]
    sin = dsin_ref[...]
    dq_ref[...] = _bf((p[:, 0:w] * cos + p[:, w:2 * w] * sin) * (DIFF_SUB ** -0.5 * LOG2E))
    k = p[:, 2 * w:3 * w] * cos + p[:, 3 * w:4 * w] * sin
    v = p[:, 4 * w:5 * w]
    _store_layer_rows(dk_ref, k)
    _store_layer_rows(dv_ref, v)
    dkb_ref[...] = _bf(k)
    dvb_ref[...] = _bf(v)


def _in_projection(x, scale, shift, w_all, b_f, pos, tm, depth, layer, state):
    b, t, _ = x.shape
    w = BRANCH_W
    nw = w_all.shape[1]
    bfp = jnp.concatenate([b_f.astype(F32), jnp.zeros((LANES - N_HEADS,), F32)]).reshape(1, LANES)
    rcos, rsin = _rope_tables(pos, HEAD_DIM, HEAD_DIM, RET_THETA, N_HEADS)
    dcos, dsin = _rope_tables(pos, DIFF_SUB, ROT_DIM, ROPE_THETA, N_SUB)
    row = lambda width: pl.BlockSpec((None, tm, width), lambda i, j: (i, j, 0))
    table = pl.BlockSpec((tm, w), lambda i, j: (j, 0))
    plain = lambda width, dt: (jax.ShapeDtypeStruct((b, t, width), dt), row(width))
    slot = lambda width: _layer_slot(depth, layer, b, t, tm, width, F32, state is None)
    outs = [plain(w, BF16), plain(w, BF16), plain(w, BF16), plain(w, F32), plain(w, F32),
            plain(w, BF16), slot(w), slot(w), plain(w, BF16), plain(w, BF16), slot(LANES),
            plain(w, BF16), slot(w), slot(w), plain(w, BF16), plain(w, BF16)]
    prior = () if state is None else tuple(state)
    n_in = 9
    aliases = {n_in: 6, n_in + 1: 7, n_in + 2: 10, n_in + 3: 12, n_in + 4: 13} if prior else {}
    res = pl.pallas_call(
        _in_proj_kernel,
        out_shape=tuple(o[0] for o in outs),
        grid=(b, t // tm),
        in_specs=[
            row(D_MODEL),
            pl.BlockSpec((None, 1, D_MODEL), lambda i, j: (i, 0, 0)),
            pl.BlockSpec((None, 1, D_MODEL), lambda i, j: (i, 0, 0)),
            pl.BlockSpec((D_MODEL, nw), lambda i, j: (0, 0), pipeline_mode=pl.Buffered(1)),
            pl.BlockSpec((1, LANES), lambda i, j: (0, 0)),
            table, table, table, table,
        ] + [pl.BlockSpec(memory_space=pl.ANY)] * len(prior),
        out_specs=tuple(o[1] for o in outs),
        input_output_aliases=aliases,
        compiler_params=_params(2),
        name="mixer_in_projection",
    )(x, scale, shift, w_all, bfp, rcos, rsin, dcos, dsin, *prior)
    names = ('rq', 'rk', 'rv', 'rg', 'glu', 'fq', 'fk', 'fv', 'fkb', 'fvb', 'lf', 'dq', 'dk', 'dv', 'dkb', 'dvb')
    pr = dict(zip(names, res))
    pr['state'] = (pr['fk'], pr['fv'], pr['lf'], pr['dk'], pr['dv'])
    return pr


def _cumsum_kernel(lf_ref, col_ref, row_ref):
    n = KV_BLOCK
    nb = lf_ref.shape[0] // n
    r = lax.broadcasted_iota(jnp.int32, (n, n), 0)
    c = lax.broadcasted_iota(jnp.int32, (n, n), 1)
    tri = jnp.where(c <= r, 1.0, 0.0).astype(BF16)
    carry = jnp.zeros((1, LANES), F32)
    for jb in range(nb):
        x = lf_ref[jb * n:(jb + 1) * n, :]
        hi = _bf(x)
        r1 = x - hi.astype(F32)
        mid = _bf(r1)
        lo = _bf(r1 - mid.astype(F32))
        cs = _dot(tri, hi) + _dot(tri, mid) + _dot(tri, lo) + carry
        carry = cs[n - 1:n, :]
        scaled = cs * LOG2E
        col_ref[jb * n:(jb + 1) * n, :] = scaled
        row_ref[jb] = scaled.T[0:8, :]


def _logf_cumsum(lf):
    b, tk, _ = lf.shape
    nb = tk // KV_BLOCK
    return pl.pallas_call(
        _cumsum_kernel,
        out_shape=(jax.ShapeDtypeStruct((b, tk, LANES), F32),
                   jax.ShapeDtypeStruct((b, nb, 8, KV_BLOCK), F32)),
        grid=(b,),
        in_specs=[pl.BlockSpec((None, tk, LANES), lambda i: (i, 0, 0))],
        out_specs=(pl.BlockSpec((None, tk, LANES), lambda i: (i, 0, 0)),
                   pl.BlockSpec((None, nb, 8, KV_BLOCK), lambda i: (i, 0, 0, 0))),
        compiler_params=_params(1),
        name="logf_cumsum",
    )(lf)


def _attn_kernel(*refs, fox, off, tk_valid, tq, lam_init):
    if fox:
        q_ref, k_ref, v_ref, fc_ref, fr_ref, o_ref, vt_scr, qm_scr, mb_scr, m_scr, r_scr, acc_scr, s_scr = refs
    else:
        q_ref, k_ref, v_ref, lam_ref, g_ref, o_ref, vt_scr, qm_scr, mb_scr, m_scr, r_scr, acc_scr, s_scr = refs
    i = pl.program_id(1)
    w = BRANCH_W
    n_sub, tqp, _ = qm_scr.shape
    sub_w = w // n_sub
    subs_per_head = n_sub // N_HEADS
    nb = vt_scr.shape[0]
    hd = HEAD_DIM
    q0 = off + i * tq
    nfull = q0 // KV_BLOCK

    @pl.when(i == 0)
    def _():
        for jb in range(nb):
            vt = v_ref[jb * KV_BLOCK:(jb + 1) * KV_BLOCK, :].T
            for h in range(N_HEADS):
                vt_scr[jb, h, 0:hd, :] = vt[h * hd:(h + 1) * hd, :]
                vt_scr[jb, h, hd:, :] = jnp.ones((vt_scr.shape[2] - hd, KV_BLOCK), BF16)
        kpos = lax.broadcasted_iota(jnp.int32, (KV_BLOCK, tqp), 0)
        qpos = lax.broadcasted_iota(jnp.int32, (KV_BLOCK, tqp), 1)
        if fox:
            vis = kpos <= qpos
        else:
            vis = (kpos // CHUNK) <= (qpos // CHUNK)
        vis = vis & (kpos < tk_valid - nfull * KV_BLOCK)
        mb_scr[...] = jnp.where(vis, 0.0, NEG)

    q = q_ref[...]
    lane_sub = lax.broadcasted_iota(jnp.int32, (1, w), 1) // sub_w
    if tqp > tq:
        qm_scr[...] = jnp.zeros_like(qm_scr)
    for n in range(n_sub):
        qm_scr[n, 0:tq, :] = jnp.where(lane_sub == n, q, jnp.zeros_like(q))
    m_scr[...] = jnp.full(m_scr.shape, NEG, F32)
    r_scr[...] = jnp.full(r_scr.shape, NEG, F32)
    acc_scr[...] = jnp.zeros_like(acc_scr)

    def scores(j, n, masked):
        start = pl.multiple_of(j * KV_BLOCK, KV_BLOCK)
        h = n // subs_per_head
        s = _dot_nt(k_ref[pl.ds(start, KV_BLOCK), :], qm_scr[n])
        if fox:
            s = s + (fr_ref[h:h + 1, 0:tqp] - fc_ref[pl.ds(start, KV_BLOCK), h:h + 1])
        if masked:
            s = s + mb_scr[...]
        s_scr[n] = s
        m_scr[n] = jnp.maximum(m_scr[n], jnp.max(s, axis=0, keepdims=True))

    def values(j, n):
        h = n // subs_per_head
        m = m_scr[n]
        alpha = jnp.exp2(r_scr[n] - m)
        r_scr[n] = m
        p = jnp.exp2(s_scr[n] - m)
        acc_scr[n] = alpha * acc_scr[n] + _dot(vt_scr[j, h], _bf(p))

    @pl.when(nfull == 0)
    def _():
        for n in range(n_sub):
            scores(0, n, True)

    @pl.when(nfull > 0)
    def _():
        for n in range(n_sub):
            scores(0, n, False)

    def advance(j):
        for n in range(n_sub):
            values(j - 1, n)
            scores(j, n, False)

    n_mid = jnp.maximum(nfull - 1, 0)

    def body(i2, carry):
        advance(1 + 2 * i2)
        advance(2 + 2 * i2)
        return carry

    lax.fori_loop(0, n_mid // 2, body, 0)

    @pl.when(n_mid % 2 == 1)
    def _():
        advance(nfull - 1)

    @pl.when(nfull > 0)
    def _():
        for n in range(n_sub):
            values(nfull - 1, n)
            scores(nfull, n, True)

    for n in range(n_sub):
        values(nfull, n)

    def normalised(n):
        acc = acc_scr[n]
        return acc[0:hd] / acc[hd:hd + 1]

    if fox:
        out_t = jnp.concatenate([normalised(h) for h in range(N_HEADS)], axis=0)
        o_ref[...] = _bf(out_t.T[0:tq, :])
    else:
        lamv = lam_ref[...]
        lam = (jnp.exp(jnp.sum(lamv[0:1] * lamv[1:2], axis=-1, keepdims=True))
               - jnp.exp(jnp.sum(lamv[2:3] * lamv[3:4], axis=-1, keepdims=True)) + lam_init)
        parts = [normalised(2 * h) - lam * normalised(2 * h + 1) for h in range(N_HEADS)]
        dy = jnp.concatenate(parts, axis=0).T[0:tq, :]
        ms = _head_sum(dy * dy, HEAD_DIM) * (1.0 / HEAD_DIM)
        o_ref[...] = _bf(dy * lax.rsqrt(ms + EPS) * g_ref[...] * (1.0 - lam_init))


def _attention(q, kb, vb, extras, *, fox, tq, off, tk_valid, lam_init=0.0):
    b, t, w = q.shape
    tk = kb.shape[1]
    nb = tk // KV_BLOCK
    tqp = max(tq, LANES)
    n_sub = N_HEADS if fox else N_SUB
    assert off % KV_BLOCK == 0 and (tq == KV_BLOCK or t == tq)
    kernel = functools.partial(_attn_kernel, fox=fox, off=off, tk_valid=tk_valid, tq=tq, lam_init=lam_init)
    in_specs = [
        pl.BlockSpec((None, tq, w), lambda i, j: (i, j, 0)),
        pl.BlockSpec((None, tk, w), lambda i, j: (i, 0, 0)),
        pl.BlockSpec((None, tk, w), lambda i, j: (i, 0, 0)),
    ]
    if fox:
        in_specs += [
            pl.BlockSpec((None, tk, LANES), lambda i, j: (i, 0, 0)),
            pl.BlockSpec((None, None, 8, KV_BLOCK), lambda i, j: (i, (off + j * tq) // KV_BLOCK, 0, 0)),
        ]
    else:
        in_specs += [
            pl.BlockSpec((4, DIFF_SUB), lambda i, j: (0, 0)),
            pl.BlockSpec((1, w), lambda i, j: (0, 0)),
        ]
    return pl.pallas_call(
        kernel,
        out_shape=jax.ShapeDtypeStruct((b, t, w), BF16),
        grid=(b, t // tq),
        in_specs=in_specs,
        out_specs=pl.BlockSpec((None, tq, w), lambda i, j: (i, j, 0)),
        scratch_shapes=[
            pltpu.VMEM((nb, N_HEADS, HEAD_DIM + BF16_ROWS, KV_BLOCK), BF16),
            pltpu.VMEM((n_sub, tqp, w), BF16),
            pltpu.VMEM((KV_BLOCK, tqp), F32),
            pltpu.VMEM((n_sub, 1, tqp), F32),
            pltpu.VMEM((n_sub, 1, tqp), F32),
            pltpu.VMEM((n_sub, HEAD_DIM + BF16_ROWS, tqp), F32),
            pltpu.VMEM((n_sub, KV_BLOCK, tqp), F32),
        ],
        compiler_params=_params(2),
        name="fox_attention" if fox else "diff_attention",
    )(q, kb, vb, *extras)


def _fox_attention(q, kb, vb, fcol, frow, tq, off, tk_valid):
    return _attention(q, kb, vb, (fcol, frow), fox=True, tq=tq, off=off, tk_valid=tk_valid)


def _diff_attention(q, kb, vb, diff_lambda, subln_g, tq, off, tk_valid, lam_init):
    g = jnp.tile(subln_g.astype(F32), N_HEADS).reshape(1, BRANCH_W)
    return _attention(q, kb, vb, (diff_lambda.astype(F32), g), fox=False, tq=tq, off=off,
                      tk_valid=tk_valid, lam_init=lam_init)


def _decode_attn_kernel(*refs, fox, p_len, lam_init):
    if fox:
        q_ref, kc_ref, vc_ref, kn_ref, vn_ref, fc_ref, fr_ref, o_ref = refs
    else:
        q_ref, kc_ref, vc_ref, kn_ref, vn_ref, lam_ref, g_ref, o_ref = refs
    w = BRANCH_W
    t = q_ref.shape[0]
    n_sub = N_HEADS if fox else N_SUB
    sub_w = w // n_sub
    rows = n_sub * t
    q = q_ref[...]
    lane = lax.broadcasted_iota(jnp.int32, (1, w), 1)
    qs = jnp.concatenate([jnp.where(lane // sub_w == n, q, jnp.zeros_like(q)) for n in range(n_sub)], axis=0)
    if fox:
        fcn = fc_ref[...]
        fq = jnp.concatenate([fcn[:, h:h + 1] for h in range(N_HEADS)], axis=0)

    def step(kb, vb, fk, mask, carry):
        m, l, acc = carry
        s = _dot_nt(qs, kb)
        if fox:
            fk_rows = jnp.concatenate([jnp.broadcast_to(fk[h:h + 1, :], (t, KV_BLOCK)) for h in range(N_HEADS)],
                                      axis=0)
            s = s + (fq - fk_rows)
        if mask is not None:
            s = jnp.where(mask, s, NEG)
        m_new = jnp.maximum(m, jnp.max(s, axis=-1, keepdims=True))
        alpha = jnp.exp2(m - m_new)
        p = jnp.exp2(s - m_new)
        l = alpha * l + jnp.sum(p, axis=-1, keepdims=True)
        acc = alpha * acc + _dot(_bf(p), vb)
        return m_new, l, acc

    def cached(j, carry):
        start = pl.multiple_of(j * KV_BLOCK, KV_BLOCK)
        fk = fr_ref[j] if fox else None
        return step(_bf(kc_ref[pl.ds(start, KV_BLOCK), :]), _bf(vc_ref[pl.ds(start, KV_BLOCK), :]), fk, None, carry)

    init = (jnp.full((rows, 1), NEG, F32), jnp.zeros((rows, 1), F32), jnp.zeros((rows, w), F32))
    nbc = p_len // KV_BLOCK
    carry = lax.fori_loop(0, nbc, cached, init)
    qpos = p_len + lax.broadcasted_iota(jnp.int32, (rows, KV_BLOCK), 0) % t
    kpos = p_len + lax.broadcasted_iota(jnp.int32, (rows, KV_BLOCK), 1)
    vis = (kpos <= qpos) if fox else ((kpos // CHUNK) <= (qpos // CHUNK))
    vis = vis & (kpos < p_len + t)
    m, l, acc = step(kn_ref[...], vn_ref[...], fr_ref[nbc] if fox else None, vis, carry)
    o = acc / l
    lane_head = lane // HEAD_DIM
    if fox:
        out = sum(jnp.where(lane_head == h, o[h * t:(h + 1) * t], 0.0) for h in range(N_HEADS))
        o_ref[...] = _bf(out)
    else:
        lamv = lam_ref[...]
        lam = (jnp.exp(jnp.sum(lamv[0:1] * lamv[1:2], axis=-1, keepdims=True))
               - jnp.exp(jnp.sum(lamv[2:3] * lamv[3:4], axis=-1, keepdims=True)) + lam_init)
        dy = sum(jnp.where(lane_head == h, o[2 * h * t:(2 * h + 1) * t] - lam * o[(2 * h + 1) * t:(2 * h + 2) * t], 0.0)
                 for h in range(N_HEADS))
        ms = _head_sum(dy * dy, HEAD_DIM) * (1.0 / HEAD_DIM)
        o_ref[...] = _bf(dy * lax.rsqrt(ms + EPS) * g_ref[...] * (1.0 - lam_init))


def _decode_attention(q, k_cache, v_cache, new_kv, extras, *, fox, lam_init=0.0):
    b, t, w = q.shape
    p_len = k_cache.shape[1]
    assert p_len % KV_BLOCK == 0 and t <= KV_BLOCK and t % 8 == 0 and p_len % t == 0
    nb = p_len // KV_BLOCK + 1
    in_specs = [
        pl.BlockSpec((None, t, w), lambda i: (i, 0, 0)),
        pl.BlockSpec((None, p_len, w), lambda i: (i, 0, 0)),
        pl.BlockSpec((None, p_len, w), lambda i: (i, 0, 0)),
        pl.BlockSpec((None, KV_BLOCK, w), lambda i: (i, 0, 0)),
        pl.BlockSpec((None, KV_BLOCK, w), lambda i: (i, 0, 0)),
    ]
    if fox:
        in_specs += [
            pl.BlockSpec((None, t, LANES), lambda i: (i, p_len // t, 0)),
            pl.BlockSpec((None, nb, 8, KV_BLOCK), lambda i: (i, 0, 0, 0)),
        ]
    else:
        in_specs += [pl.BlockSpec((4, DIFF_SUB), lambda i: (0, 0)), pl.BlockSpec((1, w), lambda i: (0, 0))]
    return pl.pallas_call(
        functools.partial(_decode_attn_kernel, fox=fox, p_len=p_len, lam_init=lam_init),
        out_shape=jax.ShapeDtypeStruct((b, t, w), BF16),
        grid=(b,),
        in_specs=in_specs,
        out_specs=pl.BlockSpec((None, t, w), lambda i: (i, 0, 0)),
        compiler_params=_params(1),
        name="fox_decode_attention" if fox else "diff_decode_attention",
    )(q, k_cache, v_cache, *new_kv, *extras)


ROUTE_W1, ROUTE_W2, ROUTE_I1, ROUTE_I2 = 8, 9, 10, 11


def _top2_route(logits):
    lane = lax.broadcasted_iota(jnp.int32, logits.shape, 1).astype(F32)
    lg = jnp.where(lane < N_EXPERTS, logits, -jnp.inf)
    v1 = jnp.max(lg, axis=-1, keepdims=True)
    i1 = jnp.min(jnp.where(lg == v1, lane, float(LANES)), axis=-1, keepdims=True)
    lg2 = jnp.where(lane == i1, -jnp.inf, lg)
    v2 = jnp.max(lg2, axis=-1, keepdims=True)
    i2 = jnp.min(jnp.where(lg2 == v2, lane, float(LANES)), axis=-1, keepdims=True)
    e2 = jnp.exp(v2 - v1)
    w1 = 1.0 / (1.0 + e2)
    w2 = e2 / (1.0 + e2)
    rec = jnp.where(lane == i1, w1, 0.0) + jnp.where(lane == i2, w2, 0.0)
    for slot, val in ((ROUTE_W1, w1), (ROUTE_W2, w2), (ROUTE_I1, i1), (ROUTE_I2, i2)):
        rec = rec + jnp.where(lane == float(slot), val, 0.0)
    return rec


def _merge_kernel(*refs, with_router):
    if with_router:
        (x_ref, sc1_ref, sh1_ref, g1_ref, sc2_ref, sh2_ref, hr_ref, hf_ref, hc_ref, hd_ref,
         wg_ref, wb_ref, wo_ref, lg_ref, lb_ref, wr_ref, br_ref, x1_ref, u2_ref, rcol_ref, rrow_ref) = refs
    else:
        (x_ref, sc1_ref, sh1_ref, g1_ref, sc2_ref, sh2_ref, hr_ref, hf_ref, hc_ref, hd_ref,
         wg_ref, wb_ref, wo_ref, lg_ref, lb_ref, x1_ref, u2_ref) = refs
    x = x_ref[...]
    u = _bf(x * (1.0 + sc1_ref[...]) + sh1_ref[...])
    merged = None
    for n, h_ref in enumerate((hr_ref, hf_ref, hc_ref, hd_ref)):
        gate = _dot(u, wg_ref[:, n * D_MODEL:(n + 1) * D_MODEL])
        term = _sigmoid(gate) * _dot(h_ref[...], wb_ref[n])
        merged = term if merged is None else merged + term
    mix = _dot(_bf(merged), wo_ref[...])
    x1 = _layer_norm_rows(ALPHA * x + g1_ref[...] * mix, lg_ref[...], lb_ref[...])
    x1_ref[...] = x1
    u2 = x1 * (1.0 + sc2_ref[...]) + sh2_ref[...]
    u2_ref[...] = u2.astype(u2_ref.dtype)
    if with_router:
        rec = _top2_route(_dot(_bf(u2), wr_ref[...]) + br_ref[...])
        rcol_ref[...] = rec
        rrow_ref[...] = rec.T[8:16, :]


def _merge(x2d, mods, branches, w_gate, w_branch, w_out, ln_g, ln_b, router, tm, tiles_per_group):
    n = x2d.shape[0]
    r = mods[0].shape[1]
    w = BRANCH_W
    with_router = router is not None
    row = lambda width: pl.BlockSpec((tm, width), lambda i: (i, 0))
    mod_spec = pl.BlockSpec((None, r, D_MODEL), lambda i: (i // tiles_per_group, 0, 0))
    full = lambda shape: pl.BlockSpec(shape, lambda i: (0,) * len(shape))
    resident = lambda shape: pl.BlockSpec(shape, lambda i: (0,) * len(shape), pipeline_mode=pl.Buffered(1))
    in_specs = ([row(D_MODEL)] + [mod_spec] * 5 + [row(w)] * 4
                + [resident((D_MODEL, 4 * D_MODEL)), resident((4, w, D_MODEL)), resident((D_MODEL, D_MODEL)),
                   full((1, D_MODEL)), full((1, D_MODEL))])
    args = [x2d, *mods, *branches, w_gate, w_branch, w_out, ln_g.reshape(1, D_MODEL), ln_b.reshape(1, D_MODEL)]
    out_shape = [jax.ShapeDtypeStruct((n, D_MODEL), F32),
                 jax.ShapeDtypeStruct((n, D_MODEL), F32 if with_router else BF16)]
    out_specs = [row(D_MODEL), row(D_MODEL)]
    if with_router:
        in_specs += [full((D_MODEL, LANES)), full((1, LANES))]
        args += list(router)
        out_shape += [jax.ShapeDtypeStruct((n, LANES), F32), jax.ShapeDtypeStruct((n // tm, 8, tm), F32)]
        out_specs += [row(LANES), pl.BlockSpec((None, 8, tm), lambda i: (i, 0, 0))]
    return pl.pallas_call(
        functools.partial(_merge_kernel, with_router=with_router),
        out_shape=tuple(out_shape),
        grid=(n // tm,),
        in_specs=in_specs,
        out_specs=tuple(out_specs),
        compiler_params=_params(1),
        name="merge_outproj_ln",
    )(*args)


def _swiglu_chunks(u, w_up_ref, w_down_ref, hidden, chunk):
    acc = None
    for c0 in range(0, hidden, chunk):
        c1 = min(c0 + chunk, hidden)
        a = _dot(u, w_up_ref[:, c0:c1])
        g = _dot(u, w_up_ref[:, hidden + c0:hidden + c1])
        y = _dot(_bf(a * _sigmoid(a) * g), w_down_ref[c0:c1, :])
        acc = y if acc is None else acc + y
    return acc


def _ffn_kernel(u_ref, x1_ref, g2_ref, wu_ref, wd_ref, lg_ref, lb_ref, o_ref, *, chunk):
    f = _swiglu_chunks(u_ref[...], wu_ref, wd_ref, D_FF, chunk)
    z = ALPHA * x1_ref[...] + g2_ref[...] * f
    o_ref[...] = _layer_norm_rows(z, lg_ref[...], lb_ref[...])


def _ffn_dense(u2, x1, gate2, w_up, w_down, ln_g, ln_b, tm, tiles_per_group, chunk):
    n = u2.shape[0]
    r = gate2.shape[1]
    resident = lambda shape: pl.BlockSpec(shape, lambda i: (0,) * len(shape), pipeline_mode=pl.Buffered(1))
    return pl.pallas_call(
        functools.partial(_ffn_kernel, chunk=chunk),
        out_shape=jax.ShapeDtypeStruct((n, D_MODEL), F32),
        grid=(n // tm,),
        in_specs=[
            pl.BlockSpec((tm, D_MODEL), lambda i: (i, 0)),
            pl.BlockSpec((tm, D_MODEL), lambda i: (i, 0)),
            pl.BlockSpec((None, r, D_MODEL), lambda i: (i // tiles_per_group, 0, 0)),
            resident((D_MODEL, 2 * D_FF)),
            resident((D_FF, D_MODEL)),
            pl.BlockSpec((1, D_MODEL), lambda i: (0, 0)),
            pl.BlockSpec((1, D_MODEL), lambda i: (0, 0)),
        ],
        out_specs=pl.BlockSpec((tm, D_MODEL), lambda i: (i, 0)),
        compiler_params=_params(1),
        name="ffn_dense",
    )(u2, x1, gate2, w_up, w_down, ln_g.reshape(1, D_MODEL), ln_b.reshape(1, D_MODEL))


def _moe_kernel(u_ref, x1_ref, g2_ref, cmb_ref, wa_ref, wg_ref, wd_ref, lg_ref, lb_ref, o_ref, acc_scr):
    e = pl.program_id(1)
    j = pl.program_id(2)

    @pl.when((e == 0) & (j == 0))
    def _():
        acc_scr[...] = jnp.zeros_like(acc_scr)

    u = u_ref[...]
    a = _dot(u, wa_ref[...])
    g = _dot(u, wg_ref[...])
    y = _dot(_bf(a * _sigmoid(a) * g), wd_ref[...])
    cmb = cmb_ref[...]
    lane = lax.broadcasted_iota(jnp.int32, cmb.shape, 1)
    ce = jnp.sum(jnp.where(lane == e, cmb, 0.0), axis=-1, keepdims=True)
    acc_scr[...] += ce * y

    @pl.when((e == pl.num_programs(1) - 1) & (j == pl.num_programs(2) - 1))
    def _():
        z = ALPHA * x1_ref[...] + g2_ref[...] * acc_scr[...]
        o_ref[...] = _layer_norm_rows(z, lg_ref[...], lb_ref[...])


def _ffn_experts(u2, x1, gate2, combine, w_in, w_out, ln_g, ln_b, tm, tiles_per_group, th):
    n = u2.shape[0]
    r = gate2.shape[1]
    nh = D_EXPERT // th
    return pl.pallas_call(
        _moe_kernel,
        out_shape=jax.ShapeDtypeStruct((n, D_MODEL), F32),
        grid=(n // tm, N_EXPERTS, nh),
        in_specs=[
            pl.BlockSpec((tm, D_MODEL), lambda i, e, j: (i, 0)),
            pl.BlockSpec((tm, D_MODEL), lambda i, e, j: (i, 0)),
            pl.BlockSpec((None, r, D_MODEL), lambda i, e, j: (i // tiles_per_group, 0, 0)),
            pl.BlockSpec((tm, LANES), lambda i, e, j: (i, 0)),
            pl.BlockSpec((None, D_MODEL, th), lambda i, e, j: (e, 0, j)),
            pl.BlockSpec((None, D_MODEL, th), lambda i, e, j: (e, 0, nh + j)),
            pl.BlockSpec((None, th, D_MODEL), lambda i, e, j: (e, j, 0)),
            pl.BlockSpec((1, D_MODEL), lambda i, e, j: (0, 0)),
            pl.BlockSpec((1, D_MODEL), lambda i, e, j: (0, 0)),
        ],
        out_specs=pl.BlockSpec((tm, D_MODEL), lambda i, e, j: (i, 0)),
        scratch_shapes=[pltpu.VMEM((tm, D_MODEL), F32)],
        compiler_params=_params(3),
        name="ffn_experts",
    )(u2, x1, gate2, combine, w_in, w_in, w_out, ln_g.reshape(1, D_MODEL), ln_b.reshape(1, D_MODEL))


EXPERT_TILE = 1024


def _rank_kernel(rr_ref, rank_ref, cnt_ref, carry_scr):
    b = pl.program_id(0)

    @pl.when(b == 0)
    def _():
        carry_scr[...] = jnp.zeros_like(carry_scr)

    rr = rr_ref[...]
    tm = rr.shape[1]
    i1 = rr[2:3]
    i2 = rr[3:4]
    e = lax.broadcasted_iota(jnp.int32, (N_EXPERTS, tm), 0).astype(F32)
    oh1 = jnp.where(e == i1, 1.0, 0.0)
    oh2 = jnp.where(e == i2, 1.0, 0.0)
    sel = oh1 + oh2
    r = lax.broadcasted_iota(jnp.int32, (tm, tm), 0)
    c = lax.broadcasted_iota(jnp.int32, (tm, tm), 1)
    triu = jnp.where(r <= c, 1.0, 0.0).astype(BF16)
    csum = _dot(_bf(sel), triu)
    carry = carry_scr[:, 0:1]
    rank = csum - sel + carry
    r1 = jnp.sum(oh1 * rank, axis=0, keepdims=True)
    r2 = jnp.sum(oh2 * rank, axis=0, keepdims=True)
    rec = jnp.concatenate([i1, i2, r1, r2, jnp.zeros((4, tm), F32)], axis=0)
    rank_ref[...] = rec.astype(jnp.int32)
    total = carry + csum[:, tm - 1:tm]
    carry_scr[...] = jnp.broadcast_to(total, carry_scr.shape)
    cnt_ref[...] = jnp.broadcast_to(total, cnt_ref.shape).astype(jnp.int32)


def _route_ranks(route_row):
    nblk, _, tm = route_row.shape
    return pl.pallas_call(
        _rank_kernel,
        out_shape=(jax.ShapeDtypeStruct((nblk, 8, tm), jnp.int32),
                   jax.ShapeDtypeStruct((N_EXPERTS, LANES), jnp.int32)),
        grid=(nblk,),
        in_specs=[pl.BlockSpec((None, 8, tm), lambda b: (b, 0, 0))],
        out_specs=(pl.BlockSpec((None, 8, tm), lambda b: (b, 0, 0)),
                   pl.BlockSpec((N_EXPERTS, LANES), lambda b: (0, 0))),
        scratch_shapes=[pltpu.VMEM((N_EXPERTS, LANES), F32)],
        compiler_params=_params(1),
        name="route_ranks",
    )(route_row)


def _row_copy(src_ref, src_row, dst_ref, dst_row, sem):
    return pltpu.make_async_copy(src_ref.at[pl.ds(src_row, 1)], dst_ref.at[pl.ds(dst_row, 1)], sem)


def _dispatch_kernel(last_ref, slot_ref, u_ref, xs_ref, zero_scr, sem):
    tm = u_ref.shape[0]
    gt = zero_scr.shape[0]

    @pl.when(pl.program_id(0) == 0)
    def _():
        zero_scr[...] = jnp.zeros_like(zero_scr)
        for e in range(N_EXPERTS):
            @pl.when(last_ref[e] >= 0)
            def _():
                first = pl.multiple_of(last_ref[e], gt)
                fill = pltpu.make_async_copy(zero_scr, xs_ref.at[pl.ds(first, gt)], sem)
                fill.start()
                fill.wait()

    def start(r, carry):
        for k in range(2):
            _row_copy(u_ref, r, xs_ref, slot_ref[k, r], sem).start()
        return carry

    lax.fori_loop(0, tm, start, 0, unroll=8)

    def wait(r, carry):
        for k in range(2):
            _row_copy(u_ref, 0, xs_ref, 0, sem).wait()
        return carry

    lax.fori_loop(0, tm, wait, 0, unroll=8)


def _dispatch(u2, slots, last_tile, n_slots, group_tile):
    n, d = u2.shape
    nblk, _, tm = slots.shape
    grid_spec = pltpu.PrefetchScalarGridSpec(
        num_scalar_prefetch=1,
        grid=(nblk,),
        in_specs=[
            pl.BlockSpec((None, 2, tm), lambda b, last: (b, 0, 0), memory_space=pltpu.SMEM),
            pl.BlockSpec((tm, d), lambda b, last: (b, 0)),
        ],
        out_specs=pl.BlockSpec(memory_space=pl.ANY),
        scratch_shapes=[pltpu.VMEM((group_tile, d), F32), pltpu.SemaphoreType.DMA],
    )
    return pl.pallas_call(
        _dispatch_kernel,
        out_shape=jax.ShapeDtypeStruct((n_slots, d), F32),
        grid_spec=grid_spec,
        compiler_params=_params(1),
        name="expert_dispatch",
    )(last_tile, slots, u2)


def _grouped_kernel(toff_ref, nt_ref, xs_ref, wi_ref, wo_ref, ys_ref, *, chunk):
    @pl.when(pl.program_id(0) < nt_ref[0])
    def _():
        ys_ref[...] = _swiglu_chunks(_bf(xs_ref[...]), wi_ref, wo_ref, D_EXPERT, chunk)


def _grouped_experts(xs, tile_off, n_tiles, w_in, w_out, tm, chunk):
    n_slots, d = xs.shape

    def expert(i, toff):
        e = 0
        for k in range(1, N_EXPERTS):
            e = e + (i >= toff[k]).astype(jnp.int32)
        return e

    def tile(i, nt):
        return jnp.minimum(i, nt[0] - 1)

    grid_spec = pltpu.PrefetchScalarGridSpec(
        num_scalar_prefetch=2,
        grid=(n_slots // tm,),
        in_specs=[
            pl.BlockSpec((tm, d), lambda i, toff, nt: (tile(i, nt), 0)),
            pl.BlockSpec((None, d, 2 * D_EXPERT), lambda i, toff, nt: (expert(tile(i, nt), toff), 0, 0),
                         pipeline_mode=pl.Buffered(1)),
            pl.BlockSpec((None, D_EXPERT, d), lambda i, toff, nt: (expert(tile(i, nt), toff), 0, 0),
                         pipeline_mode=pl.Buffered(1)),
        ],
        out_specs=pl.BlockSpec((tm, d), lambda i, toff, nt: (tile(i, nt), 0)),
    )
    return pl.pallas_call(
        functools.partial(_grouped_kernel, chunk=chunk),
        out_shape=jax.ShapeDtypeStruct((n_slots, d), F32),
        grid_spec=grid_spec,
        compiler_params=_params(1),
        name="grouped_experts",
    )(tile_off, n_tiles, xs, w_in, w_out)


def _combine_kernel(slot_ref, ys_ref, x1_ref, g2_ref, rcol_ref, lg_ref, lb_ref, o_ref, y1_scr, y2_scr, sem):
    tm = x1_ref.shape[0]
    bufs = (y1_scr, y2_scr)

    def start(r, carry):
        for k in range(2):
            _row_copy(ys_ref, slot_ref[k, r], bufs[k], r, sem).start()
        return carry

    lax.fori_loop(0, tm, start, 0, unroll=8)

    def wait(r, carry):
        for k in range(2):
            _row_copy(ys_ref, 0, bufs[k], 0, sem).wait()
        return carry

    lax.fori_loop(0, tm, wait, 0, unroll=8)
    rec = rcol_ref[...]
    f = rec[:, ROUTE_W1:ROUTE_W1 + 1] * y1_scr[...] + rec[:, ROUTE_W2:ROUTE_W2 + 1] * y2_scr[...]
    z = ALPHA * x1_ref[...] + g2_ref[...] * f
    o_ref[...] = _layer_norm_rows(z, lg_ref[...], lb_ref[...])


def _combine(ys, slots, x1, gate2, route_col, ln_g, ln_b, tiles_per_group):
    n, d = x1.shape
    nblk, _, tm = slots.shape
    r = gate2.shape[1]
    return pl.pallas_call(
        _combine_kernel,
        out_shape=jax.ShapeDtypeStruct((n, d), F32),
        grid=(nblk,),
        in_specs=[
            pl.BlockSpec((None, 2, tm), lambda b: (b, 0, 0), memory_space=pltpu.SMEM),
            pl.BlockSpec(memory_space=pl.ANY),
            pl.BlockSpec((tm, d), lambda b: (b, 0)),
            pl.BlockSpec((None, r, d), lambda b: (b // tiles_per_group, 0, 0)),
            pl.BlockSpec((tm, LANES), lambda b: (b, 0)),
            pl.BlockSpec((1, d), lambda b: (0, 0)),
            pl.BlockSpec((1, d), lambda b: (0, 0)),
        ],
        out_specs=pl.BlockSpec((tm, d), lambda b: (b, 0)),
        scratch_shapes=[pltpu.VMEM((tm, d), F32), pltpu.VMEM((tm, d), F32), pltpu.SemaphoreType.DMA],
        compiler_params=_params(1),
        name="expert_combine_ln",
    )(slots, ys, x1, gate2, route_col, ln_g.reshape(1, d), ln_b.reshape(1, d))


def _ffn_routed(u2, x1, gate2, route_col, route_row, w_in, w_out, ln_g, ln_b, tiles_per_group):
    n = u2.shape[0]
    tm = min(EXPERT_TILE, n)
    assert n % tm == 0
    ranks, counts = _route_ranks(route_row)
    cnt = counts[:, 0]
    padded = ((cnt + tm - 1) // tm) * tm
    off = (jnp.cumsum(padded) - padded).astype(jnp.int32)
    last_tile = jnp.where(padded > 0, off + padded - tm, -1).astype(jnp.int32)
    n_slots = 2 * n + N_EXPERTS * tm
    slots = off[ranks[:, 0:2, :]] + ranks[:, 2:4, :]
    xs = _dispatch(u2, slots, last_tile, n_slots, tm)
    ys = _grouped_experts(xs, (off // tm).astype(jnp.int32), (jnp.sum(padded) // tm).astype(jnp.int32).reshape(1),
                          w_in, w_out, tm, 512)
    return _combine(ys, slots, x1, gate2, route_col, ln_g, ln_b, tiles_per_group)


def _split_w_in(w_in_l):
    w = BRANCH_W
    col = lambda off, n: w_in_l[:, off:off + n]
    o = 0
    ret_q, ret_k, ret_v, ret_g = (col(o + i * w, w) for i in range(4)); o += 4 * w
    fox_q, fox_k, fox_v = (col(o + i * w, w) for i in range(3)); o += 3 * w
    fox_f = col(o, N_HEADS); o += N_HEADS
    conv = col(o, 2 * w); o += 2 * w
    diff_q, diff_k, diff_v = (col(o + i * w, w) for i in range(3)); o += 3 * w
    gate = col(o, 4 * D_MODEL)
    rot_r = lambda m: _rotated_columns(m, HEAD_DIM, HEAD_DIM)
    rot_d = lambda m: _rotated_columns(m, DIFF_SUB, ROT_DIM)
    w_ret = jnp.concatenate([ret_q, rot_r(ret_q), ret_k, rot_r(ret_k), ret_v, ret_g], axis=1)
    w_fox = jnp.concatenate([fox_q, fox_k, fox_v, fox_f, jnp.zeros((D_MODEL, LANES - N_HEADS), F32)], axis=1)
    w_diff = jnp.concatenate([diff_q, rot_d(diff_q), diff_k, rot_d(diff_k), diff_v], axis=1)
    return _bf(jnp.concatenate([w_ret, conv, w_fox, w_diff], axis=1)), _bf(gate)


def _block_diag_state(s):
    b = s.shape[0]
    eye = jnp.eye(N_HEADS, dtype=s.dtype)
    return jnp.einsum('bhde,hg->bhdge', s, eye).reshape(b, BRANCH_W, BRANCH_W)


def _diag_blocks(s_bd):
    b = s_bd.shape[0]
    s4 = s_bd.reshape(b, N_HEADS, HEAD_DIM, N_HEADS, HEAD_DIM)
    return jnp.stack([s4[:, h, :, h, :] for h in range(N_HEADS)], axis=1)


def _pad_rows(a, rows):
    return jnp.concatenate([a, jnp.zeros((a.shape[0], rows - a.shape[1]) + a.shape[2:], a.dtype)], axis=1)


def _trunk_layer(x, mod, pos, past, l, lw, sample, depth, kv_state):
    b, t, _ = x.shape
    d = D_MODEL
    w = BRANCH_W
    shift1, scale1, gate1 = (mod[0][:, None, i * d:(i + 1) * d] for i in range(3))
    shift2, scale2, gate2 = (mod[1][:, None, i * d:(i + 1) * d] for i in range(3))
    lam_init = 0.8 - 0.6 * math.exp(-0.3 * l)
    chunk = min(t, KV_BLOCK)
    pr = _in_projection(x, scale1, shift1, lw['w_mix'], lw['b_fox_f'], pos, min(t, 512), depth, l, kv_state)
    kv_state = pr['state']

    s0 = jnp.zeros((b, w, w), F32) if past is None else _block_diag_state(past[5])
    h_ret, s_bd = _retention(pr['rq'], pr['rk'], pr['rv'], pr['rg'], s0, chunk)
    ret_state = _diag_blocks(s_bd)

    if past is None:
        hist = jnp.zeros((b, HIST_ROWS, w), F32)
    else:
        hist = jnp.concatenate([jnp.zeros((b, HIST_ROWS - (CONV_W - 1), w), F32), past[6]], axis=1)
    h_conv, tail = _conv_mixer(pr['glu'], hist, lw['w_conv'], lw['b_conv'], lw['conv_ln_g'], lw['conv_ln_b'], chunk)
    conv_buf = tail[:, HIST_ROWS - (CONV_W - 1):, :]

    fq, fkb, fvb, lf = pr['fq'], pr['fkb'], pr['fvb'], pr['lf'][l]
    if past is None:
        off, tk_valid = 0, t
        k_all, v_all, lf_all, fox_new = fkb, fvb, lf, None
    else:
        p_len = past[0].shape[1]
        off, tk_valid = p_len, p_len + t
        tk_pad = p_len + KV_BLOCK
        k_all, v_all = past[0].reshape(b, p_len, w), past[1].reshape(b, p_len, w)
        fox_new = (_pad_rows(fkb, KV_BLOCK), _pad_rows(fvb, KV_BLOCK))
        past_lf = jnp.concatenate([past[2].astype(F32), jnp.zeros((b, p_len, LANES - N_HEADS), F32)], axis=2)
        lf_all = _pad_rows(jnp.concatenate([past_lf, lf], axis=1), tk_pad)
    fcol, frow = _logf_cumsum(lf_all)
    tq = min(t, KV_BLOCK)
    if past is None:
        h_fox = _fox_attention(fq, k_all, v_all, fcol, frow, tq, off, tk_valid)
    else:
        h_fox = _decode_attention(fq, k_all, v_all, fox_new, (fcol, frow), fox=True)

    dq, dkb, dvb = pr['dq'], pr['dkb'], pr['dvb']
    if past is None:
        dk_all, dv_all, diff_new = dkb, dvb, None
    else:
        dk_all, dv_all = past[3].reshape(b, p_len, w), past[4].reshape(b, p_len, w)
        diff_new = (_pad_rows(dkb, KV_BLOCK), _pad_rows(dvb, KV_BLOCK))
    if past is None:
        h_diff = _diff_attention(dq, dk_all, dv_all, lw['diff_lambda'], lw['diff_subln_g'], tq, off, tk_valid, lam_init)
    else:
        g_sub = jnp.tile(lw['diff_subln_g'].astype(F32), N_HEADS).reshape(1, w)
        h_diff = _decode_attention(dq, dk_all, dv_all, diff_new, (lw['diff_lambda'].astype(F32), g_sub), fox=False,
                                   lam_init=lam_init)

    n = b * t
    if sample:
        tm_merge = tm_ffn = min(n, 256)
        tpg_merge = tpg_ffn = 1
        rows_mod = lambda m, tm: jnp.repeat(m, t, axis=1).reshape(n // tm, tm, d)
    else:
        tm_merge, tm_ffn = min(t, 512), min(t, 1024)
        tpg_merge, tpg_ffn = t // tm_merge, t // tm_ffn
        rows_mod = lambda m, tm: m
    mods = tuple(rows_mod(m, tm_merge) for m in (scale1, shift1, gate1, scale2, shift2))
    branches = tuple(h.reshape(n, w) for h in (h_ret, h_fox, h_conv, h_diff))
    merged = _merge(x.reshape(n, d), mods, branches, lw['w_gate'], lw['w_branch'], lw['w_out'],
                    lw['ln_g'][0], lw['ln_b'][0], lw.get('router'), tm_merge, tpg_merge)
    g2 = rows_mod(gate2, tm_ffn)
    if 'router' in lw and not sample:
        x1, u2, route_col, route_row = merged
        x2 = _ffn_routed(u2, x1, rows_mod(gate2, tm_merge), route_col, route_row, lw['w_exp_in'], lw['w_exp_out'],
                         lw['ln_g'][1], lw['ln_b'][1], tpg_merge)
    elif 'router' in lw:
        x1, u2, route_col, _ = merged
        x2 = _ffn_experts(_bf(u2), x1, g2, route_col, lw['w_exp_in'], lw['w_exp_out'], lw['ln_g'][1], lw['ln_b'][1],
                          tm_ffn, tpg_ffn, 512)
    else:
        x1, u2 = merged
        x2 = _ffn_dense(u2, x1, g2, lw['w_ffn_in'], lw['w_ffn_out'], lw['ln_g'][1], lw['ln_b'][1],
                        tm_ffn, tpg_ffn, 512)
    return x2.reshape(b, t, d), kv_state, (ret_state, conv_buf)


def _state_outputs(kv_state, small):
    fk, fv, lf, dk, dv = kv_state
    depth, b, t, w = fk.shape
    heads = lambda a, nh: a.reshape(depth, b, t, nh, w // nh)
    ret_state, conv_buf = (jnp.stack(a) for a in zip(*small))
    return (heads(fk, N_HEADS), heads(fv, N_HEADS), lf[..., :N_HEADS], heads(dk, N_SUB), heads(dv, N_HEADS),
            ret_state, conv_buf)


def kernel(x_prompt, x_sample, c_prompt, c_sample, cache_fox_k, cache_fox_v, cache_fox_logf, cache_diff_k, cache_diff_v, state_ret, state_conv, w_in, b_fox_f, w_conv, b_conv, conv_ln_g, conv_ln_b, diff_lambda, diff_subln_g, w_branch, w_out, w_ada, b_ada, ln_g, ln_b, w_ffn_in, w_ffn_out, w_router, b_router, w_exp_in, w_exp_out):
    depth = w_in.shape[0]
    bp = x_prompt.shape[0]
    past_len = cache_fox_k.shape[2]
    pos_p = jnp.arange(x_prompt.shape[1], dtype=jnp.int32)
    pos_s = past_len + jnp.arange(x_sample.shape[1], dtype=jnp.int32)
    mod = _modulation(jnp.concatenate([c_prompt, c_sample], axis=0), w_ada, b_ada)
    yp, ys = x_prompt, x_sample
    kv_p = kv_s = None
    small_p, small_s = [], []
    for l in range(depth):
        w_mix, w_gate = _split_w_in(w_in[l])
        lw = dict(w_mix=w_mix, w_gate=w_gate,
                  b_fox_f=b_fox_f[l], w_conv=w_conv[l], b_conv=b_conv[l], conv_ln_g=conv_ln_g[l],
                  conv_ln_b=conv_ln_b[l], diff_lambda=diff_lambda[l], diff_subln_g=diff_subln_g[l],
                  w_branch=_bf(w_branch[l]), w_out=_bf(w_out[l]), ln_g=ln_g[l], ln_b=ln_b[l])
        if l % 2 == 0:
            lw['w_ffn_in'] = _bf(w_ffn_in[l // 2])
            lw['w_ffn_out'] = _bf(w_ffn_out[l // 2])
        else:
            wr = jnp.concatenate([w_router[l // 2], jnp.zeros((D_MODEL, LANES - N_EXPERTS), F32)], axis=1)
            br = jnp.concatenate([b_router[l // 2].astype(F32), jnp.zeros((LANES - N_EXPERTS,), F32)]).reshape(1, LANES)
            lw['router'] = (_bf(wr), br)
            lw['w_exp_in'] = _bf(w_exp_in[l // 2])
            lw['w_exp_out'] = _bf(w_exp_out[l // 2])
        past_l = (cache_fox_k[l], cache_fox_v[l], cache_fox_logf[l], cache_diff_k[l], cache_diff_v[l],
                  state_ret[l], state_conv[l])
        yp, kv_p, st_p = _trunk_layer(yp, mod[l][:, :bp], pos_p, None, l, lw, False, depth, kv_p)
        ys, kv_s, st_s = _trunk_layer(ys, mod[l][:, bp:], pos_s, past_l, l, lw, True, depth, kv_s)
        small_p.append(st_p)
        small_s.append(st_s)
    return (yp, ys) + _state_outputs(kv_p, small_p) + _state_outputs(kv_s, small_s)
```

```python
import functools
import math

import jax
import jax.numpy as jnp
from jax import lax
from jax.experimental import pallas as pl
from jax.experimental.pallas import tpu as pltpu

D_MODEL = 1024
BRANCH_W = 256
HEAD_DIM = 64
N_HEADS = 4
DIFF_SUB = 32
N_SUB = 8
ROT_DIM = DIFF_SUB // 4
RET_THETA = 10000.0
ROPE_THETA = 500000.0
CHUNK = 64
CONV_W = 31
D_FF = 2816
N_EXPERTS = 8
D_EXPERT = 3584
DEPTH = 2
ALPHA = (2.0 * DEPTH) ** 0.25
EPS = 1e-5
NEG = -1e30
LOG2E = math.log2(math.e)

LANES = 128
BF16_ROWS = 16
KV_BLOCK = 256
HIST_ROWS = 32
VMEM_LIMIT = 56 * 1024 * 1024

F32 = jnp.float32
BF16 = jnp.bfloat16


def _bf(x):
    return x.astype(BF16)


def _dot(a, b):
    return jnp.dot(a, b, preferred_element_type=F32)


def _dot_nt(a, b):
    return lax.dot_general(a, b, (((1,), (1,)), ((), ())), preferred_element_type=F32)


def _dot_tn(a, b):
    return lax.dot_general(a, b, (((0,), (0,)), ((), ())), preferred_element_type=F32)


def _sigmoid(x):
    return 1.0 / (1.0 + jnp.exp(-x))


def _params(n_axes):
    return pltpu.CompilerParams(dimension_semantics=("arbitrary",) * n_axes,
                                vmem_limit_bytes=VMEM_LIMIT)


def _head_sum(y, width):
    n = y.shape[-1]
    r = lax.broadcasted_iota(jnp.int32, (n, n), 0) // width
    c = lax.broadcasted_iota(jnp.int32, (n, n), 1) // width
    bd = jnp.where(r == c, 1.0, 0.0).astype(BF16)
    hi = _bf(y)
    lo = _bf(y - hi.astype(F32))
    return _dot(hi, bd) + _dot(lo, bd)


def _layer_norm_rows(z, g, b):
    mu = jnp.mean(z, axis=-1, keepdims=True)
    d = z - mu
    var = jnp.mean(d * d, axis=-1, keepdims=True)
    return d * lax.rsqrt(var + EPS) * g + b


def _mod_kernel(c_ref, w_ref, b_ref, o_ref):
    c = c_ref[...]
    sc = _bf(c * _sigmoid(c))
    o_ref[...] = _dot(sc, _bf(w_ref[...])) + b_ref[...]


def _modulation(c_all, w_ada, b_ada):
    rows = c_all.shape[0]
    depth = w_ada.shape[0]
    d3 = w_ada.shape[-1]
    nj = d3 // D_MODEL
    return pl.pallas_call(
        _mod_kernel,
        out_shape=jax.ShapeDtypeStruct((depth, 2, rows, d3), F32),
        grid=(depth * 2, nj),
        in_specs=[
            pl.BlockSpec((rows, D_MODEL), lambda i, j: (0, 0)),
            pl.BlockSpec((None, None, D_MODEL, D_MODEL), lambda i, j: (i // 2, i % 2, 0, j)),
            pl.BlockSpec((None, None, 1, D_MODEL), lambda i, j: (i // 2, i % 2, 0, j)),
        ],
        out_specs=pl.BlockSpec((None, None, rows, D_MODEL), lambda i, j: (i // 2, i % 2, 0, j)),
        compiler_params=_params(2),
        name="adaln_modulation",
    )(c_all, w_ada, b_ada.reshape(depth, 2, 1, d3))


def _ret_kernel(q_ref, k_ref, v_ref, g_ref, dmask_ref, qdec_ref, kdec_ref, cdec_ref, s0_ref,
                h_ref, sout_ref, s_scr):
    c = pl.program_id(1)

    @pl.when(c == 0)
    def _():
        s_scr[...] = s0_ref[...]

    w = BRANCH_W
    chunk = qdec_ref.shape[0]
    lane_head = lax.broadcasted_iota(jnp.int32, (1, w), 1) // HEAD_DIM
    r = lax.broadcasted_iota(jnp.int32, (w, w), 0) // HEAD_DIM
    cc = lax.broadcasted_iota(jnp.int32, (w, w), 1) // HEAD_DIM
    s_prev = s_scr[...]
    for c0 in range(0, q_ref.shape[0], chunk):
        q = q_ref[c0:c0 + chunk, :]
        kb = k_ref[c0:c0 + chunk, :]
        vb = v_ref[c0:c0 + chunk, :]
        g = g_ref[c0:c0 + chunk, :]
        y = jnp.zeros((chunk, w), F32)
        for h in range(N_HEADS):
            mh = lane_head == h
            qh = jnp.where(mh, q, jnp.zeros_like(q))
            a = _dot_nt(qh, kb) * dmask_ref[h]
            y = y + jnp.where(mh, _dot(_bf(a), vb), 0.0)
        y = y + _dot(_bf(q.astype(F32) * qdec_ref[...]), _bf(s_prev))
        kv = _dot_tn(_bf(kb.astype(F32) * kdec_ref[...]), vb)
        s_prev = cdec_ref[...] * s_prev + jnp.where(r == cc, kv, 0.0)
        mu = _head_sum(y, HEAD_DIM) * (1.0 / HEAD_DIM)
        d = y - mu
        var = _head_sum(d * d, HEAD_DIM) * (1.0 / HEAD_DIM)
        hn = d * lax.rsqrt(var + EPS)
        h_ref[c0:c0 + chunk, :] = _bf(hn * (g * _sigmoid(g)))
    s_scr[...] = s_prev
    sout_ref[...] = s_prev


def _retention_tables(chunk):
    log_g = jnp.log1p(-jnp.exp2(-5.0 - jnp.arange(N_HEADS, dtype=F32)))
    idx = jnp.arange(chunk, dtype=F32)
    dist = jnp.abs(idx[:, None] - idx[None, :])
    sub = jnp.arange(chunk) // CHUNK
    vis = sub[None, :] <= sub[:, None]
    dmask = jnp.where(vis[None], jnp.exp(log_g[:, None, None] * dist[None]), 0.0)
    lg_lane = jnp.repeat(log_g, HEAD_DIM)[None, :]
    qdec = jnp.exp(lg_lane * (idx[:, None] + 1.0))
    kdec = jnp.exp(lg_lane * (chunk - 1.0 - idx[:, None]))
    cdec = jnp.exp(lg_lane * chunk)
    return dmask.astype(F32), qdec, kdec, cdec


def _rope_tables(pos, dim, rot_dim, theta, n_rep):
    half = rot_dim // 2
    inv_freq = jnp.exp(-math.log(theta) * jnp.arange(half, dtype=F32) / half)
    ang = pos.astype(F32)[:, None] * inv_freq[None, :]
    t = pos.shape[0]
    cos = jnp.concatenate([jnp.cos(ang), jnp.cos(ang), jnp.ones((t, dim - rot_dim), F32)], axis=1)
    sin = jnp.concatenate([jnp.sin(ang), jnp.sin(ang), jnp.zeros((t, dim - rot_dim), F32)], axis=1)
    return jnp.tile(cos, (1, n_rep)), jnp.tile(sin, (1, n_rep))


def _rotated_columns(w, dim, rot_dim):
    half = rot_dim // 2
    k, n = w.shape
    wh = w.reshape(k, n // dim, dim)
    rot = jnp.concatenate([-wh[..., half:rot_dim], wh[..., :half], jnp.zeros_like(wh[..., rot_dim:])], axis=-1)
    return rot.reshape(k, n)


def _retention(q, k, v, g, s0_bd, chunk):
    b, t, w = q.shape
    rows = 2 * chunk if t % (2 * chunk) == 0 else chunk
    dmask, qdec, kdec, cdec = _retention_tables(chunk)
    full = lambda shape: pl.BlockSpec(shape, lambda i, j: (0,) * len(shape))
    row = pl.BlockSpec((None, rows, w), lambda i, j: (i, j, 0))
    return pl.pallas_call(
        _ret_kernel,
        out_shape=(jax.ShapeDtypeStruct((b, t, w), BF16), jax.ShapeDtypeStruct((b, w, w), F32)),
        grid=(b, t // rows),
        in_specs=[
            row, row, row, row,
            full((N_HEADS, chunk, chunk)),
            full((chunk, w)),
            full((chunk, w)),
            full((1, w)),
            pl.BlockSpec((None, w, w), lambda i, j: (i, 0, 0)),
        ],
        out_specs=(row, pl.BlockSpec((None, w, w), lambda i, j: (i, 0, 0))),
        scratch_shapes=[pltpu.VMEM((w, w), F32)],
        compiler_params=_params(2),
        name="retention_mixer",
    )(q, k, v, g, dmask, qdec, kdec, cdec, s0_bd)


def _conv_kernel(glu_ref, hist_ref, wc_ref, bc_ref, lg_ref, lb_ref, h_ref, tail_ref, xp_scr):
    c = pl.program_id(1)
    rows = glu_ref.shape[0]
    w = BRANCH_W
    pad = HIST_ROWS - (CONV_W - 1)

    @pl.when(c == 0)
    def _():
        xp_scr[0:HIST_ROWS, :] = hist_ref[...]

    xp_scr[HIST_ROWS:HIST_ROWS + rows, :] = glu_ref[...]
    sub = 8
    acc = jnp.zeros((rows, w), F32) + bc_ref[...]
    for rho in range(sub):
        z = None
        for m in range((pad + CONV_W - 1) // sub + 1):
            j = sub * m + rho - pad
            if not 0 <= j < CONV_W:
                continue
            span = rows if rho == 0 else rows + sub
            term = xp_scr[sub * m:sub * m + span, :] * wc_ref[j:j + 1, :]
            z = term if z is None else z + term
        acc = acc + z[rho:rho + rows, :]
    y = _layer_norm_rows(acc, lg_ref[...], lb_ref[...])
    h_ref[...] = _bf(y * _sigmoid(y))
    tail = xp_scr[rows:rows + HIST_ROWS, :]
    tail_ref[...] = tail
    xp_scr[0:HIST_ROWS, :] = tail


def _conv_mixer(glu, hist, w_conv, b_conv, ln_g, ln_b, chunk):
    b, t, w = glu.shape
    nc = t // chunk
    full = lambda shape: pl.BlockSpec(shape, lambda i, j: (0,) * len(shape))
    wc = jnp.concatenate([w_conv, jnp.zeros((HIST_ROWS - CONV_W, w), F32)], axis=0)
    return pl.pallas_call(
        _conv_kernel,
        out_shape=(jax.ShapeDtypeStruct((b, t, w), BF16), jax.ShapeDtypeStruct((b, HIST_ROWS, w), F32)),
        grid=(b, nc),
        in_specs=[
            pl.BlockSpec((None, chunk, w), lambda i, j: (i, j, 0)),
            pl.BlockSpec((None, HIST_ROWS, w), lambda i, j: (i, 0, 0)),
            full((HIST_ROWS, w)),
            full((1, w)),
            full((1, w)),
            full((1, w)),
        ],
        out_specs=(pl.BlockSpec((None, chunk, w), lambda i, j: (i, j, 0)),
                   pl.BlockSpec((None, HIST_ROWS, w), lambda i, j: (i, 0, 0))),
        scratch_shapes=[pltpu.VMEM((HIST_ROWS + chunk, w), F32)],
        compiler_params=_params(2),
        name="conv_mixer",
    )(glu, hist, wc, b_conv.reshape(1, w), ln_g.reshape(1, w), ln_b.reshape(1, w))


def _layer_slot(depth, layer, b, t, tm, width, dt, first):
    shape = jax.ShapeDtypeStruct((depth, b, t, width), dt)
    if first:
        return shape, pl.BlockSpec((depth, None, tm, width), lambda i, j: (0, i, j, 0))
    return shape, pl.BlockSpec((None, None, tm, width), lambda i, j: (layer, i, j, 0))


def _store_layer_rows(ref, rows):
    if len(ref.shape) == 3:
        for d in range(ref.shape[0]):
            ref[d] = rows
    else:
        ref[...] = rows


def _in_proj_kernel(x_ref, sc_ref, sh_ref, w_ref, bf_ref, rcos_ref, rsin_ref, dcos_ref, dsin_ref, *refs):
    (rq_ref, rk_ref, rv_ref, rg_ref, glu_ref, fq_ref, fk_ref, fv_ref, fkb_ref, fvb_ref, lf_ref,
     dq_ref, dk_ref, dv_ref, dkb_ref, dvb_ref) = refs[-16:]
    w = BRANCH_W
    u = _bf(x_ref[...] * (1.0 + sc_ref[...]) + sh_ref[...])
    o = 0
    p = _dot(u, w_ref[:, o:o + 6 * w])
    cos = rcos_ref[...]
    sin = rsin_ref[...]
    rq_ref[...] = _bf(p[:, 0:w] * cos + p[:, w:2 * w] * sin)
    rk_ref[...] = _bf((p[:, 2 * w:3 * w] * cos + p[:, 3 * w:4 * w] * sin) * (HEAD_DIM ** -0.5))
    rv_ref[...] = _bf(p[:, 4 * w:5 * w])
    rg_ref[...] = p[:, 5 * w:6 * w]
    o += 6 * w
    p = _dot(u, w_ref[:, o:o + 2 * w])
    glu_ref[...] = p[:, :w] * _sigmoid(p[:, w:])
    o += 2 * w
    p = _dot(u, w_ref[:, o:o + 3 * w + LANES])
    fq_ref[...] = _bf(p[:, 0:w] * (HEAD_DIM ** -0.5 * LOG2E))
    k = p[:, w:2 * w]
    v = p[:, 2 * w:3 * w]
    _store_layer_rows(fk_ref, k)
    _store_layer_rows(fv_ref, v)
    fkb_ref[...] = _bf(k)
    fvb_ref[...] = _bf(v)
    z = p[:, 3 * w:] + bf_ref[...]
    lf = jnp.minimum(z, 0.0) - jnp.log(1.0 + jnp.exp(-jnp.abs(z)))
    lane = lax.broadcasted_iota(jnp.int32, lf.shape, 1)
    _store_layer_rows(lf_ref, jnp.where(lane < N_HEADS, lf, 0.0))
    o += 3 * w + LANES
    p = _dot(u, w_ref[:, o:o + 5 * w])
    cos = dcos_ref[...]
    sin = dsin_ref[...]
    dq_ref[...] = _bf((p[:, 0:w] * cos + p[:, w:2 * w] * sin) * (DIFF_SUB ** -0.5 * LOG2E))
    k = p[:, 2 * w:3 * w] * cos + p[:, 3 * w:4 * w] * sin
    v = p[:, 4 * w:5 * w]
    _store_layer_rows(dk_ref, k)
    _store_layer_rows(dv_ref, v)
    dkb_ref[...] = _bf(k)
    dvb_ref[...] = _bf(v)


def _in_projection(x, scale, shift, w_all, b_f, pos, tm, depth, layer, state):
    b, t, _ = x.shape
    w = BRANCH_W
    nw = w_all.shape[1]
    bfp = jnp.concatenate([b_f.astype(F32), jnp.zeros((LANES - N_HEADS,), F32)]).reshape(1, LANES)
    rcos, rsin = _rope_tables(pos, HEAD_DIM, HEAD_DIM, RET_THETA, N_HEADS)
    dcos, dsin = _rope_tables(pos, DIFF_SUB, ROT_DIM, ROPE_THETA, N_SUB)
    row = lambda width: pl.BlockSpec((None, tm, width), lambda i, j: (i, j, 0))
    table = pl.BlockSpec((tm, w), lambda i, j: (j, 0))
    plain = lambda width, dt: (jax.ShapeDtypeStruct((b, t, width), dt), row(width))
    slot = lambda width: _layer_slot(depth, layer, b, t, tm, width, F32, state is None)
    outs = [plain(w, BF16), plain(w, BF16), plain(w, BF16), plain(w, F32), plain(w, F32),
            plain(w, BF16), slot(w), slot(w), plain(w, BF16), plain(w, BF16), slot(LANES),
            plain(w, BF16), slot(w), slot(w), plain(w, BF16), plain(w, BF16)]
    prior = () if state is None else tuple(state)
    n_in = 9
    aliases = {n_in: 6, n_in + 1: 7, n_in + 2: 10, n_in + 3: 12, n_in + 4: 13} if prior else {}
    res = pl.pallas_call(
        _in_proj_kernel,
        out_shape=tuple(o[0] for o in outs),
        grid=(b, t // tm),
        in_specs=[
            row(D_MODEL),
            pl.BlockSpec((None, 1, D_MODEL), lambda i, j: (i, 0, 0)),
            pl.BlockSpec((None, 1, D_MODEL), lambda i, j: (i, 0, 0)),
            pl.BlockSpec((D_MODEL, nw), lambda i, j: (0, 0), pipeline_mode=pl.Buffered(1)),
            pl.BlockSpec((1, LANES), lambda i, j: (0, 0)),
            table, table, table, table,
        ] + [pl.BlockSpec(memory_space=pl.ANY)] * len(prior),
        out_specs=tuple(o[1] for o in outs),
        input_output_aliases=aliases,
        compiler_params=_params(2),
        name="mixer_in_projection",
    )(x, scale, shift, w_all, bfp, rcos, rsin, dcos, dsin, *prior)
    names = ('rq', 'rk', 'rv', 'rg', 'glu', 'fq', 'fk', 'fv', 'fkb', 'fvb', 'lf', 'dq', 'dk', 'dv', 'dkb', 'dvb')
    pr = dict(zip(names, res))
    pr['state'] = (pr['fk'], pr['fv'], pr['lf'], pr['dk'], pr['dv'])
    return pr


def _cumsum_kernel(lf_ref, col_ref, row_ref):
    n = KV_BLOCK
    nb = lf_ref.shape[0] // n
    r = lax.broadcasted_iota(jnp.int32, (n, n), 0)
    c = lax.broadcasted_iota(jnp.int32, (n, n), 1)
    tri = jnp.where(c <= r, 1.0, 0.0).astype(BF16)
    carry = jnp.zeros((1, LANES), F32)
    for jb in range(nb):
        x = lf_ref[jb * n:(jb + 1) * n, :]
        hi = _bf(x)
        r1 = x - hi.astype(F32)
        mid = _bf(r1)
        lo = _bf(r1 - mid.astype(F32))
        cs = _dot(tri, hi) + _dot(tri, mid) + _dot(tri, lo) + carry
        carry = cs[n - 1:n, :]
        scaled = cs * LOG2E
        col_ref[jb * n:(jb + 1) * n, :] = scaled
        row_ref[jb] = scaled.T[0:8, :]


def _logf_cumsum(lf):
    b, tk, _ = lf.shape
    nb = tk // KV_BLOCK
    return pl.pallas_call(
        _cumsum_kernel,
        out_shape=(jax.ShapeDtypeStruct((b, tk, LANES), F32),
                   jax.ShapeDtypeStruct((b, nb, 8, KV_BLOCK), F32)),
        grid=(b,),
        in_specs=[pl.BlockSpec((None, tk, LANES), lambda i: (i, 0, 0))],
        out_specs=(pl.BlockSpec((None, tk, LANES), lambda i: (i, 0, 0)),
                   pl.BlockSpec((None, nb, 8, KV_BLOCK), lambda i: (i, 0, 0, 0))),
        compiler_params=_params(1),
        name="logf_cumsum",
    )(lf)


def _attn_kernel(*refs, fox, off, tk_valid, tq, lam_init):
    if fox:
        q_ref, k_ref, v_ref, fc_ref, fr_ref, o_ref, vt_scr, qm_scr, mb_scr, m_scr, r_scr, acc_scr, s_scr = refs
    else:
        q_ref, k_ref, v_ref, lam_ref, g_ref, o_ref, vt_scr, qm_scr, mb_scr, m_scr, r_scr, acc_scr, s_scr = refs
    i = pl.program_id(1)
    w = BRANCH_W
    n_sub, tqp, _ = qm_scr.shape
    sub_w = w // n_sub
    subs_per_head = n_sub // N_HEADS
    nb = vt_scr.shape[0]
    hd = HEAD_DIM
    q0 = off + i * tq
    nfull = q0 // KV_BLOCK

    @pl.when(i == 0)
    def _():
        for jb in range(nb):
            vt = v_ref[jb * KV_BLOCK:(jb + 1) * KV_BLOCK, :].T
            for h in range(N_HEADS):
                vt_scr[jb, h, 0:hd, :] = vt[h * hd:(h + 1) * hd, :]
                vt_scr[jb, h, hd:, :] = jnp.ones((vt_scr.shape[2] - hd, KV_BLOCK), BF16)
        kpos = lax.broadcasted_iota(jnp.int32, (KV_BLOCK, tqp), 0)
        qpos = lax.broadcasted_iota(jnp.int32, (KV_BLOCK, tqp), 1)
        if fox:
            vis = kpos <= qpos
        else:
            vis = (kpos // CHUNK) <= (qpos // CHUNK)
        vis = vis & (kpos < tk_valid - nfull * KV_BLOCK)
        mb_scr[...] = jnp.where(vis, 0.0, NEG)

    q = q_ref[...]
    lane_sub = lax.broadcasted_iota(jnp.int32, (1, w), 1) // sub_w
    if tqp > tq:
        qm_scr[...] = jnp.zeros_like(qm_scr)
    for n in range(n_sub):
        qm_scr[n, 0:tq, :] = jnp.where(lane_sub == n, q, jnp.zeros_like(q))
    m_scr[...] = jnp.full(m_scr.shape, NEG, F32)
    r_scr[...] = jnp.full(r_scr.shape, NEG, F32)
    acc_scr[...] = jnp.zeros_like(acc_scr)

    def scores(j, n, masked):
        start = pl.multiple_of(j * KV_BLOCK, KV_BLOCK)
        h = n // subs_per_head
        s = _dot_nt(k_ref[pl.ds(start, KV_BLOCK), :], qm_scr[n])
        if fox:
            s = s + (fr_ref[h:h + 1, 0:tqp] - fc_ref[pl.ds(start, KV_BLOCK), h:h + 1])
        if masked:
            s = s + mb_scr[...]
        s_scr[n] = s
        m_scr[n] = jnp.maximum(m_scr[n], jnp.max(s, axis=0, keepdims=True))

    def values(j, n):
        h = n // subs_per_head
        m = m_scr[n]
        alpha = jnp.exp2(r_scr[n] - m)
        r_scr[n] = m
        p = jnp.exp2(s_scr[n] - m)
        acc_scr[n] = alpha * acc_scr[n] + _dot(vt_scr[j, h], _bf(p))

    @pl.when(nfull == 0)
    def _():
        for n in range(n_sub):
            scores(0, n, True)

    @pl.when(nfull > 0)
    def _():
        for n in range(n_sub):
            scores(0, n, False)

    def advance(j):
        for n in range(n_sub):
            values(j - 1, n)
            scores(j, n, False)

    n_mid = jnp.maximum(nfull - 1, 0)

    def body(i2, carry):
        advance(1 + 2 * i2)
        advance(2 + 2 * i2)
        return carry

    lax.fori_loop(0, n_mid // 2, body, 0)

    @pl.when(n_mid % 2 == 1)
    def _():
        advance(nfull - 1)

    @pl.when(nfull > 0)
    def _():
        for n in range(n_sub):
            values(nfull - 1, n)
            scores(nfull, n, True)

    for n in range(n_sub):
        values(nfull, n)

    def normalised(n):
        acc = acc_scr[n]
        return acc[0:hd] / acc[hd:hd + 1]

    if fox:
        out_t = jnp.concatenate([normalised(h) for h in range(N_HEADS)], axis=0)
        o_ref[...] = _bf(out_t.T[0:tq, :])
    else:
        lamv = lam_ref[...]
        lam = (jnp.exp(jnp.sum(lamv[0:1] * lamv[1:2], axis=-1, keepdims=True))
               - jnp.exp(jnp.sum(lamv[2:3] * lamv[3:4], axis=-1, keepdims=True)) + lam_init)
        parts = [normalised(2 * h) - lam * normalised(2 * h + 1) for h in range(N_HEADS)]
        dy = jnp.concatenate(parts, axis=0).T[0:tq, :]
        ms = _head_sum(dy * dy, HEAD_DIM) * (1.0 / HEAD_DIM)
        o_ref[...] = _bf(dy * lax.rsqrt(ms + EPS) * g_ref[...] * (1.0 - lam_init))


def _attention(q, kb, vb, extras, *, fox, tq, off, tk_valid, lam_init=0.0):
    b, t, w = q.shape
    tk = kb.shape[1]
    nb = tk // KV_BLOCK
    tqp = max(tq, LANES)
    n_sub = N_HEADS if fox else N_SUB
    assert off % KV_BLOCK == 0 and (tq == KV_BLOCK or t == tq)
    kernel = functools.partial(_attn_kernel, fox=fox, off=off, tk_valid=tk_valid, tq=tq, lam_init=lam_init)
    in_specs = [
        pl.BlockSpec((None, tq, w), lambda i, j: (i, j, 0)),
        pl.BlockSpec((None, tk, w), lambda i, j: (i, 0, 0)),
        pl.BlockSpec((None, tk, w), lambda i, j: (i, 0, 0)),
    ]
    if fox:
        in_specs += [
            pl.BlockSpec((None, tk, LANES), lambda i, j: (i, 0, 0)),
            pl.BlockSpec((None, None, 8, KV_BLOCK), lambda i, j: (i, (off + j * tq) // KV_BLOCK, 0, 0)),
        ]
    else:
        in_specs += [
            pl.BlockSpec((4, DIFF_SUB), lambda i, j: (0, 0)),
            pl.BlockSpec((1, w), lambda i, j: (0, 0)),
        ]
    return pl.pallas_call(
        kernel,
        out_shape=jax.ShapeDtypeStruct((b, t, w), BF16),
        grid=(b, t // tq),
        in_specs=in_specs,
        out_specs=pl.BlockSpec((None, tq, w), lambda i, j: (i, j, 0)),
        scratch_shapes=[
            pltpu.VMEM((nb, N_HEADS, HEAD_DIM + BF16_ROWS, KV_BLOCK), BF16),
            pltpu.VMEM((n_sub, tqp, w), BF16),
            pltpu.VMEM((KV_BLOCK, tqp), F32),
            pltpu.VMEM((n_sub, 1, tqp), F32),
            pltpu.VMEM((n_sub, 1, tqp), F32),
            pltpu.VMEM((n_sub, HEAD_DIM + BF16_ROWS, tqp), F32),
            pltpu.VMEM((n_sub, KV_BLOCK, tqp), F32),
        ],
        compiler_params=_params(2),
        name="fox_attention" if fox else "diff_attention",
    )(q, kb, vb, *extras)


def _fox_attention(q, kb, vb, fcol, frow, tq, off, tk_valid):
    return _attention(q, kb, vb, (fcol, frow), fox=True, tq=tq, off=off, tk_valid=tk_valid)


def _diff_attention(q, kb, vb, diff_lambda, subln_g, tq, off, tk_valid, lam_init):
    g = jnp.tile(subln_g.astype(F32), N_HEADS).reshape(1, BRANCH_W)
    return _attention(q, kb, vb, (diff_lambda.astype(F32), g), fox=False, tq=tq, off=off,
                      tk_valid=tk_valid, lam_init=lam_init)


def _decode_attn_kernel(*refs, fox, p_len, lam_init):
    if fox:
        q_ref, kc_ref, vc_ref, kn_ref, vn_ref, fc_ref, fr_ref, o_ref = refs
    else:
        q_ref, kc_ref, vc_ref, kn_ref, vn_ref, lam_ref, g_ref, o_ref = refs
    w = BRANCH_W
    t = q_ref.shape[0]
    n_sub = N_HEADS if fox else N_SUB
    sub_w = w // n_sub
    rows = n_sub * t
    q = q_ref[...]
    lane = lax.broadcasted_iota(jnp.int32, (1, w), 1)
    qs = jnp.concatenate([jnp.where(lane // sub_w == n, q, jnp.zeros_like(q)) for n in range(n_sub)], axis=0)
    if fox:
        fcn = fc_ref[...]
        fq = jnp.concatenate([fcn[:, h:h + 1] for h in range(N_HEADS)], axis=0)

    def step(kb, vb, fk, mask, carry):
        m, l, acc = carry
        s = _dot_nt(qs, kb)
        if fox:
            fk_rows = jnp.concatenate([jnp.broadcast_to(fk[h:h + 1, :], (t, KV_BLOCK)) for h in range(N_HEADS)],
                                      axis=0)
            s = s + (fq - fk_rows)
        if mask is not None:
            s = jnp.where(mask, s, NEG)
        m_new = jnp.maximum(m, jnp.max(s, axis=-1, keepdims=True))
        alpha = jnp.exp2(m - m_new)
        p = jnp.exp2(s - m_new)
        l = alpha * l + jnp.sum(p, axis=-1, keepdims=True)
        acc = alpha * acc + _dot(_bf(p), vb)
        return m_new, l, acc

    def cached(j, carry):
        start = pl.multiple_of(j * KV_BLOCK, KV_BLOCK)
        fk = fr_ref[j] if fox else None
        return step(_bf(kc_ref[pl.ds(start, KV_BLOCK), :]), _bf(vc_ref[pl.ds(start, KV_BLOCK), :]), fk, None, carry)

    init = (jnp.full((rows, 1), NEG, F32), jnp.zeros((rows, 1), F32), jnp.zeros((rows, w), F32))
    nbc = p_len // KV_BLOCK
    carry = lax.fori_loop(0, nbc, cached, init)
    qpos = p_len + lax.broadcasted_iota(jnp.int32, (rows, KV_BLOCK), 0) % t
    kpos = p_len + lax.broadcasted_iota(jnp.int32, (rows, KV_BLOCK), 1)
    vis = (kpos <= qpos) if fox else ((kpos // CHUNK) <= (qpos // CHUNK))
    vis = vis & (kpos < p_len + t)
    m, l, acc = step(kn_ref[...], vn_ref[...], fr_ref[nbc] if fox else None, vis, carry)
    o = acc / l
    lane_head = lane // HEAD_DIM
    if fox:
        out = sum(jnp.where(lane_head == h, o[h * t:(h + 1) * t], 0.0) for h in range(N_HEADS))
        o_ref[...] = _bf(out)
    else:
        lamv = lam_ref[...]
        lam = (jnp.exp(jnp.sum(lamv[0:1] * lamv[1:2], axis=-1, keepdims=True))
               - jnp.exp(jnp.sum(lamv[2:3] * lamv[3:4], axis=-1, keepdims=True)) + lam_init)
        dy = sum(jnp.where(lane_head == h, o[2 * h * t:(2 * h + 1) * t] - lam * o[(2 * h + 1) * t:(2 * h + 2) * t], 0.0)
                 for h in range(N_HEADS))
        ms = _head_sum(dy * dy, HEAD_DIM) * (1.0 / HEAD_DIM)
        o_ref[...] = _bf(dy * lax.rsqrt(ms + EPS) * g_ref[...] * (1.0 - lam_init))


def _decode_attention(q, k_cache, v_cache, new_kv, extras, *, fox, lam_init=0.0):
    b, t, w = q.shape
    p_len = k_cache.shape[1]
    assert p_len % KV_BLOCK == 0 and t <= KV_BLOCK and t % 8 == 0 and p_len % t == 0
    nb = p_len // KV_BLOCK + 1
    in_specs = [
        pl.BlockSpec((None, t, w), lambda i: (i, 0, 0)),
        pl.BlockSpec((None, p_len, w), lambda i: (i, 0, 0)),
        pl.BlockSpec((None, p_len, w), lambda i: (i, 0, 0)),
        pl.BlockSpec((None, KV_BLOCK, w), lambda i: (i, 0, 0)),
        pl.BlockSpec((None, KV_BLOCK, w), lambda i: (i, 0, 0)),
    ]
    if fox:
        in_specs += [
            pl.BlockSpec((None, t, LANES), lambda i: (i, p_len // t, 0)),
            pl.BlockSpec((None, nb, 8, KV_BLOCK), lambda i: (i, 0, 0, 0)),
        ]
    else:
        in_specs += [pl.BlockSpec((4, DIFF_SUB), lambda i: (0, 0)), pl.BlockSpec((1, w), lambda i: (0, 0))]
    return pl.pallas_call(
        functools.partial(_decode_attn_kernel, fox=fox, p_len=p_len, lam_init=lam_init),
        out_shape=jax.ShapeDtypeStruct((b, t, w), BF16),
        grid=(b,),
        in_specs=in_specs,
        out_specs=pl.BlockSpec((None, t, w), lambda i: (i, 0, 0)),
        compiler_params=_params(1),
        name="fox_decode_attention" if fox else "diff_decode_attention",
    )(q, k_cache, v_cache, *new_kv, *extras)


ROUTE_W1, ROUTE_W2, ROUTE_I1, ROUTE_I2 = 8, 9, 10, 11


def _top2_route(logits):
    lane = lax.broadcasted_iota(jnp.int32, logits.shape, 1).astype(F32)
    lg = jnp.where(lane < N_EXPERTS, logits, -jnp.inf)
    v1 = jnp.max(lg, axis=-1, keepdims=True)
    i1 = jnp.min(jnp.where(lg == v1, lane, float(LANES)), axis=-1, keepdims=True)
    lg2 = jnp.where(lane == i1, -jnp.inf, lg)
    v2 = jnp.max(lg2, axis=-1, keepdims=True)
    i2 = jnp.min(jnp.where(lg2 == v2, lane, float(LANES)), axis=-1, keepdims=True)
    e2 = jnp.exp(v2 - v1)
    w1 = 1.0 / (1.0 + e2)
    w2 = e2 / (1.0 + e2)
    rec = jnp.where(lane == i1, w1, 0.0) + jnp.where(lane == i2, w2, 0.0)
    for slot, val in ((ROUTE_W1, w1), (ROUTE_W2, w2), (ROUTE_I1, i1), (ROUTE_I2, i2)):
        rec = rec + jnp.where(lane == float(slot), val, 0.0)
    return rec


def _merge_kernel(*refs, with_router):
    if with_router:
        (x_ref, sc1_ref, sh1_ref, g1_ref, sc2_ref, sh2_ref, hr_ref, hf_ref, hc_ref, hd_ref,
         wg_ref, wb_ref, wo_ref, lg_ref, lb_ref, wr_ref, br_ref, x1_ref, u2_ref, rcol_ref, rrow_ref) = refs
    else:
        (x_ref, sc1_ref, sh1_ref, g1_ref, sc2_ref, sh2_ref, hr_ref, hf_ref, hc_ref, hd_ref,
         wg_ref, wb_ref, wo_ref, lg_ref, lb_ref, x1_ref, u2_ref) = refs
    x = x_ref[...]
    u = _bf(x * (1.0 + sc1_ref[...]) + sh1_ref[...])
    merged = None
    for n, h_ref in enumerate((hr_ref, hf_ref, hc_ref, hd_ref)):
        gate = _dot(u, wg_ref[:, n * D_MODEL:(n + 1) * D_MODEL])
        term = _sigmoid(gate) * _dot(h_ref[...], wb_ref[n])
        merged = term if merged is None else merged + term
    mix = _dot(_bf(merged), wo_ref[...])
    x1 = _layer_norm_rows(ALPHA * x + g1_ref[...] * mix, lg_ref[...], lb_ref[...])
    x1_ref[...] = x1
    u2 = x1 * (1.0 + sc2_ref[...]) + sh2_ref[...]
    u2_ref[...] = u2.astype(u2_ref.dtype)
    if with_router:
        rec = _top2_route(_dot(_bf(u2), wr_ref[...]) + br_ref[...])
        rcol_ref[...] = rec
        rrow_ref[...] = rec.T[8:16, :]


def _merge(x2d, mods, branches, w_gate, w_branch, w_out, ln_g, ln_b, router, tm, tiles_per_group):
    n = x2d.shape[0]
    r = mods[0].shape[1]
    w = BRANCH_W
    with_router = router is not None
    row = lambda width: pl.BlockSpec((tm, width), lambda i: (i, 0))
    mod_spec = pl.BlockSpec((None, r, D_MODEL), lambda i: (i // tiles_per_group, 0, 0))
    full = lambda shape: pl.BlockSpec(shape, lambda i: (0,) * len(shape))
    resident = lambda shape: pl.BlockSpec(shape, lambda i: (0,) * len(shape), pipeline_mode=pl.Buffered(1))
    in_specs = ([row(D_MODEL)] + [mod_spec] * 5 + [row(w)] * 4
                + [resident((D_MODEL, 4 * D_MODEL)), resident((4, w, D_MODEL)), resident((D_MODEL, D_MODEL)),
                   full((1, D_MODEL)), full((1, D_MODEL))])
    args = [x2d, *mods, *branches, w_gate, w_branch, w_out, ln_g.reshape(1, D_MODEL), ln_b.reshape(1, D_MODEL)]
    out_shape = [jax.ShapeDtypeStruct((n, D_MODEL), F32),
                 jax.ShapeDtypeStruct((n, D_MODEL), F32 if with_router else BF16)]
    out_specs = [row(D_MODEL), row(D_MODEL)]
    if with_router:
        in_specs += [full((D_MODEL, LANES)), full((1, LANES))]
        args += list(router)
        out_shape += [jax.ShapeDtypeStruct((n, LANES), F32), jax.ShapeDtypeStruct((n // tm, 8, tm), F32)]
        out_specs += [row(LANES), pl.BlockSpec((None, 8, tm), lambda i: (i, 0, 0))]
    return pl.pallas_call(
        functools.partial(_merge_kernel, with_router=with_router),
        out_shape=tuple(out_shape),
        grid=(n // tm,),
        in_specs=in_specs,
        out_specs=tuple(out_specs),
        compiler_params=_params(1),
        name="merge_outproj_ln",
    )(*args)


def _swiglu_chunks(u, w_up_ref, w_down_ref, hidden, chunk):
    acc = None
    for c0 in range(0, hidden, chunk):
        c1 = min(c0 + chunk, hidden)
        a = _dot(u, w_up_ref[:, c0:c1])
        g = _dot(u, w_up_ref[:, hidden + c0:hidden + c1])
        y = _dot(_bf(a * _sigmoid(a) * g), w_down_ref[c0:c1, :])
        acc = y if acc is None else acc + y
    return acc


def _ffn_kernel(u_ref, x1_ref, g2_ref, wu_ref, wd_ref, lg_ref, lb_ref, o_ref, *, chunk):
    f = _swiglu_chunks(u_ref[...], wu_ref, wd_ref, D_FF, chunk)
    z = ALPHA * x1_ref[...] + g2_ref[...] * f
    o_ref[...] = _layer_norm_rows(z, lg_ref[...], lb_ref[...])


def _ffn_dense(u2, x1, gate2, w_up, w_down, ln_g, ln_b, tm, tiles_per_group, chunk):
    n = u2.shape[0]
    r = gate2.shape[1]
    resident = lambda shape: pl.BlockSpec(shape, lambda i: (0,) * len(shape), pipeline_mode=pl.Buffered(1))
    return pl.pallas_call(
        functools.partial(_ffn_kernel, chunk=chunk),
        out_shape=jax.ShapeDtypeStruct((n, D_MODEL), F32),
        grid=(n // tm,),
        in_specs=[
            pl.BlockSpec((tm, D_MODEL), lambda i: (i, 0)),
            pl.BlockSpec((tm, D_MODEL), lambda i: (i, 0)),
            pl.BlockSpec((None, r, D_MODEL), lambda i: (i // tiles_per_group, 0, 0)),
            resident((D_MODEL, 2 * D_FF)),
            resident((D_FF, D_MODEL)),
            pl.BlockSpec((1, D_MODEL), lambda i: (0, 0)),
            pl.BlockSpec((1, D_MODEL), lambda i: (0, 0)),
        ],
        out_specs=pl.BlockSpec((tm, D_MODEL), lambda i: (i, 0)),
        compiler_params=_params(1),
        name="ffn_dense",
    )(u2, x1, gate2, w_up, w_down, ln_g.reshape(1, D_MODEL), ln_b.reshape(1, D_MODEL))


def _moe_kernel(u_ref, x1_ref, g2_ref, cmb_ref, wa_ref, wg_ref, wd_ref, lg_ref, lb_ref, o_ref, acc_scr):
    e = pl.program_id(1)
    j = pl.program_id(2)

    @pl.when((e == 0) & (j == 0))
    def _():
        acc_scr[...] = jnp.zeros_like(acc_scr)

    u = u_ref[...]
    a = _dot(u, wa_ref[...])
    g = _dot(u, wg_ref[...])
    y = _dot(_bf(a * _sigmoid(a) * g), wd_ref[...])
    cmb = cmb_ref[...]
    lane = lax.broadcasted_iota(jnp.int32, cmb.shape, 1)
    ce = jnp.sum(jnp.where(lane == e, cmb, 0.0), axis=-1, keepdims=True)
    acc_scr[...] += ce * y

    @pl.when((e == pl.num_programs(1) - 1) & (j == pl.num_programs(2) - 1))
    def _():
        z = ALPHA * x1_ref[...] + g2_ref[...] * acc_scr[...]
        o_ref[...] = _layer_norm_rows(z, lg_ref[...], lb_ref[...])


def _ffn_experts(u2, x1, gate2, combine, w_in, w_out, ln_g, ln_b, tm, tiles_per_group, th):
    n = u2.shape[0]
    r = gate2.shape[1]
    nh = D_EXPERT // th
    return pl.pallas_call(
        _moe_kernel,
        out_shape=jax.ShapeDtypeStruct((n, D_MODEL), F32),
        grid=(n // tm, N_EXPERTS, nh),
        in_specs=[
            pl.BlockSpec((tm, D_MODEL), lambda i, e, j: (i, 0)),
            pl.BlockSpec((tm, D_MODEL), lambda i, e, j: (i, 0)),
            pl.BlockSpec((None, r, D_MODEL), lambda i, e, j: (i // tiles_per_group, 0, 0)),
            pl.BlockSpec((tm, LANES), lambda i, e, j: (i, 0)),
            pl.BlockSpec((None, D_MODEL, th), lambda i, e, j: (e, 0, j)),
            pl.BlockSpec((None, D_MODEL, th), lambda i, e, j: (e, 0, nh + j)),
            pl.BlockSpec((None, th, D_MODEL), lambda i, e, j: (e, j, 0)),
            pl.BlockSpec((1, D_MODEL), lambda i, e, j: (0, 0)),
            pl.BlockSpec((1, D_MODEL), lambda i, e, j: (0, 0)),
        ],
        out_specs=pl.BlockSpec((tm, D_MODEL), lambda i, e, j: (i, 0)),
        scratch_shapes=[pltpu.VMEM((tm, D_MODEL), F32)],
        compiler_params=_params(3),
        name="ffn_experts",
    )(u2, x1, gate2, combine, w_in, w_in, w_out, ln_g.reshape(1, D_MODEL), ln_b.reshape(1, D_MODEL))


EXPERT_TILE = 1024


def _rank_kernel(rr_ref, rank_ref, cnt_ref, carry_scr):
    b = pl.program_id(0)

    @pl.when(b == 0)
    def _():
        carry_scr[...] = jnp.zeros_like(carry_scr)

    rr = rr_ref[...]
    tm = rr.shape[1]
    i1 = rr[2:3]
    i2 = rr[3:4]
    e = lax.broadcasted_iota(jnp.int32, (N_EXPERTS, tm), 0).astype(F32)
    oh1 = jnp.where(e == i1, 1.0, 0.0)
    oh2 = jnp.where(e == i2, 1.0, 0.0)
    sel = oh1 + oh2
    r = lax.broadcasted_iota(jnp.int32, (tm, tm), 0)
    c = lax.broadcasted_iota(jnp.int32, (tm, tm), 1)
    triu = jnp.where(r <= c, 1.0, 0.0).astype(BF16)
    csum = _dot(_bf(sel), triu)
    carry = carry_scr[:, 0:1]
    rank = csum - sel + carry
    r1 = jnp.sum(oh1 * rank, axis=0, keepdims=True)
    r2 = jnp.sum(oh2 * rank, axis=0, keepdims=True)
    rec = jnp.concatenate([i1, i2, r1, r2, jnp.zeros((4, tm), F32)], axis=0)
    rank_ref[...] = rec.astype(jnp.int32)
    total = carry + csum[:, tm - 1:tm]
    carry_scr[...] = jnp.broadcast_to(total, carry_scr.shape)
    cnt_ref[...] = jnp.broadcast_to(total, cnt_ref.shape).astype(jnp.int32)


def _route_ranks(route_row):
    nblk, _, tm = route_row.shape
    return pl.pallas_call(
        _rank_kernel,
        out_shape=(jax.ShapeDtypeStruct((nblk, 8, tm), jnp.int32),
                   jax.ShapeDtypeStruct((N_EXPERTS, LANES), jnp.int32)),
        grid=(nblk,),
        in_specs=[pl.BlockSpec((None, 8, tm), lambda b: (b, 0, 0))],
        out_specs=(pl.BlockSpec((None, 8, tm), lambda b: (b, 0, 0)),
                   pl.BlockSpec((N_EXPERTS, LANES), lambda b: (0, 0))),
        scratch_shapes=[pltpu.VMEM((N_EXPERTS, LANES), F32)],
        compiler_params=_params(1),
        name="route_ranks",
    )(route_row)


def _row_copy(src_ref, src_row, dst_ref, dst_row, sem):
    return pltpu.make_async_copy(src_ref.at[pl.ds(src_row, 1)], dst_ref.at[pl.ds(dst_row, 1)], sem)


def _dispatch_kernel(last_ref, slot_ref, u_ref, xs_ref, zero_scr, sem):
    tm = u_ref.shape[0]
    gt = zero_scr.shape[0]

    @pl.when(pl.program_id(0) == 0)
    def _():
        zero_scr[...] = jnp.zeros_like(zero_scr)
        for e in range(N_EXPERTS):
            @pl.when(last_ref[e] >= 0)
            def _():
                first = pl.multiple_of(last_ref[e], gt)
                fill = pltpu.make_async_copy(zero_scr, xs_ref.at[pl.ds(first, gt)], sem)
                fill.start()
                fill.wait()

    def start(r, carry):
        for k in range(2):
            _row_copy(u_ref, r, xs_ref, slot_ref[k, r], sem).start()
        return carry

    lax.fori_loop(0, tm, start, 0, unroll=8)

    def wait(r, carry):
        for k in range(2):
            _row_copy(u_ref, 0, xs_ref, 0, sem).wait()
        return carry

    lax.fori_loop(0, tm, wait, 0, unroll=8)


def _dispatch(u2, slots, last_tile, n_slots, group_tile):
    n, d = u2.shape
    nblk, _, tm = slots.shape
    grid_spec = pltpu.PrefetchScalarGridSpec(
        num_scalar_prefetch=1,
        grid=(nblk,),
        in_specs=[
            pl.BlockSpec((None, 2, tm), lambda b, last: (b, 0, 0), memory_space=pltpu.SMEM),
            pl.BlockSpec((tm, d), lambda b, last: (b, 0)),
        ],
        out_specs=pl.BlockSpec(memory_space=pl.ANY),
        scratch_shapes=[pltpu.VMEM((group_tile, d), F32), pltpu.SemaphoreType.DMA],
    )
    return pl.pallas_call(
        _dispatch_kernel,
        out_shape=jax.ShapeDtypeStruct((n_slots, d), F32),
        grid_spec=grid_spec,
        compiler_params=_params(1),
        name="expert_dispatch",
    )(last_tile, slots, u2)


def _grouped_kernel(toff_ref, nt_ref, xs_ref, wi_ref, wo_ref, ys_ref, *, chunk):
    @pl.when(pl.program_id(0) < nt_ref[0])
    def _():
        ys_ref[...] = _swiglu_chunks(_bf(xs_ref[...]), wi_ref, wo_ref, D_EXPERT, chunk)


def _grouped_experts(xs, tile_off, n_tiles, w_in, w_out, tm, chunk):
    n_slots, d = xs.shape

    def expert(i, toff):
        e = 0
        for k in range(1, N_EXPERTS):
            e = e + (i >= toff[k]).astype(jnp.int32)
        return e

    def tile(i, nt):
        return jnp.minimum(i, nt[0] - 1)

    grid_spec = pltpu.PrefetchScalarGridSpec(
        num_scalar_prefetch=2,
        grid=(n_slots // tm,),
        in_specs=[
            pl.BlockSpec((tm, d), lambda i, toff, nt: (tile(i, nt), 0)),
            pl.BlockSpec((None, d, 2 * D_EXPERT), lambda i, toff, nt: (expert(tile(i, nt), toff), 0, 0),
                         pipeline_mode=pl.Buffered(1)),
            pl.BlockSpec((None, D_EXPERT, d), lambda i, toff, nt: (expert(tile(i, nt), toff), 0, 0),
                         pipeline_mode=pl.Buffered(1)),
        ],
        out_specs=pl.BlockSpec((tm, d), lambda i, toff, nt: (tile(i, nt), 0)),
    )
    return pl.pallas_call(
        functools.partial(_grouped_kernel, chunk=chunk),
        out_shape=jax.ShapeDtypeStruct((n_slots, d), F32),
        grid_spec=grid_spec,
        compiler_params=_params(1),
        name="grouped_experts",
    )(tile_off, n_tiles, xs, w_in, w_out)


def _combine_kernel(slot_ref, ys_ref, x1_ref, g2_ref, rcol_ref, lg_ref, lb_ref, o_ref, y1_scr, y2_scr, sem):
    tm = x1_ref.shape[0]
    bufs = (y1_scr, y2_scr)

    def start(r, carry):
        for k in range(2):
            _row_copy(ys_ref, slot_ref[k, r], bufs[k], r, sem).start()
        return carry

    lax.fori_loop(0, tm, start, 0, unroll=8)

    def wait(r, carry):
        for k in range(2):
            _row_copy(ys_ref, 0, bufs[k], 0, sem).wait()
        return carry

    lax.fori_loop(0, tm, wait, 0, unroll=8)
    rec = rcol_ref[...]
    f = rec[:, ROUTE_W1:ROUTE_W1 + 1] * y1_scr[...] + rec[:, ROUTE_W2:ROUTE_W2 + 1] * y2_scr[...]
    z = ALPHA * x1_ref[...] + g2_ref[...] * f
    o_ref[...] = _layer_norm_rows(z, lg_ref[...], lb_ref[...])


def _combine(ys, slots, x1, gate2, route_col, ln_g, ln_b, tiles_per_group):
    n, d = x1.shape
    nblk, _, tm = slots.shape
    r = gate2.shape[1]
    return pl.pallas_call(
        _combine_kernel,
        out_shape=jax.ShapeDtypeStruct((n, d), F32),
        grid=(nblk,),
        in_specs=[
            pl.BlockSpec((None, 2, tm), lambda b: (b, 0, 0), memory_space=pltpu.SMEM),
            pl.BlockSpec(memory_space=pl.ANY),
            pl.BlockSpec((tm, d), lambda b: (b, 0)),
            pl.BlockSpec((None, r, d), lambda b: (b // tiles_per_group, 0, 0)),
            pl.BlockSpec((tm, LANES), lambda b: (b, 0)),
            pl.BlockSpec((1, d), lambda b: (0, 0)),
            pl.BlockSpec((1, d), lambda b: (0, 0)),
        ],
        out_specs=pl.BlockSpec((tm, d), lambda b: (b, 0)),
        scratch_shapes=[pltpu.VMEM((tm, d), F32), pltpu.VMEM((tm, d), F32), pltpu.SemaphoreType.DMA],
        compiler_params=_params(1),
        name="expert_combine_ln",
    )(slots, ys, x1, gate2, route_col, ln_g.reshape(1, d), ln_b.reshape(1, d))


def _ffn_routed(u2, x1, gate2, route_col, route_row, w_in, w_out, ln_g, ln_b, tiles_per_group):
    n = u2.shape[0]
    tm = min(EXPERT_TILE, n)
    assert n % tm == 0
    ranks, counts = _route_ranks(route_row)
    cnt = counts[:, 0]
    padded = ((cnt + tm - 1) // tm) * tm
    off = (jnp.cumsum(padded) - padded).astype(jnp.int32)
    last_tile = jnp.where(padded > 0, off + padded - tm, -1).astype(jnp.int32)
    n_slots = 2 * n + N_EXPERTS * tm
    slots = off[ranks[:, 0:2, :]] + ranks[:, 2:4, :]
    xs = _dispatch(u2, slots, last_tile, n_slots, tm)
    ys = _grouped_experts(xs, (off // tm).astype(jnp.int32), (jnp.sum(padded) // tm).astype(jnp.int32).reshape(1),
                          w_in, w_out, tm, 512)
    return _combine(ys, slots, x1, gate2, route_col, ln_g, ln_b, tiles_per_group)


def _split_w_in(w_in_l):
    w = BRANCH_W
    col = lambda off, n: w_in_l[:, off:off + n]
    o = 0
    ret_q, ret_k, ret_v, ret_g = (col(o + i * w, w) for i in range(4)); o += 4 * w
    fox_q, fox_k, fox_v = (col(o + i * w, w) for i in range(3)); o += 3 * w
    fox_f = col(o, N_HEADS); o += N_HEADS
    conv = col(o, 2 * w); o += 2 * w
    diff_q, diff_k, diff_v = (col(o + i * w, w) for i in range(3)); o += 3 * w
    gate = col(o, 4 * D_MODEL)
    rot_r = lambda m: _rotated_columns(m, HEAD_DIM, HEAD_DIM)
    rot_d = lambda m: _rotated_columns(m, DIFF_SUB, ROT_DIM)
    w_ret = jnp.concatenate([ret_q, rot_r(ret_q), ret_k, rot_r(ret_k), ret_v, ret_g], axis=1)
    w_fox = jnp.concatenate([fox_q, fox_k, fox_v, fox_f, jnp.zeros((D_MODEL, LANES - N_HEADS), F32)], axis=1)
    w_diff = jnp.concatenate([diff_q, rot_d(diff_q), diff_k, rot_d(diff_k), diff_v], axis=1)
    return _bf(jnp.concatenate([w_ret, conv, w_fox, w_diff], axis=1)), _bf(gate)


def _block_diag_state(s):
    b = s.shape[0]
    eye = jnp.eye(N_HEADS, dtype=s.dtype)
    return jnp.einsum('bhde,hg->bhdge', s, eye).reshape(b, BRANCH_W, BRANCH_W)


def _diag_blocks(s_bd):
    b = s_bd.shape[0]
    s4 = s_bd.reshape(b, N_HEADS, HEAD_DIM, N_HEADS, HEAD_DIM)
    return jnp.stack([s4[:, h, :, h, :] for h in range(N_HEADS)], axis=1)


def _pad_rows(a, rows):
    return jnp.concatenate([a, jnp.zeros((a.shape[0], rows - a.shape[1]) + a.shape[2:], a.dtype)], axis=1)


def _trunk_layer(x, mod, pos, past, l, lw, sample, depth, kv_state):
    b, t, _ = x.shape
    d = D_MODEL
    w = BRANCH_W
    shift1, scale1, gate1 = (mod[0][:, None, i * d:(i + 1) * d] for i in range(3))
    shift2, scale2, gate2 = (mod[1][:, None, i * d:(i + 1) * d] for i in range(3))
    lam_init = 0.8 - 0.6 * math.exp(-0.3 * l)
    chunk = min(t, KV_BLOCK)
    pr = _in_projection(x, scale1, shift1, lw['w_mix'], lw['b_fox_f'], pos, min(t, 512), depth, l, kv_state)
    kv_state = pr['state']

    s0 = jnp.zeros((b, w, w), F32) if past is None else _block_diag_state(past[5])
    h_ret, s_bd = _retention(pr['rq'], pr['rk'], pr['rv'], pr['rg'], s0, chunk)
    ret_state = _diag_blocks(s_bd)

    if past is None:
        hist = jnp.zeros((b, HIST_ROWS, w), F32)
    else:
        hist = jnp.concatenate([jnp.zeros((b, HIST_ROWS - (CONV_W - 1), w), F32), past[6]], axis=1)
    h_conv, tail = _conv_mixer(pr['glu'], hist, lw['w_conv'], lw['b_conv'], lw['conv_ln_g'], lw['conv_ln_b'], chunk)
    conv_buf = tail[:, HIST_ROWS - (CONV_W - 1):, :]

    fq, fkb, fvb, lf = pr['fq'], pr['fkb'], pr['fvb'], pr['lf'][l]
    if past is None:
        off, tk_valid = 0, t
        k_all, v_all, lf_all, fox_new = fkb, fvb, lf, None
    else:
        p_len = past[0].shape[1]
        off, tk_valid = p_len, p_len + t
        tk_pad = p_len + KV_BLOCK
        k_all, v_all = past[0].reshape(b, p_len, w), past[1].reshape(b, p_len, w)
        fox_new = (_pad_rows(fkb, KV_BLOCK), _pad_rows(fvb, KV_BLOCK))
        past_lf = jnp.concatenate([past[2].astype(F32), jnp.zeros((b, p_len, LANES - N_HEADS), F32)], axis=2)
        lf_all = _pad_rows(jnp.concatenate([past_lf, lf], axis=1), tk_pad)
    fcol, frow = _logf_cumsum(lf_all)
    tq = min(t, KV_BLOCK)
    if past is None:
        h_fox = _fox_attention(fq, k_all, v_all, fcol, frow, tq, off, tk_valid)
    else:
        h_fox = _decode_attention(fq, k_all, v_all, fox_new, (fcol, frow), fox=True)

    dq, dkb, dvb = pr['dq'], pr['dkb'], pr['dvb']
    if past is None:
        dk_all, dv_all, diff_new = dkb, dvb, None
    else:
        dk_all, dv_all = past[3].reshape(b, p_len, w), past[4].reshape(b, p_len, w)
        diff_new = (_pad_rows(dkb, KV_BLOCK), _pad_rows(dvb, KV_BLOCK))
    if past is None:
        h_diff = _diff_attention(dq, dk_all, dv_all, lw['diff_lambda'], lw['diff_subln_g'], tq, off, tk_valid, lam_init)
    else:
        g_sub = jnp.tile(lw['diff_subln_g'].astype(F32), N_HEADS).reshape(1, w)
        h_diff = _decode_attention(dq, dk_all, dv_all, diff_new, (lw['diff_lambda'].astype(F32), g_sub), fox=False,
                                   lam_init=lam_init)

    n = b * t
    if sample:
        tm_merge = tm_ffn = min(n, 256)
        tpg_merge = tpg_ffn = 1
        rows_mod = lambda m, tm: jnp.repeat(m, t, axis=1).reshape(n // tm, tm, d)
    else:
        tm_merge, tm_ffn = min(t, 512), min(t, 1024)
        tpg_merge, tpg_ffn = t // tm_merge, t // tm_ffn
        rows_mod = lambda m, tm: m
    mods = tuple(rows_mod(m, tm_merge) for m in (scale1, shift1, gate1, scale2, shift2))
    branches = tuple(h.reshape(n, w) for h in (h_ret, h_fox, h_conv, h_diff))
    merged = _merge(x.reshape(n, d), mods, branches, lw['w_gate'], lw['w_branch'], lw['w_out'],
                    lw['ln_g'][0], lw['ln_b'][0], lw.get('router'), tm_merge, tpg_merge)
    g2 = rows_mod(gate2, tm_ffn)
    if 'router' in lw and not sample:
        x1, u2, route_col, route_row = merged
        x2 = _ffn_routed(u2, x1, rows_mod(gate2, tm_merge), route_col, route_row, lw['w_exp_in'], lw['w_exp_out'],
                         lw['ln_g'][1], lw['ln_b'][1], tpg_merge)
    elif 'router' in lw:
        x1, u2, route_col, _ = merged
        x2 = _ffn_experts(_bf(u2), x1, g2, route_col, lw['w_exp_in'], lw['w_exp_out'], lw['ln_g'][1], lw['ln_b'][1],
                          tm_ffn, tpg_ffn, 512)
    else:
        x1, u2 = merged
        x2 = _ffn_dense(u2, x1, g2, lw['w_ffn_in'], lw['w_ffn_out'], lw['ln_g'][1], lw['ln_b'][1],
                        tm_ffn, tpg_ffn, 512)
    return x2.reshape(b, t, d), kv_state, (ret_state, conv_buf)


def _state_outputs(kv_state, small):
    fk, fv, lf, dk, dv = kv_state
    depth, b, t, w = fk.shape
    heads = lambda a, nh: a.reshape(depth, b, t, nh, w // nh)
    ret_state, conv_buf = (jnp.stack(a) for a in zip(*small))
    return (heads(fk, N_HEADS), heads(fv, N_HEADS), lf[..., :N_HEADS], heads(dk, N_SUB), heads(dv, N_HEADS),
            ret_state, conv_buf)


def kernel(x_prompt, x_sample, c_prompt, c_sample, cache_fox_k, cache_fox_v, cache_fox_logf, cache_diff_k, cache_diff_v, state_ret, state_conv, w_in, b_fox_f, w_conv, b_conv, conv_ln_g, conv_ln_b, diff_lambda, diff_subln_g, w_branch, w_out, w_ada, b_ada, ln_g, ln_b, w_ffn_in, w_ffn_out, w_router, b_router, w_exp_in, w_exp_out):
    depth = w_in.shape[0]
    bp = x_prompt.shape[0]
    past_len = cache_fox_k.shape[2]
    pos_p = jnp.arange(x_prompt.shape[1], dtype=jnp.int32)
    pos_s = past_len + jnp.arange(x_sample.shape[1], dtype=jnp.int32)
    mod = _modulation(jnp.concatenate([c_prompt, c_sample], axis=0), w_ada, b_ada)
    yp, ys = x_prompt, x_sample
    kv_p = kv_s = None
    small_p, small_s = [], []
    for l in range(depth):
        w_mix, w_gate = _split_w_in(w_in[l])
        lw = dict(w_mix=w_mix, w_gate=w_gate,
                  b_fox_f=b_fox_f[l], w_conv=w_conv[l], b_conv=b_conv[l], conv_ln_g=conv_ln_g[l],
                  conv_ln_b=conv_ln_b[l], diff_lambda=diff_lambda[l], diff_subln_g=diff_subln_g[l],
                  w_branch=_bf(w_branch[l]), w_out=_bf(w_out[l]), ln_g=ln_g[l], ln_b=ln_b[l])
        if l % 2 == 0:
            lw['w_ffn_in'] = _bf(w_ffn_in[l // 2])
            lw['w_ffn_out'] = _bf(w_ffn_out[l // 2])
        else:
            wr = jnp.concatenate([w_router[l // 2], jnp.zeros((D_MODEL, LANES - N_EXPERTS), F32)], axis=1)
            br = jnp.concatenate([b_router[l // 2].astype(F32), jnp.zeros((LANES - N_EXPERTS,), F32)]).reshape(1, LANES)
            lw['router'] = (_bf(wr), br)
            lw['w_exp_in'] = _bf(w_exp_in[l // 2])
            lw['w_exp_out'] = _bf(w_exp_out[l // 2])
        past_l = (cache_fox_k[l], cache_fox_v[l], cache_fox_logf[l], cache_diff_k[l], cache_diff_v[l],
                  state_ret[l], state_conv[l])
        yp, kv_p, st_p = _trunk_layer(yp, mod[l][:, :bp], pos_p, None, l, lw, False, depth, kv_p)
        ys, kv_s, st_s = _trunk_layer(ys, mod[l][:, bp:], pos_s, past_l, l, lw, True, depth, kv_s)
        small_p.append(st_p)
        small_s.append(st_s)
    return (yp, ys) + _state_outputs(kv_p, small_p) + _state_outputs(kv_s, small_s)
```

```python
import functools
import math

import jax
import jax.numpy as jnp
from jax import lax
from jax.experimental import pallas as pl
from jax.experimental.pallas import tpu as pltpu

D_MODEL = 1024
BRANCH_W = 256
HEAD_DIM = 64
N_HEADS = 4
DIFF_SUB = 32
N_SUB = 8
ROT_DIM = DIFF_SUB // 4
RET_THETA = 10000.0
ROPE_THETA = 500000.0
CHUNK = 64
CONV_W = 31
D_FF = 2816
N_EXPERTS = 8
D_EXPERT = 3584
DEPTH = 2
ALPHA = (2.0 * DEPTH) ** 0.25
EPS = 1e-5
NEG = -1e30
LOG2E = math.log2(math.e)

LANES = 128
BF16_ROWS = 16
KV_BLOCK = 256
HIST_ROWS = 32
VMEM_LIMIT = 56 * 1024 * 1024

F32 = jnp.float32
BF16 = jnp.bfloat16


def _bf(x):
    return x.astype(BF16)


def _dot(a, b):
    return jnp.dot(a, b, preferred_element_type=F32)


def _dot_nt(a, b):
    return lax.dot_general(a, b, (((1,), (1,)), ((), ())), preferred_element_type=F32)


def _dot_tn(a, b):
    return lax.dot_general(a, b, (((0,), (0,)), ((), ())), preferred_element_type=F32)


def _sigmoid(x):
    return 1.0 / (1.0 + jnp.exp(-x))


def _params(n_axes):
    return pltpu.CompilerParams(dimension_semantics=("arbitrary",) * n_axes,
                                vmem_limit_bytes=VMEM_LIMIT)


def _head_sum(y, width):
    n = y.shape[-1]
    r = lax.broadcasted_iota(jnp.int32, (n, n), 0) // width
    c = lax.broadcasted_iota(jnp.int32, (n, n), 1) // width
    bd = jnp.where(r == c, 1.0, 0.0).astype(BF16)
    hi = _bf(y)
    lo = _bf(y - hi.astype(F32))
    return _dot(hi, bd) + _dot(lo, bd)


def _layer_norm_rows(z, g, b):
    mu = jnp.mean(z, axis=-1, keepdims=True)
    d = z - mu
    var = jnp.mean(d * d, axis=-1, keepdims=True)
    return d * lax.rsqrt(var + EPS) * g + b


def _mod_kernel(c_ref, w_ref, b_ref, o_ref):
    c = c_ref[...]
    sc = _bf(c * _sigmoid(c))
    o_ref[...] = _dot(sc, _bf(w_ref[...])) + b_ref[...]


def _modulation(c_all, w_ada, b_ada):
    rows = c_all.shape[0]
    depth = w_ada.shape[0]
    d3 = w_ada.shape[-1]
    nj = d3 // D_MODEL
    return pl.pallas_call(
        _mod_kernel,
        out_shape=jax.ShapeDtypeStruct((depth, 2, rows, d3), F32),
        grid=(depth * 2, nj),
        in_specs=[
            pl.BlockSpec((rows, D_MODEL), lambda i, j: (0, 0)),
            pl.BlockSpec((None, None, D_MODEL, D_MODEL), lambda i, j: (i // 2, i % 2, 0, j)),
            pl.BlockSpec((None, None, 1, D_MODEL), lambda i, j: (i // 2, i % 2, 0, j)),
        ],
        out_specs=pl.BlockSpec((None, None, rows, D_MODEL), lambda i, j: (i // 2, i % 2, 0, j)),
        compiler_params=_params(2),
        name="adaln_modulation",
    )(c_all, w_ada, b_ada.reshape(depth, 2, 1, d3))


def _ret_kernel(q_ref, k_ref, v_ref, g_ref, dmask_ref, qdec_ref, kdec_ref, cdec_ref, s0_ref,
                h_ref, sout_ref, s_scr):
    c = pl.program_id(1)

    @pl.when(c == 0)
    def _():
        s_scr[...] = s0_ref[...]

    w = BRANCH_W
    chunk = qdec_ref.shape[0]
    lane_head = lax.broadcasted_iota(jnp.int32, (1, w), 1) // HEAD_DIM
    r = lax.broadcasted_iota(jnp.int32, (w, w), 0) // HEAD_DIM
    cc = lax.broadcasted_iota(jnp.int32, (w, w), 1) // HEAD_DIM
    s_prev = s_scr[...]
    for c0 in range(0, q_ref.shape[0], chunk):
        q = q_ref[c0:c0 + chunk, :]
        kb = k_ref[c0:c0 + chunk, :]
        vb = v_ref[c0:c0 + chunk, :]
        g = g_ref[c0:c0 + chunk, :]
        y = jnp.zeros((chunk, w), F32)
        for h in range(N_HEADS):
            mh = lane_head == h
            qh = jnp.where(mh, q, jnp.zeros_like(q))
            a = _dot_nt(qh, kb) * dmask_ref[h]
            y = y + jnp.where(mh, _dot(_bf(a), vb), 0.0)
        y = y + _dot(_bf(q.astype(F32) * qdec_ref[...]), _bf(s_prev))
        kv = _dot_tn(_bf(kb.astype(F32) * kdec_ref[...]), vb)
        s_prev = cdec_ref[...] * s_prev + jnp.where(r == cc, kv, 0.0)
        mu = _head_sum(y, HEAD_DIM) * (1.0 / HEAD_DIM)
        d = y - mu
        var = _head_sum(d * d, HEAD_DIM) * (1.0 / HEAD_DIM)
        hn = d * lax.rsqrt(var + EPS)
        h_ref[c0:c0 + chunk, :] = _bf(hn * (g * _sigmoid(g)))
    s_scr[...] = s_prev
    sout_ref[...] = s_prev


def _retention_tables(chunk):
    log_g = jnp.log1p(-jnp.exp2(-5.0 - jnp.arange(N_HEADS, dtype=F32)))
    idx = jnp.arange(chunk, dtype=F32)
    dist = jnp.abs(idx[:, None] - idx[None, :])
    sub = jnp.arange(chunk) // CHUNK
    vis = sub[None, :] <= sub[:, None]
    dmask = jnp.where(vis[None], jnp.exp(log_g[:, None, None] * dist[None]), 0.0)
    lg_lane = jnp.repeat(log_g, HEAD_DIM)[None, :]
    qdec = jnp.exp(lg_lane * (idx[:, None] + 1.0))
    kdec = jnp.exp(lg_lane * (chunk - 1.0 - idx[:, None]))
    cdec = jnp.exp(lg_lane * chunk)
    return dmask.astype(F32), qdec, kdec, cdec


def _rope_tables(pos, dim, rot_dim, theta, n_rep):
    half = rot_dim // 2
    inv_freq = jnp.exp(-math.log(theta) * jnp.arange(half, dtype=F32) / half)
    ang = pos.astype(F32)[:, None] * inv_freq[None, :]
    t = pos.shape[0]
    cos = jnp.concatenate([jnp.cos(ang), jnp.cos(ang), jnp.ones((t, dim - rot_dim), F32)], axis=1)
    sin = jnp.concatenate([jnp.sin(ang), jnp.sin(ang), jnp.zeros((t, dim - rot_dim), F32)], axis=1)
    return jnp.tile(cos, (1, n_rep)), jnp.tile(sin, (1, n_rep))


def _rotated_columns(w, dim, rot_dim):
    half = rot_dim // 2
    k, n = w.shape
    wh = w.reshape(k, n // dim, dim)
    rot = jnp.concatenate([-wh[..., half:rot_dim], wh[..., :half], jnp.zeros_like(wh[..., rot_dim:])], axis=-1)
    return rot.reshape(k, n)


def _retention(q, k, v, g, s0_bd, chunk):
    b, t, w = q.shape
    rows = next(n * chunk for n in (4, 2, 1) if t % (n * chunk) == 0)
    dmask, qdec, kdec, cdec = _retention_tables(chunk)
    full = lambda shape: pl.BlockSpec(shape, lambda i, j: (0,) * len(shape))
    row = pl.BlockSpec((None, rows, w), lambda i, j: (i, j, 0))
    return pl.pallas_call(
        _ret_kernel,
        out_shape=(jax.ShapeDtypeStruct((b, t, w), BF16), jax.ShapeDtypeStruct((b, w, w), F32)),
        grid=(b, t // rows),
        in_specs=[
            row, row, row, row,
            full((N_HEADS, chunk, chunk)),
            full((chunk, w)),
            full((chunk, w)),
            full((1, w)),
            pl.BlockSpec((None, w, w), lambda i, j: (i, 0, 0)),
        ],
        out_specs=(row, pl.BlockSpec((None, w, w), lambda i, j: (i, 0, 0))),
        scratch_shapes=[pltpu.VMEM((w, w), F32)],
        compiler_params=_params(2),
        name="retention_mixer",
    )(q, k, v, g, dmask, qdec, kdec, cdec, s0_bd)


def _conv_kernel(glu_ref, hist_ref, wc_ref, bc_ref, lg_ref, lb_ref, h_ref, tail_ref, xp_scr):
    c = pl.program_id(1)
    rows = glu_ref.shape[0]
    w = BRANCH_W
    pad = HIST_ROWS - (CONV_W - 1)

    @pl.when(c == 0)
    def _():
        xp_scr[0:HIST_ROWS, :] = hist_ref[...]

    xp_scr[HIST_ROWS:HIST_ROWS + rows, :] = glu_ref[...]
    sub = 8
    acc = jnp.zeros((rows, w), F32) + bc_ref[...]
    for rho in range(sub):
        z = None
        for m in range((pad + CONV_W - 1) // sub + 1):
            j = sub * m + rho - pad
            if not 0 <= j < CONV_W:
                continue
            span = rows if rho == 0 else rows + sub
            term = xp_scr[sub * m:sub * m + span, :] * wc_ref[j:j + 1, :]
            z = term if z is None else z + term
        acc = acc + z[rho:rho + rows, :]
    y = _layer_norm_rows(acc, lg_ref[...], lb_ref[...])
    h_ref[...] = _bf(y * _sigmoid(y))
    tail = xp_scr[rows:rows + HIST_ROWS, :]
    tail_ref[...] = tail
    xp_scr[0:HIST_ROWS, :] = tail


def _conv_mixer(glu, hist, w_conv, b_conv, ln_g, ln_b, chunk):
    b, t, w = glu.shape
    nc = t // chunk
    full = lambda shape: pl.BlockSpec(shape, lambda i, j: (0,) * len(shape))
    wc = jnp.concatenate([w_conv, jnp.zeros((HIST_ROWS - CONV_W, w), F32)], axis=0)
    return pl.pallas_call(
        _conv_kernel,
        out_shape=(jax.ShapeDtypeStruct((b, t, w), BF16), jax.ShapeDtypeStruct((b, HIST_ROWS, w), F32)),
        grid=(b, nc),
        in_specs=[
            pl.BlockSpec((None, chunk, w), lambda i, j: (i, j, 0)),
            pl.BlockSpec((None, HIST_ROWS, w), lambda i, j: (i, 0, 0)),
            full((HIST_ROWS, w)),
            full((1, w)),
            full((1, w)),
            full((1, w)),
        ],
        out_specs=(pl.BlockSpec((None, chunk, w), lambda i, j: (i, j, 0)),
                   pl.BlockSpec((None, HIST_ROWS, w), lambda i, j: (i, 0, 0))),
        scratch_shapes=[pltpu.VMEM((HIST_ROWS + chunk, w), F32)],
        compiler_params=_params(2),
        name="conv_mixer",
    )(glu, hist, wc, b_conv.reshape(1, w), ln_g.reshape(1, w), ln_b.reshape(1, w))


def _layer_slot(depth, layer, b, t, tm, width, dt, first):
    shape = jax.ShapeDtypeStruct((depth, b, t, width), dt)
    if first:
        return shape, pl.BlockSpec((depth, None, tm, width), lambda i, j: (0, i, j, 0))
    return shape, pl.BlockSpec((None, None, tm, width), lambda i, j: (layer, i, j, 0))


def _store_layer_rows(ref, rows):
    if len(ref.shape) == 3:
        for d in range(ref.shape[0]):
            ref[d] = rows
    else:
        ref[...] = rows


def _in_proj_kernel(x_ref, sc_ref, sh_ref, w_ref, bf_ref, rcos_ref, rsin_ref, dcos_ref, dsin_ref, *refs):
    (rq_ref, rk_ref, rv_ref, rg_ref, glu_ref, fq_ref, fk_ref, fv_ref, fkb_ref, fvb_ref, lf_ref,
     dq_ref, dk_ref, dv_ref, dkb_ref, dvb_ref) = refs[-16:]
    w = BRANCH_W
    u = _bf(x_ref[...] * (1.0 + sc_ref[...]) + sh_ref[...])
    o = 0
    p = _dot(u, w_ref[:, o:o + 6 * w])
    cos = rcos_ref[...]
    sin = rsin_ref[...]
    rq_ref[...] = _bf(p[:, 0:w] * cos + p[:, w:2 * w] * sin)
    rk_ref[...] = _bf((p[:, 2 * w:3 * w] * cos + p[:, 3 * w:4 * w] * sin) * (HEAD_DIM ** -0.5))
    rv_ref[...] = _bf(p[:, 4 * w:5 * w])
    rg_ref[...] = p[:, 5 * w:6 * w]
    o += 6 * w
    p = _dot(u, w_ref[:, o:o + 2 * w])
    glu_ref[...] = p[:, :w] * _sigmoid(p[:, w:])
    o += 2 * w
    p = _dot(u, w_ref[:, o:o + 3 * w + LANES])
    fq_ref[...] = _bf(p[:, 0:w] * (HEAD_DIM ** -0.5 * LOG2E))
    k = p[:, w:2 * w]
    v = p[:, 2 * w:3 * w]
    _store_layer_rows(fk_ref, k)
    _store_layer_rows(fv_ref, v)
    fkb_ref[...] = _bf(k)
    fvb_ref[...] = _bf(v)
    z = p[:, 3 * w:] + bf_ref[...]
    lf = jnp.minimum(z, 0.0) - jnp.log(1.0 + jnp.exp(-jnp.abs(z)))
    lane = lax.broadcasted_iota(jnp.int32, lf.shape, 1)
    _store_layer_rows(lf_ref, jnp.where(lane < N_HEADS, lf, 0.0))
    o += 3 * w + LANES
    p = _dot(u, w_ref[:, o:o + 5 * w])
    cos = dcos_ref[...]
    sin = dsin_ref[...]
    dq_ref[...] = _bf((p[:, 0:w] * cos + p[:, w:2 * w] * sin) * (DIFF_SUB ** -0.5 * LOG2E))
    k = p[:, 2 * w:3 * w] * cos + p[:, 3 * w:4 * w] * sin
    v = p[:, 4 * w:5 * w]
    _store_layer_rows(dk_ref, k)
    _store_layer_rows(dv_ref, v)
    dkb_ref[...] = _bf(k)
    dvb_ref[...] = _bf(v)


def _in_projection(x, scale, shift, w_all, b_f, pos, tm, depth, layer, state):
    b, t, _ = x.shape
    w = BRANCH_W
    nw = w_all.shape[1]
    bfp = jnp.concatenate([b_f.astype(F32), jnp.zeros((LANES - N_HEADS,), F32)]).reshape(1, LANES)
    rcos, rsin = _rope_tables(pos, HEAD_DIM, HEAD_DIM, RET_THETA, N_HEADS)
    dcos, dsin = _rope_tables(pos, DIFF_SUB, ROT_DIM, ROPE_THETA, N_SUB)
    row = lambda width: pl.BlockSpec((None, tm, width), lambda i, j: (i, j, 0))
    table = pl.BlockSpec((tm, w), lambda i, j: (j, 0))
    plain = lambda width, dt: (jax.ShapeDtypeStruct((b, t, width), dt), row(width))
    slot = lambda width: _layer_slot(depth, layer, b, t, tm, width, F32, state is None)
    outs = [plain(w, BF16), plain(w, BF16), plain(w, BF16), plain(w, F32), plain(w, F32),
            plain(w, BF16), slot(w), slot(w), plain(w, BF16), plain(w, BF16), slot(LANES),
            plain(w, BF16), slot(w), slot(w), plain(w, BF16), plain(w, BF16)]
    prior = () if state is None else tuple(state)
    n_in = 9
    aliases = {n_in: 6, n_in + 1: 7, n_in + 2: 10, n_in + 3: 12, n_in + 4: 13} if prior else {}
    res = pl.pallas_call(
        _in_proj_kernel,
        out_shape=tuple(o[0] for o in outs),
        grid=(b, t // tm),
        in_specs=[
            row(D_MODEL),
            pl.BlockSpec((None, 1, D_MODEL), lambda i, j: (i, 0, 0)),
            pl.BlockSpec((None, 1, D_MODEL), lambda i, j: (i, 0, 0)),
            pl.BlockSpec((D_MODEL, nw), lambda i, j: (0, 0), pipeline_mode=pl.Buffered(1)),
            pl.BlockSpec((1, LANES), lambda i, j: (0, 0)),
            table, table, table, table,
        ] + [pl.BlockSpec(memory_space=pl.ANY)] * len(prior),
        out_specs=tuple(o[1] for o in outs),
        input_output_aliases=aliases,
        compiler_params=_params(2),
        name="mixer_in_projection",
    )(x, scale, shift, w_all, bfp, rcos, rsin, dcos, dsin, *prior)
    names = ('rq', 'rk', 'rv', 'rg', 'glu', 'fq', 'fk', 'fv', 'fkb', 'fvb', 'lf', 'dq', 'dk', 'dv', 'dkb', 'dvb')
    pr = dict(zip(names, res))
    pr['state'] = (pr['fk'], pr['fv'], pr['lf'], pr['dk'], pr['dv'])
    return pr


def _cumsum_kernel(lf_ref, col_ref, row_ref):
    n = KV_BLOCK
    nb = lf_ref.shape[0] // n
    r = lax.broadcasted_iota(jnp.int32, (n, n), 0)
    c = lax.broadcasted_iota(jnp.int32, (n, n), 1)
    tri = jnp.where(c <= r, 1.0, 0.0).astype(BF16)
    carry = jnp.zeros((1, LANES), F32)
    for jb in range(nb):
        x = lf_ref[jb * n:(jb + 1) * n, :]
        hi = _bf(x)
        r1 = x - hi.astype(F32)
        mid = _bf(r1)
        lo = _bf(r1 - mid.astype(F32))
        cs = _dot(tri, hi) + _dot(tri, mid) + _dot(tri, lo) + carry
        carry = cs[n - 1:n, :]
        scaled = cs * LOG2E
        col_ref[jb * n:(jb + 1) * n, :] = scaled
        row_ref[jb] = scaled.T[0:8, :]


def _logf_cumsum(lf):
    b, tk, _ = lf.shape
    nb = tk // KV_BLOCK
    return pl.pallas_call(
        _cumsum_kernel,
        out_shape=(jax.ShapeDtypeStruct((b, tk, LANES), F32),
                   jax.ShapeDtypeStruct((b, nb, 8, KV_BLOCK), F32)),
        grid=(b,),
        in_specs=[pl.BlockSpec((None, tk, LANES), lambda i: (i, 0, 0))],
        out_specs=(pl.BlockSpec((None, tk, LANES), lambda i: (i, 0, 0)),
                   pl.BlockSpec((None, nb, 8, KV_BLOCK), lambda i: (i, 0, 0, 0))),
        compiler_params=_params(1),
        name="logf_cumsum",
    )(lf)


def _attn_kernel(*refs, fox, off, tk_valid, tq, lam_init):
    if fox:
        q_ref, k_ref, v_ref, fc_ref, fr_ref, o_ref, vt_scr, qm_scr, mb_scr, m_scr, r_scr, acc_scr, s_scr = refs
    else:
        q_ref, k_ref, v_ref, lam_ref, g_ref, o_ref, vt_scr, qm_scr, mb_scr, m_scr, r_scr, acc_scr, s_scr = refs
    i = pl.program_id(1)
    w = BRANCH_W
    n_sub, tqp, _ = qm_scr.shape
    sub_w = w // n_sub
    subs_per_head = n_sub // N_HEADS
    nb = vt_scr.shape[0]
    hd = HEAD_DIM
    q0 = off + i * tq
    nfull = q0 // KV_BLOCK

    @pl.when(i == 0)
    def _():
        for jb in range(nb):
            vt = v_ref[jb * KV_BLOCK:(jb + 1) * KV_BLOCK, :].T
            for h in range(N_HEADS):
                vt_scr[jb, h, 0:hd, :] = vt[h * hd:(h + 1) * hd, :]
                vt_scr[jb, h, hd:, :] = jnp.ones((vt_scr.shape[2] - hd, KV_BLOCK), BF16)
        kpos = lax.broadcasted_iota(jnp.int32, (KV_BLOCK, tqp), 0)
        qpos = lax.broadcasted_iota(jnp.int32, (KV_BLOCK, tqp), 1)
        if fox:
            vis = kpos <= qpos
        else:
            vis = (kpos // CHUNK) <= (qpos // CHUNK)
        vis = vis & (kpos < tk_valid - nfull * KV_BLOCK)
        mb_scr[...] = jnp.where(vis, 0.0, NEG)

    q = q_ref[...]
    lane_sub = lax.broadcasted_iota(jnp.int32, (1, w), 1) // sub_w
    if tqp > tq:
        qm_scr[...] = jnp.zeros_like(qm_scr)
    for n in range(n_sub):
        qm_scr[n, 0:tq, :] = jnp.where(lane_sub == n, q, jnp.zeros_like(q))
    m_scr[...] = jnp.full(m_scr.shape, NEG, F32)
    r_scr[...] = jnp.full(r_scr.shape, NEG, F32)
    acc_scr[...] = jnp.zeros_like(acc_scr)

    def scores(j, n, masked):
        start = pl.multiple_of(j * KV_BLOCK, KV_BLOCK)
        h = n // subs_per_head
        s = _dot_nt(k_ref[pl.ds(start, KV_BLOCK), :], qm_scr[n])
        if fox:
            s = s + (fr_ref[h:h + 1, 0:tqp] - fc_ref[pl.ds(start, KV_BLOCK), h:h + 1])
        if masked:
            s = s + mb_scr[...]
        s_scr[n] = s
        m_scr[n] = jnp.maximum(m_scr[n], jnp.max(s, axis=0, keepdims=True))

    def values(j, n):
        h = n // subs_per_head
        m = m_scr[n]
        alpha = jnp.exp2(r_scr[n] - m)
        r_scr[n] = m
        p = jnp.exp2(s_scr[n] - m)
        acc_scr[n] = alpha * acc_scr[n] + _dot(vt_scr[j, h], _bf(p))

    @pl.when(nfull == 0)
    def _():
        for n in range(n_sub):
            scores(0, n, True)

    @pl.when(nfull > 0)
    def _():
        for n in range(n_sub):
            scores(0, n, False)

    def advance(j):
        for n in range(n_sub):
            values(j - 1, n)
            scores(j, n, False)

    n_mid = jnp.maximum(nfull - 1, 0)

    def body(i2, carry):
        advance(1 + 2 * i2)
        advance(2 + 2 * i2)
        return carry

    lax.fori_loop(0, n_mid // 2, body, 0)

    @pl.when(n_mid % 2 == 1)
    def _():
        advance(nfull - 1)

    @pl.when(nfull > 0)
    def _():
        for n in range(n_sub):
            values(nfull - 1, n)
            scores(nfull, n, True)

    for n in range(n_sub):
        values(nfull, n)

    def normalised(n):
        acc = acc_scr[n]
        return acc[0:hd] / acc[hd:hd + 1]

    if fox:
        out_t = jnp.concatenate([normalised(h) for h in range(N_HEADS)], axis=0)
        o_ref[...] = _bf(out_t.T[0:tq, :])
    else:
        lamv = lam_ref[...]
        lam = (jnp.exp(jnp.sum(lamv[0:1] * lamv[1:2], axis=-1, keepdims=True))
               - jnp.exp(jnp.sum(lamv[2:3] * lamv[3:4], axis=-1, keepdims=True)) + lam_init)
        parts = [normalised(2 * h) - lam * normalised(2 * h + 1) for h in range(N_HEADS)]
        dy = jnp.concatenate(parts, axis=0).T[0:tq, :]
        ms = _head_sum(dy * dy, HEAD_DIM) * (1.0 / HEAD_DIM)
        o_ref[...] = _bf(dy * lax.rsqrt(ms + EPS) * g_ref[...] * (1.0 - lam_init))


def _attention(q, kb, vb, extras, *, fox, tq, off, tk_valid, lam_init=0.0):
    b, t, w = q.shape
    tk = kb.shape[1]
    nb = tk // KV_BLOCK
    tqp = max(tq, LANES)
    n_sub = N_HEADS if fox else N_SUB
    assert off % KV_BLOCK == 0 and (tq == KV_BLOCK or t == tq)
    kernel = functools.partial(_attn_kernel, fox=fox, off=off, tk_valid=tk_valid, tq=tq, lam_init=lam_init)
    in_specs = [
        pl.BlockSpec((None, tq, w), lambda i, j: (i, j, 0)),
        pl.BlockSpec((None, tk, w), lambda i, j: (i, 0, 0)),
        pl.BlockSpec((None, tk, w), lambda i, j: (i, 0, 0)),
    ]
    if fox:
        in_specs += [
            pl.BlockSpec((None, tk, LANES), lambda i, j: (i, 0, 0)),
            pl.BlockSpec((None, None, 8, KV_BLOCK), lambda i, j: (i, (off + j * tq) // KV_BLOCK, 0, 0)),
        ]
    else:
        in_specs += [
            pl.BlockSpec((4, DIFF_SUB), lambda i, j: (0, 0)),
            pl.BlockSpec((1, w), lambda i, j: (0, 0)),
        ]
    return pl.pallas_call(
        kernel,
        out_shape=jax.ShapeDtypeStruct((b, t, w), BF16),
        grid=(b, t // tq),
        in_specs=in_specs,
        out_specs=pl.BlockSpec((None, tq, w), lambda i, j: (i, j, 0)),
        scratch_shapes=[
            pltpu.VMEM((nb, N_HEADS, HEAD_DIM + BF16_ROWS, KV_BLOCK), BF16),
            pltpu.VMEM((n_sub, tqp, w), BF16),
            pltpu.VMEM((KV_BLOCK, tqp), F32),
            pltpu.VMEM((n_sub, 1, tqp), F32),
            pltpu.VMEM((n_sub, 1, tqp), F32),
            pltpu.VMEM((n_sub, HEAD_DIM + BF16_ROWS, tqp), F32),
            pltpu.VMEM((n_sub, KV_BLOCK, tqp), F32),
        ],
        compiler_params=_params(2),
        name="fox_attention" if fox else "diff_attention",
    )(q, kb, vb, *extras)


def _fox_attention(q, kb, vb, fcol, frow, tq, off, tk_valid):
    return _attention(q, kb, vb, (fcol, frow), fox=True, tq=tq, off=off, tk_valid=tk_valid)


def _diff_attention(q, kb, vb, diff_lambda, subln_g, tq, off, tk_valid, lam_init):
    g = jnp.tile(subln_g.astype(F32), N_HEADS).reshape(1, BRANCH_W)
    return _attention(q, kb, vb, (diff_lambda.astype(F32), g), fox=False, tq=tq, off=off,
                      tk_valid=tk_valid, lam_init=lam_init)


def _decode_attn_kernel(*refs, fox, p_len, lam_init):
    if fox:
        q_ref, kc_ref, vc_ref, kn_ref, vn_ref, fc_ref, fr_ref, o_ref = refs
    else:
        q_ref, kc_ref, vc_ref, kn_ref, vn_ref, lam_ref, g_ref, o_ref = refs
    w = BRANCH_W
    t = q_ref.shape[0]
    n_sub = N_HEADS if fox else N_SUB
    sub_w = w // n_sub
    rows = n_sub * t
    q = q_ref[...]
    lane = lax.broadcasted_iota(jnp.int32, (1, w), 1)
    qs = jnp.concatenate([jnp.where(lane // sub_w == n, q, jnp.zeros_like(q)) for n in range(n_sub)], axis=0)
    if fox:
        fcn = fc_ref[...]
        fq = jnp.concatenate([fcn[:, h:h + 1] for h in range(N_HEADS)], axis=0)

    def step(kb, vb, fk, mask, carry):
        m, l, acc = carry
        s = _dot_nt(qs, kb)
        if fox:
            fk_rows = jnp.concatenate([jnp.broadcast_to(fk[h:h + 1, :], (t, KV_BLOCK)) for h in range(N_HEADS)],
                                      axis=0)
            s = s + (fq - fk_rows)
        if mask is not None:
            s = jnp.where(mask, s, NEG)
        m_new = jnp.maximum(m, jnp.max(s, axis=-1, keepdims=True))
        alpha = jnp.exp2(m - m_new)
        p = jnp.exp2(s - m_new)
        l = alpha * l + jnp.sum(p, axis=-1, keepdims=True)
        acc = alpha * acc + _dot(_bf(p), vb)
        return m_new, l, acc

    def cached(j, carry):
        start = pl.multiple_of(j * KV_BLOCK, KV_BLOCK)
        fk = fr_ref[j] if fox else None
        return step(_bf(kc_ref[pl.ds(start, KV_BLOCK), :]), _bf(vc_ref[pl.ds(start, KV_BLOCK), :]), fk, None, carry)

    init = (jnp.full((rows, 1), NEG, F32), jnp.zeros((rows, 1), F32), jnp.zeros((rows, w), F32))
    nbc = p_len // KV_BLOCK
    carry = lax.fori_loop(0, nbc, cached, init)
    qpos = p_len + lax.broadcasted_iota(jnp.int32, (rows, KV_BLOCK), 0) % t
    kpos = p_len + lax.broadcasted_iota(jnp.int32, (rows, KV_BLOCK), 1)
    vis = (kpos <= qpos) if fox else ((kpos // CHUNK) <= (qpos // CHUNK))
    vis = vis & (kpos < p_len + t)
    m, l, acc = step(kn_ref[...], vn_ref[...], fr_ref[nbc] if fox else None, vis, carry)
    o = acc / l
    lane_head = lane // HEAD_DIM
    if fox:
        out = sum(jnp.where(lane_head == h, o[h * t:(h + 1) * t], 0.0) for h in range(N_HEADS))
        o_ref[...] = _bf(out)
    else:
        lamv = lam_ref[...]
        lam = (jnp.exp(jnp.sum(lamv[0:1] * lamv[1:2], axis=-1, keepdims=True))
               - jnp.exp(jnp.sum(lamv[2:3] * lamv[3:4], axis=-1, keepdims=True)) + lam_init)
        dy = sum(jnp.where(lane_head == h, o[2 * h * t:(2 * h + 1) * t] - lam * o[(2 * h + 1) * t:(2 * h + 2) * t], 0.0)
                 for h in range(N_HEADS))
        ms = _head_sum(dy * dy, HEAD_DIM) * (1.0 / HEAD_DIM)
        o_ref[...] = _bf(dy * lax.rsqrt(ms + EPS) * g_ref[...] * (1.0 - lam_init))


def _decode_attention(q, k_cache, v_cache, new_kv, extras, *, fox, lam_init=0.0):
    b, t, w = q.shape
    p_len = k_cache.shape[1]
    assert p_len % KV_BLOCK == 0 and t <= KV_BLOCK and t % 8 == 0 and p_len % t == 0
    nb = p_len // KV_BLOCK + 1
    in_specs = [
        pl.BlockSpec((None, t, w), lambda i: (i, 0, 0)),
        pl.BlockSpec((None, p_len, w), lambda i: (i, 0, 0)),
        pl.BlockSpec((None, p_len, w), lambda i: (i, 0, 0)),
        pl.BlockSpec((None, KV_BLOCK, w), lambda i: (i, 0, 0)),
        pl.BlockSpec((None, KV_BLOCK, w), lambda i: (i, 0, 0)),
    ]
    if fox:
        in_specs += [
            pl.BlockSpec((None, t, LANES), lambda i: (i, p_len // t, 0)),
            pl.BlockSpec((None, nb, 8, KV_BLOCK), lambda i: (i, 0, 0, 0)),
        ]
    else:
        in_specs += [pl.BlockSpec((4, DIFF_SUB), lambda i: (0, 0)), pl.BlockSpec((1, w), lambda i: (0, 0))]
    return pl.pallas_call(
        functools.partial(_decode_attn_kernel, fox=fox, p_len=p_len, lam_init=lam_init),
        out_shape=jax.ShapeDtypeStruct((b, t, w), BF16),
        grid=(b,),
        in_specs=in_specs,
        out_specs=pl.BlockSpec((None, t, w), lambda i: (i, 0, 0)),
        compiler_params=_params(1),
        name="fox_decode_attention" if fox else "diff_decode_attention",
    )(q, k_cache, v_cache, *new_kv, *extras)


ROUTE_W1, ROUTE_W2, ROUTE_I1, ROUTE_I2 = 8, 9, 10, 11


def _top2_route(logits):
    lane = lax.broadcasted_iota(jnp.int32, logits.shape, 1).astype(F32)
    lg = jnp.where(lane < N_EXPERTS, logits, -jnp.inf)
    v1 = jnp.max(lg, axis=-1, keepdims=True)
    i1 = jnp.min(jnp.where(lg == v1, lane, float(LANES)), axis=-1, keepdims=True)
    lg2 = jnp.where(lane == i1, -jnp.inf, lg)
    v2 = jnp.max(lg2, axis=-1, keepdims=True)
    i2 = jnp.min(jnp.where(lg2 == v2, lane, float(LANES)), axis=-1, keepdims=True)
    e2 = jnp.exp(v2 - v1)
    w1 = 1.0 / (1.0 + e2)
    w2 = e2 / (1.0 + e2)
    rec = jnp.where(lane == i1, w1, 0.0) + jnp.where(lane == i2, w2, 0.0)
    for slot, val in ((ROUTE_W1, w1), (ROUTE_W2, w2), (ROUTE_I1, i1), (ROUTE_I2, i2)):
        rec = rec + jnp.where(lane == float(slot), val, 0.0)
    return rec


def _merge_kernel(*refs, with_router):
    if with_router:
        (x_ref, sc1_ref, sh1_ref, g1_ref, sc2_ref, sh2_ref, hr_ref, hf_ref, hc_ref, hd_ref,
         wg_ref, wb_ref, wo_ref, lg_ref, lb_ref, wr_ref, br_ref, x1_ref, u2_ref, rcol_ref, rrow_ref) = refs
    else:
        (x_ref, sc1_ref, sh1_ref, g1_ref, sc2_ref, sh2_ref, hr_ref, hf_ref, hc_ref, hd_ref,
         wg_ref, wb_ref, wo_ref, lg_ref, lb_ref, x1_ref, u2_ref) = refs
    x = x_ref[...]
    u = _bf(x * (1.0 + sc1_ref[...]) + sh1_ref[...])
    merged = None
    for n, h_ref in enumerate((hr_ref, hf_ref, hc_ref, hd_ref)):
        gate = _dot(u, wg_ref[:, n * D_MODEL:(n + 1) * D_MODEL])
        term = _sigmoid(gate) * _dot(h_ref[...], wb_ref[n])
        merged = term if merged is None else merged + term
    mix = _dot(_bf(merged), wo_ref[...])
    x1 = _layer_norm_rows(ALPHA * x + g1_ref[...] * mix, lg_ref[...], lb_ref[...])
    x1_ref[...] = x1
    u2 = x1 * (1.0 + sc2_ref[...]) + sh2_ref[...]
    u2_ref[...] = u2.astype(u2_ref.dtype)
    if with_router:
        rec = _top2_route(_dot(_bf(u2), wr_ref[...]) + br_ref[...])
        rcol_ref[...] = rec
        rrow_ref[...] = rec.T[8:16, :]


def _merge(x2d, mods, branches, w_gate, w_branch, w_out, ln_g, ln_b, router, tm, tiles_per_group):
    n = x2d.shape[0]
    r = mods[0].shape[1]
    w = BRANCH_W
    with_router = router is not None
    row = lambda width: pl.BlockSpec((tm, width), lambda i: (i, 0))
    mod_spec = pl.BlockSpec((None, r, D_MODEL), lambda i: (i // tiles_per_group, 0, 0))
    full = lambda shape: pl.BlockSpec(shape, lambda i: (0,) * len(shape))
    resident = lambda shape: pl.BlockSpec(shape, lambda i: (0,) * len(shape), pipeline_mode=pl.Buffered(1))
    in_specs = ([row(D_MODEL)] + [mod_spec] * 5 + [row(w)] * 4
                + [resident((D_MODEL, 4 * D_MODEL)), resident((4, w, D_MODEL)), resident((D_MODEL, D_MODEL)),
                   full((1, D_MODEL)), full((1, D_MODEL))])
    args = [x2d, *mods, *branches, w_gate, w_branch, w_out, ln_g.reshape(1, D_MODEL), ln_b.reshape(1, D_MODEL)]
    out_shape = [jax.ShapeDtypeStruct((n, D_MODEL), F32),
                 jax.ShapeDtypeStruct((n, D_MODEL), F32 if with_router else BF16)]
    out_specs = [row(D_MODEL), row(D_MODEL)]
    if with_router:
        in_specs += [full((D_MODEL, LANES)), full((1, LANES))]
        args += list(router)
        out_shape += [jax.ShapeDtypeStruct((n, LANES), F32), jax.ShapeDtypeStruct((n // tm, 8, tm), F32)]
        out_specs += [row(LANES), pl.BlockSpec((None, 8, tm), lambda i: (i, 0, 0))]
    return pl.pallas_call(
        functools.partial(_merge_kernel, with_router=with_router),
        out_shape=tuple(out_shape),
        grid=(n // tm,),
        in_specs=in_specs,
        out_specs=tuple(out_specs),
        compiler_params=_params(1),
        name="merge_outproj_ln",
    )(*args)


def _swiglu_chunks(u, w_up_ref, w_down_ref, hidden, chunk):
    acc = None
    for c0 in range(0, hidden, chunk):
        c1 = min(c0 + chunk, hidden)
        a = _dot(u, w_up_ref[:, c0:c1])
        g = _dot(u, w_up_ref[:, hidden + c0:hidden + c1])
        y = _dot(_bf(a * _sigmoid(a) * g), w_down_ref[c0:c1, :])
        acc = y if acc is None else acc + y
    return acc


def _ffn_kernel(u_ref, x1_ref, g2_ref, wu_ref, wd_ref, lg_ref, lb_ref, o_ref, *, chunk):
    f = _swiglu_chunks(u_ref[...], wu_ref, wd_ref, D_FF, chunk)
    z = ALPHA * x1_ref[...] + g2_ref[...] * f
    o_ref[...] = _layer_norm_rows(z, lg_ref[...], lb_ref[...])


def _ffn_dense(u2, x1, gate2, w_up, w_down, ln_g, ln_b, tm, tiles_per_group, chunk):
    n = u2.shape[0]
    r = gate2.shape[1]
    resident = lambda shape: pl.BlockSpec(shape, lambda i: (0,) * len(shape), pipeline_mode=pl.Buffered(1))
    return pl.pallas_call(
        functools.partial(_ffn_kernel, chunk=chunk),
        out_shape=jax.ShapeDtypeStruct((n, D_MODEL), F32),
        grid=(n // tm,),
        in_specs=[
            pl.BlockSpec((tm, D_MODEL), lambda i: (i, 0)),
            pl.BlockSpec((tm, D_MODEL), lambda i: (i, 0)),
            pl.BlockSpec((None, r, D_MODEL), lambda i: (i // tiles_per_group, 0, 0)),
            resident((D_MODEL, 2 * D_FF)),
            resident((D_FF, D_MODEL)),
            pl.BlockSpec((1, D_MODEL), lambda i: (0, 0)),
            pl.BlockSpec((1, D_MODEL), lambda i: (0, 0)),
        ],
        out_specs=pl.BlockSpec((tm, D_MODEL), lambda i: (i, 0)),
        compiler_params=_params(1),
        name="ffn_dense",
    )(u2, x1, gate2, w_up, w_down, ln_g.reshape(1, D_MODEL), ln_b.reshape(1, D_MODEL))


def _moe_kernel(u_ref, x1_ref, g2_ref, cmb_ref, wa_ref, wg_ref, wd_ref, lg_ref, lb_ref, o_ref, acc_scr):
    e = pl.program_id(1)
    j = pl.program_id(2)

    @pl.when((e == 0) & (j == 0))
    def _():
        acc_scr[...] = jnp.zeros_like(acc_scr)

    u = u_ref[...]
    a = _dot(u, wa_ref[...])
    g = _dot(u, wg_ref[...])
    y = _dot(_bf(a * _sigmoid(a) * g), wd_ref[...])
    cmb = cmb_ref[...]
    lane = lax.broadcasted_iota(jnp.int32, cmb.shape, 1)
    ce = jnp.sum(jnp.where(lane == e, cmb, 0.0), axis=-1, keepdims=True)
    acc_scr[...] += ce * y

    @pl.when((e == pl.num_programs(1) - 1) & (j == pl.num_programs(2) - 1))
    def _():
        z = ALPHA * x1_ref[...] + g2_ref[...] * acc_scr[...]
        o_ref[...] = _layer_norm_rows(z, lg_ref[...], lb_ref[...])


def _ffn_experts(u2, x1, gate2, combine, w_in, w_out, ln_g, ln_b, tm, tiles_per_group, th):
    n = u2.shape[0]
    r = gate2.shape[1]
    nh = D_EXPERT // th
    return pl.pallas_call(
        _moe_kernel,
        out_shape=jax.ShapeDtypeStruct((n, D_MODEL), F32),
        grid=(n // tm, N_EXPERTS, nh),
        in_specs=[
            pl.BlockSpec((tm, D_MODEL), lambda i, e, j: (i, 0)),
            pl.BlockSpec((tm, D_MODEL), lambda i, e, j: (i, 0)),
            pl.BlockSpec((None, r, D_MODEL), lambda i, e, j: (i // tiles_per_group, 0, 0)),
            pl.BlockSpec((tm, LANES), lambda i, e, j: (i, 0)),
            pl.BlockSpec((None, D_MODEL, th), lambda i, e, j: (e, 0, j)),
            pl.BlockSpec((None, D_MODEL, th), lambda i, e, j: (e, 0, nh + j)),
            pl.BlockSpec((None, th, D_MODEL), lambda i, e, j: (e, j, 0)),
            pl.BlockSpec((1, D_MODEL), lambda i, e, j: (0, 0)),
            pl.BlockSpec((1, D_MODEL), lambda i, e, j: (0, 0)),
        ],
        out_specs=pl.BlockSpec((tm, D_MODEL), lambda i, e, j: (i, 0)),
        scratch_shapes=[pltpu.VMEM((tm, D_MODEL), F32)],
        compiler_params=_params(3),
        name="ffn_experts",
    )(u2, x1, gate2, combine, w_in, w_in, w_out, ln_g.reshape(1, D_MODEL), ln_b.reshape(1, D_MODEL))


EXPERT_TILE = 1024


def _rank_kernel(rr_ref, rank_ref, cnt_ref, carry_scr):
    b = pl.program_id(0)

    @pl.when(b == 0)
    def _():
        carry_scr[...] = jnp.zeros_like(carry_scr)

    rr = rr_ref[...]
    tm = rr.shape[1]
    i1 = rr[2:3]
    i2 = rr[3:4]
    e = lax.broadcasted_iota(jnp.int32, (N_EXPERTS, tm), 0).astype(F32)
    oh1 = jnp.where(e == i1, 1.0, 0.0)
    oh2 = jnp.where(e == i2, 1.0, 0.0)
    sel = oh1 + oh2
    r = lax.broadcasted_iota(jnp.int32, (tm, tm), 0)
    c = lax.broadcasted_iota(jnp.int32, (tm, tm), 1)
    triu = jnp.where(r <= c, 1.0, 0.0).astype(BF16)
    csum = _dot(_bf(sel), triu)
    carry = carry_scr[:, 0:1]
    rank = csum - sel + carry
    r1 = jnp.sum(oh1 * rank, axis=0, keepdims=True)
    r2 = jnp.sum(oh2 * rank, axis=0, keepdims=True)
    rec = jnp.concatenate([i1, i2, r1, r2, jnp.zeros((4, tm), F32)], axis=0)
    rank_ref[...] = rec.astype(jnp.int32)
    total = carry + csum[:, tm - 1:tm]
    carry_scr[...] = jnp.broadcast_to(total, carry_scr.shape)
    cnt_ref[...] = jnp.broadcast_to(total, cnt_ref.shape).astype(jnp.int32)


def _route_ranks(route_row):
    nblk, _, tm = route_row.shape
    return pl.pallas_call(
        _rank_kernel,
        out_shape=(jax.ShapeDtypeStruct((nblk, 8, tm), jnp.int32),
                   jax.ShapeDtypeStruct((N_EXPERTS, LANES), jnp.int32)),
        grid=(nblk,),
        in_specs=[pl.BlockSpec((None, 8, tm), lambda b: (b, 0, 0))],
        out_specs=(pl.BlockSpec((None, 8, tm), lambda b: (b, 0, 0)),
                   pl.BlockSpec((N_EXPERTS, LANES), lambda b: (0, 0))),
        scratch_shapes=[pltpu.VMEM((N_EXPERTS, LANES), F32)],
        compiler_params=_params(1),
        name="route_ranks",
    )(route_row)


def _row_copy(src_ref, src_row, dst_ref, dst_row, sem):
    return pltpu.make_async_copy(src_ref.at[pl.ds(src_row, 1)], dst_ref.at[pl.ds(dst_row, 1)], sem)


def _dispatch_kernel(last_ref, slot_ref, u_ref, xs_ref, zero_scr, sem):
    tm = u_ref.shape[0]
    gt = zero_scr.shape[0]

    @pl.when(pl.program_id(0) == 0)
    def _():
        zero_scr[...] = jnp.zeros_like(zero_scr)
        for e in range(N_EXPERTS):
            @pl.when(last_ref[e] >= 0)
            def _():
                first = pl.multiple_of(last_ref[e], gt)
                fill = pltpu.make_async_copy(zero_scr, xs_ref.at[pl.ds(first, gt)], sem)
                fill.start()
                fill.wait()

    def start(r, carry):
        for k in range(2):
            _row_copy(u_ref, r, xs_ref, slot_ref[k, r], sem).start()
        return carry

    lax.fori_loop(0, tm, start, 0, unroll=8)

    def wait(r, carry):
        for k in range(2):
            _row_copy(u_ref, 0, xs_ref, 0, sem).wait()
        return carry

    lax.fori_loop(0, tm, wait, 0, unroll=8)


def _dispatch(u2, slots, last_tile, n_slots, group_tile):
    n, d = u2.shape
    nblk, _, tm = slots.shape
    grid_spec = pltpu.PrefetchScalarGridSpec(
        num_scalar_prefetch=1,
        grid=(nblk,),
        in_specs=[
            pl.BlockSpec((None, 2, tm), lambda b, last: (b, 0, 0), memory_space=pltpu.SMEM),
            pl.BlockSpec((tm, d), lambda b, last: (b, 0)),
        ],
        out_specs=pl.BlockSpec(memory_space=pl.ANY),
        scratch_shapes=[pltpu.VMEM((group_tile, d), F32), pltpu.SemaphoreType.DMA],
    )
    return pl.pallas_call(
        _dispatch_kernel,
        out_shape=jax.ShapeDtypeStruct((n_slots, d), F32),
        grid_spec=grid_spec,
        compiler_params=_params(1),
        name="expert_dispatch",
    )(last_tile, slots, u2)


def _grouped_kernel(toff_ref, nt_ref, xs_ref, wi_ref, wo_ref, ys_ref, *, chunk):
    @pl.when(pl.program_id(0) < nt_ref[0])
    def _():
        ys_ref[...] = _swiglu_chunks(_bf(xs_ref[...]), wi_ref, wo_ref, D_EXPERT, chunk)


def _grouped_experts(xs, tile_off, n_tiles, w_in, w_out, tm, chunk):
    n_slots, d = xs.shape

    def expert(i, toff):
        e = 0
        for k in range(1, N_EXPERTS):
            e = e + (i >= toff[k]).astype(jnp.int32)
        return e

    def tile(i, nt):
        return jnp.minimum(i, nt[0] - 1)

    grid_spec = pltpu.PrefetchScalarGridSpec(
        num_scalar_prefetch=2,
        grid=(n_slots // tm,),
        in_specs=[
            pl.BlockSpec((tm, d), lambda i, toff, nt: (tile(i, nt), 0)),
            pl.BlockSpec((None, d, 2 * D_EXPERT), lambda i, toff, nt: (expert(tile(i, nt), toff), 0, 0),
                         pipeline_mode=pl.Buffered(1)),
            pl.BlockSpec((None, D_EXPERT, d), lambda i, toff, nt: (expert(tile(i, nt), toff), 0, 0),
                         pipeline_mode=pl.Buffered(1)),
        ],
        out_specs=pl.BlockSpec((tm, d), lambda i, toff, nt: (tile(i, nt), 0)),
    )
    return pl.pallas_call(
        functools.partial(_grouped_kernel, chunk=chunk),
        out_shape=jax.ShapeDtypeStruct((n_slots, d), F32),
        grid_spec=grid_spec,
        compiler_params=_params(1),
        name="grouped_experts",
    )(tile_off, n_tiles, xs, w_in, w_out)


def _combine_kernel(slot_ref, ys_ref, x1_ref, g2_ref, rcol_ref, lg_ref, lb_ref, o_ref, y1_scr, y2_scr, sem):
    tm = x1_ref.shape[0]
    bufs = (y1_scr, y2_scr)

    def start(r, carry):
        for k in range(2):
            _row_copy(ys_ref, slot_ref[k, r], bufs[k], r, sem).start()
        return carry

    lax.fori_loop(0, tm, start, 0, unroll=8)

    def wait(r, carry):
        for k in range(2):
            _row_copy(ys_ref, 0, bufs[k], 0, sem).wait()
        return carry

    lax.fori_loop(0, tm, wait, 0, unroll=8)
    rec = rcol_ref[...]
    f = rec[:, ROUTE_W1:ROUTE_W1 + 1] * y1_scr[...] + rec[:, ROUTE_W2:ROUTE_W2 + 1] * y2_scr[...]
    z = ALPHA * x1_ref[...] + g2_ref[...] * f
    o_ref[...] = _layer_norm_rows(z, lg_ref[...], lb_ref[...])


def _combine(ys, slots, x1, gate2, route_col, ln_g, ln_b, tiles_per_group):
    n, d = x1.shape
    nblk, _, tm = slots.shape
    r = gate2.shape[1]
    return pl.pallas_call(
        _combine_kernel,
        out_shape=jax.ShapeDtypeStruct((n, d), F32),
        grid=(nblk,),
        in_specs=[
            pl.BlockSpec((None, 2, tm), lambda b: (b, 0, 0), memory_space=pltpu.SMEM),
            pl.BlockSpec(memory_space=pl.ANY),
            pl.BlockSpec((tm, d), lambda b: (b, 0)),
            pl.BlockSpec((None, r, d), lambda b: (b // tiles_per_group, 0, 0)),
            pl.BlockSpec((tm, LANES), lambda b: (b, 0)),
            pl.BlockSpec((1, d), lambda b: (0, 0)),
            pl.BlockSpec((1, d), lambda b: (0, 0)),
        ],
        out_specs=pl.BlockSpec((tm, d), lambda b: (b, 0)),
        scratch_shapes=[pltpu.VMEM((tm, d), F32), pltpu.VMEM((tm, d), F32), pltpu.SemaphoreType.DMA],
        compiler_params=_params(1),
        name="expert_combine_ln",
    )(slots, ys, x1, gate2, route_col, ln_g.reshape(1, d), ln_b.reshape(1, d))


def _ffn_routed(u2, x1, gate2, route_col, route_row, w_in, w_out, ln_g, ln_b, tiles_per_group):
    n = u2.shape[0]
    tm = min(EXPERT_TILE, n)
    assert n % tm == 0
    ranks, counts = _route_ranks(route_row)
    cnt = counts[:, 0]
    padded = ((cnt + tm - 1) // tm) * tm
    off = (jnp.cumsum(padded) - padded).astype(jnp.int32)
    last_tile = jnp.where(padded > 0, off + padded - tm, -1).astype(jnp.int32)
    n_slots = 2 * n + N_EXPERTS * tm
    slots = off[ranks[:, 0:2, :]] + ranks[:, 2:4, :]
    xs = _dispatch(u2, slots, last_tile, n_slots, tm)
    ys = _grouped_experts(xs, (off // tm).astype(jnp.int32), (jnp.sum(padded) // tm).astype(jnp.int32).reshape(1),
                          w_in, w_out, tm, 512)
    return _combine(ys, slots, x1, gate2, route_col, ln_g, ln_b, tiles_per_group)


def _split_w_in(w_in_l):
    w = BRANCH_W
    col = lambda off, n: w_in_l[:, off:off + n]
    o = 0
    ret_q, ret_k, ret_v, ret_g = (col(o + i * w, w) for i in range(4)); o += 4 * w
    fox_q, fox_k, fox_v = (col(o + i * w, w) for i in range(3)); o += 3 * w
    fox_f = col(o, N_HEADS); o += N_HEADS
    conv = col(o, 2 * w); o += 2 * w
    diff_q, diff_k, diff_v = (col(o + i * w, w) for i in range(3)); o += 3 * w
    gate = col(o, 4 * D_MODEL)
    rot_r = lambda m: _rotated_columns(m, HEAD_DIM, HEAD_DIM)
    rot_d = lambda m: _rotated_columns(m, DIFF_SUB, ROT_DIM)
    w_ret = jnp.concatenate([ret_q, rot_r(ret_q), ret_k, rot_r(ret_k), ret_v, ret_g], axis=1)
    w_fox = jnp.concatenate([fox_q, fox_k, fox_v, fox_f, jnp.zeros((D_MODEL, LANES - N_HEADS), F32)], axis=1)
    w_diff = jnp.concatenate([diff_q, rot_d(diff_q), diff_k, rot_d(diff_k), diff_v], axis=1)
    return _bf(jnp.concatenate([w_ret, conv, w_fox, w_diff], axis=1)), _bf(gate)


def _block_diag_state(s):
    b = s.shape[0]
    eye = jnp.eye(N_HEADS, dtype=s.dtype)
    return jnp.einsum('bhde,hg->bhdge', s, eye).reshape(b, BRANCH_W, BRANCH_W)


def _diag_blocks(s_bd):
    b = s_bd.shape[0]
    s4 = s_bd.reshape(b, N_HEADS, HEAD_DIM, N_HEADS, HEAD_DIM)
    return jnp.stack([s4[:, h, :, h, :] for h in range(N_HEADS)], axis=1)


def _pad_rows(a, rows):
    return jnp.concatenate([a, jnp.zeros((a.shape[0], rows - a.shape[1]) + a.shape[2:], a.dtype)], axis=1)


def _trunk_layer(x, mod, pos, past, l, lw, sample, depth, kv_state):
    b, t, _ = x.shape
    d = D_MODEL
    w = BRANCH_W
    shift1, scale1, gate1 = (mod[0][:, None, i * d:(i + 1) * d] for i in range(3))
    shift2, scale2, gate2 = (mod[1][:, None, i * d:(i + 1) * d] for i in range(3))
    lam_init = 0.8 - 0.6 * math.exp(-0.3 * l)
    chunk = min(t, KV_BLOCK)
    pr = _in_projection(x, scale1, shift1, lw['w_mix'], lw['b_fox_f'], pos, min(t, 512), depth, l, kv_state)
    kv_state = pr['state']

    s0 = jnp.zeros((b, w, w), F32) if past is None else _block_diag_state(past[5])
    h_ret, s_bd = _retention(pr['rq'], pr['rk'], pr['rv'], pr['rg'], s0, chunk)
    ret_state = _diag_blocks(s_bd)

    if past is None:
        hist = jnp.zeros((b, HIST_ROWS, w), F32)
    else:
        hist = jnp.concatenate([jnp.zeros((b, HIST_ROWS - (CONV_W - 1), w), F32), past[6]], axis=1)
    h_conv, tail = _conv_mixer(pr['glu'], hist, lw['w_conv'], lw['b_conv'], lw['conv_ln_g'], lw['conv_ln_b'], chunk)
    conv_buf = tail[:, HIST_ROWS - (CONV_W - 1):, :]

    fq, fkb, fvb, lf = pr['fq'], pr['fkb'], pr['fvb'], pr['lf'][l]
    if past is None:
        off, tk_valid = 0, t
        k_all, v_all, lf_all, fox_new = fkb, fvb, lf, None
    else:
        p_len = past[0].shape[1]
        off, tk_valid = p_len, p_len + t
        tk_pad = p_len + KV_BLOCK
        k_all, v_all = past[0].reshape(b, p_len, w), past[1].reshape(b, p_len, w)
        fox_new = (_pad_rows(fkb, KV_BLOCK), _pad_rows(fvb, KV_BLOCK))
        past_lf = jnp.concatenate([past[2].astype(F32), jnp.zeros((b, p_len, LANES - N_HEADS), F32)], axis=2)
        lf_all = _pad_rows(jnp.concatenate([past_lf, lf], axis=1), tk_pad)
    fcol, frow = _logf_cumsum(lf_all)
    tq = min(t, KV_BLOCK)
    if past is None:
        h_fox = _fox_attention(fq, k_all, v_all, fcol, frow, tq, off, tk_valid)
    else:
        h_fox = _decode_attention(fq, k_all, v_all, fox_new, (fcol, frow), fox=True)

    dq, dkb, dvb = pr['dq'], pr['dkb'], pr['dvb']
    if past is None:
        dk_all, dv_all, diff_new = dkb, dvb, None
    else:
        dk_all, dv_all = past[3].reshape(b, p_len, w), past[4].reshape(b, p_len, w)
        diff_new = (_pad_rows(dkb, KV_BLOCK), _pad_rows(dvb, KV_BLOCK))
    if past is None:
        h_diff = _diff_attention(dq, dk_all, dv_all, lw['diff_lambda'], lw['diff_subln_g'], tq, off, tk_valid, lam_init)
    else:
        g_sub = jnp.tile(lw['diff_subln_g'].astype(F32), N_HEADS).reshape(1, w)
        h_diff = _decode_attention(dq, dk_all, dv_all, diff_new, (lw['diff_lambda'].astype(F32), g_sub), fox=False,
                                   lam_init=lam_init)

    n = b * t
    if sample:
        tm_merge = tm_ffn = min(n, 256)
        tpg_merge = tpg_ffn = 1
        rows_mod = lambda m, tm: jnp.repeat(m, t, axis=1).reshape(n // tm, tm, d)
    else:
        tm_merge, tm_ffn = min(t, 512), min(t, 1024)
        tpg_merge, tpg_ffn = t // tm_merge, t // tm_ffn
        rows_mod = lambda m, tm: m
    mods = tuple(rows_mod(m, tm_merge) for m in (scale1, shift1, gate1, scale2, shift2))
    branches = tuple(h.reshape(n, w) for h in (h_ret, h_fox, h_conv, h_diff))
    merged = _merge(x.reshape(n, d), mods, branches, lw['w_gate'], lw['w_branch'], lw['w_out'],
                    lw['ln_g'][0], lw['ln_b'][0], lw.get('router'), tm_merge, tpg_merge)
    g2 = rows_mod(gate2, tm_ffn)
    if 'router' in lw and not sample:
        x1, u2, route_col, route_row = merged
        x2 = _ffn_routed(u2, x1, rows_mod(gate2, tm_merge), route_col, route_row, lw['w_exp_in'], lw['w_exp_out'],
                         lw['ln_g'][1], lw['ln_b'][1], tpg_merge)
    elif 'router' in lw:
        x1, u2, route_col, _ = merged
        x2 = _ffn_experts(_bf(u2), x1, g2, route_col, lw['w_exp_in'], lw['w_exp_out'], lw['ln_g'][1], lw['ln_b'][1],
                          tm_ffn, tpg_ffn, 512)
    else:
        x1, u2 = merged
        x2 = _ffn_dense(u2, x1, g2, lw['w_ffn_in'], lw['w_ffn_out'], lw['ln_g'][1], lw['ln_b'][1],
                        tm_ffn, tpg_ffn, 512)
    return x2.reshape(b, t, d), kv_state, (ret_state, conv_buf)


def _state_outputs(kv_state, small):
    fk, fv, lf, dk, dv = kv_state
    depth, b, t, w = fk.shape
    heads = lambda a, nh: a.reshape(depth, b, t, nh, w // nh)
    ret_state, conv_buf = (jnp.stack(a) for a in zip(*small))
    return (heads(fk, N_HEADS), heads(fv, N_HEADS), lf[..., :N_HEADS], heads(dk, N_SUB), heads(dv, N_HEADS),
            ret_state, conv_buf)


def kernel(x_prompt, x_sample, c_prompt, c_sample, cache_fox_k, cache_fox_v, cache_fox_logf, cache_diff_k, cache_diff_v, state_ret, state_conv, w_in, b_fox_f, w_conv, b_conv, conv_ln_g, conv_ln_b, diff_lambda, diff_subln_g, w_branch, w_out, w_ada, b_ada, ln_g, ln_b, w_ffn_in, w_ffn_out, w_router, b_router, w_exp_in, w_exp_out):
    depth = w_in.shape[0]
    bp = x_prompt.shape[0]
    past_len = cache_fox_k.shape[2]
    pos_p = jnp.arange(x_prompt.shape[1], dtype=jnp.int32)
    pos_s = past_len + jnp.arange(x_sample.shape[1], dtype=jnp.int32)
    mod = _modulation(jnp.concatenate([c_prompt, c_sample], axis=0), w_ada, b_ada)
    yp, ys = x_prompt, x_sample
    kv_p = kv_s = None
    small_p, small_s = [], []
    for l in range(depth):
        w_mix, w_gate = _split_w_in(w_in[l])
        lw = dict(w_mix=w_mix, w_gate=w_gate,
                  b_fox_f=b_fox_f[l], w_conv=w_conv[l], b_conv=b_conv[l], conv_ln_g=conv_ln_g[l],
                  conv_ln_b=conv_ln_b[l], diff_lambda=diff_lambda[l], diff_subln_g=diff_subln_g[l],
                  w_branch=_bf(w_branch[l]), w_out=_bf(w_out[l]), ln_g=ln_g[l], ln_b=ln_b[l])
        if l % 2 == 0:
            lw['w_ffn_in'] = _bf(w_ffn_in[l // 2])
            lw['w_ffn_out'] = _bf(w_ffn_out[l // 2])
        else:
            wr = jnp.concatenate([w_router[l // 2], jnp.zeros((D_MODEL, LANES - N_EXPERTS), F32)], axis=1)
            br = jnp.concatenate([b_router[l // 2].astype(F32), jnp.zeros((LANES - N_EXPERTS,), F32)]).reshape(1, LANES)
            lw['router'] = (_bf(wr), br)
            lw['w_exp_in'] = _bf(w_exp_in[l // 2])
            lw['w_exp_out'] = _bf(w_exp_out[l // 2])
        past_l = (cache_fox_k[l], cache_fox_v[l], cache_fox_logf[l], cache_diff_k[l], cache_diff_v[l],
                  state_ret[l], state_conv[l])
        yp, kv_p, st_p = _trunk_layer(yp, mod[l][:, :bp], pos_p, None, l, lw, False, depth, kv_p)
        ys, kv_s, st_s = _trunk_layer(ys, mod[l][:, bp:], pos_s, past_l, l, lw, True, depth, kv_s)
        small_p.append(st_p)
        small_s.append(st_s)
    return (yp, ys) + _state_outputs(kv_p, small_p) + _state_outputs(kv_s, small_s)
```
